```python
import jax
import jax.numpy as jnp
from jax import lax
import numpy as np

D_MODEL = 1024
BATCH = 8
SEQ = 4096
DEPTH = 1

CHUNK = 64
D_MLSTM = D_MODEL // 2
D_POOL = D_MODEL - D_MLSTM
N_HEADS = 4
HEAD_DIM = D_MLSTM // N_HEADS
CONV_WIDTH = 4
POOL_WINDOWS = (2, 4, 8, 16)
N_POOL_GROUPS = len(POOL_WINDOWS)
POOL_GROUP_DIM = D_POOL // N_POOL_GROUPS
N_EXPERT_GROUPS = 4
EXPERTS_PER_GROUP = 8
N_EXPERTS = N_EXPERT_GROUPS * EXPERTS_PER_GROUP
TOP_K = 2
D_EXPERT = D_MODEL // 2
MOE_BLOCK = 128
EPS = 1e-6

COL_U = 0
COL_V = COL_U + D_MLSTM
COL_O = COL_V + D_MLSTM
COL_I = COL_O + D_MLSTM
COL_F = COL_I + N_HEADS
COL_P = COL_F + N_HEADS
IN_COLS = COL_P + D_POOL

kernel_name = "hybrid_mlstm_pool_hmoe_block"


def rms_norm(x, g):
    xf = x.astype(jnp.float32)
    y = xf * lax.rsqrt(jnp.mean(xf * xf, axis=-1, keepdims=True) + EPS)
    return (y * g.astype(jnp.float32)).astype(x.dtype)


def causal_conv(x, w, b):
    k_w, ch = w.shape
    y = lax.conv_general_dilated(x, w[:, None, :], window_strides=(1,), padding=[(k_w - 1, 0)],
                                 dimension_numbers=('NWC', 'WIO', 'NWC'), feature_group_count=ch)
    return y + b


def mlstm_chunkwise(q, k, v, i_pre, f_pre):
    bsz, nh, seq, dh = q.shape
    nc = seq // CHUNK
    q = q.astype(jnp.float32) * (dh ** -0.5)
    k = k.astype(jnp.float32)
    v = v.astype(jnp.float32)
    log_i = i_pre.astype(jnp.float32)
    log_f = jax.nn.log_sigmoid(f_pre.astype(jnp.float32))

    def to_chunks(a):
        return jnp.moveaxis(a.reshape(bsz, nh, nc, CHUNK, *a.shape[3:]), 2, 0)

    qc, kc, vc, ic = to_chunks(q), to_chunks(k), to_chunks(v), to_chunks(log_i)
    bc = jnp.cumsum(to_chunks(log_f), axis=-1)
    causal = jnp.tril(jnp.ones((CHUNK, CHUNK), dtype=bool))

    def step(carry, inp):
        c_st, n_st, m_st = carry
        q_, k_, v_, ig, b_ = inp
        d_log = b_[..., :, None] - b_[..., None, :] + ig[..., None, :]
        d_log = jnp.where(causal, d_log, -jnp.inf)
        inter = b_ + m_st[..., None]
        m_t = jnp.maximum(inter, jnp.max(d_log, axis=-1))
        w_intra = jnp.exp(d_log - m_t[..., None])
        a_inter = jnp.exp(inter - m_t)
        s = jnp.einsum('bhtd,bhsd->bhts', q_, k_) * w_intra
        num = (a_inter[..., None] * jnp.einsum('bhed,bhtd->bhte', c_st, q_)
               + jnp.einsum('bhts,bhse->bhte', s, v_))
        den = a_inter * jnp.einsum('bhd,bhtd->bht', n_st, q_) + jnp.sum(s, axis=-1)
        h = num / jnp.maximum(jnp.abs(den), jnp.exp(-m_t))[..., None]
        b_last = b_[..., -1]
        w_log = b_last[..., None] - b_ + ig
        m_new = jnp.maximum(b_last + m_st, jnp.max(w_log, axis=-1))
        a_prev = jnp.exp(b_last + m_st - m_new)
        w_s = jnp.exp(w_log - m_new[..., None])
        c_new = a_prev[..., None, None] * c_st + jnp.einsum('bhs,bhse,bhsd->bhed', w_s, v_, k_)
        n_new = a_prev[..., None] * n_st + jnp.einsum('bhs,bhsd->bhd', w_s, k_)
        return (c_new, n_new, m_new), h

    init = (jnp.zeros((bsz, nh, dh, dh), jnp.float32), jnp.zeros((bsz, nh, dh), jnp.float32),
            jnp.zeros((bsz, nh), jnp.float32))
    _, hs = lax.scan(step, init, (qc, kc, vc, ic, bc))
    return jnp.moveaxis(hs, 0, 2).reshape(bsz, nh, seq, dh)


def mlstm_group(u, v, o_pre, i_pre, f_pre, conv_w, conv_b, w_q, w_k, b_i, b_f, norm_g, skip):
    bsz, seq, _ = u.shape
    uc = jax.nn.silu(causal_conv(u, conv_w, conv_b))
    uh = uc.reshape(bsz, seq, N_HEADS, HEAD_DIM)
    q = jnp.einsum('bshd,hde->bhse', uh, w_q)
    k = jnp.einsum('bshd,hde->bhse', uh, w_k)
    vh = v.reshape(bsz, seq, N_HEADS, HEAD_DIM).transpose(0, 2, 1, 3)
    ig = (i_pre + b_i).transpose(0, 2, 1)
    fg = (f_pre + b_f).transpose(0, 2, 1)
    h = mlstm_chunkwise(q, k, vh, ig, fg)
    h = h * lax.rsqrt(jnp.mean(h * h, axis=-1, keepdims=True) + EPS)
    h = h * norm_g.astype(jnp.float32).reshape(N_HEADS, 1, HEAD_DIM)
    h = h.transpose(0, 2, 1, 3).reshape(bsz, seq, D_MLSTM)
    out = jax.nn.sigmoid(o_pre.astype(jnp.float32)) * (h + skip.astype(jnp.float32) * uc.astype(jnp.float32))
    return out.astype(u.dtype)


def pool_group(p, w_pool, b_pool, pool_scale):
    bsz, seq, _ = p.shape
    pf = p.astype(jnp.float32).reshape(bsz, seq, N_POOL_GROUPS, POOL_GROUP_DIM)
    cs = jnp.pad(jnp.cumsum(pf, axis=1), ((0, 0), (1, 0), (0, 0), (0, 0)))
    t = jnp.arange(seq, dtype=jnp.int32)
    win = jnp.array(POOL_WINDOWS, dtype=jnp.int32)
    start = jnp.maximum(t[:, None] + 1 - win[None, :], 0)
    lo = cs[:, start, jnp.arange(N_POOL_GROUPS)[None, :]]
    count = jnp.minimum(t[:, None] + 1, win[None, :]).astype(jnp.float32)
    pooled = (cs[:, 1:] - lo) / count[None, :, :, None] - pf
    y = jnp.einsum('bsgc,gcd->bsgd', pooled, w_pool.astype(jnp.float32)).reshape(bsz, seq, D_POOL)
    y = (y + b_pool.astype(jnp.float32)) * pool_scale.astype(jnp.float32)
    return y.astype(p.dtype)


def hierarchical_moe(h, w_rg, b_rg, w_re, b_re, w_gate, w_up, w_down):
    bsz, seq, dm = h.shape
    n_tok = bsz * seq
    hf = h.reshape(n_tok, dm)
    g_logits = (hf @ w_rg + b_rg).astype(jnp.float32)
    g_prob = jax.nn.softmax(g_logits, axis=-1)
    g_idx = jnp.argmax(g_logits, axis=-1).astype(jnp.int32)
    g_gate = jnp.take_along_axis(g_prob, g_idx[:, None], axis=1)[:, 0]
    e_logits = (hf @ w_re + b_re).astype(jnp.float32).reshape(n_tok, N_EXPERT_GROUPS, EXPERTS_PER_GROUP)
    e_logits = jnp.take_along_axis(e_logits, g_idx[:, None, None], axis=1)[:, 0]
    top_val, top_loc = lax.top_k(e_logits, TOP_K)
    gates = jax.nn.softmax(top_val, axis=-1) * g_gate[:, None]
    expert = g_idx[:, None] * EXPERTS_PER_GROUP + top_loc.astype(jnp.int32)
    n_asg = n_tok * TOP_K
    flat_e = expert.reshape(n_asg)
    flat_tok = jnp.repeat(jnp.arange(n_tok, dtype=jnp.int32), TOP_K)
    flat_gate = gates.reshape(n_asg)
    order = jnp.argsort(flat_e)
    sorted_e = flat_e[order]
    counts = jnp.bincount(flat_e, length=N_EXPERTS).astype(jnp.int32)
    starts = jnp.cumsum(counts) - counts
    padded = (counts + MOE_BLOCK - 1) // MOE_BLOCK * MOE_BLOCK
    pad_ends = jnp.cumsum(padded)
    pad_starts = pad_ends - padded
    rank = jnp.arange(n_asg, dtype=jnp.int32) - starts[sorted_e]
    dest = pad_starts[sorted_e] + rank
    n_blocks = (n_asg + MOE_BLOCK - 1) // MOE_BLOCK + N_EXPERTS
    n_slots = n_blocks * MOE_BLOCK
    slot_tok = jnp.full((n_slots,), n_tok, jnp.int32).at[dest].set(flat_tok[order])
    slot_gate = jnp.zeros((n_slots,), jnp.float32).at[dest].set(flat_gate[order])
    block_pos = jnp.arange(n_blocks, dtype=jnp.int32) * MOE_BLOCK
    block_expert = jnp.minimum(jnp.searchsorted(pad_ends, block_pos, side='right'), N_EXPERTS - 1)

    def run_block(args):
        tok, e = args
        xb = hf[jnp.minimum(tok, n_tok - 1)]
        a = xb @ w_gate[e]
        b = xb @ w_up[e]
        return (jax.nn.silu(a) * b) @ w_down[e]

    out = lax.map(run_block, (slot_tok.reshape(n_blocks, MOE_BLOCK), block_expert))
    out = (out.reshape(n_slots, dm).astype(jnp.float32) * slot_gate[:, None]).astype(h.dtype)
    y = jnp.zeros((n_tok, dm), h.dtype).at[slot_tok].add(out, mode='drop')
    return y.reshape(bsz, seq, dm)


def setup_inputs(seed: int = 0) -> dict:
    key = jax.random.key(seed)
    ks = jax.random.split(key, 28)
    L = DEPTH

    def nrm(k, shape, scale):
        return jax.random.normal(k, shape, jnp.float32) * scale

    return {
        'x': nrm(ks[0], (BATCH, SEQ, D_MODEL), 1.0),
        'c': nrm(ks[1], (BATCH, D_MODEL), 1.0),
        'ada_w': nrm(ks[2], (L, D_MODEL, 6 * D_MODEL), D_MODEL ** -0.5),
        'ada_b': nrm(ks[3], (L, 6 * D_MODEL), 0.02),
        'norm1_g': 1.0 + nrm(ks[4], (L, D_MODEL), 0.02),
        'w_in': nrm(ks[5], (L, D_MODEL, IN_COLS), D_MODEL ** -0.5),
        'conv_w': nrm(ks[6], (L, CONV_WIDTH, D_MLSTM), CONV_WIDTH ** -0.5),
        'conv_b': nrm(ks[7], (L, D_MLSTM), 0.02),
        'w_q': nrm(ks[8], (L, N_HEADS, HEAD_DIM, HEAD_DIM), HEAD_DIM ** -0.5),
        'w_k': nrm(ks[9], (L, N_HEADS, HEAD_DIM, HEAD_DIM), HEAD_DIM ** -0.5),
        'b_igate': nrm(ks[10], (L, N_HEADS), 0.1),
        'b_fgate': jnp.linspace(3.0, 6.0, N_HEADS, dtype=jnp.float32)[None, :] + nrm(ks[11], (L, N_HEADS), 0.1),
        'mlstm_norm_g': 1.0 + nrm(ks[12], (L, D_MLSTM), 0.02),
        'mlstm_skip': 1.0 + nrm(ks[13], (L, D_MLSTM), 0.1),
        'w_pool': nrm(ks[14], (L, N_POOL_GROUPS, POOL_GROUP_DIM, POOL_GROUP_DIM), POOL_GROUP_DIM ** -0.5),
        'b_pool': nrm(ks[15], (L, D_POOL), 0.02),
        'pool_scale': 1.0 + nrm(ks[16], (L, D_POOL), 0.1),
        'w_out': nrm(ks[17], (L, D_MODEL, D_MODEL), D_MODEL ** -0.5),
        'norm2_g': 1.0 + nrm(ks[18], (L, D_MODEL), 0.02),
        'w_router_group': nrm(ks[19], (L, D_MODEL, N_EXPERT_GROUPS), D_MODEL ** -0.5),
        'b_router_group': nrm(ks[20], (L, N_EXPERT_GROUPS), 0.01),
        'w_router_expert': nrm(ks[21], (L, D_MODEL, N_EXPERTS), D_MODEL ** -0.5),
        'b_router_expert': nrm(ks[22], (L, N_EXPERTS), 0.01),
        'w_expert_gate': nrm(ks[23], (L, N_EXPERTS, D_MODEL, D_EXPERT), D_MODEL ** -0.5),
        'w_expert_up': nrm(ks[24], (L, N_EXPERTS, D_MODEL, D_EXPERT), D_MODEL ** -0.5),
        'w_expert_down': nrm(ks[25], (L, N_EXPERTS, D_EXPERT, D_MODEL), D_EXPERT ** -0.5),
        'final_g': 1.0 + nrm(ks[26], (D_MODEL,), 0.02),
    }


def reference(x, c, ada_w, ada_b, norm1_g, w_in, conv_w, conv_b, w_q, w_k, b_igate, b_fgate,
              mlstm_norm_g, mlstm_skip, w_pool, b_pool, pool_scale, w_out, norm2_g,
              w_router_group, b_router_group, w_router_expert, b_router_expert,
              w_expert_gate, w_expert_up, w_expert_down, final_g):
    bsz, seq, dm = x.shape
    for l in range(DEPTH):
        mod = (jax.nn.silu(c) @ ada_w[l] + ada_b[l]).reshape(bsz, 6, dm)
        shift_a, scale_a, gate_a = mod[:, 0, None, :], mod[:, 1, None, :], mod[:, 2, None, :]
        shift_f, scale_f, gate_f = mod[:, 3, None, :], mod[:, 4, None, :], mod[:, 5, None, :]
        h = rms_norm(x, norm1_g[l]) * (1.0 + scale_a) + shift_a
        proj = h @ w_in[l]
        y_m = mlstm_group(proj[..., COL_U:COL_V], proj[..., COL_V:COL_O], proj[..., COL_O:COL_I],
                          proj[..., COL_I:COL_F], proj[..., COL_F:COL_P], conv_w[l], conv_b[l],
                          w_q[l], w_k[l], b_igate[l], b_fgate[l], mlstm_norm_g[l], mlstm_skip[l])
        y_p = pool_group(proj[..., COL_P:IN_COLS], w_pool[l], b_pool[l], pool_scale[l])
        mix = jnp.concatenate([y_m, y_p], axis=-1) @ w_out[l]
        x = x + gate_a * mix
        h = rms_norm(x, norm2_g[l]) * (1.0 + scale_f) + shift_f
        x = x + gate_f * hierarchical_moe(h, w_router_group[l], b_router_group[l], w_router_expert[l],
                                          b_router_expert[l], w_expert_gate[l], w_expert_up[l],
                                          w_expert_down[l])
    return rms_norm(x, final_g)
```

```python
import functools

import jax
import jax.numpy as jnp
from jax import lax
from jax.experimental import pallas as pl
from jax.experimental.pallas import tpu as pltpu

F32 = jnp.float32
BF16 = jnp.bfloat16
I32 = jnp.int32

EPS = 1e-6
N_HEADS = 4
HEAD_DIM = 128
CONV_WIDTH = 4
POOL_WINDOWS = (2, 4, 8, 16)
N_GROUPS = 4
EXPERTS_PER_GROUP = 8
N_EXPERTS = N_GROUPS * EXPERTS_PER_GROUP
TOP_K = 2

LANES = 128
SUBLANES = 8
CHUNK = 128
SEQ_TILE = 512
ROUTE_TILE = 512
DISPATCH_TILE = 512
COMBINE_TILE = 256
EXPERT_BLOCK = 256
LOGIT_ROWS = 48
UHIST = 8
PHIST = 16
VMEM_LIMIT = 56 * 1024 * 1024


def _sigmoid(x):
    return 1.0 / (1.0 + jnp.exp(-x))


def _ada_kernel(c_ref, w_ref, b_ref, o_ref):
    c = c_ref[...]
    s = c * _sigmoid(c)
    o_ref[...] = jnp.dot(s, w_ref[...], preferred_element_type=F32,
                         precision=lax.Precision.HIGHEST) + b_ref[...]


def _ada(c, ada_w, ada_b):
    bsz, dm = c.shape
    n = ada_w.shape[1]
    tn = 1024
    return pl.pallas_call(
        _ada_kernel,
        grid=(n // tn,),
        in_specs=[pl.BlockSpec((bsz, dm), lambda j: (0, 0)),
                  pl.BlockSpec((dm, tn), lambda j: (0, j)),
                  pl.BlockSpec((1, tn), lambda j: (0, j))],
        out_specs=pl.BlockSpec((bsz, tn), lambda j: (0, j)),
        out_shape=jax.ShapeDtypeStruct((bsz, n), F32),
        compiler_params=pltpu.CompilerParams(dimension_semantics=("arbitrary",),
                                             vmem_limit_bytes=VMEM_LIMIT),
        name="ada",
    )(c, ada_w, ada_b.reshape(1, n))


def _split3(x):
    hi = x.astype(BF16)
    r1 = x - hi.astype(F32)
    mid = r1.astype(BF16)
    lo = (r1 - mid.astype(F32)).astype(BF16)
    return hi, mid, lo


def _mixer_kernel(x_ref, mod_ref, g1_ref, win_ref, gbias_ref, convw_ref, convb_ref, wqk_ref,
                  ng_ref, skip_ref, wpool_ref, bpool_ref, pscale_ref, wout_ref, g2_ref,
                  wrt_ref, rbias_ref,
                  x1_ref, h2_ref, lgt_ref,
                  proj_s, uc_s, qk_s, gate_s, mix_s, uhist_s, phist_s, ctv_s, ctn_s, mprev_s,
                  *, ts, dm, dml):
    s_idx = pl.program_id(1)
    n_chunks = ts // CHUNK

    @pl.when(s_idx == 0)
    def _():
        uhist_s[...] = jnp.zeros_like(uhist_s)
        phist_s[...] = jnp.zeros_like(phist_s)
        ctv_s[...] = jnp.zeros_like(ctv_s)
        ctn_s[...] = jnp.zeros_like(ctn_s)
        mprev_s[...] = jnp.zeros_like(mprev_s)

    x = x_ref[0]
    mod = mod_ref[0]
    r = lax.rsqrt(jnp.mean(x * x, axis=-1, keepdims=True) + EPS)
    h = (x * r) * (g1_ref[...] * (1.0 + mod[1:2])) + mod[0:1]
    proj_s[...] = jnp.dot(h.astype(BF16), win_ref[...], preferred_element_type=F32)

    u = proj_s[:, 0:dml]
    uext = jnp.concatenate([uhist_s[...], u], axis=0)
    acc = u * convw_ref[CONV_WIDTH - 1:CONV_WIDTH, :]
    for j in range(CONV_WIDTH - 1):
        sh = CONV_WIDTH - 1 - j
        acc = acc + pltpu.roll(uext, sh, axis=0)[UHIST:, :] * convw_ref[j:j + 1, :]
    uhist_s[...] = u[ts - UHIST:, :]
    conv = acc + convb_ref[...]
    uc = conv * _sigmoid(conv)
    uc_s[...] = uc

    for hd in range(N_HEADS):
        qk_s[:, 2 * HEAD_DIM * hd:2 * HEAD_DIM * (hd + 1)] = jnp.dot(
            uc[:, HEAD_DIM * hd:HEAD_DIM * (hd + 1)].astype(BF16), wqk_ref[hd],
            preferred_element_type=F32)

    g = proj_s[:, 4 * dml:4 * dml + LANES] + gbias_ref[...]
    lane = lax.broadcasted_iota(I32, (ts, LANES), 1)
    logf = -(jnp.maximum(-g, 0.0) + jnp.log1p(jnp.exp(-jnp.abs(g))))
    gate_s[...] = jnp.where(lane < N_HEADS, g, logf)

    row_i = lax.broadcasted_iota(I32, (CHUNK, CHUNK), 0)
    col_i = lax.broadcasted_iota(I32, (CHUNK, CHUNK), 1)
    causal = row_i >= col_i
    tril = jnp.where(causal, 1.0, 0.0).astype(BF16)
    lane_c = lax.broadcasted_iota(I32, (CHUNK, LANES), 1)
    ones_blk = jnp.ones((CHUNK, HEAD_DIM), BF16)
    q_scale = HEAD_DIM ** -0.5

    def chunk_body(c, carry):
        r0 = pl.multiple_of(c * CHUNK, CHUNK)
        gc = gate_s[pl.ds(r0, CHUNK), :]
        hi, mid, lo = _split3(gc)
        cs = jnp.dot(tril, jnp.concatenate([hi, mid, lo], axis=1), preferred_element_type=F32)
        bcum = cs[:, 0:LANES] + cs[:, LANES:2 * LANES] + cs[:, 2 * LANES:3 * LANES]
        comb = jnp.where(lane_c < N_HEADS, gc, bcum)
        comb_t = comb.T
        for hd in range(N_HEADS):
            b_bc = jnp.broadcast_to(bcum[:, N_HEADS + hd:N_HEADS + hd + 1], (CHUNK, CHUNK))
            i_row = comb_t[hd:hd + 1, :]
            b_row = comb_t[N_HEADS + hd:N_HEADS + hd + 1, :]
            dlog = jnp.where(causal, (b_bc - b_row) + i_row, -jnp.inf)
            rmax = jnp.max(dlog, axis=-1, keepdims=True)
            inter = b_bc + mprev_s[hd]
            m_t = jnp.maximum(inter, rmax)
            wm = jnp.exp(dlog - m_t)
            a_inter = jnp.exp(inter - m_t)
            e_negm = jnp.exp(-m_t)

            c0 = 2 * HEAD_DIM * hd
            q_c = qk_s[pl.ds(r0, CHUNK), c0:c0 + HEAD_DIM] * q_scale
            k_c = qk_s[pl.ds(r0, CHUNK), c0 + HEAD_DIM:c0 + 2 * HEAD_DIM]
            v_c = proj_s[pl.ds(r0, CHUNK), dml + HEAD_DIM * hd:dml + HEAD_DIM * (hd + 1)]
            p_mat = lax.dot_general(q_c.astype(BF16), k_c.astype(BF16),
                                    (((1,), (1,)), ((), ())), preferred_element_type=F32)
            s_mat = (p_mat * wm).astype(BF16)
            qa = (q_c * a_inter).astype(BF16)
            v_aug = jnp.concatenate([v_c.astype(BF16), ones_blk], axis=1)
            ct_aug = jnp.concatenate([ctv_s[hd], ctn_s[hd]], axis=1).astype(BF16)
            numden = jnp.dot(jnp.concatenate([s_mat, qa], axis=1),
                             jnp.concatenate([v_aug, ct_aug], axis=0),
                             preferred_element_type=F32)
            num = numden[:, 0:HEAD_DIM]
            den = numden[:, HEAD_DIM:2 * HEAD_DIM]
            hh = num / jnp.maximum(jnp.abs(den), e_negm)
            ms = jnp.mean(hh * hh, axis=-1, keepdims=True)
            hn = hh * lax.rsqrt(ms + EPS) * ng_ref[:, HEAD_DIM * hd:HEAD_DIM * (hd + 1)]
            o_c = proj_s[pl.ds(r0, CHUNK), 2 * dml + HEAD_DIM * hd:2 * dml + HEAD_DIM * (hd + 1)]
            uc_c = uc_s[pl.ds(r0, CHUNK), HEAD_DIM * hd:HEAD_DIM * (hd + 1)]
            out_c = _sigmoid(o_c) * (hn + skip_ref[:, HEAD_DIM * hd:HEAD_DIM * (hd + 1)] * uc_c)
            mix_s[pl.ds(r0, CHUNK), HEAD_DIM * hd:HEAD_DIM * (hd + 1)] = out_c.astype(BF16)

            w_row = wm[CHUNK - 1:CHUNK, :]
            ktw = (k_c.T * w_row).astype(BF16)
            upd = jnp.dot(ktw, v_aug, preferred_element_type=F32)
            a_prev = a_inter[CHUNK - 1:CHUNK, :]
            ctv_s[hd] = a_prev * ctv_s[hd] + upd[:, 0:HEAD_DIM]
            ctn_s[hd] = a_prev * ctn_s[hd] + upd[:, HEAD_DIM:2 * HEAD_DIM]
            mprev_s[hd] = jnp.broadcast_to(m_t[CHUNK - 1:CHUNK, :], (CHUNK, LANES))
        return carry

    lax.fori_loop(0, n_chunks, chunk_body, 0)

    p = proj_s[:, 3 * dml:4 * dml]
    pext = jnp.concatenate([phist_s[...], p], axis=0)
    phist_s[...] = p[ts - PHIST:, :]
    t_glob = lax.broadcasted_iota(I32, (ts, LANES), 0) + s_idx * ts + 1
    gd = LANES
    for gi, win in enumerate(POOL_WINDOWS):
        pe = pext[:, gd * gi:gd * (gi + 1)]
        acc_p = pe
        step = 1
        while step < win:
            acc_p = acc_p + pltpu.roll(acc_p, step, axis=0)
            step *= 2
        cnt = jnp.minimum(t_glob, win).astype(F32)
        pooled = acc_p[PHIST:, :] / cnt - p[:, gd * gi:gd * (gi + 1)]
        yp = jnp.dot(pooled.astype(BF16), wpool_ref[gi], preferred_element_type=F32)
        yp = (yp + bpool_ref[:, gd * gi:gd * (gi + 1)]) * pscale_ref[:, gd * gi:gd * (gi + 1)]
        mix_s[:, dml + gd * gi:dml + gd * (gi + 1)] = yp.astype(BF16)

    mix = jnp.dot(mix_s[...], wout_ref[...], preferred_element_type=F32)
    x1 = x + mod[2:3] * mix
    x1_ref[0] = x1
    r2 = lax.rsqrt(jnp.mean(x1 * x1, axis=-1, keepdims=True) + EPS)
    h2 = (x1 * r2) * (g2_ref[...] * (1.0 + mod[4:5])) + mod[3:4]
    h2_ref[0] = h2
    lgt_ref[0] = lax.dot_general(wrt_ref[...], h2.astype(BF16), (((1,), (1,)), ((), ())),
                                 preferred_element_type=F32) + rbias_ref[...]


def _mixer(x, mod, g1, w_in_r, gbias, conv_w, conv_b, wqk, ng, skip, wpool, bpool, pscale,
           w_out, g2, wrt, rbias):
    bsz, seq, dm = x.shape
    dml = conv_w.shape[1]
    ts = min(SEQ_TILE, seq)
    ncols = w_in_r.shape[1]
    full = lambda a: pl.BlockSpec(a.shape, lambda b, s: (0,) * a.ndim)
    kern = functools.partial(_mixer_kernel, ts=ts, dm=dm, dml=dml)
    return pl.pallas_call(
        kern,
        grid=(bsz, seq // ts),
        in_specs=[pl.BlockSpec((1, ts, dm), lambda b, s: (b, s, 0)),
                  pl.BlockSpec((1, 6, dm), lambda b, s: (b, 0, 0)),
                  full(g1), full(w_in_r), full(gbias), full(conv_w), full(conv_b), full(wqk),
                  full(ng), full(skip), full(wpool), full(bpool), full(pscale), full(w_out),
                  full(g2), full(wrt), full(rbias)],
        out_specs=[pl.BlockSpec((1, ts, dm), lambda b, s: (b, s, 0)),
                   pl.BlockSpec((1, ts, dm), lambda b, s: (b, s, 0)),
                   pl.BlockSpec((1, LOGIT_ROWS, ts), lambda b, s: (b, 0, s))],
        out_shape=[jax.ShapeDtypeStruct((bsz, seq, dm), F32),
                   jax.ShapeDtypeStruct((bsz, seq, dm), F32),
                   jax.ShapeDtypeStruct((bsz, LOGIT_ROWS, seq), F32)],
        scratch_shapes=[pltpu.VMEM((ts, ncols), F32),
                        pltpu.VMEM((ts, dml), F32),
                        pltpu.VMEM((ts, 2 * dml), F32),
                        pltpu.VMEM((ts, LANES), F32),
                        pltpu.VMEM((ts, dm), BF16),
                        pltpu.VMEM((UHIST, dml), F32),
                        pltpu.VMEM((PHIST, dm - dml), F32),
                        pltpu.VMEM((N_HEADS, HEAD_DIM, HEAD_DIM), F32),
                        pltpu.VMEM((N_HEADS, HEAD_DIM, HEAD_DIM), F32),
                        pltpu.VMEM((N_HEADS, CHUNK, LANES), F32)],
        compiler_params=pltpu.CompilerParams(dimension_semantics=("arbitrary", "arbitrary"),
                                             vmem_limit_bytes=VMEM_LIMIT),
        name="mixer",
    )(x, mod, g1, w_in_r, gbias, conv_w, conv_b, wqk, ng, skip, wpool, bpool, pscale, w_out, g2,
      wrt, rbias)


def _route_kernel(lgt_ref, idx_ref, gcol_ref, cnt_ref, carry_s, *, tr):
    first = (pl.program_id(0) == 0) & (pl.program_id(1) == 0)

    @pl.when(first)
    def _():
        carry_s[...] = jnp.zeros_like(carry_s)

    lg = lgt_ref[0]
    best = lg[0:1]
    gidx = jnp.zeros((1, tr), I32)
    for j in range(1, N_GROUPS):
        cand = lg[j:j + 1]
        better = cand > best
        gidx = jnp.where(better, j, gidx)
        best = jnp.where(better, cand, best)
    sumexp = jnp.zeros((1, tr), F32)
    for j in range(N_GROUPS):
        sumexp = sumexp + jnp.exp(lg[j:j + 1] - best)
    g_gate = 1.0 / sumexp

    sel = lg[SUBLANES:2 * SUBLANES]
    for j in range(1, N_GROUPS):
        sel = jnp.where(gidx == j, lg[SUBLANES * (j + 1):SUBLANES * (j + 2)], sel)
    sub = lax.broadcasted_iota(I32, (EXPERTS_PER_GROUP, tr), 0)
    v1 = jnp.max(sel, axis=0, keepdims=True)
    i1 = jnp.min(jnp.where(sel == v1, sub, EXPERTS_PER_GROUP), axis=0, keepdims=True)
    sel2 = jnp.where(sub == i1, -jnp.inf, sel)
    v2 = jnp.max(sel2, axis=0, keepdims=True)
    i2 = jnp.min(jnp.where(sel2 == v2, sub, EXPERTS_PER_GROUP), axis=0, keepdims=True)
    e2 = jnp.exp(v2 - v1)
    den = 1.0 + e2
    gate0 = (1.0 / den) * g_gate
    gate1 = (e2 / den) * g_gate
    ex0 = gidx * EXPERTS_PER_GROUP + i1
    ex1 = gidx * EXPERTS_PER_GROUP + i2

    erow = lax.broadcasted_iota(I32, (N_EXPERTS, tr), 0)
    oh0 = erow == ex0
    oh1 = erow == ex1
    oh = jnp.where(oh0 | oh1, 1.0, 0.0).astype(BF16)
    tr_r = lax.broadcasted_iota(I32, (tr, tr), 0)
    tr_c = lax.broadcasted_iota(I32, (tr, tr), 1)
    upper = jnp.where(tr_r < tr_c, 1.0, 0.0).astype(BF16)
    carry = carry_s[...]
    before = jnp.dot(oh, upper, preferred_element_type=F32)
    before = before + jnp.concatenate([carry] * (tr // LANES), axis=1)
    rank0 = jnp.sum(jnp.where(oh0, before, 0.0), axis=0, keepdims=True)
    rank1 = jnp.sum(jnp.where(oh1, before, 0.0), axis=0, keepdims=True)
    carry = carry + jnp.dot(oh, jnp.ones((tr, LANES), BF16), preferred_element_type=F32)
    carry_s[...] = carry
    cnt_ref[...] = carry

    zrow = jnp.zeros((SUBLANES - 4, tr), I32)
    idx_ref[0] = jnp.concatenate([ex0, ex1, rank0.astype(I32), rank1.astype(I32), zrow], axis=0)

    for q in range(tr // LANES):
        tile = jnp.concatenate([gate0[:, q * LANES:(q + 1) * LANES],
                                gate1[:, q * LANES:(q + 1) * LANES],
                                jnp.zeros((LANES - 2, LANES), F32)], axis=0)
        gcol_ref[q * LANES:(q + 1) * LANES, :] = tile.T


def _route(lgt):
    bsz, _, seq = lgt.shape
    tr = min(ROUTE_TILE, seq)
    nst = seq // tr
    return pl.pallas_call(
        functools.partial(_route_kernel, tr=tr),
        grid=(bsz, nst),
        in_specs=[pl.BlockSpec((1, LOGIT_ROWS, tr), lambda b, s: (b, 0, s))],
        out_specs=[pl.BlockSpec((1, SUBLANES, tr), lambda b, s: (b, 0, s)),
                   pl.BlockSpec((tr, LANES), lambda b, s: (b * nst + s, 0)),
                   pl.BlockSpec((N_EXPERTS, LANES), lambda b, s: (0, 0))],
        out_shape=[jax.ShapeDtypeStruct((bsz, SUBLANES, seq), I32),
                   jax.ShapeDtypeStruct((bsz * seq, LANES), F32),
                   jax.ShapeDtypeStruct((N_EXPERTS, LANES), F32)],
        scratch_shapes=[pltpu.VMEM((N_EXPERTS, LANES), F32)],
        compiler_params=pltpu.CompilerParams(dimension_semantics=("arbitrary", "arbitrary"),
                                             vmem_limit_bytes=VMEM_LIMIT),
        name="route",
    )(lgt)


def _dispatch_kernel(starts_ref, idx_ref, h_ref, xs_ref, sem, *, td, nst):
    base = (pl.program_id(0) * nst + pl.program_id(1)) * td

    def issue(j, carry):
        d0 = starts_ref[idx_ref[0, 0, j]] + idx_ref[0, 2, j]
        d1 = starts_ref[idx_ref[0, 1, j]] + idx_ref[0, 3, j]
        src = h_ref.at[pl.ds(base + j, 1)]
        pltpu.make_async_copy(src, xs_ref.at[pl.ds(d0, 1)], sem).start()
        pltpu.make_async_copy(src, xs_ref.at[pl.ds(d1, 1)], sem).start()
        return carry

    lax.fori_loop(0, td, issue, 0)

    def drain(j, carry):
        pltpu.make_async_copy(h_ref.at[pl.ds(0, 1)], xs_ref.at[pl.ds(0, 1)], sem).wait()
        pltpu.make_async_copy(h_ref.at[pl.ds(0, 1)], xs_ref.at[pl.ds(0, 1)], sem).wait()
        return carry

    lax.fori_loop(0, td, drain, 0)


def _dispatch(starts, idx, h2):
    n_tok, dm = h2.shape
    bsz, _, seq = idx.shape
    td = min(DISPATCH_TILE, seq)
    nst = seq // td
    return pl.pallas_call(
        functools.partial(_dispatch_kernel, td=td, nst=nst),
        grid_spec=pltpu.PrefetchScalarGridSpec(
            num_scalar_prefetch=1,
            grid=(bsz, nst),
            in_specs=[pl.BlockSpec((1, SUBLANES, td), lambda b, s, st: (b, 0, s),
                                   memory_space=pltpu.SMEM),
                      pl.BlockSpec(memory_space=pl.ANY)],
            out_specs=pl.BlockSpec(memory_space=pl.ANY),
            scratch_shapes=[pltpu.SemaphoreType.DMA(())]),
        out_shape=jax.ShapeDtypeStruct((n_tok * TOP_K, dm), F32),
        compiler_params=pltpu.CompilerParams(dimension_semantics=("arbitrary", "arbitrary"),
                                             has_side_effects=True),
        name="dispatch",
    )(starts, idx, h2)


def _experts_kernel(blk_ref, exp_ref, lo_ref, hi_ref, first_ref, newexp_ref,
                    xs_ref, wg_ref, wu_ref, wd_ref, ys_ref, wgu_s, wd_s, *, de):
    i = pl.program_id(0)

    @pl.when(newexp_ref[i] == 1)
    def _():
        wgu_s[:, 0:de] = wg_ref[0].astype(BF16)
        wgu_s[:, de:2 * de] = wu_ref[0].astype(BF16)
        wd_s[...] = wd_ref[0].astype(BF16)

    lo = lo_ref[i]
    hi = hi_ref[i]

    @pl.when(hi > lo)
    def _():
        xb = xs_ref[...].astype(BF16)
        ab = jnp.dot(xb, wgu_s[...], preferred_element_type=F32)
        a = ab[:, 0:de]
        b = ab[:, de:2 * de]
        hmid = (a * _sigmoid(a)) * b
        y = jnp.dot(hmid.astype(BF16), wd_s[...], preferred_element_type=F32)
        rows = lax.broadcasted_iota(I32, y.shape, 0)
        mask = (rows >= lo) & (rows < hi)

        @pl.when(first_ref[i] == 1)
        def _():
            ys_ref[...] = jnp.where(mask, y, 0.0)

        @pl.when(first_ref[i] == 0)
        def _():
            ys_ref[...] = jnp.where(mask, y, ys_ref[...])


def _experts(items, xs, w_gate, w_up, w_down):
    n_slots, dm = xs.shape
    de = w_gate.shape[2]
    n_items = items[0].shape[0]
    blk = EXPERT_BLOCK
    return pl.pallas_call(
        functools.partial(_experts_kernel, de=de),
        grid_spec=pltpu.PrefetchScalarGridSpec(
            num_scalar_prefetch=6,
            grid=(n_items,),
            in_specs=[pl.BlockSpec((blk, dm), lambda i, b, e, lo, hi, f, n: (b[i], 0)),
                      pl.BlockSpec((1, dm, de), lambda i, b, e, lo, hi, f, n: (e[i], 0, 0)),
                      pl.BlockSpec((1, dm, de), lambda i, b, e, lo, hi, f, n: (e[i], 0, 0)),
                      pl.BlockSpec((1, de, dm), lambda i, b, e, lo, hi, f, n: (e[i], 0, 0))],
            out_specs=pl.BlockSpec((blk, dm), lambda i, b, e, lo, hi, f, n: (b[i], 0)),
            scratch_shapes=[pltpu.VMEM((dm, 2 * de), BF16), pltpu.VMEM((de, dm), BF16)]),
        out_shape=jax.ShapeDtypeStruct((n_slots, dm), F32),
        compiler_params=pltpu.CompilerParams(dimension_semantics=("arbitrary",),
                                             vmem_limit_bytes=VMEM_LIMIT),
        name="experts",
    )(*items, xs, w_gate, w_up, w_down)


def _expert_items(counts, n_slots):
    blk = EXPERT_BLOCK
    n_blocks = n_slots // blk
    n_items = n_blocks + N_EXPERTS - 1
    starts = jnp.cumsum(counts) - counts
    ends = starts + counts
    first_blk = starts // blk
    last_blk = jnp.maximum(ends - 1, 0) // blk
    per_e = jnp.where(counts > 0, last_blk - first_blk + 1, 0)
    item_end = jnp.cumsum(per_e)
    item_start = item_end - per_e
    total = item_end[-1]
    it = jnp.arange(n_items, dtype=I32)
    valid = it < total
    e_of = jnp.minimum(jnp.searchsorted(item_end, it, side="right"), N_EXPERTS - 1).astype(I32)
    e_last = e_of[jnp.maximum(total - 1, 0)]
    e_of = jnp.where(valid, e_of, e_last)
    b_of = jnp.where(valid, first_blk[e_of] + (it - item_start[e_of]), n_blocks - 1).astype(I32)
    lo = jnp.where(valid, jnp.clip(starts[e_of] - b_of * blk, 0, blk), 0).astype(I32)
    hi = jnp.where(valid, jnp.clip(ends[e_of] - b_of * blk, 0, blk), 0).astype(I32)
    prev_b = jnp.concatenate([jnp.full((1,), -1, I32), b_of[:-1]])
    prev_e = jnp.concatenate([jnp.full((1,), -1, I32), e_of[:-1]])
    first = (valid & (b_of != prev_b)).astype(I32)
    newexp = (valid & (e_of != prev_e)).astype(I32)
    return starts.astype(I32), (b_of, e_of, lo, hi, first, newexp)


def _combine_kernel(starts_ref, idx_ref, gcol_ref, x1_ref, mod_ref, fg_ref, ys_ref, o_ref,
                    buf, sem, *, tc):
    def issue(j, carry):
        d0 = starts_ref[idx_ref[0, 0, j]] + idx_ref[0, 2, j]
        d1 = starts_ref[idx_ref[0, 1, j]] + idx_ref[0, 3, j]
        pltpu.make_async_copy(ys_ref.at[pl.ds(d0, 1)], buf.at[0, pl.ds(j, 1)], sem).start()
        pltpu.make_async_copy(ys_ref.at[pl.ds(d1, 1)], buf.at[1, pl.ds(j, 1)], sem).start()
        return carry

    lax.fori_loop(0, tc, issue, 0)

    def drain(j, carry):
        pltpu.make_async_copy(ys_ref.at[pl.ds(0, 1)], buf.at[0, pl.ds(0, 1)], sem).wait()
        pltpu.make_async_copy(ys_ref.at[pl.ds(0, 1)], buf.at[1, pl.ds(0, 1)], sem).wait()
        return carry

    lax.fori_loop(0, tc, drain, 0)

    gc = gcol_ref[...]
    y = gc[:, 0:1] * buf[0] + gc[:, 1:2] * buf[1]
    x2 = x1_ref[...] + mod_ref[0][5:6] * y
    r = lax.rsqrt(jnp.mean(x2 * x2, axis=-1, keepdims=True) + EPS)
    o_ref[...] = (x2 * r) * fg_ref[...]


def _combine(starts, idx, gcol, x1, mod, final_g, ys):
    n_tok, dm = x1.shape
    bsz, _, seq = idx.shape
    tc = min(COMBINE_TILE, seq)
    nst = seq // tc
    return pl.pallas_call(
        functools.partial(_combine_kernel, tc=tc),
        grid_spec=pltpu.PrefetchScalarGridSpec(
            num_scalar_prefetch=1,
            grid=(bsz, nst),
            in_specs=[pl.BlockSpec((1, SUBLANES, tc), lambda b, s, st: (b, 0, s),
                                   memory_space=pltpu.SMEM),
                      pl.BlockSpec((tc, LANES), lambda b, s, st: (b * nst + s, 0)),
                      pl.BlockSpec((tc, dm), lambda b, s, st: (b * nst + s, 0)),
                      pl.BlockSpec((1, 6, dm), lambda b, s, st: (b, 0, 0)),
                      pl.BlockSpec((1, dm), lambda b, s, st: (0, 0)),
                      pl.BlockSpec(memory_space=pl.ANY)],
            out_specs=pl.BlockSpec((tc, dm), lambda b, s, st: (b * nst + s, 0)),
            scratch_shapes=[pltpu.VMEM((TOP_K, tc, dm), F32), pltpu.SemaphoreType.DMA(())]),
        out_shape=jax.ShapeDtypeStruct((n_tok, dm), F32),
        compiler_params=pltpu.CompilerParams(dimension_semantics=("arbitrary", "arbitrary"),
                                             vmem_limit_bytes=VMEM_LIMIT),
        name="combine",
    )(starts, idx, gcol, x1, mod, final_g.reshape(1, dm), ys)


def _layer(x, c, ada_w, ada_b, norm1_g, w_in, conv_w, conv_b, w_q, w_k, b_igate, b_fgate,
           mlstm_norm_g, mlstm_skip, w_pool, b_pool, pool_scale, w_out, norm2_g,
           w_rg, b_rg, w_re, b_re, w_eg, w_eu, w_ed, out_g):
    bsz, seq, dm = x.shape
    dml = conv_w.shape[1]
    n_tok = bsz * seq
    ts = min(SEQ_TILE, seq)

    mod = _ada(c, ada_w, ada_b).reshape(bsz, 6, dm)

    col_i = 3 * dml
    col_p = col_i + 2 * N_HEADS
    w_gate_cols = jnp.pad(w_in[:, col_i:col_p], ((0, 0), (0, LANES - 2 * N_HEADS)))
    w_in_r = jnp.concatenate([w_in[:, :col_i], w_in[:, col_p:], w_gate_cols], axis=1).astype(BF16)
    gbias = jnp.pad(jnp.concatenate([b_igate, b_fgate]), (0, LANES - 2 * N_HEADS)).reshape(1, LANES)
    wqk = jnp.concatenate([w_q, w_k], axis=-1).astype(BF16)
    wrt = jnp.zeros((LOGIT_ROWS, dm), F32)
    wrt = wrt.at[0:N_GROUPS].set(w_rg.T).at[SUBLANES:SUBLANES + N_EXPERTS].set(w_re.T).astype(BF16)
    rb = jnp.zeros((LOGIT_ROWS,), F32).at[0:N_GROUPS].set(b_rg).at[SUBLANES:SUBLANES + N_EXPERTS].set(b_re)
    rbias = jnp.broadcast_to(rb[:, None], (LOGIT_ROWS, ts))

    x1, h2, lgt = _mixer(x, mod, norm1_g.reshape(1, dm), w_in_r, gbias, conv_w,
                         conv_b.reshape(1, dml), wqk, mlstm_norm_g.reshape(1, dml),
                         mlstm_skip.reshape(1, dml), w_pool.astype(BF16),
                         b_pool.reshape(1, dm - dml), pool_scale.reshape(1, dm - dml),
                         w_out.astype(BF16), norm2_g.reshape(1, dm), wrt, rbias)

    idx, gcol, cnt = _route(lgt)
    counts = cnt[:, 0].astype(I32)
    starts, items = _expert_items(counts, n_tok * TOP_K)

    xs = _dispatch(starts, idx, h2.reshape(n_tok, dm))
    ys = _experts(items, xs, w_eg, w_eu, w_ed)
    out = _combine(starts, idx, gcol, x1.reshape(n_tok, dm), mod, out_g, ys)
    return out.reshape(bsz, seq, dm)


def kernel(x, c, ada_w, ada_b, norm1_g, w_in, conv_w, conv_b, w_q, w_k, b_igate, b_fgate, mlstm_norm_g, mlstm_skip, w_pool, b_pool, pool_scale, w_out, norm2_g, w_router_group, b_router_group, w_router_expert, b_router_expert, w_expert_gate, w_expert_up, w_expert_down, final_g):
    depth = ada_w.shape[0]
    assert depth == 1, "the final norm is fused into the last layer's combine kernel"
    l = 0
    return _layer(x, c, ada_w[l], ada_b[l], norm1_g[l], w_in[l], conv_w[l], conv_b[l], w_q[l],
                  w_k[l], b_igate[l], b_fgate[l], mlstm_norm_g[l], mlstm_skip[l], w_pool[l],
                  b_pool[l], pool_scale[l], w_out[l], norm2_g[l], w_router_group[l],
                  b_router_group[l], w_router_expert[l], b_router_expert[l],
                  w_expert_gate[l], w_expert_up[l], w_expert_down[l], final_g)
```

```python
import functools

import jax
import jax.numpy as jnp
from jax import lax
from jax.experimental import pallas as pl
from jax.experimental.pallas import tpu as pltpu
from jax.experimental.pallas import tpu_sc as plsc

F32 = jnp.float32
BF16 = jnp.bfloat16
I32 = jnp.int32
U32 = jnp.uint32

EPS = 1e-6
N_HEADS = 4
HEAD_DIM = 128
CONV_WIDTH = 4
POOL_WINDOWS = (2, 4, 8, 16)
N_GROUPS = 4
EXPERTS_PER_GROUP = 8
N_EXPERTS = N_GROUPS * EXPERTS_PER_GROUP
TOP_K = 2

LANES = 128
SUBLANES = 8
CHUNK = 128
SEQ_TILE = 512
ROUTE_TILE = 512
SC_CHUNK = 64
COMBINE_TILE = 512
EXPERT_BLOCK = 256
LOGIT_ROWS = 48
UHIST = 8
PHIST = 16
VMEM_LIMIT = 56 * 1024 * 1024


def _sigmoid(x):
    return 1.0 / (1.0 + jnp.exp(-x))


def _pack_bf16_pairs(x):
    w = x.shape[1] // 2
    hi = lax.bitcast_convert_type(x[:, :w].astype(BF16).astype(F32), U32)
    lo = lax.bitcast_convert_type(x[:, w:].astype(BF16).astype(F32), U32)
    return lax.bitcast_convert_type((hi & jnp.uint32(0xFFFF0000)) | (lo >> 16), I32)


def _unpack_bf16_pairs(words):
    u = lax.bitcast_convert_type(words, U32)
    hi = lax.bitcast_convert_type(u & jnp.uint32(0xFFFF0000), F32)
    lo = lax.bitcast_convert_type(u << 16, F32)
    return jnp.concatenate([hi, lo], axis=1)


def _ada_kernel(c_ref, w_ref, b_ref, o_ref):
    c = c_ref[...]
    s = c * _sigmoid(c)
    o_ref[...] = jnp.dot(s, w_ref[...], preferred_element_type=F32,
                         precision=lax.Precision.HIGHEST) + b_ref[...]


def _ada(c, ada_w, ada_b):
    bsz, dm = c.shape
    n = ada_w.shape[1]
    tn = 1024
    return pl.pallas_call(
        _ada_kernel,
        grid=(n // tn,),
        in_specs=[pl.BlockSpec((bsz, dm), lambda j: (0, 0)),
                  pl.BlockSpec((dm, tn), lambda j: (0, j)),
                  pl.BlockSpec((1, tn), lambda j: (0, j))],
        out_specs=pl.BlockSpec((bsz, tn), lambda j: (0, j)),
        out_shape=jax.ShapeDtypeStruct((bsz, n), F32),
        compiler_params=pltpu.CompilerParams(dimension_semantics=("arbitrary",),
                                             vmem_limit_bytes=VMEM_LIMIT),
        name="ada",
    )(c, ada_w, ada_b.reshape(1, n))


def _split3(x):
    hi = x.astype(BF16)
    r1 = x - hi.astype(F32)
    mid = r1.astype(BF16)
    lo = (r1 - mid.astype(F32)).astype(BF16)
    return hi, mid, lo


def _mixer_kernel(x_ref, mod_ref, g1_ref, win_ref, gbias_ref, convw_ref, convb_ref, wqk_ref,
                  ng_ref, skip_ref, wpool_ref, bpool_ref, pscale_ref, wout_ref, g2_ref,
                  wrt_ref, rbias_ref,
                  x1_ref, h2_ref, lgt_ref,
                  proj_s, uc_s, qk_s, gate_s, mix_s, uhist_s, phist_s, ctv_s, ctn_s, mprev_s,
                  *, ts, dm, dml):
    s_idx = pl.program_id(1)
    n_chunks = ts // CHUNK

    @pl.when(s_idx == 0)
    def _():
        uhist_s[...] = jnp.zeros_like(uhist_s)
        phist_s[...] = jnp.zeros_like(phist_s)
        ctv_s[...] = jnp.zeros_like(ctv_s)
        ctn_s[...] = jnp.zeros_like(ctn_s)
        mprev_s[...] = jnp.zeros_like(mprev_s)

    x = x_ref[0]
    mod = mod_ref[0]
    r = lax.rsqrt(jnp.mean(x * x, axis=-1, keepdims=True) + EPS)
    h = (x * r) * (g1_ref[...] * (1.0 + mod[1:2])) + mod[0:1]
    proj_s[...] = jnp.dot(h.astype(BF16), win_ref[...], preferred_element_type=F32)

    u = proj_s[:, 0:dml]
    uext = jnp.concatenate([uhist_s[...], u], axis=0)
    acc = u * convw_ref[CONV_WIDTH - 1:CONV_WIDTH, :]
    for j in range(CONV_WIDTH - 1):
        sh = CONV_WIDTH - 1 - j
        acc = acc + pltpu.roll(uext, sh, axis=0)[UHIST:, :] * convw_ref[j:j + 1, :]
    uhist_s[...] = u[ts - UHIST:, :]
    conv = acc + convb_ref[...]
    uc = conv * _sigmoid(conv)
    uc_s[...] = uc

    for hd in range(N_HEADS):
        qk_s[:, 2 * HEAD_DIM * hd:2 * HEAD_DIM * (hd + 1)] = jnp.dot(
            uc[:, HEAD_DIM * hd:HEAD_DIM * (hd + 1)].astype(BF16), wqk_ref[hd],
            preferred_element_type=F32)

    g = proj_s[:, 4 * dml:4 * dml + LANES] + gbias_ref[...]
    lane = lax.broadcasted_iota(I32, (ts, LANES), 1)
    logf = -(jnp.maximum(-g, 0.0) + jnp.log1p(jnp.exp(-jnp.abs(g))))
    gate_s[...] = jnp.where(lane < N_HEADS, g, logf)

    row_i = lax.broadcasted_iota(I32, (CHUNK, CHUNK), 0)
    col_i = lax.broadcasted_iota(I32, (CHUNK, CHUNK), 1)
    causal = row_i >= col_i
    tril = jnp.where(causal, 1.0, 0.0).astype(BF16)
    lane_c = lax.broadcasted_iota(I32, (CHUNK, LANES), 1)
    ones_blk = jnp.ones((CHUNK, HEAD_DIM), BF16)
    q_scale = HEAD_DIM ** -0.5

    def chunk_body(c, carry):
        r0 = pl.multiple_of(c * CHUNK, CHUNK)
        gc = gate_s[pl.ds(r0, CHUNK), :]
        hi, mid, lo = _split3(gc)
        cs = jnp.dot(tril, jnp.concatenate([hi, mid, lo], axis=1), preferred_element_type=F32)
        bcum = cs[:, 0:LANES] + cs[:, LANES:2 * LANES] + cs[:, 2 * LANES:3 * LANES]
        comb = jnp.where(lane_c < N_HEADS, gc, bcum)
        comb_t = comb.T
        for hd in range(N_HEADS):
            b_bc = jnp.broadcast_to(bcum[:, N_HEADS + hd:N_HEADS + hd + 1], (CHUNK, CHUNK))
            i_row = comb_t[hd:hd + 1, :]
            b_row = comb_t[N_HEADS + hd:N_HEADS + hd + 1, :]
            dlog = jnp.where(causal, (b_bc - b_row) + i_row, -jnp.inf)
            rmax = jnp.max(dlog, axis=-1, keepdims=True)
            inter = b_bc + mprev_s[hd]
            m_t = jnp.maximum(inter, rmax)
            wm = jnp.exp(dlog - m_t)
            a_inter = jnp.exp(inter - m_t)
            e_negm = jnp.exp(-m_t)

            c0 = 2 * HEAD_DIM * hd
            q_c = qk_s[pl.ds(r0, CHUNK), c0:c0 + HEAD_DIM] * q_scale
            k_c = qk_s[pl.ds(r0, CHUNK), c0 + HEAD_DIM:c0 + 2 * HEAD_DIM]
            v_c = proj_s[pl.ds(r0, CHUNK), dml + HEAD_DIM * hd:dml + HEAD_DIM * (hd + 1)]
            p_mat = lax.dot_general(q_c.astype(BF16), k_c.astype(BF16),
                                    (((1,), (1,)), ((), ())), preferred_element_type=F32)
            s_mat = (p_mat * wm).astype(BF16)
            qa = (q_c * a_inter).astype(BF16)
            v_aug = jnp.concatenate([v_c.astype(BF16), ones_blk], axis=1)
            ct_aug = jnp.concatenate([ctv_s[hd], ctn_s[hd]], axis=1).astype(BF16)
            numden = jnp.dot(jnp.concatenate([s_mat, qa], axis=1),
                             jnp.concatenate([v_aug, ct_aug], axis=0),
                             preferred_element_type=F32)
            num = numden[:, 0:HEAD_DIM]
            den = numden[:, HEAD_DIM:2 * HEAD_DIM]
            hh = num / jnp.maximum(jnp.abs(den), e_negm)
            ms = jnp.mean(hh * hh, axis=-1, keepdims=True)
            hn = hh * lax.rsqrt(ms + EPS) * ng_ref[:, HEAD_DIM * hd:HEAD_DIM * (hd + 1)]
            o_c = proj_s[pl.ds(r0, CHUNK), 2 * dml + HEAD_DIM * hd:2 * dml + HEAD_DIM * (hd + 1)]
            uc_c = uc_s[pl.ds(r0, CHUNK), HEAD_DIM * hd:HEAD_DIM * (hd + 1)]
            out_c = _sigmoid(o_c) * (hn + skip_ref[:, HEAD_DIM * hd:HEAD_DIM * (hd + 1)] * uc_c)
            mix_s[pl.ds(r0, CHUNK), HEAD_DIM * hd:HEAD_DIM * (hd + 1)] = out_c.astype(BF16)

            w_row = wm[CHUNK - 1:CHUNK, :]
            ktw = (k_c.T * w_row).astype(BF16)
            upd = jnp.dot(ktw, v_aug, preferred_element_type=F32)
            a_prev = a_inter[CHUNK - 1:CHUNK, :]
            ctv_s[hd] = a_prev * ctv_s[hd] + upd[:, 0:HEAD_DIM]
            ctn_s[hd] = a_prev * ctn_s[hd] + upd[:, HEAD_DIM:2 * HEAD_DIM]
            mprev_s[hd] = jnp.broadcast_to(m_t[CHUNK - 1:CHUNK, :], (CHUNK, LANES))
        return carry

    lax.fori_loop(0, n_chunks, chunk_body, 0)

    p = proj_s[:, 3 * dml:4 * dml]
    pext = jnp.concatenate([phist_s[...], p], axis=0)
    phist_s[...] = p[ts - PHIST:, :]
    t_glob = lax.broadcasted_iota(I32, (ts, LANES), 0) + s_idx * ts + 1
    gd = LANES
    for gi, win in enumerate(POOL_WINDOWS):
        pe = pext[:, gd * gi:gd * (gi + 1)]
        acc_p = pe
        step = 1
        while step < win:
            acc_p = acc_p + pltpu.roll(acc_p, step, axis=0)
            step *= 2
        cnt = jnp.minimum(t_glob, win).astype(F32)
        pooled = acc_p[PHIST:, :] / cnt - p[:, gd * gi:gd * (gi + 1)]
        yp = jnp.dot(pooled.astype(BF16), wpool_ref[gi], preferred_element_type=F32)
        yp = (yp + bpool_ref[:, gd * gi:gd * (gi + 1)]) * pscale_ref[:, gd * gi:gd * (gi + 1)]
        mix_s[:, dml + gd * gi:dml + gd * (gi + 1)] = yp.astype(BF16)

    mix = jnp.dot(mix_s[...], wout_ref[...], preferred_element_type=F32)
    x1 = x + mod[2:3] * mix
    x1_ref[0] = x1
    r2 = lax.rsqrt(jnp.mean(x1 * x1, axis=-1, keepdims=True) + EPS)
    h2 = (x1 * r2) * (g2_ref[...] * (1.0 + mod[4:5])) + mod[3:4]
    h2_ref[0] = _pack_bf16_pairs(h2)
    lgt_ref[0] = lax.dot_general(wrt_ref[...], h2.astype(BF16), (((1,), (1,)), ((), ())),
                                 preferred_element_type=F32) + rbias_ref[...]


def _mixer(x, mod, g1, w_in_r, gbias, conv_w, conv_b, wqk, ng, skip, wpool, bpool, pscale,
           w_out, g2, wrt, rbias):
    bsz, seq, dm = x.shape
    dml = conv_w.shape[1]
    ts = min(SEQ_TILE, seq)
    ncols = w_in_r.shape[1]
    full = lambda a: pl.BlockSpec(a.shape, lambda b, s: (0,) * a.ndim)
    kern = functools.partial(_mixer_kernel, ts=ts, dm=dm, dml=dml)
    return pl.pallas_call(
        kern,
        grid=(bsz, seq // ts),
        in_specs=[pl.BlockSpec((1, ts, dm), lambda b, s: (b, s, 0)),
                  pl.BlockSpec((1, 6, dm), lambda b, s: (b, 0, 0)),
                  full(g1), full(w_in_r), full(gbias), full(conv_w), full(conv_b), full(wqk),
                  full(ng), full(skip), full(wpool), full(bpool), full(pscale), full(w_out),
                  full(g2), full(wrt), full(rbias)],
        out_specs=[pl.BlockSpec((1, ts, dm), lambda b, s: (b, s, 0)),
                   pl.BlockSpec((1, ts, dm // 2), lambda b, s: (b, s, 0)),
                   pl.BlockSpec((1, LOGIT_ROWS, ts), lambda b, s: (b, 0, s))],
        out_shape=[jax.ShapeDtypeStruct((bsz, seq, dm), F32),
                   jax.ShapeDtypeStruct((bsz, seq, dm // 2), I32),
                   jax.ShapeDtypeStruct((bsz, LOGIT_ROWS, seq), F32)],
        scratch_shapes=[pltpu.VMEM((ts, ncols), F32),
                        pltpu.VMEM((ts, dml), F32),
                        pltpu.VMEM((ts, 2 * dml), F32),
                        pltpu.VMEM((ts, LANES), F32),
                        pltpu.VMEM((ts, dm), BF16),
                        pltpu.VMEM((UHIST, dml), F32),
                        pltpu.VMEM((PHIST, dm - dml), F32),
                        pltpu.VMEM((N_HEADS, HEAD_DIM, HEAD_DIM), F32),
                        pltpu.VMEM((N_HEADS, HEAD_DIM, HEAD_DIM), F32),
                        pltpu.VMEM((N_HEADS, CHUNK, LANES), F32)],
        compiler_params=pltpu.CompilerParams(dimension_semantics=("arbitrary", "arbitrary"),
                                             vmem_limit_bytes=VMEM_LIMIT),
        name="mixer",
    )(x, mod, g1, w_in_r, gbias, conv_w, conv_b, wqk, ng, skip, wpool, bpool, pscale, w_out, g2,
      wrt, rbias)


def _route_kernel(lgt_ref, idx_ref, gcol_ref, cnt_ref, carry_s, *, tr):
    first = (pl.program_id(0) == 0) & (pl.program_id(1) == 0)

    @pl.when(first)
    def _():
        carry_s[...] = jnp.zeros_like(carry_s)

    lg = lgt_ref[0]
    best = lg[0:1]
    gidx = jnp.zeros((1, tr), I32)
    for j in range(1, N_GROUPS):
        cand = lg[j:j + 1]
        better = cand > best
        gidx = jnp.where(better, j, gidx)
        best = jnp.where(better, cand, best)
    sumexp = jnp.zeros((1, tr), F32)
    for j in range(N_GROUPS):
        sumexp = sumexp + jnp.exp(lg[j:j + 1] - best)
    g_gate = 1.0 / sumexp

    sel = lg[SUBLANES:2 * SUBLANES]
    for j in range(1, N_GROUPS):
        sel = jnp.where(gidx == j, lg[SUBLANES * (j + 1):SUBLANES * (j + 2)], sel)
    sub = lax.broadcasted_iota(I32, (EXPERTS_PER_GROUP, tr), 0)
    v1 = jnp.max(sel, axis=0, keepdims=True)
    i1 = jnp.min(jnp.where(sel == v1, sub, EXPERTS_PER_GROUP), axis=0, keepdims=True)
    sel2 = jnp.where(sub == i1, -jnp.inf, sel)
    v2 = jnp.max(sel2, axis=0, keepdims=True)
    i2 = jnp.min(jnp.where(sel2 == v2, sub, EXPERTS_PER_GROUP), axis=0, keepdims=True)
    e2 = jnp.exp(v2 - v1)
    den = 1.0 + e2
    gate0 = (1.0 / den) * g_gate
    gate1 = (e2 / den) * g_gate
    ex0 = gidx * EXPERTS_PER_GROUP + i1
    ex1 = gidx * EXPERTS_PER_GROUP + i2

    erow = lax.broadcasted_iota(I32, (N_EXPERTS, tr), 0)
    oh0 = erow == ex0
    oh1 = erow == ex1
    oh = jnp.where(oh0 | oh1, 1.0, 0.0).astype(BF16)
    tr_r = lax.broadcasted_iota(I32, (tr, tr), 0)
    tr_c = lax.broadcasted_iota(I32, (tr, tr), 1)
    upper = jnp.where(tr_r < tr_c, 1.0, 0.0).astype(BF16)
    carry = carry_s[...]
    before = jnp.dot(oh, upper, preferred_element_type=F32)
    before = before + jnp.concatenate([carry] * (tr // LANES), axis=1)
    rank0 = jnp.sum(jnp.where(oh0, before, 0.0), axis=0, keepdims=True)
    rank1 = jnp.sum(jnp.where(oh1, before, 0.0), axis=0, keepdims=True)
    carry = carry + jnp.dot(oh, jnp.ones((tr, LANES), BF16), preferred_element_type=F32)
    carry_s[...] = carry
    cnt_ref[...] = carry

    zrow = jnp.zeros((SUBLANES - 4, tr), I32)
    idx_ref[...] = jnp.concatenate([ex0, ex1, rank0.astype(I32), rank1.astype(I32), zrow], axis=0)

    for q in range(tr // LANES):
        tile = jnp.concatenate([gate0[:, q * LANES:(q + 1) * LANES],
                                gate1[:, q * LANES:(q + 1) * LANES],
                                jnp.zeros((LANES - 2, LANES), F32)], axis=0)
        gcol_ref[q * LANES:(q + 1) * LANES, :] = tile.T


def _route(lgt):
    bsz, _, seq = lgt.shape
    tr = min(ROUTE_TILE, seq)
    nst = seq // tr
    return pl.pallas_call(
        functools.partial(_route_kernel, tr=tr),
        grid=(bsz, nst),
        in_specs=[pl.BlockSpec((1, LOGIT_ROWS, tr), lambda b, s: (b, 0, s))],
        out_specs=[pl.BlockSpec((SUBLANES, tr), lambda b, s: (0, b * nst + s)),
                   pl.BlockSpec((tr, LANES), lambda b, s: (b * nst + s, 0)),
                   pl.BlockSpec((N_EXPERTS, LANES), lambda b, s: (0, 0))],
        out_shape=[jax.ShapeDtypeStruct((SUBLANES, bsz * seq), I32),
                   jax.ShapeDtypeStruct((bsz * seq, LANES), F32),
                   jax.ShapeDtypeStruct((N_EXPERTS, LANES), F32)],
        scratch_shapes=[pltpu.VMEM((N_EXPERTS, LANES), F32)],
        compiler_params=pltpu.CompilerParams(dimension_semantics=("arbitrary", "arbitrary"),
                                             vmem_limit_bytes=VMEM_LIMIT),
        name="route",
    )(lgt)


def _sc_workers():
    info = plsc.get_sparse_core_info()
    return info.num_cores, info.num_cores * info.num_subcores


def _dispatch(h2p, dest0, dest1):
    n_tok, width = h2p.shape
    n_cores, n_workers = _sc_workers()
    per_w = n_tok // n_workers
    ch = min(SC_CHUNK, per_w)
    assert n_tok % n_workers == 0 and per_w % ch == 0 and ch % SUBLANES == 0
    mesh = plsc.VectorSubcoreMesh(core_axis_name="c", subcore_axis_name="s")

    @functools.partial(
        pl.kernel, mesh=mesh,
        out_type=jax.ShapeDtypeStruct((n_tok * TOP_K, width), h2p.dtype),
        scratch_types=[pltpu.VMEM((ch,), I32), pltpu.VMEM((ch,), I32),
                       pltpu.VMEM((ch, width), h2p.dtype)],
        name="dispatch")
    def scatter(h_hbm, d0_hbm, d1_hbm, xs_hbm, i0_v, i1_v, rows_v):
        wid = lax.axis_index("s") * n_cores + lax.axis_index("c")
        base = wid * per_w

        @pl.loop(0, per_w, step=ch)
        def _(off):
            t0 = base + off
            pltpu.sync_copy(d0_hbm.at[pl.ds(t0, ch)], i0_v)
            pltpu.sync_copy(d1_hbm.at[pl.ds(t0, ch)], i1_v)
            pltpu.sync_copy(h_hbm.at[pl.ds(t0, ch)], rows_v)
            pltpu.sync_copy(rows_v, xs_hbm.at[i0_v])
            pltpu.sync_copy(rows_v, xs_hbm.at[i1_v])

    return scatter(h2p, dest0, dest1)


def _collect(ys, dest0, dest1):
    n_tok = dest0.shape[0]
    width = ys.shape[1]
    n_cores, n_workers = _sc_workers()
    per_w = n_tok // n_workers
    ch = min(SC_CHUNK, per_w)
    assert n_tok % n_workers == 0 and per_w % ch == 0 and ch % SUBLANES == 0
    mesh = plsc.VectorSubcoreMesh(core_axis_name="c", subcore_axis_name="s")

    @functools.partial(
        pl.kernel, mesh=mesh,
        out_type=jax.ShapeDtypeStruct((TOP_K, n_tok, width), ys.dtype),
        scratch_types=[pltpu.VMEM((ch,), I32), pltpu.VMEM((ch,), I32),
                       pltpu.VMEM((ch, width), ys.dtype), pltpu.VMEM((ch, width), ys.dtype)],
        name="collect")
    def gather(ys_hbm, d0_hbm, d1_hbm, o_hbm, i0_v, i1_v, r0_v, r1_v):
        wid = lax.axis_index("s") * n_cores + lax.axis_index("c")
        base = wid * per_w

        @pl.loop(0, per_w, step=ch)
        def _(off):
            t0 = base + off
            pltpu.sync_copy(d0_hbm.at[pl.ds(t0, ch)], i0_v)
            pltpu.sync_copy(d1_hbm.at[pl.ds(t0, ch)], i1_v)
            pltpu.sync_copy(ys_hbm.at[i0_v], r0_v)
            pltpu.sync_copy(ys_hbm.at[i1_v], r1_v)
            pltpu.sync_copy(r0_v, o_hbm.at[0, pl.ds(t0, ch)])
            pltpu.sync_copy(r1_v, o_hbm.at[1, pl.ds(t0, ch)])

    return gather(ys, dest0, dest1)


def _experts_kernel(blk_ref, exp_ref, lo_ref, hi_ref, first_ref, newexp_ref,
                    xs_ref, wg_ref, wu_ref, wd_ref, ys_ref, wgu_s, wd_s, *, de):
    i = pl.program_id(0)

    @pl.when(newexp_ref[i] == 1)
    def _():
        wgu_s[:, 0:de] = wg_ref[0].astype(BF16)
        wgu_s[:, de:2 * de] = wu_ref[0].astype(BF16)
        wd_s[...] = wd_ref[0].astype(BF16)

    lo = lo_ref[i]
    hi = hi_ref[i]

    @pl.when(hi > lo)
    def _():
        xb = _unpack_bf16_pairs(xs_ref[...]).astype(BF16)
        ab = jnp.dot(xb, wgu_s[...], preferred_element_type=F32)
        a = ab[:, 0:de]
        b = ab[:, de:2 * de]
        hmid = (a * _sigmoid(a)) * b
        y = _pack_bf16_pairs(jnp.dot(hmid.astype(BF16), wd_s[...], preferred_element_type=F32))
        rows = lax.broadcasted_iota(I32, y.shape, 0)
        mask = (rows >= lo) & (rows < hi)

        @pl.when(first_ref[i] == 1)
        def _():
            ys_ref[...] = jnp.where(mask, y, 0)

        @pl.when(first_ref[i] == 0)
        def _():
            ys_ref[...] = jnp.where(mask, y, ys_ref[...])


def _experts(items, xs, w_gate, w_up, w_down):
    n_slots, width = xs.shape
    _, dm, de = w_gate.shape
    n_items = items[0].shape[0]
    blk = EXPERT_BLOCK
    return pl.pallas_call(
        functools.partial(_experts_kernel, de=de),
        grid_spec=pltpu.PrefetchScalarGridSpec(
            num_scalar_prefetch=6,
            grid=(n_items,),
            in_specs=[pl.BlockSpec((blk, width), lambda i, b, e, lo, hi, f, n: (b[i], 0)),
                      pl.BlockSpec((1, dm, de), lambda i, b, e, lo, hi, f, n: (e[i], 0, 0)),
                      pl.BlockSpec((1, dm, de), lambda i, b, e, lo, hi, f, n: (e[i], 0, 0)),
                      pl.BlockSpec((1, de, dm), lambda i, b, e, lo, hi, f, n: (e[i], 0, 0))],
            out_specs=pl.BlockSpec((blk, width), lambda i, b, e, lo, hi, f, n: (b[i], 0)),
            scratch_shapes=[pltpu.VMEM((dm, 2 * de), BF16), pltpu.VMEM((de, dm), BF16)]),
        out_shape=jax.ShapeDtypeStruct((n_slots, width), I32),
        compiler_params=pltpu.CompilerParams(dimension_semantics=("arbitrary",),
                                             vmem_limit_bytes=VMEM_LIMIT),
        name="experts",
    )(*items, xs, w_gate, w_up, w_down)


def _pick(onehot, table):
    return jnp.sum(jnp.where(onehot, table[None, :], 0), axis=1).astype(I32)


def _expert_items(counts, n_slots):
    blk = EXPERT_BLOCK
    n_blocks = n_slots // blk
    n_items = n_blocks + N_EXPERTS - 1
    starts = jnp.cumsum(counts) - counts
    ends = starts + counts
    first_blk = starts // blk
    last_blk = jnp.maximum(ends - 1, 0) // blk
    per_e = jnp.where(counts > 0, last_blk - first_blk + 1, 0)
    item_end = jnp.cumsum(per_e)
    item_start = item_end - per_e
    total = item_end[-1]
    it = jnp.arange(n_items, dtype=I32)
    eids = jnp.arange(N_EXPERTS, dtype=I32)
    valid = it < total
    e_raw = jnp.sum((item_end[None, :] <= it[:, None]).astype(I32), axis=1)
    e_last = jnp.sum(jnp.where(it == total - 1, e_raw, 0))
    e_of = jnp.where(valid, e_raw, e_last).astype(I32)
    onehot = eids[None, :] == e_of[:, None]
    b_of = jnp.where(valid, _pick(onehot, first_blk) + it - _pick(onehot, item_start),
                     n_blocks - 1).astype(I32)
    lo = jnp.where(valid, jnp.clip(_pick(onehot, starts) - b_of * blk, 0, blk), 0).astype(I32)
    hi = jnp.where(valid, jnp.clip(_pick(onehot, ends) - b_of * blk, 0, blk), 0).astype(I32)
    prev_b = jnp.concatenate([jnp.full((1,), -1, I32), b_of[:-1]])
    prev_e = jnp.concatenate([jnp.full((1,), -1, I32), e_of[:-1]])
    first = (valid & (b_of != prev_b)).astype(I32)
    newexp = (valid & (e_of != prev_e)).astype(I32)
    return starts.astype(I32), (b_of, e_of, lo, hi, first, newexp)


def _slot_of(starts, expert, rank):
    onehot = jnp.arange(N_EXPERTS, dtype=I32)[None, :] == expert[:, None]
    return _pick(onehot, starts) + rank


def _combine_kernel(rows_ref, gcol_ref, x1_ref, mod_ref, fg_ref, o_ref):
    gc = gcol_ref[...]
    y = gc[:, 0:1] * _unpack_bf16_pairs(rows_ref[0]) + gc[:, 1:2] * _unpack_bf16_pairs(rows_ref[1])
    x2 = x1_ref[...] + mod_ref[0][5:6] * y
    r = lax.rsqrt(jnp.mean(x2 * x2, axis=-1, keepdims=True) + EPS)
    o_ref[...] = (x2 * r) * fg_ref[...]


def _combine(rows, gcol, x1, mod, final_g, seq):
    n_tok, dm = x1.shape
    width = rows.shape[2]
    tc = min(COMBINE_TILE, seq)
    nst = seq // tc
    return pl.pallas_call(
        _combine_kernel,
        grid=(n_tok // seq, nst),
        in_specs=[pl.BlockSpec((TOP_K, tc, width), lambda b, s: (0, b * nst + s, 0)),
                  pl.BlockSpec((tc, LANES), lambda b, s: (b * nst + s, 0)),
                  pl.BlockSpec((tc, dm), lambda b, s: (b * nst + s, 0)),
                  pl.BlockSpec((1, 6, dm), lambda b, s: (b, 0, 0)),
                  pl.BlockSpec((1, dm), lambda b, s: (0, 0))],
        out_specs=pl.BlockSpec((tc, dm), lambda b, s: (b * nst + s, 0)),
        out_shape=jax.ShapeDtypeStruct((n_tok, dm), F32),
        compiler_params=pltpu.CompilerParams(dimension_semantics=("arbitrary", "arbitrary"),
                                             vmem_limit_bytes=VMEM_LIMIT),
        name="combine",
    )(rows, gcol, x1, mod, final_g.reshape(1, dm))


def _layer(x, c, ada_w, ada_b, norm1_g, w_in, conv_w, conv_b, w_q, w_k, b_igate, b_fgate,
           mlstm_norm_g, mlstm_skip, w_pool, b_pool, pool_scale, w_out, norm2_g,
           w_rg, b_rg, w_re, b_re, w_eg, w_eu, w_ed, out_g):
    bsz, seq, dm = x.shape
    dml = conv_w.shape[1]
    n_tok = bsz * seq
    ts = min(SEQ_TILE, seq)

    mod = _ada(c, ada_w, ada_b).reshape(bsz, 6, dm)

    col_i = 3 * dml
    col_p = col_i + 2 * N_HEADS
    w_gate_cols = jnp.pad(w_in[:, col_i:col_p], ((0, 0), (0, LANES - 2 * N_HEADS)))
    w_in_r = jnp.concatenate([w_in[:, :col_i], w_in[:, col_p:], w_gate_cols], axis=1).astype(BF16)
    gbias = jnp.pad(jnp.concatenate([b_igate, b_fgate]), (0, LANES - 2 * N_HEADS)).reshape(1, LANES)
    wqk = jnp.concatenate([w_q, w_k], axis=-1).astype(BF16)
    wrt = jnp.zeros((LOGIT_ROWS, dm), F32)
    wrt = wrt.at[0:N_GROUPS].set(w_rg.T).at[SUBLANES:SUBLANES + N_EXPERTS].set(w_re.T).astype(BF16)
    rb = jnp.zeros((LOGIT_ROWS,), F32).at[0:N_GROUPS].set(b_rg).at[SUBLANES:SUBLANES + N_EXPERTS].set(b_re)
    rbias = jnp.broadcast_to(rb[:, None], (LOGIT_ROWS, ts))

    x1, h2, lgt = _mixer(x, mod, norm1_g.reshape(1, dm), w_in_r, gbias, conv_w,
                         conv_b.reshape(1, dml), wqk, mlstm_norm_g.reshape(1, dml),
                         mlstm_skip.reshape(1, dml), w_pool.astype(BF16),
                         b_pool.reshape(1, dm - dml), pool_scale.reshape(1, dm - dml),
                         w_out.astype(BF16), norm2_g.reshape(1, dm), wrt, rbias)

    idx, gcol, cnt = _route(lgt)
    counts = cnt[:, 0].astype(I32)
    starts, items = _expert_items(counts, n_tok * TOP_K)
    dest0 = _slot_of(starts, idx[0], idx[2])
    dest1 = _slot_of(starts, idx[1], idx[3])

    xs = _dispatch(h2.reshape(n_tok, dm // 2), dest0, dest1)
    ys = _experts(items, xs, w_eg, w_eu, w_ed)
    rows = _collect(ys, dest0, dest1)
    out = _combine(rows, gcol, x1.reshape(n_tok, dm), mod, out_g, seq)
    return out.reshape(bsz, seq, dm)


def kernel(x, c, ada_w, ada_b, norm1_g, w_in, conv_w, conv_b, w_q, w_k, b_igate, b_fgate, mlstm_norm_g, mlstm_skip, w_pool, b_pool, pool_scale, w_out, norm2_g, w_router_group, b_router_group, w_router_expert, b_router_expert, w_expert_gate, w_expert_up, w_expert_down, final_g):
    depth = ada_w.shape[0]
    assert depth == 1, "the final norm is fused into the last layer's combine kernel"
    l = 0
    return _layer(x, c, ada_w[l], ada_b[l], norm1_g[l], w_in[l], conv_w[l], conv_b[l], w_q[l],
                  w_k[l], b_igate[l], b_fgate[l], mlstm_norm_g[l], mlstm_skip[l], w_pool[l],
                  b_pool[l], pool_scale[l], w_out[l], norm2_g[l], w_router_group[l],
                  b_router_group[l], w_router_expert[l], b_router_expert[l],
                  w_expert_gate[l], w_expert_up[l], w_expert_down[l], final_g)
```

```python
import functools

import jax
import jax.numpy as jnp
from jax import lax
from jax.experimental import pallas as pl
from jax.experimental.pallas import tpu as pltpu
from jax.experimental.pallas import tpu_sc as plsc

F32 = jnp.float32
BF16 = jnp.bfloat16
I32 = jnp.int32
U32 = jnp.uint32

EPS = 1e-6
N_HEADS = 4
HEAD_DIM = 128
CONV_WIDTH = 4
POOL_WINDOWS = (2, 4, 8, 16)
N_GROUPS = 4
EXPERTS_PER_GROUP = 8
N_EXPERTS = N_GROUPS * EXPERTS_PER_GROUP
TOP_K = 2

LANES = 128
SUBLANES = 8
CHUNK = 128
SEQ_TILE = 512
ROUTE_TILE = 512
SC_CHUNK = 64
COMBINE_TILE = 512
EXPERT_BLOCK = 256
LOGIT_ROWS = 48
UHIST = 8
PHIST = 16
VMEM_LIMIT = 60 * 1024 * 1024


def _sigmoid(x):
    return 1.0 / (1.0 + jnp.exp(-x))


def _pack_bf16_pairs(x):
    w = x.shape[1] // 2
    half_ulp = jnp.uint32(0x8000)
    hi = lax.bitcast_convert_type(x[:, :w], U32) + half_ulp
    lo = lax.bitcast_convert_type(x[:, w:], U32) + half_ulp
    return lax.bitcast_convert_type((hi & jnp.uint32(0xFFFF0000)) | (lo >> 16), I32)


def _unpack_bf16_pairs(words):
    u = lax.bitcast_convert_type(words, U32)
    hi = lax.bitcast_convert_type(u & jnp.uint32(0xFFFF0000), F32)
    lo = lax.bitcast_convert_type(u << 16, F32)
    return jnp.concatenate([hi, lo], axis=1)


def _ada_kernel(c_ref, w_ref, b_ref, o_ref):
    c = c_ref[...]
    s = c * _sigmoid(c)
    o_ref[...] = jnp.dot(s, w_ref[...], preferred_element_type=F32,
                         precision=lax.Precision.HIGHEST) + b_ref[...]


def _ada(c, ada_w, ada_b):
    bsz, dm = c.shape
    n = ada_w.shape[1]
    tn = 1024
    return pl.pallas_call(
        _ada_kernel,
        grid=(n // tn,),
        in_specs=[pl.BlockSpec((bsz, dm), lambda j: (0, 0)),
                  pl.BlockSpec((dm, tn), lambda j: (0, j)),
                  pl.BlockSpec((1, tn), lambda j: (0, j))],
        out_specs=pl.BlockSpec((bsz, tn), lambda j: (0, j)),
        out_shape=jax.ShapeDtypeStruct((bsz, n), F32),
        compiler_params=pltpu.CompilerParams(dimension_semantics=("arbitrary",),
                                             vmem_limit_bytes=VMEM_LIMIT),
        name="ada",
    )(c, ada_w, ada_b.reshape(1, n))


def _split3(x):
    hi = x.astype(BF16)
    r1 = x - hi.astype(F32)
    mid = r1.astype(BF16)
    lo = (r1 - mid.astype(F32)).astype(BF16)
    return hi, mid, lo


def _mixer_kernel(x_ref, mod_ref, g1_ref, win_ref, gbias_ref, convw_ref, convb_ref, wqk_ref,
                  ng_ref, skip_ref, wpool_ref, bpool_ref, pscale_ref, wout_ref, g2_ref,
                  wrt_ref, rbias_ref,
                  x1_ref, h2_ref, lgt_ref,
                  uext_s, pext_s, proj_s, uc_s, qk_s, gate_s, mix_s, pool4_s, ctv_s, ctn_s, mprev_s,
                  *, ts, dm, dml):
    s_idx = pl.program_id(1)
    n_chunks = ts // CHUNK
    dp = dm - dml

    @pl.when(s_idx == 0)
    def _():
        uext_s[0:UHIST, :] = jnp.zeros((UHIST, dml), F32)
        pext_s[0:PHIST, :] = jnp.zeros((PHIST, dp), F32)
        ctv_s[...] = jnp.zeros_like(ctv_s)
        ctn_s[...] = jnp.zeros_like(ctn_s)
        mprev_s[...] = jnp.zeros_like(mprev_s)

    row_i = lax.broadcasted_iota(I32, (CHUNK, CHUNK), 0)
    col_i = lax.broadcasted_iota(I32, (CHUNK, CHUNK), 1)
    causal = row_i >= col_i
    tril = jnp.where(causal, 1.0, 0.0).astype(BF16)
    lane_c = lax.broadcasted_iota(I32, (CHUNK, LANES), 1)
    ones_blk = jnp.ones((CHUNK, HEAD_DIM), BF16)
    q_scale = HEAD_DIM ** -0.5
    t_glob = lax.broadcasted_iota(I32, (ts, LANES), 0) + s_idx * ts + 1

    col_o = dml
    col_g = 2 * dml

    x = x_ref[0]
    mod = mod_ref[0]
    r = lax.rsqrt(jnp.mean(x * x, axis=-1, keepdims=True) + EPS)
    h = (x * r) * (g1_ref[...] * (1.0 + mod[1:2])) + mod[0:1]
    res = jnp.dot(h.astype(BF16), win_ref[...], preferred_element_type=F32)
    uext_s[UHIST:, :] = res[:, 0:dml]
    pext_s[PHIST:, :] = res[:, dml:dm]
    proj_s[...] = res[:, dm:]

    acc = None
    for j in range(CONV_WIDTH):
        tap = uext_s[pl.ds(UHIST - (CONV_WIDTH - 1 - j), ts), :] * convw_ref[j:j + 1, :]
        acc = tap if acc is None else acc + tap
    conv = acc + convb_ref[...]
    uc = conv * _sigmoid(conv)
    uc_s[...] = uc

    for hd in range(N_HEADS):
        qk_s[:, 2 * HEAD_DIM * hd:2 * HEAD_DIM * (hd + 1)] = jnp.dot(
            uc[:, HEAD_DIM * hd:HEAD_DIM * (hd + 1)].astype(BF16), wqk_ref[hd],
            preferred_element_type=F32)

    g = proj_s[:, col_g:col_g + LANES] + gbias_ref[...]
    lane = lax.broadcasted_iota(I32, (ts, LANES), 1)
    logf = -(jnp.maximum(-g, 0.0) + jnp.log1p(jnp.exp(-jnp.abs(g))))
    gate_s[...] = jnp.where(lane < N_HEADS, g, logf)

    pairs = [(c, hd) for c in range(n_chunks) for hd in range(N_HEADS)]
    bcums, comb_ts = [], []
    for c in range(n_chunks):
        gc = gate_s[pl.ds(c * CHUNK, CHUNK), :]
        hi, mid, lo = _split3(gc)
        cs = jnp.dot(tril, jnp.concatenate([hi, mid, lo], axis=1), preferred_element_type=F32)
        bcum = cs[:, 0:LANES] + cs[:, LANES:2 * LANES] + cs[:, 2 * LANES:3 * LANES]
        bcums.append(bcum)
        comb_ts.append(jnp.where(lane_c < N_HEADS, gc, bcum).T)

    def rows(ref, c, lo_col, width=HEAD_DIM):
        return ref[pl.ds(c * CHUNK, CHUNK), lo_col:lo_col + width]

    b_bcs, dlogs, rmaxs, p_mats = {}, {}, {}, {}
    for c, hd in pairs:
        b_bc = jnp.broadcast_to(bcums[c][:, N_HEADS + hd:N_HEADS + hd + 1], (CHUNK, CHUNK))
        i_row = comb_ts[c][hd:hd + 1, :]
        b_row = comb_ts[c][N_HEADS + hd:N_HEADS + hd + 1, :]
        dlog = jnp.where(causal, (b_bc - b_row) + i_row, -jnp.inf)
        b_bcs[c, hd], dlogs[c, hd] = b_bc, dlog
        rmaxs[c, hd] = jnp.max(dlog, axis=-1, keepdims=True)
        q_c = rows(qk_s, c, 2 * HEAD_DIM * hd) * q_scale
        k_c = rows(qk_s, c, 2 * HEAD_DIM * hd + HEAD_DIM)
        p_mats[c, hd] = lax.dot_general(q_c.astype(BF16), k_c.astype(BF16),
                                        (((1,), (1,)), ((), ())), preferred_element_type=F32)

    inters, m_ts = {}, {}
    for hd in range(N_HEADS):
        m_prev = mprev_s[hd]
        for c in range(n_chunks):
            inter = b_bcs[c, hd] + m_prev
            m_t = jnp.maximum(inter, rmaxs[c, hd])
            inters[c, hd], m_ts[c, hd] = inter, m_t
            m_prev = jnp.broadcast_to(m_t[CHUNK - 1:CHUNK, :], (CHUNK, LANES))
        mprev_s[hd] = m_prev

    lhs, v_augs, upds, a_prevs, e_negms = {}, {}, {}, {}, {}
    for c, hd in pairs:
        m_t = m_ts[c, hd]
        wm = jnp.exp(dlogs[c, hd] - m_t)
        a_inter = jnp.exp(inters[c, hd] - m_t)
        e_negms[c, hd] = jnp.exp(-m_t)
        q_c = rows(qk_s, c, 2 * HEAD_DIM * hd) * q_scale
        k_c = rows(qk_s, c, 2 * HEAD_DIM * hd + HEAD_DIM)
        v_c = rows(proj_s, c, HEAD_DIM * hd)
        s_mat = (p_mats[c, hd] * wm).astype(BF16)
        qa = (q_c * a_inter).astype(BF16)
        lhs[c, hd] = jnp.concatenate([s_mat, qa], axis=1)
        v_aug = jnp.concatenate([v_c.astype(BF16), ones_blk], axis=1)
        v_augs[c, hd] = v_aug
        ktw = (k_c.T * wm[CHUNK - 1:CHUNK, :]).astype(BF16)
        upds[c, hd] = jnp.dot(ktw, v_aug, preferred_element_type=F32)
        a_prevs[c, hd] = a_inter[CHUNK - 1:CHUNK, :]

    ct_in = {}
    for hd in range(N_HEADS):
        ctv, ctn = ctv_s[hd], ctn_s[hd]
        for c in range(n_chunks):
            ct_in[c, hd] = jnp.concatenate([ctv, ctn], axis=1).astype(BF16)
            ctv = a_prevs[c, hd] * ctv + upds[c, hd][:, 0:HEAD_DIM]
            ctn = a_prevs[c, hd] * ctn + upds[c, hd][:, HEAD_DIM:2 * HEAD_DIM]
        ctv_s[hd], ctn_s[hd] = ctv, ctn

    for c, hd in pairs:
        numden = jnp.dot(lhs[c, hd], jnp.concatenate([v_augs[c, hd], ct_in[c, hd]], axis=0),
                         preferred_element_type=F32)
        num = numden[:, 0:HEAD_DIM]
        den = numden[:, HEAD_DIM:2 * HEAD_DIM]
        hh = num / jnp.maximum(jnp.abs(den), e_negms[c, hd])
        ms = jnp.mean(hh * hh, axis=-1, keepdims=True)
        hn = hh * lax.rsqrt(ms + EPS) * ng_ref[:, HEAD_DIM * hd:HEAD_DIM * (hd + 1)]
        o_c = rows(proj_s, c, col_o + HEAD_DIM * hd)
        uc_c = rows(uc_s, c, HEAD_DIM * hd)
        out_c = _sigmoid(o_c) * (hn + skip_ref[:, HEAD_DIM * hd:HEAD_DIM * (hd + 1)] * uc_c)
        mix_s[pl.ds(c * CHUNK, CHUNK), HEAD_DIM * hd:HEAD_DIM * (hd + 1)] = out_c.astype(BF16)

    def pe(shift, rows, lanes):
        return pext_s[pl.ds(PHIST - shift, rows), lanes]

    gd = LANES
    sums = []
    for gi in range(2):
        lanes = slice(gd * gi, gd * (gi + 1))
        tot = pe(0, ts, lanes)
        for j in range(1, POOL_WINDOWS[gi]):
            tot = tot + pe(j, ts, lanes)
        sums.append(tot)
    wide = slice(2 * gd, 4 * gd)
    s4 = pe(12, ts + 12, wide)
    for j in range(1, 4):
        s4 = s4 + pe(12 + j, ts + 12, wide)
    pool4_s[0:ts + 12, :] = s4
    s8 = pool4_s[pl.ds(4, ts + 8), :] + pool4_s[pl.ds(0, ts + 8), :]
    sums.append(s8[8:, 0:gd])
    sums.append(s8[8:, gd:2 * gd] + s8[0:ts, gd:2 * gd])
    for gi, win in enumerate(POOL_WINDOWS):
        lanes = slice(gd * gi, gd * (gi + 1))
        cnt = jnp.minimum(t_glob, win).astype(F32)
        pooled = sums[gi] / cnt - pe(0, ts, lanes)
        yp = jnp.dot(pooled.astype(BF16), wpool_ref[gi], preferred_element_type=F32)
        yp = (yp + bpool_ref[:, lanes]) * pscale_ref[:, lanes]
        mix_s[:, dml + gd * gi:dml + gd * (gi + 1)] = yp.astype(BF16)

    mix = jnp.dot(mix_s[...], wout_ref[...], preferred_element_type=F32)
    x1 = x + mod[2:3] * mix
    x1_ref[...] = x1
    r2 = lax.rsqrt(jnp.mean(x1 * x1, axis=-1, keepdims=True) + EPS)
    h2 = (x1 * r2) * (g2_ref[...] * (1.0 + mod[4:5])) + mod[3:4]
    h2_ref[...] = _pack_bf16_pairs(h2)
    lgt_ref[...] = lax.dot_general(wrt_ref[...], h2.astype(BF16), (((1,), (1,)), ((), ())),
                                   preferred_element_type=F32) + rbias_ref[...]

    uext_s[0:UHIST, :] = uext_s[ts:ts + UHIST, :]
    pext_s[0:PHIST, :] = pext_s[ts:ts + PHIST, :]


def _mixer(x, mod, g1, w_in_r, gbias, conv_w, conv_b, wqk, ng, skip, wpool, bpool, pscale,
           w_out, g2, wrt, rbias):
    bsz, seq, dm = x.shape
    dml = conv_w.shape[1]
    ts = min(SEQ_TILE, seq)
    ncols = w_in_r.shape[1]
    nst = seq // ts
    n_tok = bsz * seq
    assert seq % ts == 0 and ts % CHUNK == 0
    full = lambda a: pl.BlockSpec(a.shape, lambda b, s: (0,) * a.ndim)
    kern = functools.partial(_mixer_kernel, ts=ts, dm=dm, dml=dml)
    return pl.pallas_call(
        kern,
        grid=(bsz, nst),
        in_specs=[pl.BlockSpec((1, ts, dm), lambda b, s: (b, s, 0)),
                  pl.BlockSpec((1, 6, dm), lambda b, s: (b, 0, 0)),
                  full(g1), full(w_in_r), full(gbias), full(conv_w), full(conv_b), full(wqk),
                  full(ng), full(skip), full(wpool), full(bpool), full(pscale), full(w_out),
                  full(g2), full(wrt), full(rbias)],
        out_specs=[pl.BlockSpec((ts, dm), lambda b, s: (b * nst + s, 0)),
                   pl.BlockSpec((ts, dm // 2), lambda b, s: (b * nst + s, 0)),
                   pl.BlockSpec((LOGIT_ROWS, ts), lambda b, s: (0, b * nst + s))],
        out_shape=[jax.ShapeDtypeStruct((n_tok, dm), F32),
                   jax.ShapeDtypeStruct((n_tok, dm // 2), I32),
                   jax.ShapeDtypeStruct((LOGIT_ROWS, n_tok), F32)],
        scratch_shapes=[pltpu.VMEM((UHIST + ts, dml), F32),
                        pltpu.VMEM((PHIST + ts, dm - dml), F32),
                        pltpu.VMEM((ts, ncols - dm), F32),
                        pltpu.VMEM((ts, dml), F32),
                        pltpu.VMEM((ts, 2 * dml), F32),
                        pltpu.VMEM((ts, LANES), F32),
                        pltpu.VMEM((ts, dm), BF16),
                        pltpu.VMEM((ts + PHIST, 2 * LANES), F32),
                        pltpu.VMEM((N_HEADS, HEAD_DIM, HEAD_DIM), F32),
                        pltpu.VMEM((N_HEADS, HEAD_DIM, HEAD_DIM), F32),
                        pltpu.VMEM((N_HEADS, CHUNK, LANES), F32)],
        compiler_params=pltpu.CompilerParams(dimension_semantics=("arbitrary", "arbitrary"),
                                             vmem_limit_bytes=VMEM_LIMIT),
        name="mixer",
    )(x, mod, g1, w_in_r, gbias, conv_w, conv_b, wqk, ng, skip, wpool, bpool, pscale, w_out, g2,
      wrt, rbias)


def _route_kernel(lgt_ref, idx_ref, gcol_ref, cnt_ref, carry_s, *, tr):
    first = (pl.program_id(0) == 0) & (pl.program_id(1) == 0)

    @pl.when(first)
    def _():
        carry_s[...] = jnp.zeros_like(carry_s)

    lg = lgt_ref[...]
    best = lg[0:1]
    gidx = jnp.zeros((1, tr), I32)
    for j in range(1, N_GROUPS):
        cand = lg[j:j + 1]
        better = cand > best
        gidx = jnp.where(better, j, gidx)
        best = jnp.where(better, cand, best)
    sumexp = jnp.zeros((1, tr), F32)
    for j in range(N_GROUPS):
        sumexp = sumexp + jnp.exp(lg[j:j + 1] - best)
    g_gate = 1.0 / sumexp

    sel = lg[SUBLANES:2 * SUBLANES]
    for j in range(1, N_GROUPS):
        sel = jnp.where(gidx == j, lg[SUBLANES * (j + 1):SUBLANES * (j + 2)], sel)
    sub = lax.broadcasted_iota(I32, (EXPERTS_PER_GROUP, tr), 0)
    v1 = jnp.max(sel, axis=0, keepdims=True)
    i1 = jnp.min(jnp.where(sel == v1, sub, EXPERTS_PER_GROUP), axis=0, keepdims=True)
    sel2 = jnp.where(sub == i1, -jnp.inf, sel)
    v2 = jnp.max(sel2, axis=0, keepdims=True)
    i2 = jnp.min(jnp.where(sel2 == v2, sub, EXPERTS_PER_GROUP), axis=0, keepdims=True)
    e2 = jnp.exp(v2 - v1)
    den = 1.0 + e2
    gate0 = (1.0 / den) * g_gate
    gate1 = (e2 / den) * g_gate
    ex0 = gidx * EXPERTS_PER_GROUP + i1
    ex1 = gidx * EXPERTS_PER_GROUP + i2

    erow = lax.broadcasted_iota(I32, (N_EXPERTS, tr), 0)
    oh0 = erow == ex0
    oh1 = erow == ex1
    oh = jnp.where(oh0 | oh1, 1.0, 0.0).astype(BF16)
    tr_r = lax.broadcasted_iota(I32, (tr, tr), 0)
    tr_c = lax.broadcasted_iota(I32, (tr, tr), 1)
    upper = jnp.where(tr_r < tr_c, 1.0, 0.0).astype(BF16)
    carry = carry_s[...]
    before = jnp.dot(oh, upper, preferred_element_type=F32)
    before = before + jnp.concatenate([carry] * (tr // LANES), axis=1)
    rank0 = jnp.sum(jnp.where(oh0, before, 0.0), axis=0, keepdims=True)
    rank1 = jnp.sum(jnp.where(oh1, before, 0.0), axis=0, keepdims=True)
    carry = carry + jnp.dot(oh, jnp.ones((tr, LANES), BF16), preferred_element_type=F32)
    carry_s[...] = carry
    cnt_ref[...] = carry

    zrow = jnp.zeros((SUBLANES - 4, tr), I32)
    idx_ref[...] = jnp.concatenate([ex0, ex1, rank0.astype(I32), rank1.astype(I32), zrow], axis=0)

    for q in range(tr // LANES):
        tile = jnp.concatenate([gate0[:, q * LANES:(q + 1) * LANES],
                                gate1[:, q * LANES:(q + 1) * LANES],
                                jnp.zeros((LANES - 2, LANES), F32)], axis=0)
        gcol_ref[q * LANES:(q + 1) * LANES, :] = tile.T


def _route(lgt, bsz, seq):
    tr = min(ROUTE_TILE, seq)
    nst = seq // tr
    return pl.pallas_call(
        functools.partial(_route_kernel, tr=tr),
        grid=(bsz, nst),
        in_specs=[pl.BlockSpec((LOGIT_ROWS, tr), lambda b, s: (0, b * nst + s))],
        out_specs=[pl.BlockSpec((SUBLANES, tr), lambda b, s: (0, b * nst + s)),
                   pl.BlockSpec((tr, LANES), lambda b, s: (b * nst + s, 0)),
                   pl.BlockSpec((N_EXPERTS, LANES), lambda b, s: (0, 0))],
        out_shape=[jax.ShapeDtypeStruct((SUBLANES, bsz * seq), I32),
                   jax.ShapeDtypeStruct((bsz * seq, LANES), F32),
                   jax.ShapeDtypeStruct((N_EXPERTS, LANES), F32)],
        scratch_shapes=[pltpu.VMEM((N_EXPERTS, LANES), F32)],
        compiler_params=pltpu.CompilerParams(dimension_semantics=("arbitrary", "arbitrary"),
                                             vmem_limit_bytes=VMEM_LIMIT),
        name="route",
    )(lgt)


def _sc_workers():
    info = plsc.get_sparse_core_info()
    return info.num_cores, info.num_cores * info.num_subcores


def _dispatch(h2p, dest0, dest1):
    n_tok, width = h2p.shape
    n_cores, n_workers = _sc_workers()
    per_w = n_tok // n_workers
    ch = min(SC_CHUNK, per_w)
    assert n_tok % n_workers == 0 and per_w % ch == 0 and ch % SUBLANES == 0
    mesh = plsc.VectorSubcoreMesh(core_axis_name="c", subcore_axis_name="s")

    @functools.partial(
        pl.kernel, mesh=mesh,
        out_type=jax.ShapeDtypeStruct((n_tok * TOP_K, width), h2p.dtype),
        scratch_types=[pltpu.VMEM((ch,), I32), pltpu.VMEM((ch,), I32),
                       pltpu.VMEM((ch, width), h2p.dtype)],
        name="dispatch")
    def scatter(h_hbm, d0_hbm, d1_hbm, xs_hbm, i0_v, i1_v, rows_v):
        wid = lax.axis_index("s") * n_cores + lax.axis_index("c")
        base = wid * per_w

        @pl.loop(0, per_w, step=ch)
        def _(off):
            t0 = base + off
            pltpu.sync_copy(d0_hbm.at[pl.ds(t0, ch)], i0_v)
            pltpu.sync_copy(d1_hbm.at[pl.ds(t0, ch)], i1_v)
            pltpu.sync_copy(h_hbm.at[pl.ds(t0, ch)], rows_v)
            pltpu.sync_copy(rows_v, xs_hbm.at[i0_v])
            pltpu.sync_copy(rows_v, xs_hbm.at[i1_v])

    return scatter(h2p, dest0, dest1)


def _collect(ys, dest0, dest1):
    n_tok = dest0.shape[0]
    width = ys.shape[1]
    n_cores, n_workers = _sc_workers()
    per_w = n_tok // n_workers
    ch = min(SC_CHUNK, per_w)
    assert n_tok % n_workers == 0 and per_w % ch == 0 and ch % SUBLANES == 0
    mesh = plsc.VectorSubcoreMesh(core_axis_name="c", subcore_axis_name="s")

    @functools.partial(
        pl.kernel, mesh=mesh,
        out_type=jax.ShapeDtypeStruct((TOP_K, n_tok, width), ys.dtype),
        scratch_types=[pltpu.VMEM((ch,), I32), pltpu.VMEM((ch,), I32),
                       pltpu.VMEM((ch, width), ys.dtype), pltpu.VMEM((ch, width), ys.dtype)],
        name="collect")
    def gather(ys_hbm, d0_hbm, d1_hbm, o_hbm, i0_v, i1_v, r0_v, r1_v):
        wid = lax.axis_index("s") * n_cores + lax.axis_index("c")
        base = wid * per_w

        @pl.loop(0, per_w, step=ch)
        def _(off):
            t0 = base + off
            pltpu.sync_copy(d0_hbm.at[pl.ds(t0, ch)], i0_v)
            pltpu.sync_copy(d1_hbm.at[pl.ds(t0, ch)], i1_v)
            pltpu.sync_copy(ys_hbm.at[i0_v], r0_v)
            pltpu.sync_copy(ys_hbm.at[i1_v], r1_v)
            pltpu.sync_copy(r0_v, o_hbm.at[0, pl.ds(t0, ch)])
            pltpu.sync_copy(r1_v, o_hbm.at[1, pl.ds(t0, ch)])

    return gather(ys, dest0, dest1)


def _experts_kernel(blk_ref, exp_ref, lo_ref, hi_ref, first_ref, newexp_ref,
                    xs_ref, wg_ref, wu_ref, wd_ref, ys_ref, wgu_s, wd_s, *, de):
    i = pl.program_id(0)

    @pl.when(newexp_ref[i] == 1)
    def _():
        wgu_s[:, 0:de] = wg_ref[0].astype(BF16)
        wgu_s[:, de:2 * de] = wu_ref[0].astype(BF16)
        wd_s[...] = wd_ref[0].astype(BF16)

    lo = lo_ref[i]
    hi = hi_ref[i]

    @pl.when(hi > lo)
    def _():
        xb = _unpack_bf16_pairs(xs_ref[...]).astype(BF16)
        ab = jnp.dot(xb, wgu_s[...], preferred_element_type=F32)
        a = ab[:, 0:de]
        b = ab[:, de:2 * de]
        hmid = (a * _sigmoid(a)) * b
        y = _pack_bf16_pairs(jnp.dot(hmid.astype(BF16), wd_s[...], preferred_element_type=F32))
        rows = lax.broadcasted_iota(I32, y.shape, 0)
        mask = (rows >= lo) & (rows < hi)

        @pl.when(first_ref[i] == 1)
        def _():
            ys_ref[...] = jnp.where(mask, y, 0)

        @pl.when(first_ref[i] == 0)
        def _():
            ys_ref[...] = jnp.where(mask, y, ys_ref[...])


def _experts(items, xs, w_gate, w_up, w_down):
    n_slots, width = xs.shape
    _, dm, de = w_gate.shape
    n_items = items[0].shape[0]
    blk = EXPERT_BLOCK
    return pl.pallas_call(
        functools.partial(_experts_kernel, de=de),
        grid_spec=pltpu.PrefetchScalarGridSpec(
            num_scalar_prefetch=6,
            grid=(n_items,),
            in_specs=[pl.BlockSpec((blk, width), lambda i, b, e, lo, hi, f, n: (b[i], 0)),
                      pl.BlockSpec((1, dm, de), lambda i, b, e, lo, hi, f, n: (e[i], 0, 0)),
                      pl.BlockSpec((1, dm, de), lambda i, b, e, lo, hi, f, n: (e[i], 0, 0)),
                      pl.BlockSpec((1, de, dm), lambda i, b, e, lo, hi, f, n: (e[i], 0, 0))],
            out_specs=pl.BlockSpec((blk, width), lambda i, b, e, lo, hi, f, n: (b[i], 0)),
            scratch_shapes=[pltpu.VMEM((dm, 2 * de), BF16), pltpu.VMEM((de, dm), BF16)]),
        out_shape=jax.ShapeDtypeStruct((n_slots, width), I32),
        compiler_params=pltpu.CompilerParams(dimension_semantics=("arbitrary",),
                                             vmem_limit_bytes=VMEM_LIMIT),
        name="experts",
    )(*items, xs, w_gate, w_up, w_down)


def _pick(onehot, table):
    return jnp.sum(jnp.where(onehot, table[None, :], 0), axis=1).astype(I32)


def _expert_items(counts, n_slots):
    blk = EXPERT_BLOCK
    n_blocks = n_slots // blk
    n_items = n_blocks + N_EXPERTS - 1
    starts = jnp.cumsum(counts) - counts
    ends = starts + counts
    first_blk = starts // blk
    last_blk = jnp.maximum(ends - 1, 0) // blk
    per_e = jnp.where(counts > 0, last_blk - first_blk + 1, 0)
    item_end = jnp.cumsum(per_e)
    item_start = item_end - per_e
    total = item_end[-1]
    it = jnp.arange(n_items, dtype=I32)
    eids = jnp.arange(N_EXPERTS, dtype=I32)
    valid = it < total
    e_raw = jnp.sum((item_end[None, :] <= it[:, None]).astype(I32), axis=1)
    e_last = jnp.sum(jnp.where(it == total - 1, e_raw, 0))
    e_of = jnp.where(valid, e_raw, e_last).astype(I32)
    onehot = eids[None, :] == e_of[:, None]
    b_of = jnp.where(valid, _pick(onehot, first_blk) + it - _pick(onehot, item_start),
                     n_blocks - 1).astype(I32)
    lo = jnp.where(valid, jnp.clip(_pick(onehot, starts) - b_of * blk, 0, blk), 0).astype(I32)
    hi = jnp.where(valid, jnp.clip(_pick(onehot, ends) - b_of * blk, 0, blk), 0).astype(I32)
    prev_b = jnp.concatenate([jnp.full((1,), -1, I32), b_of[:-1]])
    prev_e = jnp.concatenate([jnp.full((1,), -1, I32), e_of[:-1]])
    first = (valid & (b_of != prev_b)).astype(I32)
    newexp = (valid & (e_of != prev_e)).astype(I32)
    return starts.astype(I32), (b_of, e_of, lo, hi, first, newexp)


def _slot_of(starts, expert, rank):
    onehot = jnp.arange(N_EXPERTS, dtype=I32)[None, :] == expert[:, None]
    return _pick(onehot, starts) + rank


def _combine_kernel(rows_ref, gcol_ref, x1_ref, mod_ref, fg_ref, o_ref):
    gc = gcol_ref[...]
    y = gc[:, 0:1] * _unpack_bf16_pairs(rows_ref[0]) + gc[:, 1:2] * _unpack_bf16_pairs(rows_ref[1])
    x2 = x1_ref[...] + mod_ref[0][5:6] * y
    r = lax.rsqrt(jnp.mean(x2 * x2, axis=-1, keepdims=True) + EPS)
    o_ref[...] = (x2 * r) * fg_ref[...]


def _combine(rows, gcol, x1, mod, final_g, seq):
    n_tok, dm = x1.shape
    width = rows.shape[2]
    tc = min(COMBINE_TILE, seq)
    nst = seq // tc
    return pl.pallas_call(
        _combine_kernel,
        grid=(n_tok // seq, nst),
        in_specs=[pl.BlockSpec((TOP_K, tc, width), lambda b, s: (0, b * nst + s, 0)),
                  pl.BlockSpec((tc, LANES), lambda b, s: (b * nst + s, 0)),
                  pl.BlockSpec((tc, dm), lambda b, s: (b * nst + s, 0)),
                  pl.BlockSpec((1, 6, dm), lambda b, s: (b, 0, 0)),
                  pl.BlockSpec((1, dm), lambda b, s: (0, 0))],
        out_specs=pl.BlockSpec((tc, dm), lambda b, s: (b * nst + s, 0)),
        out_shape=jax.ShapeDtypeStruct((n_tok, dm), F32),
        compiler_params=pltpu.CompilerParams(dimension_semantics=("arbitrary", "arbitrary"),
                                             vmem_limit_bytes=VMEM_LIMIT),
        name="combine",
    )(rows, gcol, x1, mod, final_g.reshape(1, dm))


def _layer(x, c, ada_w, ada_b, norm1_g, w_in, conv_w, conv_b, w_q, w_k, b_igate, b_fgate,
           mlstm_norm_g, mlstm_skip, w_pool, b_pool, pool_scale, w_out, norm2_g,
           w_rg, b_rg, w_re, b_re, w_eg, w_eu, w_ed, out_g):
    bsz, seq, dm = x.shape
    dml = conv_w.shape[1]
    n_tok = bsz * seq
    ts = min(SEQ_TILE, seq)

    mod = _ada(c, ada_w, ada_b).reshape(bsz, 6, dm)

    col_v = dml
    col_o = 2 * dml
    col_i = 3 * dml
    col_p = col_i + 2 * N_HEADS
    w_gate_cols = jnp.pad(w_in[:, col_i:col_p], ((0, 0), (0, LANES - 2 * N_HEADS)))
    w_in_r = jnp.concatenate([w_in[:, :col_v], w_in[:, col_p:], w_in[:, col_v:col_o],
                              w_in[:, col_o:col_i], w_gate_cols], axis=1).astype(BF16)
    gbias = jnp.pad(jnp.concatenate([b_igate, b_fgate]), (0, LANES - 2 * N_HEADS)).reshape(1, LANES)
    wqk = jnp.concatenate([w_q, w_k], axis=-1).astype(BF16)
    wrt = jnp.zeros((LOGIT_ROWS, dm), F32)
    wrt = wrt.at[0:N_GROUPS].set(w_rg.T).at[SUBLANES:SUBLANES + N_EXPERTS].set(w_re.T).astype(BF16)
    rb = jnp.zeros((LOGIT_ROWS,), F32).at[0:N_GROUPS].set(b_rg).at[SUBLANES:SUBLANES + N_EXPERTS].set(b_re)
    rbias = jnp.broadcast_to(rb[:, None], (LOGIT_ROWS, ts))

    x1, h2, lgt = _mixer(x, mod, norm1_g.reshape(1, dm), w_in_r, gbias, conv_w,
                         conv_b.reshape(1, dml), wqk, mlstm_norm_g.reshape(1, dml),
                         mlstm_skip.reshape(1, dml), w_pool.astype(BF16),
                         b_pool.reshape(1, dm - dml), pool_scale.reshape(1, dm - dml),
                         w_out.astype(BF16), norm2_g.reshape(1, dm), wrt, rbias)

    idx, gcol, cnt = _route(lgt, bsz, seq)
    counts = cnt[:, 0].astype(I32)
    starts, items = _expert_items(counts, n_tok * TOP_K)
    dest0 = _slot_of(starts, idx[0], idx[2])
    dest1 = _slot_of(starts, idx[1], idx[3])

    xs = _dispatch(h2, dest0, dest1)
    ys = _experts(items, xs, w_eg, w_eu, w_ed)
    rows = _collect(ys, dest0, dest1)
    out = _combine(rows, gcol, x1, mod, out_g, seq)
    return out.reshape(bsz, seq, dm)


def kernel(x, c, ada_w, ada_b, norm1_g, w_in, conv_w, conv_b, w_q, w_k, b_igate, b_fgate, mlstm_norm_g, mlstm_skip, w_pool, b_pool, pool_scale, w_out, norm2_g, w_router_group, b_router_group, w_router_expert, b_router_expert, w_expert_gate, w_expert_up, w_expert_down, final_g):
    depth = ada_w.shape[0]
    assert depth == 1, "the final norm is fused into the last layer's combine kernel"
    l = 0
    return _layer(x, c, ada_w[l], ada_b[l], norm1_g[l], w_in[l], conv_w[l], conv_b[l], w_q[l],
                  w_k[l], b_igate[l], b_fgate[l], mlstm_norm_g[l], mlstm_skip[l], w_pool[l],
                  b_pool[l], pool_scale[l], w_out[l], norm2_g[l], w_router_group[l],
                  b_router_group[l], w_router_expert[l], b_router_expert[l],
                  w_expert_gate[l], w_expert_up[l], w_expert_down[l], final_g)
```

```python
import functools

import jax
import jax.numpy as jnp
from jax import lax
from jax.experimental import pallas as pl
from jax.experimental.pallas import tpu as pltpu
from jax.experimental.pallas import tpu_sc as plsc

F32 = jnp.float32
BF16 = jnp.bfloat16
I32 = jnp.int32
U32 = jnp.uint32

EPS = 1e-6
N_HEADS = 4
HEAD_DIM = 128
CONV_WIDTH = 4
POOL_WINDOWS = (2, 4, 8, 16)
N_GROUPS = 4
EXPERTS_PER_GROUP = 8
N_EXPERTS = N_GROUPS * EXPERTS_PER_GROUP
TOP_K = 2

LANES = 128
SUBLANES = 8
CHUNK = 128
SEQ_TILE = 512
ROUTE_TILE = 512
SC_CHUNK = 64
COMBINE_TILE = 512
EXPERT_BLOCK = 512
LOGIT_ROWS = 48
UHIST = 8
PHIST = 16
VMEM_LIMIT = 60 * 1024 * 1024


def _sigmoid(x):
    return 1.0 / (1.0 + jnp.exp(-x))


def _pack_bf16_pairs(x):
    w = x.shape[1] // 2
    half_ulp = jnp.uint32(0x8000)
    hi = lax.bitcast_convert_type(x[:, :w], U32) + half_ulp
    lo = lax.bitcast_convert_type(x[:, w:], U32) + half_ulp
    return lax.bitcast_convert_type((hi & jnp.uint32(0xFFFF0000)) | (lo >> 16), I32)


def _unpack_bf16_pairs(words):
    u = lax.bitcast_convert_type(words, U32)
    hi = lax.bitcast_convert_type(u & jnp.uint32(0xFFFF0000), F32)
    lo = lax.bitcast_convert_type(u << 16, F32)
    return jnp.concatenate([hi, lo], axis=1)


def _ada_kernel(c_ref, w_ref, b_ref, o_ref):
    c = c_ref[...]
    s = c * _sigmoid(c)
    o_ref[...] = jnp.dot(s, w_ref[...], preferred_element_type=F32,
                         precision=lax.Precision.HIGHEST) + b_ref[...]


def _ada(c, ada_w, ada_b):
    bsz, dm = c.shape
    n = ada_w.shape[1]
    tn = 1024
    return pl.pallas_call(
        _ada_kernel,
        grid=(n // tn,),
        in_specs=[pl.BlockSpec((bsz, dm), lambda j: (0, 0)),
                  pl.BlockSpec((dm, tn), lambda j: (0, j)),
                  pl.BlockSpec((1, tn), lambda j: (0, j))],
        out_specs=pl.BlockSpec((bsz, tn), lambda j: (0, j)),
        out_shape=jax.ShapeDtypeStruct((bsz, n), F32),
        compiler_params=pltpu.CompilerParams(dimension_semantics=("arbitrary",),
                                             vmem_limit_bytes=VMEM_LIMIT),
        name="ada",
    )(c, ada_w, ada_b.reshape(1, n))


def _split3(x):
    hi = x.astype(BF16)
    r1 = x - hi.astype(F32)
    mid = r1.astype(BF16)
    lo = (r1 - mid.astype(F32)).astype(BF16)
    return hi, mid, lo


def _mixer_kernel(x_ref, mod_ref, g1_ref, win_ref, gbias_ref, convw_ref, convb_ref, wqk_ref,
                  ng_ref, skip_ref, wpool_ref, bpool_ref, pscale_ref, wout_ref, g2_ref,
                  wrt_ref, rbias_ref,
                  x1_ref, h2_ref, lgt_ref,
                  uext_s, pext_s, proj_s, uc_s, qk_s, gate_s, mix_s, pool4_s, ctv_s, ctn_s, mprev_s,
                  *, ts, dm, dml):
    s_idx = pl.program_id(1)
    n_chunks = ts // CHUNK
    dp = dm - dml

    @pl.when(s_idx == 0)
    def _():
        uext_s[0:UHIST, :] = jnp.zeros((UHIST, dml), F32)
        pext_s[0:PHIST, :] = jnp.zeros((PHIST, dp), F32)
        ctv_s[...] = jnp.zeros_like(ctv_s)
        ctn_s[...] = jnp.zeros_like(ctn_s)
        mprev_s[...] = jnp.zeros_like(mprev_s)

    row_i = lax.broadcasted_iota(I32, (CHUNK, CHUNK), 0)
    col_i = lax.broadcasted_iota(I32, (CHUNK, CHUNK), 1)
    causal = row_i >= col_i
    tril = jnp.where(causal, 1.0, 0.0).astype(BF16)
    lane_c = lax.broadcasted_iota(I32, (CHUNK, LANES), 1)
    ones_blk = jnp.ones((CHUNK, HEAD_DIM), BF16)
    q_scale = HEAD_DIM ** -0.5
    t_glob = lax.broadcasted_iota(I32, (ts, LANES), 0) + s_idx * ts + 1

    col_o = dml
    col_g = 2 * dml

    x = x_ref[0]
    mod = mod_ref[0]
    r = lax.rsqrt(jnp.mean(x * x, axis=-1, keepdims=True) + EPS)
    h = (x * r) * (g1_ref[...] * (1.0 + mod[1:2])) + mod[0:1]
    res = jnp.dot(h.astype(BF16), win_ref[...], preferred_element_type=F32)
    uext_s[UHIST:, :] = res[:, 0:dml]
    pext_s[PHIST:, :] = res[:, dml:dm]
    proj_s[...] = res[:, dm:]

    acc = None
    for j in range(CONV_WIDTH):
        tap = uext_s[pl.ds(UHIST - (CONV_WIDTH - 1 - j), ts), :] * convw_ref[j:j + 1, :]
        acc = tap if acc is None else acc + tap
    conv = acc + convb_ref[...]
    uc = conv * _sigmoid(conv)
    uc_s[...] = uc

    for hd in range(N_HEADS):
        qk_s[:, 2 * HEAD_DIM * hd:2 * HEAD_DIM * (hd + 1)] = jnp.dot(
            uc[:, HEAD_DIM * hd:HEAD_DIM * (hd + 1)].astype(BF16), wqk_ref[hd],
            preferred_element_type=F32)

    g = proj_s[:, col_g:col_g + LANES] + gbias_ref[...]
    lane = lax.broadcasted_iota(I32, (ts, LANES), 1)
    logf = -(jnp.maximum(-g, 0.0) + jnp.log1p(jnp.exp(-jnp.abs(g))))
    gate_s[...] = jnp.where(lane < N_HEADS, g, logf)

    pairs = [(c, hd) for c in range(n_chunks) for hd in range(N_HEADS)]
    bcums, comb_ts = [], []
    for c in range(n_chunks):
        gc = gate_s[pl.ds(c * CHUNK, CHUNK), :]
        hi, mid, lo = _split3(gc)
        cs = jnp.dot(tril, jnp.concatenate([hi, mid, lo], axis=1), preferred_element_type=F32)
        bcum = cs[:, 0:LANES] + cs[:, LANES:2 * LANES] + cs[:, 2 * LANES:3 * LANES]
        bcums.append(bcum)
        comb_ts.append(jnp.where(lane_c < N_HEADS, gc, bcum).T)

    def rows(ref, c, lo_col, width=HEAD_DIM):
        return ref[pl.ds(c * CHUNK, CHUNK), lo_col:lo_col + width]

    b_bcs, dlogs, rmaxs, p_mats = {}, {}, {}, {}
    for c, hd in pairs:
        b_bc = jnp.broadcast_to(bcums[c][:, N_HEADS + hd:N_HEADS + hd + 1], (CHUNK, CHUNK))
        i_row = comb_ts[c][hd:hd + 1, :]
        b_row = comb_ts[c][N_HEADS + hd:N_HEADS + hd + 1, :]
        dlog = jnp.where(causal, (b_bc - b_row) + i_row, -jnp.inf)
        b_bcs[c, hd], dlogs[c, hd] = b_bc, dlog
        rmaxs[c, hd] = jnp.max(dlog, axis=-1, keepdims=True)
        q_c = rows(qk_s, c, 2 * HEAD_DIM * hd) * q_scale
        k_c = rows(qk_s, c, 2 * HEAD_DIM * hd + HEAD_DIM)
        p_mats[c, hd] = lax.dot_general(q_c.astype(BF16), k_c.astype(BF16),
                                        (((1,), (1,)), ((), ())), preferred_element_type=F32)

    inters, m_ts = {}, {}
    for hd in range(N_HEADS):
        m_prev = mprev_s[hd]
        for c in range(n_chunks):
            inter = b_bcs[c, hd] + m_prev
            m_t = jnp.maximum(inter, rmaxs[c, hd])
            inters[c, hd], m_ts[c, hd] = inter, m_t
            m_prev = jnp.broadcast_to(m_t[CHUNK - 1:CHUNK, :], (CHUNK, LANES))
        mprev_s[hd] = m_prev

    lhs, v_augs, upds, a_prevs, e_negms = {}, {}, {}, {}, {}
    for c, hd in pairs:
        m_t = m_ts[c, hd]
        wm = jnp.exp(dlogs[c, hd] - m_t)
        a_inter = jnp.exp(inters[c, hd] - m_t)
        e_negms[c, hd] = jnp.exp(-m_t)
        q_c = rows(qk_s, c, 2 * HEAD_DIM * hd) * q_scale
        k_c = rows(qk_s, c, 2 * HEAD_DIM * hd + HEAD_DIM)
        v_c = rows(proj_s, c, HEAD_DIM * hd)
        s_mat = (p_mats[c, hd] * wm).astype(BF16)
        qa = (q_c * a_inter).astype(BF16)
        lhs[c, hd] = jnp.concatenate([s_mat, qa], axis=1)
        v_aug = jnp.concatenate([v_c.astype(BF16), ones_blk], axis=1)
        v_augs[c, hd] = v_aug
        ktw = (k_c.T * wm[CHUNK - 1:CHUNK, :]).astype(BF16)
        upds[c, hd] = jnp.dot(ktw, v_aug, preferred_element_type=F32)
        a_prevs[c, hd] = a_inter[CHUNK - 1:CHUNK, :]

    ct_in = {}
    for hd in range(N_HEADS):
        ctv, ctn = ctv_s[hd], ctn_s[hd]
        for c in range(n_chunks):
            ct_in[c, hd] = jnp.concatenate([ctv, ctn], axis=1).astype(BF16)
            ctv = a_prevs[c, hd] * ctv + upds[c, hd][:, 0:HEAD_DIM]
            ctn = a_prevs[c, hd] * ctn + upds[c, hd][:, HEAD_DIM:2 * HEAD_DIM]
        ctv_s[hd], ctn_s[hd] = ctv, ctn

    for c, hd in pairs:
        numden = jnp.dot(lhs[c, hd], jnp.concatenate([v_augs[c, hd], ct_in[c, hd]], axis=0),
                         preferred_element_type=F32)
        num = numden[:, 0:HEAD_DIM]
        den = numden[:, HEAD_DIM:2 * HEAD_DIM]
        hh = num / jnp.maximum(jnp.abs(den), e_negms[c, hd])
        ms = jnp.mean(hh * hh, axis=-1, keepdims=True)
        hn = hh * lax.rsqrt(ms + EPS) * ng_ref[:, HEAD_DIM * hd:HEAD_DIM * (hd + 1)]
        o_c = rows(proj_s, c, col_o + HEAD_DIM * hd)
        uc_c = rows(uc_s, c, HEAD_DIM * hd)
        out_c = _sigmoid(o_c) * (hn + skip_ref[:, HEAD_DIM * hd:HEAD_DIM * (hd + 1)] * uc_c)
        mix_s[pl.ds(c * CHUNK, CHUNK), HEAD_DIM * hd:HEAD_DIM * (hd + 1)] = out_c.astype(BF16)

    def pe(shift, rows, lanes):
        return pext_s[pl.ds(PHIST - shift, rows), lanes]

    gd = LANES
    sums = []
    for gi in range(2):
        lanes = slice(gd * gi, gd * (gi + 1))
        tot = pe(0, ts, lanes)
        for j in range(1, POOL_WINDOWS[gi]):
            tot = tot + pe(j, ts, lanes)
        sums.append(tot)
    wide = slice(2 * gd, 4 * gd)
    s4 = pe(12, ts + 12, wide)
    for j in range(1, 4):
        s4 = s4 + pe(12 + j, ts + 12, wide)
    pool4_s[0:ts + 12, :] = s4
    s8 = pool4_s[pl.ds(4, ts + 8), :] + pool4_s[pl.ds(0, ts + 8), :]
    sums.append(s8[8:, 0:gd])
    sums.append(s8[8:, gd:2 * gd] + s8[0:ts, gd:2 * gd])
    for gi, win in enumerate(POOL_WINDOWS):
        lanes = slice(gd * gi, gd * (gi + 1))
        cnt = jnp.minimum(t_glob, win).astype(F32)
        pooled = sums[gi] / cnt - pe(0, ts, lanes)
        yp = jnp.dot(pooled.astype(BF16), wpool_ref[gi], preferred_element_type=F32)
        yp = (yp + bpool_ref[:, lanes]) * pscale_ref[:, lanes]
        mix_s[:, dml + gd * gi:dml + gd * (gi + 1)] = yp.astype(BF16)

    mix = jnp.dot(mix_s[...], wout_ref[...], preferred_element_type=F32)
    x1 = x + mod[2:3] * mix
    x1_ref[...] = x1
    r2 = lax.rsqrt(jnp.mean(x1 * x1, axis=-1, keepdims=True) + EPS)
    h2 = (x1 * r2) * (g2_ref[...] * (1.0 + mod[4:5])) + mod[3:4]
    h2_ref[...] = _pack_bf16_pairs(h2)
    lgt_ref[...] = lax.dot_general(wrt_ref[...], h2.astype(BF16), (((1,), (1,)), ((), ())),
                                   preferred_element_type=F32) + rbias_ref[...]

    uext_s[0:UHIST, :] = uext_s[ts:ts + UHIST, :]
    pext_s[0:PHIST, :] = pext_s[ts:ts + PHIST, :]


def _mixer(x, mod, g1, w_in_r, gbias, conv_w, conv_b, wqk, ng, skip, wpool, bpool, pscale,
           w_out, g2, wrt, rbias):
    bsz, seq, dm = x.shape
    dml = conv_w.shape[1]
    ts = min(SEQ_TILE, seq)
    ncols = w_in_r.shape[1]
    nst = seq // ts
    n_tok = bsz * seq
    assert seq % ts == 0 and ts % CHUNK == 0
    full = lambda a: pl.BlockSpec(a.shape, lambda b, s: (0,) * a.ndim)
    kern = functools.partial(_mixer_kernel, ts=ts, dm=dm, dml=dml)
    return pl.pallas_call(
        kern,
        grid=(bsz, nst),
        in_specs=[pl.BlockSpec((1, ts, dm), lambda b, s: (b, s, 0)),
                  pl.BlockSpec((1, 6, dm), lambda b, s: (b, 0, 0)),
                  full(g1), full(w_in_r), full(gbias), full(conv_w), full(conv_b), full(wqk),
                  full(ng), full(skip), full(wpool), full(bpool), full(pscale), full(w_out),
                  full(g2), full(wrt), full(rbias)],
        out_specs=[pl.BlockSpec((ts, dm), lambda b, s: (b * nst + s, 0)),
                   pl.BlockSpec((ts, dm // 2), lambda b, s: (b * nst + s, 0)),
                   pl.BlockSpec((LOGIT_ROWS, ts), lambda b, s: (0, b * nst + s))],
        out_shape=[jax.ShapeDtypeStruct((n_tok, dm), F32),
                   jax.ShapeDtypeStruct((n_tok, dm // 2), I32),
                   jax.ShapeDtypeStruct((LOGIT_ROWS, n_tok), F32)],
        scratch_shapes=[pltpu.VMEM((UHIST + ts, dml), F32),
                        pltpu.VMEM((PHIST + ts, dm - dml), F32),
                        pltpu.VMEM((ts, ncols - dm), F32),
                        pltpu.VMEM((ts, dml), F32),
                        pltpu.VMEM((ts, 2 * dml), F32),
                        pltpu.VMEM((ts, LANES), F32),
                        pltpu.VMEM((ts, dm), BF16),
                        pltpu.VMEM((ts + PHIST, 2 * LANES), F32),
                        pltpu.VMEM((N_HEADS, HEAD_DIM, HEAD_DIM), F32),
                        pltpu.VMEM((N_HEADS, HEAD_DIM, HEAD_DIM), F32),
                        pltpu.VMEM((N_HEADS, CHUNK, LANES), F32)],
        compiler_params=pltpu.CompilerParams(dimension_semantics=("arbitrary", "arbitrary"),
                                             vmem_limit_bytes=VMEM_LIMIT),
        name="mixer",
    )(x, mod, g1, w_in_r, gbias, conv_w, conv_b, wqk, ng, skip, wpool, bpool, pscale, w_out, g2,
      wrt, rbias)


def _route_kernel(lgt_ref, idx_ref, gcol_ref, cnt_ref, carry_s, *, tr):
    first = (pl.program_id(0) == 0) & (pl.program_id(1) == 0)

    @pl.when(first)
    def _():
        carry_s[...] = jnp.zeros_like(carry_s)

    lg = lgt_ref[...]
    best = lg[0:1]
    gidx = jnp.zeros((1, tr), I32)
    for j in range(1, N_GROUPS):
        cand = lg[j:j + 1]
        better = cand > best
        gidx = jnp.where(better, j, gidx)
        best = jnp.where(better, cand, best)
    sumexp = jnp.zeros((1, tr), F32)
    for j in range(N_GROUPS):
        sumexp = sumexp + jnp.exp(lg[j:j + 1] - best)
    g_gate = 1.0 / sumexp

    sel = lg[SUBLANES:2 * SUBLANES]
    for j in range(1, N_GROUPS):
        sel = jnp.where(gidx == j, lg[SUBLANES * (j + 1):SUBLANES * (j + 2)], sel)
    sub = lax.broadcasted_iota(I32, (EXPERTS_PER_GROUP, tr), 0)
    v1 = jnp.max(sel, axis=0, keepdims=True)
    i1 = jnp.min(jnp.where(sel == v1, sub, EXPERTS_PER_GROUP), axis=0, keepdims=True)
    sel2 = jnp.where(sub == i1, -jnp.inf, sel)
    v2 = jnp.max(sel2, axis=0, keepdims=True)
    i2 = jnp.min(jnp.where(sel2 == v2, sub, EXPERTS_PER_GROUP), axis=0, keepdims=True)
    e2 = jnp.exp(v2 - v1)
    den = 1.0 + e2
    gate0 = (1.0 / den) * g_gate
    gate1 = (e2 / den) * g_gate
    ex0 = gidx * EXPERTS_PER_GROUP + i1
    ex1 = gidx * EXPERTS_PER_GROUP + i2

    erow = lax.broadcasted_iota(I32, (N_EXPERTS, tr), 0)
    oh0 = erow == ex0
    oh1 = erow == ex1
    oh = jnp.where(oh0 | oh1, 1.0, 0.0).astype(BF16)
    tr_r = lax.broadcasted_iota(I32, (tr, tr), 0)
    tr_c = lax.broadcasted_iota(I32, (tr, tr), 1)
    upper = jnp.where(tr_r < tr_c, 1.0, 0.0).astype(BF16)
    carry = carry_s[...]
    before = jnp.dot(oh, upper, preferred_element_type=F32)
    before = before + jnp.concatenate([carry] * (tr // LANES), axis=1)
    rank0 = jnp.sum(jnp.where(oh0, before, 0.0), axis=0, keepdims=True)
    rank1 = jnp.sum(jnp.where(oh1, before, 0.0), axis=0, keepdims=True)
    carry = carry + jnp.dot(oh, jnp.ones((tr, LANES), BF16), preferred_element_type=F32)
    carry_s[...] = carry
    cnt_ref[...] = carry

    zrow = jnp.zeros((SUBLANES - 4, tr), I32)
    idx_ref[...] = jnp.concatenate([ex0, ex1, rank0.astype(I32), rank1.astype(I32), zrow], axis=0)

    for q in range(tr // LANES):
        tile = jnp.concatenate([gate0[:, q * LANES:(q + 1) * LANES],
                                gate1[:, q * LANES:(q + 1) * LANES],
                                jnp.zeros((LANES - 2, LANES), F32)], axis=0)
        gcol_ref[q * LANES:(q + 1) * LANES, :] = tile.T


def _route(lgt, bsz, seq):
    tr = min(ROUTE_TILE, seq)
    nst = seq // tr
    return pl.pallas_call(
        functools.partial(_route_kernel, tr=tr),
        grid=(bsz, nst),
        in_specs=[pl.BlockSpec((LOGIT_ROWS, tr), lambda b, s: (0, b * nst + s))],
        out_specs=[pl.BlockSpec((SUBLANES, tr), lambda b, s: (0, b * nst + s)),
                   pl.BlockSpec((tr, LANES), lambda b, s: (b * nst + s, 0)),
                   pl.BlockSpec((N_EXPERTS, LANES), lambda b, s: (0, 0))],
        out_shape=[jax.ShapeDtypeStruct((SUBLANES, bsz * seq), I32),
                   jax.ShapeDtypeStruct((bsz * seq, LANES), F32),
                   jax.ShapeDtypeStruct((N_EXPERTS, LANES), F32)],
        scratch_shapes=[pltpu.VMEM((N_EXPERTS, LANES), F32)],
        compiler_params=pltpu.CompilerParams(dimension_semantics=("arbitrary", "arbitrary"),
                                             vmem_limit_bytes=VMEM_LIMIT),
        name="route",
    )(lgt)


def _sc_workers():
    info = plsc.get_sparse_core_info()
    return info.num_cores, info.num_cores * info.num_subcores


def _dispatch(h2p, dest0, dest1, n_slots):
    n_tok, width = h2p.shape
    n_cores, n_workers = _sc_workers()
    per_w = n_tok // n_workers
    ch = min(SC_CHUNK, per_w)
    assert n_tok % n_workers == 0 and per_w % ch == 0 and ch % SUBLANES == 0
    mesh = plsc.VectorSubcoreMesh(core_axis_name="c", subcore_axis_name="s")

    @functools.partial(
        pl.kernel, mesh=mesh,
        out_type=jax.ShapeDtypeStruct((n_slots, width), h2p.dtype),
        scratch_types=[pltpu.VMEM((ch,), I32), pltpu.VMEM((ch,), I32),
                       pltpu.VMEM((ch, width), h2p.dtype)],
        name="dispatch")
    def scatter(h_hbm, d0_hbm, d1_hbm, xs_hbm, i0_v, i1_v, rows_v):
        wid = lax.axis_index("s") * n_cores + lax.axis_index("c")
        base = wid * per_w

        @pl.loop(0, per_w, step=ch)
        def _(off):
            t0 = base + off
            pltpu.sync_copy(d0_hbm.at[pl.ds(t0, ch)], i0_v)
            pltpu.sync_copy(d1_hbm.at[pl.ds(t0, ch)], i1_v)
            pltpu.sync_copy(h_hbm.at[pl.ds(t0, ch)], rows_v)
            pltpu.sync_copy(rows_v, xs_hbm.at[i0_v])
            pltpu.sync_copy(rows_v, xs_hbm.at[i1_v])

    return scatter(h2p, dest0, dest1)


def _collect(ys, dest0, dest1):
    n_tok = dest0.shape[0]
    width = ys.shape[1]
    n_cores, n_workers = _sc_workers()
    per_w = n_tok // n_workers
    ch = min(SC_CHUNK, per_w)
    assert n_tok % n_workers == 0 and per_w % ch == 0 and ch % SUBLANES == 0
    mesh = plsc.VectorSubcoreMesh(core_axis_name="c", subcore_axis_name="s")

    @functools.partial(
        pl.kernel, mesh=mesh,
        out_type=jax.ShapeDtypeStruct((TOP_K, n_tok, width), ys.dtype),
        scratch_types=[pltpu.VMEM((ch,), I32), pltpu.VMEM((ch,), I32),
                       pltpu.VMEM((ch, width), ys.dtype), pltpu.VMEM((ch, width), ys.dtype)],
        name="collect")
    def gather(ys_hbm, d0_hbm, d1_hbm, o_hbm, i0_v, i1_v, r0_v, r1_v):
        wid = lax.axis_index("s") * n_cores + lax.axis_index("c")
        base = wid * per_w

        @pl.loop(0, per_w, step=ch)
        def _(off):
            t0 = base + off
            pltpu.sync_copy(d0_hbm.at[pl.ds(t0, ch)], i0_v)
            pltpu.sync_copy(d1_hbm.at[pl.ds(t0, ch)], i1_v)
            pltpu.sync_copy(ys_hbm.at[i0_v], r0_v)
            pltpu.sync_copy(ys_hbm.at[i1_v], r1_v)
            pltpu.sync_copy(r0_v, o_hbm.at[0, pl.ds(t0, ch)])
            pltpu.sync_copy(r1_v, o_hbm.at[1, pl.ds(t0, ch)])

    return gather(ys, dest0, dest1)


def _experts_kernel(blk_ref, exp_ref, nvalid_ref, newexp_ref,
                    xs_ref, wg_ref, wu_ref, wd_ref, ys_ref, wgu_s, wd_s, *, de):
    i = pl.program_id(0)

    @pl.when(newexp_ref[i] == 1)
    def _():
        wgu_s[:, 0:de] = wg_ref[0].astype(BF16)
        wgu_s[:, de:2 * de] = wu_ref[0].astype(BF16)
        wd_s[...] = wd_ref[0].astype(BF16)

    n_valid = nvalid_ref[i]

    @pl.when(n_valid > 0)
    def _():
        words = xs_ref[...]
        rows = lax.broadcasted_iota(I32, words.shape, 0)
        xb = _unpack_bf16_pairs(jnp.where(rows < n_valid, words, 0)).astype(BF16)
        ab = jnp.dot(xb, wgu_s[...], preferred_element_type=F32)
        a = ab[:, 0:de]
        b = ab[:, de:2 * de]
        hmid = (a * _sigmoid(a)) * b
        y = jnp.dot(hmid.astype(BF16), wd_s[...], preferred_element_type=F32)
        ys_ref[...] = _pack_bf16_pairs(y)


def _experts(blocks, xs, w_gate, w_up, w_down):
    n_slots, width = xs.shape
    _, dm, de = w_gate.shape
    n_blocks = blocks[0].shape[0]
    blk = EXPERT_BLOCK
    assert n_slots == n_blocks * blk
    return pl.pallas_call(
        functools.partial(_experts_kernel, de=de),
        grid_spec=pltpu.PrefetchScalarGridSpec(
            num_scalar_prefetch=4,
            grid=(n_blocks,),
            in_specs=[pl.BlockSpec((blk, width), lambda i, b, e, nv, n: (b[i], 0)),
                      pl.BlockSpec((1, dm, de), lambda i, b, e, nv, n: (e[i], 0, 0)),
                      pl.BlockSpec((1, dm, de), lambda i, b, e, nv, n: (e[i], 0, 0)),
                      pl.BlockSpec((1, de, dm), lambda i, b, e, nv, n: (e[i], 0, 0))],
            out_specs=pl.BlockSpec((blk, width), lambda i, b, e, nv, n: (b[i], 0)),
            scratch_shapes=[pltpu.VMEM((dm, 2 * de), BF16), pltpu.VMEM((de, dm), BF16)]),
        out_shape=jax.ShapeDtypeStruct((n_slots, width), I32),
        compiler_params=pltpu.CompilerParams(dimension_semantics=("arbitrary",),
                                             vmem_limit_bytes=VMEM_LIMIT),
        name="experts",
    )(*blocks, xs, w_gate, w_up, w_down)


def _pick(onehot, table):
    return jnp.sum(jnp.where(onehot, table[None, :], 0), axis=1).astype(I32)


def _expert_blocks(counts, n_assign):
    blk = EXPERT_BLOCK
    n_blocks = n_assign // blk + N_EXPERTS
    per_e = (counts + blk - 1) // blk
    blk_end = jnp.cumsum(per_e)
    blk_start = blk_end - per_e
    total = blk_end[-1]
    j = jnp.arange(n_blocks, dtype=I32)
    eids = jnp.arange(N_EXPERTS, dtype=I32)
    valid = j < total
    e_raw = jnp.sum((blk_end[None, :] <= j[:, None]).astype(I32), axis=1)
    e_last = jnp.sum(jnp.where(j == total - 1, e_raw, 0))
    e_of = jnp.where(valid, e_raw, e_last).astype(I32)
    onehot = eids[None, :] == e_of[:, None]
    k_in_e = j - _pick(onehot, blk_start)
    n_valid = jnp.where(valid, jnp.clip(_pick(onehot, counts) - k_in_e * blk, 0, blk), 0).astype(I32)
    b_of = jnp.where(valid, j, total - 1).astype(I32)
    prev_e = jnp.concatenate([jnp.full((1,), -1, I32), e_of[:-1]])
    newexp = (valid & (e_of != prev_e)).astype(I32)
    return (blk_start * blk).astype(I32), (b_of, e_of, n_valid, newexp)


def _slot_of(starts, expert, rank):
    onehot = jnp.arange(N_EXPERTS, dtype=I32)[None, :] == expert[:, None]
    return _pick(onehot, starts) + rank


def _combine_kernel(rows_ref, gcol_ref, x1_ref, mod_ref, fg_ref, o_ref):
    gc = gcol_ref[...]
    y = gc[:, 0:1] * _unpack_bf16_pairs(rows_ref[0]) + gc[:, 1:2] * _unpack_bf16_pairs(rows_ref[1])
    x2 = x1_ref[...] + mod_ref[0][5:6] * y
    r = lax.rsqrt(jnp.mean(x2 * x2, axis=-1, keepdims=True) + EPS)
    o_ref[...] = (x2 * r) * fg_ref[...]


def _combine(rows, gcol, x1, mod, final_g, seq):
    n_tok, dm = x1.shape
    width = rows.shape[2]
    tc = min(COMBINE_TILE, seq)
    nst = seq // tc
    return pl.pallas_call(
        _combine_kernel,
        grid=(n_tok // seq, nst),
        in_specs=[pl.BlockSpec((TOP_K, tc, width), lambda b, s: (0, b * nst + s, 0)),
                  pl.BlockSpec((tc, LANES), lambda b, s: (b * nst + s, 0)),
                  pl.BlockSpec((tc, dm), lambda b, s: (b * nst + s, 0)),
                  pl.BlockSpec((1, 6, dm), lambda b, s: (b, 0, 0)),
                  pl.BlockSpec((1, dm), lambda b, s: (0, 0))],
        out_specs=pl.BlockSpec((tc, dm), lambda b, s: (b * nst + s, 0)),
        out_shape=jax.ShapeDtypeStruct((n_tok, dm), F32),
        compiler_params=pltpu.CompilerParams(dimension_semantics=("arbitrary", "arbitrary"),
                                             vmem_limit_bytes=VMEM_LIMIT),
        name="combine",
    )(rows, gcol, x1, mod, final_g.reshape(1, dm))


def _layer(x, c, ada_w, ada_b, norm1_g, w_in, conv_w, conv_b, w_q, w_k, b_igate, b_fgate,
           mlstm_norm_g, mlstm_skip, w_pool, b_pool, pool_scale, w_out, norm2_g,
           w_rg, b_rg, w_re, b_re, w_eg, w_eu, w_ed, out_g):
    bsz, seq, dm = x.shape
    dml = conv_w.shape[1]
    n_tok = bsz * seq
    ts = min(SEQ_TILE, seq)

    mod = _ada(c, ada_w, ada_b).reshape(bsz, 6, dm)

    col_v = dml
    col_o = 2 * dml
    col_i = 3 * dml
    col_p = col_i + 2 * N_HEADS
    w_gate_cols = jnp.pad(w_in[:, col_i:col_p], ((0, 0), (0, LANES - 2 * N_HEADS)))
    w_in_r = jnp.concatenate([w_in[:, :col_v], w_in[:, col_p:], w_in[:, col_v:col_o],
                              w_in[:, col_o:col_i], w_gate_cols], axis=1).astype(BF16)
    gbias = jnp.pad(jnp.concatenate([b_igate, b_fgate]), (0, LANES - 2 * N_HEADS)).reshape(1, LANES)
    wqk = jnp.concatenate([w_q, w_k], axis=-1).astype(BF16)
    wrt = jnp.zeros((LOGIT_ROWS, dm), F32)
    wrt = wrt.at[0:N_GROUPS].set(w_rg.T).at[SUBLANES:SUBLANES + N_EXPERTS].set(w_re.T).astype(BF16)
    rb = jnp.zeros((LOGIT_ROWS,), F32).at[0:N_GROUPS].set(b_rg).at[SUBLANES:SUBLANES + N_EXPERTS].set(b_re)
    rbias = jnp.broadcast_to(rb[:, None], (LOGIT_ROWS, ts))

    x1, h2, lgt = _mixer(x, mod, norm1_g.reshape(1, dm), w_in_r, gbias, conv_w,
                         conv_b.reshape(1, dml), wqk, mlstm_norm_g.reshape(1, dml),
                         mlstm_skip.reshape(1, dml), w_pool.astype(BF16),
                         b_pool.reshape(1, dm - dml), pool_scale.reshape(1, dm - dml),
                         w_out.astype(BF16), norm2_g.reshape(1, dm), wrt, rbias)

    idx, gcol, cnt = _route(lgt, bsz, seq)
    counts = cnt[:, 0].astype(I32)
    starts, blocks = _expert_blocks(counts, n_tok * TOP_K)
    dest0 = _slot_of(starts, idx[0], idx[2])
    dest1 = _slot_of(starts, idx[1], idx[3])

    xs = _dispatch(h2, dest0, dest1, blocks[0].shape[0] * EXPERT_BLOCK)
    ys = _experts(blocks, xs, w_eg, w_eu, w_ed)
    rows = _collect(ys, dest0, dest1)
    out = _combine(rows, gcol, x1, mod, out_g, seq)
    return out.reshape(bsz, seq, dm)


def kernel(x, c, ada_w, ada_b, norm1_g, w_in, conv_w, conv_b, w_q, w_k, b_igate, b_fgate, mlstm_norm_g, mlstm_skip, w_pool, b_pool, pool_scale, w_out, norm2_g, w_router_group, b_router_group, w_router_expert, b_router_expert, w_expert_gate, w_expert_up, w_expert_down, final_g):
    depth = ada_w.shape[0]
    assert depth == 1, "the final norm is fused into the last layer's combine kernel"
    l = 0
    return _layer(x, c, ada_w[l], ada_b[l], norm1_g[l], w_in[l], conv_w[l], conv_b[l], w_q[l],
                  w_k[l], b_igate[l], b_fgate[l], mlstm_norm_g[l], mlstm_skip[l], w_pool[l],
                  b_pool[l], pool_scale[l], w_out[l], norm2_g[l], w_router_group[l],
                  b_router_group[l], w_router_expert[l], b_router_expert[l],
                  w_expert_gate[l], w_expert_up[l], w_expert_down[l], final_g)
```

```python
import functools

import jax
import jax.numpy as jnp
from jax import lax
from jax.experimental import pallas as pl
from jax.experimental.pallas import tpu as pltpu
from jax.experimental.pallas import tpu_sc as plsc

F32 = jnp.float32
BF16 = jnp.bfloat16
I32 = jnp.int32
U32 = jnp.uint32

EPS = 1e-6
N_HEADS = 4
HEAD_DIM = 128
CONV_WIDTH = 4
POOL_WINDOWS = (2, 4, 8, 16)
N_GROUPS = 4
EXPERTS_PER_GROUP = 8
N_EXPERTS = N_GROUPS * EXPERTS_PER_GROUP
TOP_K = 2

LANES = 128
SUBLANES = 8
CHUNK = 128
SEQ_TILE = 512
BATCH_GROUPS = 2
ROUTE_TILE = 512
SC_CHUNK = 64
COMBINE_TILE = 512
EXPERT_BLOCK = 512
LOGIT_ROWS = 48
UHIST = 8
PHIST = 16
VMEM_LIMIT = 60 * 1024 * 1024


def _sigmoid(x):
    return 1.0 / (1.0 + jnp.exp(-x))


def _pack_bf16_pairs(x):
    w = x.shape[1] // 2
    half_ulp = jnp.uint32(0x8000)
    hi = lax.bitcast_convert_type(x[:, :w], U32) + half_ulp
    lo = lax.bitcast_convert_type(x[:, w:], U32) + half_ulp
    return lax.bitcast_convert_type((hi & jnp.uint32(0xFFFF0000)) | (lo >> 16), I32)


def _unpack_bf16_pairs(words):
    u = lax.bitcast_convert_type(words, U32)
    hi = lax.bitcast_convert_type(u & jnp.uint32(0xFFFF0000), F32)
    lo = lax.bitcast_convert_type(u << 16, F32)
    return jnp.concatenate([hi, lo], axis=1)


def _ada_kernel(c_ref, w_ref, b_ref, o_ref):
    c = c_ref[...]
    s = c * _sigmoid(c)
    o_ref[...] = jnp.dot(s, w_ref[...], preferred_element_type=F32,
                         precision=lax.Precision.HIGHEST) + b_ref[...]


def _ada(c, ada_w, ada_b):
    bsz, dm = c.shape
    n = ada_w.shape[1]
    tn = 1024
    return pl.pallas_call(
        _ada_kernel,
        grid=(n // tn,),
        in_specs=[pl.BlockSpec((bsz, dm), lambda j: (0, 0)),
                  pl.BlockSpec((dm, tn), lambda j: (0, j)),
                  pl.BlockSpec((1, tn), lambda j: (0, j))],
        out_specs=pl.BlockSpec((bsz, tn), lambda j: (0, j)),
        out_shape=jax.ShapeDtypeStruct((bsz, n), F32),
        compiler_params=pltpu.CompilerParams(dimension_semantics=("arbitrary",),
                                             vmem_limit_bytes=VMEM_LIMIT),
        name="ada",
    )(c, ada_w, ada_b.reshape(1, n))


def _split3(x):
    hi = x.astype(BF16)
    r1 = x - hi.astype(F32)
    mid = r1.astype(BF16)
    lo = (r1 - mid.astype(F32)).astype(BF16)
    return hi, mid, lo


def _mixer_kernel(x_ref, mod_ref, g1_ref, win_ref, gbias_ref, convw_ref, convb_ref, wqk_ref,
                  ng_ref, skip_ref, wpool_ref, bpool_ref, pscale_ref, wout_ref, g2_ref,
                  wrt_ref, rbias_ref,
                  x1_ref, h2_ref, lgt_ref,
                  uext_s, pext_s, proj_s, uc_s, qk_s, gate_s, mix_s, pool4_s, ctv_s, ctn_s, mprev_s,
                  *, ts, dm, dml):
    s_idx = pl.program_id(1)
    n_chunks = ts // CHUNK
    dp = dm - dml

    @pl.when(s_idx == 0)
    def _():
        uext_s[0:UHIST, :] = jnp.zeros((UHIST, dml), F32)
        pext_s[0:PHIST, :] = jnp.zeros((PHIST, dp), F32)
        ctv_s[...] = jnp.zeros_like(ctv_s)
        ctn_s[...] = jnp.zeros_like(ctn_s)
        mprev_s[...] = jnp.zeros_like(mprev_s)

    row_i = lax.broadcasted_iota(I32, (CHUNK, CHUNK), 0)
    col_i = lax.broadcasted_iota(I32, (CHUNK, CHUNK), 1)
    causal = row_i >= col_i
    tril = jnp.where(causal, 1.0, 0.0).astype(BF16)
    lane_c = lax.broadcasted_iota(I32, (CHUNK, LANES), 1)
    ones_blk = jnp.ones((CHUNK, HEAD_DIM), BF16)
    q_scale = HEAD_DIM ** -0.5
    t_glob = lax.broadcasted_iota(I32, (ts, LANES), 0) + s_idx * ts + 1

    col_o = dml
    col_g = 2 * dml

    x = x_ref[0]
    mod = mod_ref[0]
    r = lax.rsqrt(jnp.mean(x * x, axis=-1, keepdims=True) + EPS)
    h = (x * r) * (g1_ref[...] * (1.0 + mod[1:2])) + mod[0:1]
    res = jnp.dot(h.astype(BF16), win_ref[...], preferred_element_type=F32)
    uext_s[UHIST:, :] = res[:, 0:dml]
    pext_s[PHIST:, :] = res[:, dml:dm]
    proj_s[...] = res[:, dm:]

    acc = None
    for j in range(CONV_WIDTH):
        tap = uext_s[pl.ds(UHIST - (CONV_WIDTH - 1 - j), ts), :] * convw_ref[j:j + 1, :]
        acc = tap if acc is None else acc + tap
    conv = acc + convb_ref[...]
    uc = conv * _sigmoid(conv)
    uc_s[...] = uc

    for hd in range(N_HEADS):
        qk_s[:, 2 * HEAD_DIM * hd:2 * HEAD_DIM * (hd + 1)] = jnp.dot(
            uc[:, HEAD_DIM * hd:HEAD_DIM * (hd + 1)].astype(BF16), wqk_ref[hd],
            preferred_element_type=F32)

    g = proj_s[:, col_g:col_g + LANES] + gbias_ref[...]
    lane = lax.broadcasted_iota(I32, (ts, LANES), 1)
    logf = -(jnp.maximum(-g, 0.0) + jnp.log1p(jnp.exp(-jnp.abs(g))))
    gate_s[...] = jnp.where(lane < N_HEADS, g, logf)

    pairs = [(c, hd) for c in range(n_chunks) for hd in range(N_HEADS)]
    bcums, comb_ts = [], []
    for c in range(n_chunks):
        gc = gate_s[pl.ds(c * CHUNK, CHUNK), :]
        hi, mid, lo = _split3(gc)
        cs = jnp.dot(tril, jnp.concatenate([hi, mid, lo], axis=1), preferred_element_type=F32)
        bcum = cs[:, 0:LANES] + cs[:, LANES:2 * LANES] + cs[:, 2 * LANES:3 * LANES]
        bcums.append(bcum)
        comb_ts.append(jnp.where(lane_c < N_HEADS, gc, bcum).T)

    def rows(ref, c, lo_col, width=HEAD_DIM):
        return ref[pl.ds(c * CHUNK, CHUNK), lo_col:lo_col + width]

    b_bcs, dlogs, rmaxs, p_mats = {}, {}, {}, {}
    for c, hd in pairs:
        b_bc = jnp.broadcast_to(bcums[c][:, N_HEADS + hd:N_HEADS + hd + 1], (CHUNK, CHUNK))
        i_row = comb_ts[c][hd:hd + 1, :]
        b_row = comb_ts[c][N_HEADS + hd:N_HEADS + hd + 1, :]
        dlog = jnp.where(causal, (b_bc - b_row) + i_row, -jnp.inf)
        b_bcs[c, hd], dlogs[c, hd] = b_bc, dlog
        rmaxs[c, hd] = jnp.max(dlog, axis=-1, keepdims=True)
        q_c = rows(qk_s, c, 2 * HEAD_DIM * hd) * q_scale
        k_c = rows(qk_s, c, 2 * HEAD_DIM * hd + HEAD_DIM)
        p_mats[c, hd] = lax.dot_general(q_c.astype(BF16), k_c.astype(BF16),
                                        (((1,), (1,)), ((), ())), preferred_element_type=F32)

    inters, m_ts = {}, {}
    for hd in range(N_HEADS):
        m_prev = mprev_s[hd]
        for c in range(n_chunks):
            inter = b_bcs[c, hd] + m_prev
            m_t = jnp.maximum(inter, rmaxs[c, hd])
            inters[c, hd], m_ts[c, hd] = inter, m_t
            m_prev = jnp.broadcast_to(m_t[CHUNK - 1:CHUNK, :], (CHUNK, LANES))
        mprev_s[hd] = m_prev

    lhs, v_augs, upds, a_prevs, e_negms = {}, {}, {}, {}, {}
    for c, hd in pairs:
        m_t = m_ts[c, hd]
        wm = jnp.exp(dlogs[c, hd] - m_t)
        a_inter = jnp.exp(inters[c, hd] - m_t)
        e_negms[c, hd] = jnp.exp(-m_t)
        q_c = rows(qk_s, c, 2 * HEAD_DIM * hd) * q_scale
        k_c = rows(qk_s, c, 2 * HEAD_DIM * hd + HEAD_DIM)
        v_c = rows(proj_s, c, HEAD_DIM * hd)
        s_mat = (p_mats[c, hd] * wm).astype(BF16)
        qa = (q_c * a_inter).astype(BF16)
        lhs[c, hd] = jnp.concatenate([s_mat, qa], axis=1)
        v_aug = jnp.concatenate([v_c.astype(BF16), ones_blk], axis=1)
        v_augs[c, hd] = v_aug
        ktw = (k_c.T * wm[CHUNK - 1:CHUNK, :]).astype(BF16)
        upds[c, hd] = jnp.dot(ktw, v_aug, preferred_element_type=F32)
        a_prevs[c, hd] = a_inter[CHUNK - 1:CHUNK, :]

    ct_in = {}
    for hd in range(N_HEADS):
        ctv, ctn = ctv_s[hd], ctn_s[hd]
        for c in range(n_chunks):
            ct_in[c, hd] = jnp.concatenate([ctv, ctn], axis=1).astype(BF16)
            ctv = a_prevs[c, hd] * ctv + upds[c, hd][:, 0:HEAD_DIM]
            ctn = a_prevs[c, hd] * ctn + upds[c, hd][:, HEAD_DIM:2 * HEAD_DIM]
        ctv_s[hd], ctn_s[hd] = ctv, ctn

    for c, hd in pairs:
        numden = jnp.dot(lhs[c, hd], jnp.concatenate([v_augs[c, hd], ct_in[c, hd]], axis=0),
                         preferred_element_type=F32)
        num = numden[:, 0:HEAD_DIM]
        den = numden[:, HEAD_DIM:2 * HEAD_DIM]
        hh = num / jnp.maximum(jnp.abs(den), e_negms[c, hd])
        ms = jnp.mean(hh * hh, axis=-1, keepdims=True)
        hn = hh * lax.rsqrt(ms + EPS) * ng_ref[:, HEAD_DIM * hd:HEAD_DIM * (hd + 1)]
        o_c = rows(proj_s, c, col_o + HEAD_DIM * hd)
        uc_c = rows(uc_s, c, HEAD_DIM * hd)
        out_c = _sigmoid(o_c) * (hn + skip_ref[:, HEAD_DIM * hd:HEAD_DIM * (hd + 1)] * uc_c)
        mix_s[pl.ds(c * CHUNK, CHUNK), HEAD_DIM * hd:HEAD_DIM * (hd + 1)] = out_c.astype(BF16)

    def pe(shift, rows, lanes):
        return pext_s[pl.ds(PHIST - shift, rows), lanes]

    gd = LANES
    sums = []
    for gi in range(2):
        lanes = slice(gd * gi, gd * (gi + 1))
        tot = pe(0, ts, lanes)
        for j in range(1, POOL_WINDOWS[gi]):
            tot = tot + pe(j, ts, lanes)
        sums.append(tot)
    wide = slice(2 * gd, 4 * gd)
    s4 = pe(12, ts + 12, wide)
    for j in range(1, 4):
        s4 = s4 + pe(12 + j, ts + 12, wide)
    pool4_s[0:ts + 12, :] = s4
    s8 = pool4_s[pl.ds(4, ts + 8), :] + pool4_s[pl.ds(0, ts + 8), :]
    sums.append(s8[8:, 0:gd])
    sums.append(s8[8:, gd:2 * gd] + s8[0:ts, gd:2 * gd])
    for gi, win in enumerate(POOL_WINDOWS):
        lanes = slice(gd * gi, gd * (gi + 1))
        cnt = jnp.minimum(t_glob, win).astype(F32)
        pooled = sums[gi] / cnt - pe(0, ts, lanes)
        yp = jnp.dot(pooled.astype(BF16), wpool_ref[gi], preferred_element_type=F32)
        yp = (yp + bpool_ref[:, lanes]) * pscale_ref[:, lanes]
        mix_s[:, dml + gd * gi:dml + gd * (gi + 1)] = yp.astype(BF16)

    mix = jnp.dot(mix_s[...], wout_ref[...], preferred_element_type=F32)
    x1 = x + mod[2:3] * mix
    x1_ref[...] = x1
    r2 = lax.rsqrt(jnp.mean(x1 * x1, axis=-1, keepdims=True) + EPS)
    h2 = (x1 * r2) * (g2_ref[...] * (1.0 + mod[4:5])) + mod[3:4]
    h2_ref[...] = _pack_bf16_pairs(h2)
    lgt_ref[...] = lax.dot_general(wrt_ref[...], h2.astype(BF16), (((1,), (1,)), ((), ())),
                                   preferred_element_type=F32) + rbias_ref[...]

    uext_s[0:UHIST, :] = uext_s[ts:ts + UHIST, :]
    pext_s[0:PHIST, :] = pext_s[ts:ts + PHIST, :]


def _mixer(x, mod, g1, w_in_r, gbias, conv_w, conv_b, wqk, ng, skip, wpool, bpool, pscale,
           w_out, g2, wrt, rbias, b0, nb):
    _, seq, dm = x.shape
    dml = conv_w.shape[1]
    ts = min(SEQ_TILE, seq)
    ncols = w_in_r.shape[1]
    nst = seq // ts
    n_tok = nb * seq
    assert seq % ts == 0 and ts % CHUNK == 0
    full = lambda a: pl.BlockSpec(a.shape, lambda b, s: (0,) * a.ndim)
    kern = functools.partial(_mixer_kernel, ts=ts, dm=dm, dml=dml)
    return pl.pallas_call(
        kern,
        grid=(nb, nst),
        in_specs=[pl.BlockSpec((1, ts, dm), lambda b, s: (b + b0, s, 0)),
                  pl.BlockSpec((1, 6, dm), lambda b, s: (b + b0, 0, 0)),
                  full(g1), full(w_in_r), full(gbias), full(conv_w), full(conv_b), full(wqk),
                  full(ng), full(skip), full(wpool), full(bpool), full(pscale), full(w_out),
                  full(g2), full(wrt), full(rbias)],
        out_specs=[pl.BlockSpec((ts, dm), lambda b, s: (b * nst + s, 0)),
                   pl.BlockSpec((ts, dm // 2), lambda b, s: (b * nst + s, 0)),
                   pl.BlockSpec((LOGIT_ROWS, ts), lambda b, s: (0, b * nst + s))],
        out_shape=[jax.ShapeDtypeStruct((n_tok, dm), F32),
                   jax.ShapeDtypeStruct((n_tok, dm // 2), I32),
                   jax.ShapeDtypeStruct((LOGIT_ROWS, n_tok), F32)],
        scratch_shapes=[pltpu.VMEM((UHIST + ts, dml), F32),
                        pltpu.VMEM((PHIST + ts, dm - dml), F32),
                        pltpu.VMEM((ts, ncols - dm), F32),
                        pltpu.VMEM((ts, dml), F32),
                        pltpu.VMEM((ts, 2 * dml), F32),
                        pltpu.VMEM((ts, LANES), F32),
                        pltpu.VMEM((ts, dm), BF16),
                        pltpu.VMEM((ts + PHIST, 2 * LANES), F32),
                        pltpu.VMEM((N_HEADS, HEAD_DIM, HEAD_DIM), F32),
                        pltpu.VMEM((N_HEADS, HEAD_DIM, HEAD_DIM), F32),
                        pltpu.VMEM((N_HEADS, CHUNK, LANES), F32)],
        compiler_params=pltpu.CompilerParams(dimension_semantics=("arbitrary", "arbitrary"),
                                             vmem_limit_bytes=VMEM_LIMIT),
        name="mixer",
    )(x, mod, g1, w_in_r, gbias, conv_w, conv_b, wqk, ng, skip, wpool, bpool, pscale, w_out, g2,
      wrt, rbias)


def _route_kernel(lgt_ref, idx_ref, gcol_ref, cnt_ref, carry_s, *, tr):
    first = (pl.program_id(0) == 0) & (pl.program_id(1) == 0)

    @pl.when(first)
    def _():
        carry_s[...] = jnp.zeros_like(carry_s)

    lg = lgt_ref[...]
    best = lg[0:1]
    gidx = jnp.zeros((1, tr), I32)
    for j in range(1, N_GROUPS):
        cand = lg[j:j + 1]
        better = cand > best
        gidx = jnp.where(better, j, gidx)
        best = jnp.where(better, cand, best)
    sumexp = jnp.zeros((1, tr), F32)
    for j in range(N_GROUPS):
        sumexp = sumexp + jnp.exp(lg[j:j + 1] - best)
    g_gate = 1.0 / sumexp

    sel = lg[SUBLANES:2 * SUBLANES]
    for j in range(1, N_GROUPS):
        sel = jnp.where(gidx == j, lg[SUBLANES * (j + 1):SUBLANES * (j + 2)], sel)
    sub = lax.broadcasted_iota(I32, (EXPERTS_PER_GROUP, tr), 0)
    v1 = jnp.max(sel, axis=0, keepdims=True)
    i1 = jnp.min(jnp.where(sel == v1, sub, EXPERTS_PER_GROUP), axis=0, keepdims=True)
    sel2 = jnp.where(sub == i1, -jnp.inf, sel)
    v2 = jnp.max(sel2, axis=0, keepdims=True)
    i2 = jnp.min(jnp.where(sel2 == v2, sub, EXPERTS_PER_GROUP), axis=0, keepdims=True)
    e2 = jnp.exp(v2 - v1)
    den = 1.0 + e2
    gate0 = (1.0 / den) * g_gate
    gate1 = (e2 / den) * g_gate
    ex0 = gidx * EXPERTS_PER_GROUP + i1
    ex1 = gidx * EXPERTS_PER_GROUP + i2

    erow = lax.broadcasted_iota(I32, (N_EXPERTS, tr), 0)
    oh0 = erow == ex0
    oh1 = erow == ex1
    oh = jnp.where(oh0 | oh1, 1.0, 0.0).astype(BF16)
    tr_r = lax.broadcasted_iota(I32, (tr, tr), 0)
    tr_c = lax.broadcasted_iota(I32, (tr, tr), 1)
    upper = jnp.where(tr_r < tr_c, 1.0, 0.0).astype(BF16)
    carry = carry_s[...]
    before = jnp.dot(oh, upper, preferred_element_type=F32)
    before = before + jnp.concatenate([carry] * (tr // LANES), axis=1)
    rank0 = jnp.sum(jnp.where(oh0, before, 0.0), axis=0, keepdims=True)
    rank1 = jnp.sum(jnp.where(oh1, before, 0.0), axis=0, keepdims=True)
    carry = carry + jnp.dot(oh, jnp.ones((tr, LANES), BF16), preferred_element_type=F32)
    carry_s[...] = carry
    cnt_ref[...] = carry

    zrow = jnp.zeros((SUBLANES - 4, tr), I32)
    idx_ref[...] = jnp.concatenate([ex0, ex1, rank0.astype(I32), rank1.astype(I32), zrow], axis=0)

    for q in range(tr // LANES):
        tile = jnp.concatenate([gate0[:, q * LANES:(q + 1) * LANES],
                                gate1[:, q * LANES:(q + 1) * LANES],
                                jnp.zeros((LANES - 2, LANES), F32)], axis=0)
        gcol_ref[q * LANES:(q + 1) * LANES, :] = tile.T


def _route(lgt, bsz, seq):
    tr = min(ROUTE_TILE, seq)
    nst = seq // tr
    return pl.pallas_call(
        functools.partial(_route_kernel, tr=tr),
        grid=(bsz, nst),
        in_specs=[pl.BlockSpec((LOGIT_ROWS, tr), lambda b, s: (0, b * nst + s))],
        out_specs=[pl.BlockSpec((SUBLANES, tr), lambda b, s: (0, b * nst + s)),
                   pl.BlockSpec((tr, LANES), lambda b, s: (b * nst + s, 0)),
                   pl.BlockSpec((N_EXPERTS, LANES), lambda b, s: (0, 0))],
        out_shape=[jax.ShapeDtypeStruct((SUBLANES, bsz * seq), I32),
                   jax.ShapeDtypeStruct((bsz * seq, LANES), F32),
                   jax.ShapeDtypeStruct((N_EXPERTS, LANES), F32)],
        scratch_shapes=[pltpu.VMEM((N_EXPERTS, LANES), F32)],
        compiler_params=pltpu.CompilerParams(dimension_semantics=("arbitrary", "arbitrary"),
                                             vmem_limit_bytes=VMEM_LIMIT),
        name="route",
    )(lgt)


def _sc_workers():
    info = plsc.get_sparse_core_info()
    return info.num_cores, info.num_cores * info.num_subcores


def _dispatch(h2p, dest0, dest1, n_slots):
    n_tok, width = h2p.shape
    n_cores, n_workers = _sc_workers()
    per_w = n_tok // n_workers
    ch = min(SC_CHUNK, per_w)
    assert n_tok % n_workers == 0 and per_w % ch == 0 and ch % SUBLANES == 0
    mesh = plsc.VectorSubcoreMesh(core_axis_name="c", subcore_axis_name="s")

    @functools.partial(
        pl.kernel, mesh=mesh,
        out_type=jax.ShapeDtypeStruct((n_slots, width), h2p.dtype),
        scratch_types=[pltpu.VMEM((ch,), I32), pltpu.VMEM((ch,), I32),
                       pltpu.VMEM((ch, width), h2p.dtype)],
        name="dispatch")
    def scatter(h_hbm, d0_hbm, d1_hbm, xs_hbm, i0_v, i1_v, rows_v):
        wid = lax.axis_index("s") * n_cores + lax.axis_index("c")
        base = wid * per_w

        @pl.loop(0, per_w, step=ch)
        def _(off):
            t0 = base + off
            pltpu.sync_copy(d0_hbm.at[pl.ds(t0, ch)], i0_v)
            pltpu.sync_copy(d1_hbm.at[pl.ds(t0, ch)], i1_v)
            pltpu.sync_copy(h_hbm.at[pl.ds(t0, ch)], rows_v)
            pltpu.sync_copy(rows_v, xs_hbm.at[i0_v])
            pltpu.sync_copy(rows_v, xs_hbm.at[i1_v])

    return scatter(h2p, dest0, dest1)


def _collect(ys, dest0, dest1):
    n_tok = dest0.shape[0]
    width = ys.shape[1]
    n_cores, n_workers = _sc_workers()
    per_w = n_tok // n_workers
    ch = min(SC_CHUNK, per_w)
    assert n_tok % n_workers == 0 and per_w % ch == 0 and ch % SUBLANES == 0
    mesh = plsc.VectorSubcoreMesh(core_axis_name="c", subcore_axis_name="s")

    @functools.partial(
        pl.kernel, mesh=mesh,
        out_type=jax.ShapeDtypeStruct((TOP_K, n_tok, width), ys.dtype),
        scratch_types=[pltpu.VMEM((ch,), I32), pltpu.VMEM((ch,), I32),
                       pltpu.VMEM((ch, width), ys.dtype), pltpu.VMEM((ch, width), ys.dtype)],
        name="collect")
    def gather(ys_hbm, d0_hbm, d1_hbm, o_hbm, i0_v, i1_v, r0_v, r1_v):
        wid = lax.axis_index("s") * n_cores + lax.axis_index("c")
        base = wid * per_w

        @pl.loop(0, per_w, step=ch)
        def _(off):
            t0 = base + off
            pltpu.sync_copy(d0_hbm.at[pl.ds(t0, ch)], i0_v)
            pltpu.sync_copy(d1_hbm.at[pl.ds(t0, ch)], i1_v)
            pltpu.sync_copy(ys_hbm.at[i0_v], r0_v)
            pltpu.sync_copy(ys_hbm.at[i1_v], r1_v)
            pltpu.sync_copy(r0_v, o_hbm.at[0, pl.ds(t0, ch)])
            pltpu.sync_copy(r1_v, o_hbm.at[1, pl.ds(t0, ch)])

    return gather(ys, dest0, dest1)


def _experts_kernel(blk_ref, exp_ref, nvalid_ref, newexp_ref,
                    xs_ref, wg_ref, wu_ref, wd_ref, ys_ref, wgu_s, wd_s, *, de):
    i = pl.program_id(0)

    @pl.when(newexp_ref[i] == 1)
    def _():
        wgu_s[:, 0:de] = wg_ref[0].astype(BF16)
        wgu_s[:, de:2 * de] = wu_ref[0].astype(BF16)
        wd_s[...] = wd_ref[0].astype(BF16)

    n_valid = nvalid_ref[i]

    @pl.when(n_valid > 0)
    def _():
        words = xs_ref[...]
        rows = lax.broadcasted_iota(I32, words.shape, 0)
        xb = _unpack_bf16_pairs(jnp.where(rows < n_valid, words, 0)).astype(BF16)
        ab = jnp.dot(xb, wgu_s[...], preferred_element_type=F32)
        a = ab[:, 0:de]
        b = ab[:, de:2 * de]
        hmid = (a * _sigmoid(a)) * b
        y = jnp.dot(hmid.astype(BF16), wd_s[...], preferred_element_type=F32)
        ys_ref[...] = _pack_bf16_pairs(y)


def _experts(blocks, xs, w_gate, w_up, w_down):
    n_slots, width = xs.shape
    _, dm, de = w_gate.shape
    n_blocks = blocks[0].shape[0]
    blk = EXPERT_BLOCK
    assert n_slots == n_blocks * blk
    return pl.pallas_call(
        functools.partial(_experts_kernel, de=de),
        grid_spec=pltpu.PrefetchScalarGridSpec(
            num_scalar_prefetch=4,
            grid=(n_blocks,),
            in_specs=[pl.BlockSpec((blk, width), lambda i, b, e, nv, n: (b[i], 0)),
                      pl.BlockSpec((1, dm, de), lambda i, b, e, nv, n: (e[i], 0, 0)),
                      pl.BlockSpec((1, dm, de), lambda i, b, e, nv, n: (e[i], 0, 0)),
                      pl.BlockSpec((1, de, dm), lambda i, b, e, nv, n: (e[i], 0, 0))],
            out_specs=pl.BlockSpec((blk, width), lambda i, b, e, nv, n: (b[i], 0)),
            scratch_shapes=[pltpu.VMEM((dm, 2 * de), BF16), pltpu.VMEM((de, dm), BF16)]),
        out_shape=jax.ShapeDtypeStruct((n_slots, width), I32),
        compiler_params=pltpu.CompilerParams(dimension_semantics=("arbitrary",),
                                             vmem_limit_bytes=VMEM_LIMIT),
        name="experts",
    )(*blocks, xs, w_gate, w_up, w_down)


def _pick(onehot, table):
    return jnp.sum(jnp.where(onehot, table[None, :], 0), axis=1).astype(I32)


def _expert_blocks(counts, n_assign):
    blk = EXPERT_BLOCK
    n_blocks = n_assign // blk + N_EXPERTS
    per_e = (counts + blk - 1) // blk
    blk_end = jnp.cumsum(per_e)
    blk_start = blk_end - per_e
    total = blk_end[-1]
    j = jnp.arange(n_blocks, dtype=I32)
    eids = jnp.arange(N_EXPERTS, dtype=I32)
    valid = j < total
    e_raw = jnp.sum((blk_end[None, :] <= j[:, None]).astype(I32), axis=1)
    e_last = jnp.sum(jnp.where(j == total - 1, e_raw, 0))
    e_of = jnp.where(valid, e_raw, e_last).astype(I32)
    onehot = eids[None, :] == e_of[:, None]
    k_in_e = j - _pick(onehot, blk_start)
    n_valid = jnp.where(valid, jnp.clip(_pick(onehot, counts) - k_in_e * blk, 0, blk), 0).astype(I32)
    b_of = jnp.where(valid, j, total - 1).astype(I32)
    prev_e = jnp.concatenate([jnp.full((1,), -1, I32), e_of[:-1]])
    newexp = (valid & (e_of != prev_e)).astype(I32)
    return (blk_start * blk).astype(I32), (b_of, e_of, n_valid, newexp)


def _slot_of(starts, expert, rank):
    onehot = jnp.arange(N_EXPERTS, dtype=I32)[None, :] == expert[:, None]
    return _pick(onehot, starts) + rank


def _combine_kernel(rows_ref, gcol_ref, x1_ref, mod_ref, fg_ref, *rest):
    o_ref = rest[-1]
    gc = gcol_ref[...]
    y = gc[:, 0:1] * _unpack_bf16_pairs(rows_ref[0]) + gc[:, 1:2] * _unpack_bf16_pairs(rows_ref[1])
    x2 = x1_ref[...] + mod_ref[0][5:6] * y
    r = lax.rsqrt(jnp.mean(x2 * x2, axis=-1, keepdims=True) + EPS)
    o_ref[...] = (x2 * r) * fg_ref[...]


def _combine(rows, gcol, x1, mod, final_g, seq, b0, bsz, out_prev):
    n_tok, dm = x1.shape
    nb = n_tok // seq
    width = rows.shape[2]
    tc = min(COMBINE_TILE, seq)
    nst = seq // tc
    in_specs = [pl.BlockSpec((TOP_K, tc, width), lambda b, s: (0, b * nst + s, 0)),
                pl.BlockSpec((tc, LANES), lambda b, s: (b * nst + s, 0)),
                pl.BlockSpec((tc, dm), lambda b, s: (b * nst + s, 0)),
                pl.BlockSpec((1, 6, dm), lambda b, s: (b + b0, 0, 0)),
                pl.BlockSpec((1, dm), lambda b, s: (0, 0))]
    args = [rows, gcol, x1, mod, final_g.reshape(1, dm)]
    aliases = {}
    if out_prev is not None:
        in_specs.append(pl.BlockSpec(memory_space=pl.ANY))
        args.append(out_prev)
        aliases = {len(args) - 1: 0}
    return pl.pallas_call(
        _combine_kernel,
        grid=(nb, nst),
        in_specs=in_specs,
        out_specs=pl.BlockSpec((tc, dm), lambda b, s: ((b + b0) * nst + s, 0)),
        out_shape=jax.ShapeDtypeStruct((bsz * seq, dm), F32),
        input_output_aliases=aliases,
        compiler_params=pltpu.CompilerParams(dimension_semantics=("arbitrary", "arbitrary"),
                                             vmem_limit_bytes=VMEM_LIMIT),
        name="combine",
    )(*args)


def _layer(x, c, ada_w, ada_b, norm1_g, w_in, conv_w, conv_b, w_q, w_k, b_igate, b_fgate,
           mlstm_norm_g, mlstm_skip, w_pool, b_pool, pool_scale, w_out, norm2_g,
           w_rg, b_rg, w_re, b_re, w_eg, w_eu, w_ed, out_g):
    bsz, seq, dm = x.shape
    dml = conv_w.shape[1]
    n_tok = bsz * seq
    ts = min(SEQ_TILE, seq)

    mod = _ada(c, ada_w, ada_b).reshape(bsz, 6, dm)

    col_v = dml
    col_o = 2 * dml
    col_i = 3 * dml
    col_p = col_i + 2 * N_HEADS
    w_gate_cols = jnp.pad(w_in[:, col_i:col_p], ((0, 0), (0, LANES - 2 * N_HEADS)))
    w_in_r = jnp.concatenate([w_in[:, :col_v], w_in[:, col_p:], w_in[:, col_v:col_o],
                              w_in[:, col_o:col_i], w_gate_cols], axis=1).astype(BF16)
    gbias = jnp.pad(jnp.concatenate([b_igate, b_fgate]), (0, LANES - 2 * N_HEADS)).reshape(1, LANES)
    wqk = jnp.concatenate([w_q, w_k], axis=-1).astype(BF16)
    wrt = jnp.zeros((LOGIT_ROWS, dm), F32)
    wrt = wrt.at[0:N_GROUPS].set(w_rg.T).at[SUBLANES:SUBLANES + N_EXPERTS].set(w_re.T).astype(BF16)
    rb = jnp.zeros((LOGIT_ROWS,), F32).at[0:N_GROUPS].set(b_rg).at[SUBLANES:SUBLANES + N_EXPERTS].set(b_re)
    rbias = jnp.broadcast_to(rb[:, None], (LOGIT_ROWS, ts))

    mixer_params = (norm1_g.reshape(1, dm), w_in_r, gbias, conv_w, conv_b.reshape(1, dml), wqk,
                    mlstm_norm_g.reshape(1, dml), mlstm_skip.reshape(1, dml), w_pool.astype(BF16),
                    b_pool.reshape(1, dm - dml), pool_scale.reshape(1, dm - dml),
                    w_out.astype(BF16), norm2_g.reshape(1, dm), wrt, rbias)

    nb = bsz // BATCH_GROUPS
    assert bsz % BATCH_GROUPS == 0
    staged = []
    for b0 in range(0, bsz, nb):
        x1, h2, lgt = _mixer(x, mod, *mixer_params, b0, nb)
        idx, gcol, cnt = _route(lgt, nb, seq)
        counts = cnt[:, 0].astype(I32)
        starts, blocks = _expert_blocks(counts, nb * seq * TOP_K)
        dest0 = _slot_of(starts, idx[0], idx[2])
        dest1 = _slot_of(starts, idx[1], idx[3])
        xs = _dispatch(h2, dest0, dest1, blocks[0].shape[0] * EXPERT_BLOCK)
        staged.append((b0, x1, gcol, dest0, dest1, blocks, xs))

    gathered = []
    for b0, x1, gcol, dest0, dest1, blocks, xs in staged:
        ys = _experts(blocks, xs, w_eg, w_eu, w_ed)
        gathered.append((b0, x1, gcol, _collect(ys, dest0, dest1)))

    out = None
    for b0, x1, gcol, rows in gathered:
        out = _combine(rows, gcol, x1, mod, out_g, seq, b0, bsz, out)
    return out.reshape(bsz, seq, dm)


def kernel(x, c, ada_w, ada_b, norm1_g, w_in, conv_w, conv_b, w_q, w_k, b_igate, b_fgate, mlstm_norm_g, mlstm_skip, w_pool, b_pool, pool_scale, w_out, norm2_g, w_router_group, b_router_group, w_router_expert, b_router_expert, w_expert_gate, w_expert_up, w_expert_down, final_g):
    depth = ada_w.shape[0]
    assert depth == 1, "the final norm is fused into the last layer's combine kernel"
    l = 0
    return _layer(x, c, ada_w[l], ada_b[l], norm1_g[l], w_in[l], conv_w[l], conv_b[l], w_q[l],
                  w_k[l], b_igate[l], b_fgate[l], mlstm_norm_g[l], mlstm_skip[l], w_pool[l],
                  b_pool[l], pool_scale[l], w_out[l], norm2_g[l], w_router_group[l],
                  b_router_group[l], w_router_expert[l], b_router_expert[l],
                  w_expert_gate[l], w_expert_up[l], w_expert_down[l], final_g)
```

```python
import functools

import jax
import jax.numpy as jnp
from jax import lax
from jax.experimental import pallas as pl
from jax.experimental.pallas import tpu as pltpu
from jax.experimental.pallas import tpu_sc as plsc

F32 = jnp.float32
BF16 = jnp.bfloat16
I32 = jnp.int32
U32 = jnp.uint32

EPS = 1e-6
N_HEADS = 4
HEAD_DIM = 128
CONV_WIDTH = 4
POOL_WINDOWS = (2, 4, 8, 16)
N_GROUPS = 4
EXPERTS_PER_GROUP = 8
N_EXPERTS = N_GROUPS * EXPERTS_PER_GROUP
TOP_K = 2

LANES = 128
SUBLANES = 8
CHUNK = 128
SEQ_TILE = 512
COMBINE_GROUPS = 4
ROUTE_TILE = 512
SC_SCATTER_CHUNK = 64
SC_GATHER_CHUNK = 32
COMBINE_TILE = 512
EXPERT_BLOCK = 512
LOGIT_ROWS = 48
UHIST = 8
PHIST = 16
VMEM_LIMIT = 60 * 1024 * 1024


def _sigmoid(x):
    return 1.0 / (1.0 + jnp.exp(-x))


def _pack_bf16_pairs(x):
    w = x.shape[1] // 2
    half_ulp = jnp.uint32(0x8000)
    hi = lax.bitcast_convert_type(x[:, :w], U32) + half_ulp
    lo = lax.bitcast_convert_type(x[:, w:], U32) + half_ulp
    return lax.bitcast_convert_type((hi & jnp.uint32(0xFFFF0000)) | (lo >> 16), I32)


def _unpack_bf16_pairs(words):
    u = lax.bitcast_convert_type(words, U32)
    hi = lax.bitcast_convert_type(u & jnp.uint32(0xFFFF0000), F32)
    lo = lax.bitcast_convert_type(u << 16, F32)
    return jnp.concatenate([hi, lo], axis=1)


def _ada_kernel(c_ref, w_ref, b_ref, o_ref):
    c = c_ref[...]
    s = c * _sigmoid(c)
    o_ref[...] = jnp.dot(s, w_ref[...], preferred_element_type=F32,
                         precision=lax.Precision.HIGHEST) + b_ref[...]


def _ada(c, ada_w, ada_b):
    bsz, dm = c.shape
    n = ada_w.shape[1]
    tn = 1024
    return pl.pallas_call(
        _ada_kernel,
        grid=(n // tn,),
        in_specs=[pl.BlockSpec((bsz, dm), lambda j: (0, 0)),
                  pl.BlockSpec((dm, tn), lambda j: (0, j)),
                  pl.BlockSpec((1, tn), lambda j: (0, j))],
        out_specs=pl.BlockSpec((bsz, tn), lambda j: (0, j)),
        out_shape=jax.ShapeDtypeStruct((bsz, n), F32),
        compiler_params=pltpu.CompilerParams(dimension_semantics=("arbitrary",),
                                             vmem_limit_bytes=VMEM_LIMIT),
        name="ada",
    )(c, ada_w, ada_b.reshape(1, n))


def _split3(x):
    hi = x.astype(BF16)
    r1 = x - hi.astype(F32)
    mid = r1.astype(BF16)
    lo = (r1 - mid.astype(F32)).astype(BF16)
    return hi, mid, lo


def _mixer_kernel(x_ref, mod_ref, g1_ref, win_ref, gbias_ref, convw_ref, convb_ref, wqk_ref,
                  ng_ref, skip_ref, wpool_ref, bpool_ref, pscale_ref, wout_ref, g2_ref,
                  wrt_ref, rbias_ref,
                  x1_ref, h2_ref, lgt_ref,
                  uext_s, pext_s, proj_s, uc_s, qk_s, gate_s, mix_s, pool4_s, ctv_s, ctn_s, mprev_s,
                  *, ts, dm, dml):
    s_idx = pl.program_id(1)
    n_chunks = ts // CHUNK
    dp = dm - dml

    @pl.when(s_idx == 0)
    def _():
        uext_s[0:UHIST, :] = jnp.zeros((UHIST, dml), F32)
        pext_s[0:PHIST, :] = jnp.zeros((PHIST, dp), F32)
        ctv_s[...] = jnp.zeros_like(ctv_s)
        ctn_s[...] = jnp.zeros_like(ctn_s)
        mprev_s[...] = jnp.zeros_like(mprev_s)

    row_i = lax.broadcasted_iota(I32, (CHUNK, CHUNK), 0)
    col_i = lax.broadcasted_iota(I32, (CHUNK, CHUNK), 1)
    causal = row_i >= col_i
    tril = jnp.where(causal, 1.0, 0.0).astype(BF16)
    lane_c = lax.broadcasted_iota(I32, (CHUNK, LANES), 1)
    ones_blk = jnp.ones((CHUNK, HEAD_DIM), BF16)
    q_scale = HEAD_DIM ** -0.5
    t_glob = lax.broadcasted_iota(I32, (ts, LANES), 0) + s_idx * ts + 1

    col_o = dml
    col_g = 2 * dml

    x = x_ref[0]
    mod = mod_ref[0]
    r = lax.rsqrt(jnp.mean(x * x, axis=-1, keepdims=True) + EPS)
    h = (x * r) * (g1_ref[...] * (1.0 + mod[1:2])) + mod[0:1]
    res = jnp.dot(h.astype(BF16), win_ref[...], preferred_element_type=F32)
    uext_s[UHIST:, :] = res[:, 0:dml]
    pext_s[PHIST:, :] = res[:, dml:dm]
    proj_s[...] = res[:, dm:]

    acc = None
    for j in range(CONV_WIDTH):
        tap = uext_s[pl.ds(UHIST - (CONV_WIDTH - 1 - j), ts), :] * convw_ref[j:j + 1, :]
        acc = tap if acc is None else acc + tap
    conv = acc + convb_ref[...]
    uc = conv * _sigmoid(conv)
    uc_s[...] = uc

    for hd in range(N_HEADS):
        qk_s[:, 2 * HEAD_DIM * hd:2 * HEAD_DIM * (hd + 1)] = jnp.dot(
            uc[:, HEAD_DIM * hd:HEAD_DIM * (hd + 1)].astype(BF16), wqk_ref[hd],
            preferred_element_type=F32)

    g = proj_s[:, col_g:col_g + LANES] + gbias_ref[...]
    lane = lax.broadcasted_iota(I32, (ts, LANES), 1)
    logf = -(jnp.maximum(-g, 0.0) + jnp.log1p(jnp.exp(-jnp.abs(g))))
    gate_s[...] = jnp.where(lane < N_HEADS, g, logf)

    pairs = [(c, hd) for c in range(n_chunks) for hd in range(N_HEADS)]
    bcums, comb_ts = [], []
    for c in range(n_chunks):
        gc = gate_s[pl.ds(c * CHUNK, CHUNK), :]
        hi, mid, lo = _split3(gc)
        cs = jnp.dot(tril, jnp.concatenate([hi, mid, lo], axis=1), preferred_element_type=F32)
        bcum = cs[:, 0:LANES] + cs[:, LANES:2 * LANES] + cs[:, 2 * LANES:3 * LANES]
        bcums.append(bcum)
        comb_ts.append(jnp.where(lane_c < N_HEADS, gc, bcum).T)

    def rows(ref, c, lo_col, width=HEAD_DIM):
        return ref[pl.ds(c * CHUNK, CHUNK), lo_col:lo_col + width]

    b_bcs, dlogs, rmaxs, p_mats = {}, {}, {}, {}
    for c, hd in pairs:
        b_bc = jnp.broadcast_to(bcums[c][:, N_HEADS + hd:N_HEADS + hd + 1], (CHUNK, CHUNK))
        i_row = comb_ts[c][hd:hd + 1, :]
        b_row = comb_ts[c][N_HEADS + hd:N_HEADS + hd + 1, :]
        dlog = jnp.where(causal, (b_bc - b_row) + i_row, -jnp.inf)
        b_bcs[c, hd], dlogs[c, hd] = b_bc, dlog
        rmaxs[c, hd] = jnp.max(dlog, axis=-1, keepdims=True)
        q_c = rows(qk_s, c, 2 * HEAD_DIM * hd) * q_scale
        k_c = rows(qk_s, c, 2 * HEAD_DIM * hd + HEAD_DIM)
        p_mats[c, hd] = lax.dot_general(q_c.astype(BF16), k_c.astype(BF16),
                                        (((1,), (1,)), ((), ())), preferred_element_type=F32)

    inters, m_ts = {}, {}
    for hd in range(N_HEADS):
        m_prev = mprev_s[hd]
        for c in range(n_chunks):
            inter = b_bcs[c, hd] + m_prev
            m_t = jnp.maximum(inter, rmaxs[c, hd])
            inters[c, hd], m_ts[c, hd] = inter, m_t
            m_prev = jnp.broadcast_to(m_t[CHUNK - 1:CHUNK, :], (CHUNK, LANES))
        mprev_s[hd] = m_prev

    lhs, v_augs, upds, a_prevs, e_negms = {}, {}, {}, {}, {}
    for c, hd in pairs:
        m_t = m_ts[c, hd]
        wm = jnp.exp(dlogs[c, hd] - m_t)
        a_inter = jnp.exp(inters[c, hd] - m_t)
        e_negms[c, hd] = jnp.exp(-m_t)
        q_c = rows(qk_s, c, 2 * HEAD_DIM * hd) * q_scale
        k_c = rows(qk_s, c, 2 * HEAD_DIM * hd + HEAD_DIM)
        v_c = rows(proj_s, c, HEAD_DIM * hd)
        s_mat = (p_mats[c, hd] * wm).astype(BF16)
        qa = (q_c * a_inter).astype(BF16)
        lhs[c, hd] = jnp.concatenate([s_mat, qa], axis=1)
        v_aug = jnp.concatenate([v_c.astype(BF16), ones_blk], axis=1)
        v_augs[c, hd] = v_aug
        ktw = (k_c.T * wm[CHUNK - 1:CHUNK, :]).astype(BF16)
        upds[c, hd] = jnp.dot(ktw, v_aug, preferred_element_type=F32)
        a_prevs[c, hd] = a_inter[CHUNK - 1:CHUNK, :]

    ct_in = {}
    for hd in range(N_HEADS):
        ctv, ctn = ctv_s[hd], ctn_s[hd]
        for c in range(n_chunks):
            ct_in[c, hd] = jnp.concatenate([ctv, ctn], axis=1).astype(BF16)
            ctv = a_prevs[c, hd] * ctv + upds[c, hd][:, 0:HEAD_DIM]
            ctn = a_prevs[c, hd] * ctn + upds[c, hd][:, HEAD_DIM:2 * HEAD_DIM]
        ctv_s[hd], ctn_s[hd] = ctv, ctn

    for c, hd in pairs:
        numden = jnp.dot(lhs[c, hd], jnp.concatenate([v_augs[c, hd], ct_in[c, hd]], axis=0),
                         preferred_element_type=F32)
        num = numden[:, 0:HEAD_DIM]
        den = numden[:, HEAD_DIM:2 * HEAD_DIM]
        hh = num / jnp.maximum(jnp.abs(den), e_negms[c, hd])
        ms = jnp.mean(hh * hh, axis=-1, keepdims=True)
        hn = hh * lax.rsqrt(ms + EPS) * ng_ref[:, HEAD_DIM * hd:HEAD_DIM * (hd + 1)]
        o_c = rows(proj_s, c, col_o + HEAD_DIM * hd)
        uc_c = rows(uc_s, c, HEAD_DIM * hd)
        out_c = _sigmoid(o_c) * (hn + skip_ref[:, HEAD_DIM * hd:HEAD_DIM * (hd + 1)] * uc_c)
        mix_s[pl.ds(c * CHUNK, CHUNK), HEAD_DIM * hd:HEAD_DIM * (hd + 1)] = out_c.astype(BF16)

    def pe(shift, rows, lanes):
        return pext_s[pl.ds(PHIST - shift, rows), lanes]

    gd = LANES
    sums = []
    for gi in range(2):
        lanes = slice(gd * gi, gd * (gi + 1))
        tot = pe(0, ts, lanes)
        for j in range(1, POOL_WINDOWS[gi]):
            tot = tot + pe(j, ts, lanes)
        sums.append(tot)
    wide = slice(2 * gd, 4 * gd)
    s4 = pe(12, ts + 12, wide)
    for j in range(1, 4):
        s4 = s4 + pe(12 + j, ts + 12, wide)
    pool4_s[0:ts + 12, :] = s4
    s8 = pool4_s[pl.ds(4, ts + 8), :] + pool4_s[pl.ds(0, ts + 8), :]
    sums.append(s8[8:, 0:gd])
    sums.append(s8[8:, gd:2 * gd] + s8[0:ts, gd:2 * gd])
    for gi, win in enumerate(POOL_WINDOWS):
        lanes = slice(gd * gi, gd * (gi + 1))
        cnt = jnp.minimum(t_glob, win).astype(F32)
        pooled = sums[gi] / cnt - pe(0, ts, lanes)
        yp = jnp.dot(pooled.astype(BF16), wpool_ref[gi], preferred_element_type=F32)
        yp = (yp + bpool_ref[:, lanes]) * pscale_ref[:, lanes]
        mix_s[:, dml + gd * gi:dml + gd * (gi + 1)] = yp.astype(BF16)

    mix = jnp.dot(mix_s[...], wout_ref[...], preferred_element_type=F32)
    x1 = x + mod[2:3] * mix
    x1_ref[...] = x1
    r2 = lax.rsqrt(jnp.mean(x1 * x1, axis=-1, keepdims=True) + EPS)
    h2 = (x1 * r2) * (g2_ref[...] * (1.0 + mod[4:5])) + mod[3:4]
    h2_ref[...] = _pack_bf16_pairs(h2)
    lgt_ref[...] = lax.dot_general(wrt_ref[...], h2.astype(BF16), (((1,), (1,)), ((), ())),
                                   preferred_element_type=F32) + rbias_ref[...]

    uext_s[0:UHIST, :] = uext_s[ts:ts + UHIST, :]
    pext_s[0:PHIST, :] = pext_s[ts:ts + PHIST, :]


def _mixer(x, mod, g1, w_in_r, gbias, conv_w, conv_b, wqk, ng, skip, wpool, bpool, pscale,
           w_out, g2, wrt, rbias, b0, nb):
    _, seq, dm = x.shape
    dml = conv_w.shape[1]
    ts = min(SEQ_TILE, seq)
    ncols = w_in_r.shape[1]
    nst = seq // ts
    n_tok = nb * seq
    assert seq % ts == 0 and ts % CHUNK == 0
    full = lambda a: pl.BlockSpec(a.shape, lambda b, s: (0,) * a.ndim)
    kern = functools.partial(_mixer_kernel, ts=ts, dm=dm, dml=dml)
    return pl.pallas_call(
        kern,
        grid=(nb, nst),
        in_specs=[pl.BlockSpec((1, ts, dm), lambda b, s: (b + b0, s, 0)),
                  pl.BlockSpec((1, 6, dm), lambda b, s: (b + b0, 0, 0)),
                  full(g1), full(w_in_r), full(gbias), full(conv_w), full(conv_b), full(wqk),
                  full(ng), full(skip), full(wpool), full(bpool), full(pscale), full(w_out),
                  full(g2), full(wrt), full(rbias)],
        out_specs=[pl.BlockSpec((ts, dm), lambda b, s: (b * nst + s, 0)),
                   pl.BlockSpec((ts, dm // 2), lambda b, s: (b * nst + s, 0)),
                   pl.BlockSpec((LOGIT_ROWS, ts), lambda b, s: (0, b * nst + s))],
        out_shape=[jax.ShapeDtypeStruct((n_tok, dm), F32),
                   jax.ShapeDtypeStruct((n_tok, dm // 2), I32),
                   jax.ShapeDtypeStruct((LOGIT_ROWS, n_tok), F32)],
        scratch_shapes=[pltpu.VMEM((UHIST + ts, dml), F32),
                        pltpu.VMEM((PHIST + ts, dm - dml), F32),
                        pltpu.VMEM((ts, ncols - dm), F32),
                        pltpu.VMEM((ts, dml), F32),
                        pltpu.VMEM((ts, 2 * dml), F32),
                        pltpu.VMEM((ts, LANES), F32),
                        pltpu.VMEM((ts, dm), BF16),
                        pltpu.VMEM((ts + PHIST, 2 * LANES), F32),
                        pltpu.VMEM((N_HEADS, HEAD_DIM, HEAD_DIM), F32),
                        pltpu.VMEM((N_HEADS, HEAD_DIM, HEAD_DIM), F32),
                        pltpu.VMEM((N_HEADS, CHUNK, LANES), F32)],
        compiler_params=pltpu.CompilerParams(dimension_semantics=("arbitrary", "arbitrary"),
                                             vmem_limit_bytes=VMEM_LIMIT),
        name="mixer",
    )(x, mod, g1, w_in_r, gbias, conv_w, conv_b, wqk, ng, skip, wpool, bpool, pscale, w_out, g2,
      wrt, rbias)


def _route_kernel(lgt_ref, idx_ref, gcol_ref, cnt_ref, carry_s, *, tr):
    first = (pl.program_id(0) == 0) & (pl.program_id(1) == 0)

    @pl.when(first)
    def _():
        carry_s[...] = jnp.zeros_like(carry_s)

    lg = lgt_ref[...]
    best = lg[0:1]
    gidx = jnp.zeros((1, tr), I32)
    for j in range(1, N_GROUPS):
        cand = lg[j:j + 1]
        better = cand > best
        gidx = jnp.where(better, j, gidx)
        best = jnp.where(better, cand, best)
    sumexp = jnp.zeros((1, tr), F32)
    for j in range(N_GROUPS):
        sumexp = sumexp + jnp.exp(lg[j:j + 1] - best)
    g_gate = 1.0 / sumexp

    sel = lg[SUBLANES:2 * SUBLANES]
    for j in range(1, N_GROUPS):
        sel = jnp.where(gidx == j, lg[SUBLANES * (j + 1):SUBLANES * (j + 2)], sel)
    sub = lax.broadcasted_iota(I32, (EXPERTS_PER_GROUP, tr), 0)
    v1 = jnp.max(sel, axis=0, keepdims=True)
    i1 = jnp.min(jnp.where(sel == v1, sub, EXPERTS_PER_GROUP), axis=0, keepdims=True)
    sel2 = jnp.where(sub == i1, -jnp.inf, sel)
    v2 = jnp.max(sel2, axis=0, keepdims=True)
    i2 = jnp.min(jnp.where(sel2 == v2, sub, EXPERTS_PER_GROUP), axis=0, keepdims=True)
    e2 = jnp.exp(v2 - v1)
    den = 1.0 + e2
    gate0 = (1.0 / den) * g_gate
    gate1 = (e2 / den) * g_gate
    ex0 = gidx * EXPERTS_PER_GROUP + i1
    ex1 = gidx * EXPERTS_PER_GROUP + i2

    erow = lax.broadcasted_iota(I32, (N_EXPERTS, tr), 0)
    oh0 = erow == ex0
    oh1 = erow == ex1
    oh = jnp.where(oh0 | oh1, 1.0, 0.0).astype(BF16)
    tr_r = lax.broadcasted_iota(I32, (tr, tr), 0)
    tr_c = lax.broadcasted_iota(I32, (tr, tr), 1)
    upper = jnp.where(tr_r < tr_c, 1.0, 0.0).astype(BF16)
    carry = carry_s[...]
    before = jnp.dot(oh, upper, preferred_element_type=F32)
    before = before + jnp.concatenate([carry] * (tr // LANES), axis=1)
    rank0 = jnp.sum(jnp.where(oh0, before, 0.0), axis=0, keepdims=True)
    rank1 = jnp.sum(jnp.where(oh1, before, 0.0), axis=0, keepdims=True)
    carry = carry + jnp.dot(oh, jnp.ones((tr, LANES), BF16), preferred_element_type=F32)
    carry_s[...] = carry
    cnt_ref[...] = carry

    zrow = jnp.zeros((SUBLANES - 4, tr), I32)
    idx_ref[...] = jnp.concatenate([ex0, ex1, rank0.astype(I32), rank1.astype(I32), zrow], axis=0)

    for q in range(tr // LANES):
        tile = jnp.concatenate([gate0[:, q * LANES:(q + 1) * LANES],
                                gate1[:, q * LANES:(q + 1) * LANES],
                                jnp.zeros((LANES - 2, LANES), F32)], axis=0)
        gcol_ref[q * LANES:(q + 1) * LANES, :] = tile.T


def _route(lgt, bsz, seq):
    tr = min(ROUTE_TILE, seq)
    nst = seq // tr
    return pl.pallas_call(
        functools.partial(_route_kernel, tr=tr),
        grid=(bsz, nst),
        in_specs=[pl.BlockSpec((LOGIT_ROWS, tr), lambda b, s: (0, b * nst + s))],
        out_specs=[pl.BlockSpec((SUBLANES, tr), lambda b, s: (0, b * nst + s)),
                   pl.BlockSpec((tr, LANES), lambda b, s: (b * nst + s, 0)),
                   pl.BlockSpec((N_EXPERTS, LANES), lambda b, s: (0, 0))],
        out_shape=[jax.ShapeDtypeStruct((SUBLANES, bsz * seq), I32),
                   jax.ShapeDtypeStruct((bsz * seq, LANES), F32),
                   jax.ShapeDtypeStruct((N_EXPERTS, LANES), F32)],
        scratch_shapes=[pltpu.VMEM((N_EXPERTS, LANES), F32)],
        compiler_params=pltpu.CompilerParams(dimension_semantics=("arbitrary", "arbitrary"),
                                             vmem_limit_bytes=VMEM_LIMIT),
        name="route",
    )(lgt)


def _sc_workers():
    info = plsc.get_sparse_core_info()
    return info.num_cores, info.num_cores * info.num_subcores


def _dispatch(h2p, dest0, dest1, n_slots):
    n_tok, width = h2p.shape
    n_cores, n_workers = _sc_workers()
    per_w = n_tok // n_workers
    ch = min(SC_SCATTER_CHUNK, per_w)
    n_ch = per_w // ch
    assert n_tok % n_workers == 0 and per_w % ch == 0 and ch % SUBLANES == 0 and n_ch % 2 == 0
    mesh = plsc.VectorSubcoreMesh(core_axis_name="c", subcore_axis_name="s")

    @functools.partial(
        pl.kernel, mesh=mesh,
        out_type=jax.ShapeDtypeStruct((n_slots, width), h2p.dtype),
        scratch_types=[pltpu.VMEM((2, ch), I32), pltpu.VMEM((2, ch), I32),
                       pltpu.VMEM((2, ch, width), h2p.dtype),
                       pltpu.SemaphoreType.DMA((2,)), pltpu.SemaphoreType.DMA((2,))],
        name="dispatch")
    def scatter(h_hbm, d0_hbm, d1_hbm, xs_hbm, i0_v, i1_v, rows_v, sem_in, sem_out):
        wid = lax.axis_index("s") * n_cores + lax.axis_index("c")
        base = wid * per_w

        def loads(t0, slot):
            return (pltpu.make_async_copy(d0_hbm.at[pl.ds(t0, ch)], i0_v.at[slot], sem_in.at[slot]),
                    pltpu.make_async_copy(d1_hbm.at[pl.ds(t0, ch)], i1_v.at[slot], sem_in.at[slot]),
                    pltpu.make_async_copy(h_hbm.at[pl.ds(t0, ch)], rows_v.at[slot], sem_in.at[slot]))

        def scatters(slot):
            return (pltpu.make_async_copy(rows_v.at[slot], xs_hbm.at[i0_v.at[slot]], sem_out.at[slot]),
                    pltpu.make_async_copy(rows_v.at[slot], xs_hbm.at[i1_v.at[slot]], sem_out.at[slot]))

        for cp in loads(base, 0):
            cp.start()

        @pl.loop(0, n_ch, step=2)
        def _(k):
            for slot in range(2):
                for cp in loads(base + (k + slot) * ch, slot):
                    cp.wait()
                out_cps = scatters(slot)
                for cp in out_cps:
                    cp.start()
                nxt = k + slot + 1

                @pl.when(nxt < n_ch)
                def _():
                    for cp in loads(base + nxt * ch, 1 - slot):
                        cp.start()

                for cp in out_cps:
                    cp.wait()

    return scatter(h2p, dest0, dest1)


def _collect(ys, dest0, dest1, tok0, n_tok):
    width = ys.shape[1]
    n_cores, n_workers = _sc_workers()
    per_w = n_tok // n_workers
    ch = min(SC_GATHER_CHUNK, per_w)
    n_ch = per_w // ch
    assert n_tok % n_workers == 0 and per_w % ch == 0 and ch % SUBLANES == 0 and n_ch % 2 == 0
    mesh = plsc.VectorSubcoreMesh(core_axis_name="c", subcore_axis_name="s")

    @functools.partial(
        pl.kernel, mesh=mesh,
        out_type=jax.ShapeDtypeStruct((TOP_K, n_tok, width), ys.dtype),
        scratch_types=[pltpu.VMEM((2, ch), I32), pltpu.VMEM((2, ch), I32),
                       pltpu.VMEM((2, ch, width), ys.dtype), pltpu.VMEM((2, ch, width), ys.dtype),
                       pltpu.SemaphoreType.DMA((2,)), pltpu.SemaphoreType.DMA((2,)),
                       pltpu.SemaphoreType.DMA((2,))],
        name="collect")
    def gather(ys_hbm, d0_hbm, d1_hbm, o_hbm, i0_v, i1_v, r0_v, r1_v, sem_idx, sem_in, sem_out):
        wid = lax.axis_index("s") * n_cores + lax.axis_index("c")
        base = wid * per_w

        def idx_loads(t0, slot):
            return (pltpu.make_async_copy(d0_hbm.at[pl.ds(tok0 + t0, ch)], i0_v.at[slot], sem_idx.at[slot]),
                    pltpu.make_async_copy(d1_hbm.at[pl.ds(tok0 + t0, ch)], i1_v.at[slot], sem_idx.at[slot]))

        def gathers(slot):
            return (pltpu.make_async_copy(ys_hbm.at[i0_v.at[slot]], r0_v.at[slot], sem_in.at[slot]),
                    pltpu.make_async_copy(ys_hbm.at[i1_v.at[slot]], r1_v.at[slot], sem_in.at[slot]))

        def stores(t0, slot):
            return (pltpu.make_async_copy(r0_v.at[slot], o_hbm.at[0, pl.ds(t0, ch)], sem_out.at[slot]),
                    pltpu.make_async_copy(r1_v.at[slot], o_hbm.at[1, pl.ds(t0, ch)], sem_out.at[slot]))

        def start_chunk(t0, slot):
            for cp in idx_loads(t0, slot):
                cp.start()
            for cp in idx_loads(t0, slot):
                cp.wait()
            for cp in gathers(slot):
                cp.start()

        start_chunk(base, 0)

        @pl.loop(0, n_ch, step=2)
        def _(k):
            for slot in range(2):
                nxt = k + slot + 1

                @pl.when(nxt < n_ch)
                def _():
                    start_chunk(base + nxt * ch, 1 - slot)

                for cp in gathers(slot):
                    cp.wait()
                out_cps = stores(base + (k + slot) * ch, slot)
                for cp in out_cps:
                    cp.start()
                for cp in out_cps:
                    cp.wait()

    return gather(ys, dest0, dest1)


def _experts_kernel(blk_ref, exp_ref, nvalid_ref, newexp_ref,
                    xs_ref, wg_ref, wu_ref, wd_ref, ys_ref, wgu_s, wd_s, *, de):
    i = pl.program_id(0)

    @pl.when(newexp_ref[i] == 1)
    def _():
        wgu_s[:, 0:de] = wg_ref[0].astype(BF16)
        wgu_s[:, de:2 * de] = wu_ref[0].astype(BF16)
        wd_s[...] = wd_ref[0].astype(BF16)

    n_valid = nvalid_ref[i]

    @pl.when(n_valid > 0)
    def _():
        words = xs_ref[...]
        rows = lax.broadcasted_iota(I32, words.shape, 0)
        xb = _unpack_bf16_pairs(jnp.where(rows < n_valid, words, 0)).astype(BF16)
        ab = jnp.dot(xb, wgu_s[...], preferred_element_type=F32)
        a = ab[:, 0:de]
        b = ab[:, de:2 * de]
        hmid = (a * _sigmoid(a)) * b
        y = jnp.dot(hmid.astype(BF16), wd_s[...], preferred_element_type=F32)
        ys_ref[...] = _pack_bf16_pairs(y)


def _experts(blocks, xs, w_gate, w_up, w_down):
    n_slots, width = xs.shape
    _, dm, de = w_gate.shape
    n_blocks = blocks[0].shape[0]
    blk = EXPERT_BLOCK
    assert n_slots == n_blocks * blk
    return pl.pallas_call(
        functools.partial(_experts_kernel, de=de),
        grid_spec=pltpu.PrefetchScalarGridSpec(
            num_scalar_prefetch=4,
            grid=(n_blocks,),
            in_specs=[pl.BlockSpec((blk, width), lambda i, b, e, nv, n: (b[i], 0)),
                      pl.BlockSpec((1, dm, de), lambda i, b, e, nv, n: (e[i], 0, 0)),
                      pl.BlockSpec((1, dm, de), lambda i, b, e, nv, n: (e[i], 0, 0)),
                      pl.BlockSpec((1, de, dm), lambda i, b, e, nv, n: (e[i], 0, 0))],
            out_specs=pl.BlockSpec((blk, width), lambda i, b, e, nv, n: (b[i], 0)),
            scratch_shapes=[pltpu.VMEM((dm, 2 * de), BF16), pltpu.VMEM((de, dm), BF16)]),
        out_shape=jax.ShapeDtypeStruct((n_slots, width), I32),
        compiler_params=pltpu.CompilerParams(dimension_semantics=("arbitrary",),
                                             vmem_limit_bytes=VMEM_LIMIT),
        name="experts",
    )(*blocks, xs, w_gate, w_up, w_down)


def _pick(onehot, table):
    return jnp.sum(jnp.where(onehot, table[None, :], 0), axis=1).astype(I32)


def _expert_blocks(counts, n_assign):
    blk = EXPERT_BLOCK
    n_blocks = n_assign // blk + N_EXPERTS
    per_e = (counts + blk - 1) // blk
    blk_end = jnp.cumsum(per_e)
    blk_start = blk_end - per_e
    total = blk_end[-1]
    j = jnp.arange(n_blocks, dtype=I32)
    eids = jnp.arange(N_EXPERTS, dtype=I32)
    valid = j < total
    e_raw = jnp.sum((blk_end[None, :] <= j[:, None]).astype(I32), axis=1)
    e_last = jnp.sum(jnp.where(j == total - 1, e_raw, 0))
    e_of = jnp.where(valid, e_raw, e_last).astype(I32)
    onehot = eids[None, :] == e_of[:, None]
    k_in_e = j - _pick(onehot, blk_start)
    n_valid = jnp.where(valid, jnp.clip(_pick(onehot, counts) - k_in_e * blk, 0, blk), 0).astype(I32)
    b_of = jnp.where(valid, j, total - 1).astype(I32)
    prev_e = jnp.concatenate([jnp.full((1,), -1, I32), e_of[:-1]])
    newexp = (valid & (e_of != prev_e)).astype(I32)
    return (blk_start * blk).astype(I32), (b_of, e_of, n_valid, newexp)


def _slot_of(starts, expert, rank):
    onehot = jnp.arange(N_EXPERTS, dtype=I32)[None, :] == expert[:, None]
    return _pick(onehot, starts) + rank


def _combine_kernel(rows_ref, gcol_ref, x1_ref, mod_ref, fg_ref, *rest):
    o_ref = rest[-1]
    gc = gcol_ref[...]
    y = gc[:, 0:1] * _unpack_bf16_pairs(rows_ref[0]) + gc[:, 1:2] * _unpack_bf16_pairs(rows_ref[1])
    x2 = x1_ref[...] + mod_ref[0][5:6] * y
    r = lax.rsqrt(jnp.mean(x2 * x2, axis=-1, keepdims=True) + EPS)
    o_ref[...] = (x2 * r) * fg_ref[...]


def _combine(rows, gcol, x1, mod, final_g, seq, b0, nb, bsz, out_prev):
    dm = x1.shape[1]
    width = rows.shape[2]
    tc = min(COMBINE_TILE, seq)
    nst = seq // tc
    in_specs = [pl.BlockSpec((TOP_K, tc, width), lambda b, s: (0, b * nst + s, 0)),
                pl.BlockSpec((tc, LANES), lambda b, s: ((b + b0) * nst + s, 0)),
                pl.BlockSpec((tc, dm), lambda b, s: ((b + b0) * nst + s, 0)),
                pl.BlockSpec((1, 6, dm), lambda b, s: (b + b0, 0, 0)),
                pl.BlockSpec((1, dm), lambda b, s: (0, 0))]
    args = [rows, gcol, x1, mod, final_g.reshape(1, dm)]
    aliases = {}
    if out_prev is not None:
        in_specs.append(pl.BlockSpec(memory_space=pl.ANY))
        args.append(out_prev)
        aliases = {len(args) - 1: 0}
    return pl.pallas_call(
        _combine_kernel,
        grid=(nb, nst),
        in_specs=in_specs,
        out_specs=pl.BlockSpec((tc, dm), lambda b, s: ((b + b0) * nst + s, 0)),
        out_shape=jax.ShapeDtypeStruct((bsz * seq, dm), F32),
        input_output_aliases=aliases,
        compiler_params=pltpu.CompilerParams(dimension_semantics=("arbitrary", "arbitrary"),
                                             vmem_limit_bytes=VMEM_LIMIT),
        name="combine",
    )(*args)


def _layer(x, c, ada_w, ada_b, norm1_g, w_in, conv_w, conv_b, w_q, w_k, b_igate, b_fgate,
           mlstm_norm_g, mlstm_skip, w_pool, b_pool, pool_scale, w_out, norm2_g,
           w_rg, b_rg, w_re, b_re, w_eg, w_eu, w_ed, out_g):
    bsz, seq, dm = x.shape
    dml = conv_w.shape[1]
    n_tok = bsz * seq
    ts = min(SEQ_TILE, seq)

    mod = _ada(c, ada_w, ada_b).reshape(bsz, 6, dm)

    col_v = dml
    col_o = 2 * dml
    col_i = 3 * dml
    col_p = col_i + 2 * N_HEADS
    w_gate_cols = jnp.pad(w_in[:, col_i:col_p], ((0, 0), (0, LANES - 2 * N_HEADS)))
    w_in_r = jnp.concatenate([w_in[:, :col_v], w_in[:, col_p:], w_in[:, col_v:col_o],
                              w_in[:, col_o:col_i], w_gate_cols], axis=1).astype(BF16)
    gbias = jnp.pad(jnp.concatenate([b_igate, b_fgate]), (0, LANES - 2 * N_HEADS)).reshape(1, LANES)
    wqk = jnp.concatenate([w_q, w_k], axis=-1).astype(BF16)
    wrt = jnp.zeros((LOGIT_ROWS, dm), F32)
    wrt = wrt.at[0:N_GROUPS].set(w_rg.T).at[SUBLANES:SUBLANES + N_EXPERTS].set(w_re.T).astype(BF16)
    rb = jnp.zeros((LOGIT_ROWS,), F32).at[0:N_GROUPS].set(b_rg).at[SUBLANES:SUBLANES + N_EXPERTS].set(b_re)
    rbias = jnp.broadcast_to(rb[:, None], (LOGIT_ROWS, ts))

    mixer_params = (norm1_g.reshape(1, dm), w_in_r, gbias, conv_w, conv_b.reshape(1, dml), wqk,
                    mlstm_norm_g.reshape(1, dml), mlstm_skip.reshape(1, dml), w_pool.astype(BF16),
                    b_pool.reshape(1, dm - dml), pool_scale.reshape(1, dm - dml),
                    w_out.astype(BF16), norm2_g.reshape(1, dm), wrt, rbias)

    x1, h2, lgt = _mixer(x, mod, *mixer_params, 0, bsz)
    idx, gcol, cnt = _route(lgt, bsz, seq)
    counts = cnt[:, 0].astype(I32)
    starts, blocks = _expert_blocks(counts, n_tok * TOP_K)
    dest0 = _slot_of(starts, idx[0], idx[2])
    dest1 = _slot_of(starts, idx[1], idx[3])
    xs = _dispatch(h2, dest0, dest1, blocks[0].shape[0] * EXPERT_BLOCK)
    ys = _experts(blocks, xs, w_eg, w_eu, w_ed)

    nb = bsz // COMBINE_GROUPS
    assert bsz % COMBINE_GROUPS == 0
    out = None
    for b0 in range(0, bsz, nb):
        rows = _collect(ys, dest0, dest1, b0 * seq, nb * seq)
        out = _combine(rows, gcol, x1, mod, out_g, seq, b0, nb, bsz, out)
    return out.reshape(bsz, seq, dm)


def kernel(x, c, ada_w, ada_b, norm1_g, w_in, conv_w, conv_b, w_q, w_k, b_igate, b_fgate, mlstm_norm_g, mlstm_skip, w_pool, b_pool, pool_scale, w_out, norm2_g, w_router_group, b_router_group, w_router_expert, b_router_expert, w_expert_gate, w_expert_up, w_expert_down, final_g):
    depth = ada_w.shape[0]
    assert depth == 1, "the final norm is fused into the last layer's combine kernel"
    l = 0
    return _layer(x, c, ada_w[l], ada_b[l], norm1_g[l], w_in[l], conv_w[l], conv_b[l], w_q[l],
                  w_k[l], b_igate[l], b_fgate[l], mlstm_norm_g[l], mlstm_skip[l], w_pool[l],
                  b_pool[l], pool_scale[l], w_out[l], norm2_g[l], w_router_group[l],
                  b_router_group[l], w_router_expert[l], b_router_expert[l],
                  w_expert_gate[l], w_expert_up[l], w_expert_down[l], final_g)
```

```python
import functools

import jax
import jax.numpy as jnp
from jax import lax
from jax.experimental import pallas as pl
from jax.experimental.pallas import tpu as pltpu
from jax.experimental.pallas import tpu_sc as plsc

F32 = jnp.float32
BF16 = jnp.bfloat16
I32 = jnp.int32
U32 = jnp.uint32

EPS = 1e-6
N_HEADS = 4
HEAD_DIM = 128
CONV_WIDTH = 4
POOL_WINDOWS = (2, 4, 8, 16)
N_GROUPS = 4
EXPERTS_PER_GROUP = 8
N_EXPERTS = N_GROUPS * EXPERTS_PER_GROUP
TOP_K = 2

LANES = 128
SUBLANES = 8
CHUNK = 128
SEQ_TILE = 512
COMBINE_GROUPS = 4
ROUTE_TILE = 512
SC_SCATTER_CHUNK = 64
SC_GATHER_CHUNK = 32
COMBINE_TILE = 512
EXPERT_BLOCK = 512
LOGIT_ROWS = 48
UHIST = 8
PHIST = 16
VMEM_LIMIT = 60 * 1024 * 1024


def _sigmoid(x):
    return 1.0 / (1.0 + jnp.exp(-x))


def _pack_bf16_pairs(x):
    w = x.shape[1] // 2
    half_ulp = jnp.uint32(0x8000)
    hi = lax.bitcast_convert_type(x[:, :w], U32) + half_ulp
    lo = lax.bitcast_convert_type(x[:, w:], U32) + half_ulp
    return lax.bitcast_convert_type((hi & jnp.uint32(0xFFFF0000)) | (lo >> 16), I32)


def _unpack_bf16_pairs(words):
    u = lax.bitcast_convert_type(words, U32)
    hi = lax.bitcast_convert_type(u & jnp.uint32(0xFFFF0000), F32)
    lo = lax.bitcast_convert_type(u << 16, F32)
    return jnp.concatenate([hi, lo], axis=1)


def _ada_kernel(c_ref, w_ref, b_ref, o_ref):
    c = c_ref[...]
    s = c * _sigmoid(c)
    o_ref[...] = jnp.dot(s, w_ref[...], preferred_element_type=F32,
                         precision=lax.Precision.HIGHEST) + b_ref[...]


def _ada(c, ada_w, ada_b):
    bsz, dm = c.shape
    n = ada_w.shape[1]
    tn = 1024
    return pl.pallas_call(
        _ada_kernel,
        grid=(n // tn,),
        in_specs=[pl.BlockSpec((bsz, dm), lambda j: (0, 0)),
                  pl.BlockSpec((dm, tn), lambda j: (0, j)),
                  pl.BlockSpec((1, tn), lambda j: (0, j))],
        out_specs=pl.BlockSpec((bsz, tn), lambda j: (0, j)),
        out_shape=jax.ShapeDtypeStruct((bsz, n), F32),
        compiler_params=pltpu.CompilerParams(dimension_semantics=("arbitrary",),
                                             vmem_limit_bytes=VMEM_LIMIT),
        name="ada",
    )(c, ada_w, ada_b.reshape(1, n))


def _split3(x):
    hi = x.astype(BF16)
    r1 = x - hi.astype(F32)
    mid = r1.astype(BF16)
    lo = (r1 - mid.astype(F32)).astype(BF16)
    return hi, mid, lo


def _mixer_kernel(x_ref, mod_ref, g1_ref, win_ref, gbias_ref, convw_ref, convb_ref, wqk_ref,
                  ng_ref, skip_ref, wpool_ref, bpool_ref, pscale_ref, wout_ref, g2_ref,
                  wrt_ref, rbias_ref,
                  x1_ref, h2_ref, lgt_ref,
                  uext_s, pext_s, proj_s, uc_s, qk_s, gate_s, mix_s, pool4_s, ctv_s, ctn_s, mprev_s,
                  *, ts, dm, dml):
    s_idx = pl.program_id(1)
    n_chunks = ts // CHUNK
    dp = dm - dml

    @pl.when(s_idx == 0)
    def _():
        uext_s[0:UHIST, :] = jnp.zeros((UHIST, dml), F32)
        pext_s[0:PHIST, :] = jnp.zeros((PHIST, dp), F32)
        ctv_s[...] = jnp.zeros_like(ctv_s)
        ctn_s[...] = jnp.zeros_like(ctn_s)
        mprev_s[...] = jnp.zeros_like(mprev_s)

    row_i = lax.broadcasted_iota(I32, (CHUNK, CHUNK), 0)
    col_i = lax.broadcasted_iota(I32, (CHUNK, CHUNK), 1)
    causal = row_i >= col_i
    tril = jnp.where(causal, 1.0, 0.0).astype(BF16)
    lane_c = lax.broadcasted_iota(I32, (CHUNK, LANES), 1)
    ones_blk = jnp.ones((CHUNK, HEAD_DIM), BF16)
    q_scale = HEAD_DIM ** -0.5
    t_glob = lax.broadcasted_iota(I32, (ts, LANES), 0) + s_idx * ts + 1

    col_o = dml
    col_g = 2 * dml

    x = x_ref[0]
    mod = mod_ref[0]
    r = lax.rsqrt(jnp.mean(x * x, axis=-1, keepdims=True) + EPS)
    h = (x * r) * (g1_ref[...] * (1.0 + mod[1:2])) + mod[0:1]
    res = jnp.dot(h.astype(BF16), win_ref[...], preferred_element_type=F32)
    uext_s[UHIST:, :] = res[:, 0:dml]
    pext_s[PHIST:, :] = res[:, dml:dm]
    proj_s[...] = res[:, dm:]

    acc = None
    for j in range(CONV_WIDTH):
        tap = uext_s[pl.ds(UHIST - (CONV_WIDTH - 1 - j), ts), :] * convw_ref[j:j + 1, :]
        acc = tap if acc is None else acc + tap
    conv = acc + convb_ref[...]
    uc = conv * _sigmoid(conv)
    uc_s[...] = uc

    for hd in range(N_HEADS):
        qk_s[:, 2 * HEAD_DIM * hd:2 * HEAD_DIM * (hd + 1)] = jnp.dot(
            uc[:, HEAD_DIM * hd:HEAD_DIM * (hd + 1)].astype(BF16), wqk_ref[hd],
            preferred_element_type=F32)

    g = proj_s[:, col_g:col_g + LANES] + gbias_ref[...]
    lane = lax.broadcasted_iota(I32, (ts, LANES), 1)
    logf = -(jnp.maximum(-g, 0.0) + jnp.log1p(jnp.exp(-jnp.abs(g))))
    gate_s[...] = jnp.where(lane < N_HEADS, g, logf)

    pairs = [(c, hd) for c in range(n_chunks) for hd in range(N_HEADS)]
    bcums, comb_ts = [], []
    for c in range(n_chunks):
        gc = gate_s[pl.ds(c * CHUNK, CHUNK), :]
        hi, mid, lo = _split3(gc)
        cs = jnp.dot(tril, jnp.concatenate([hi, mid, lo], axis=1), preferred_element_type=F32)
        bcum = cs[:, 0:LANES] + cs[:, LANES:2 * LANES] + cs[:, 2 * LANES:3 * LANES]
        bcums.append(bcum)
        comb_ts.append(jnp.where(lane_c < N_HEADS, gc, bcum).T)

    def rows(ref, c, lo_col, width=HEAD_DIM):
        return ref[pl.ds(c * CHUNK, CHUNK), lo_col:lo_col + width]

    b_bcs, dlogs, rmaxs, p_mats = {}, {}, {}, {}
    for c, hd in pairs:
        b_bc = jnp.broadcast_to(bcums[c][:, N_HEADS + hd:N_HEADS + hd + 1], (CHUNK, CHUNK))
        i_row = comb_ts[c][hd:hd + 1, :]
        b_row = comb_ts[c][N_HEADS + hd:N_HEADS + hd + 1, :]
        dlog = jnp.where(causal, (b_bc - b_row) + i_row, -jnp.inf)
        b_bcs[c, hd], dlogs[c, hd] = b_bc, dlog
        rmaxs[c, hd] = jnp.max(dlog, axis=-1, keepdims=True)
        q_c = rows(qk_s, c, 2 * HEAD_DIM * hd) * q_scale
        k_c = rows(qk_s, c, 2 * HEAD_DIM * hd + HEAD_DIM)
        p_mats[c, hd] = lax.dot_general(q_c.astype(BF16), k_c.astype(BF16),
                                        (((1,), (1,)), ((), ())), preferred_element_type=F32)

    inters, m_ts = {}, {}
    for hd in range(N_HEADS):
        m_prev = mprev_s[hd]
        for c in range(n_chunks):
            inter = b_bcs[c, hd] + m_prev
            m_t = jnp.maximum(inter, rmaxs[c, hd])
            inters[c, hd], m_ts[c, hd] = inter, m_t
            m_prev = jnp.broadcast_to(m_t[CHUNK - 1:CHUNK, :], (CHUNK, LANES))
        mprev_s[hd] = m_prev

    lhs, v_augs, upds, a_prevs, e_negms = {}, {}, {}, {}, {}
    for c, hd in pairs:
        m_t = m_ts[c, hd]
        wm = jnp.exp(dlogs[c, hd] - m_t)
        a_inter = jnp.exp(inters[c, hd] - m_t)
        e_negms[c, hd] = jnp.exp(-m_t)
        q_c = rows(qk_s, c, 2 * HEAD_DIM * hd) * q_scale
        k_c = rows(qk_s, c, 2 * HEAD_DIM * hd + HEAD_DIM)
        v_c = rows(proj_s, c, HEAD_DIM * hd)
        s_mat = (p_mats[c, hd] * wm).astype(BF16)
        qa = (q_c * a_inter).astype(BF16)
        lhs[c, hd] = jnp.concatenate([s_mat, qa], axis=1)
        v_aug = jnp.concatenate([v_c.astype(BF16), ones_blk], axis=1)
        v_augs[c, hd] = v_aug
        ktw = (k_c.T * wm[CHUNK - 1:CHUNK, :]).astype(BF16)
        upds[c, hd] = jnp.dot(ktw, v_aug, preferred_element_type=F32)
        a_prevs[c, hd] = a_inter[CHUNK - 1:CHUNK, :]

    ct_in = {}
    for hd in range(N_HEADS):
        ctv, ctn = ctv_s[hd], ctn_s[hd]
        for c in range(n_chunks):
            ct_in[c, hd] = jnp.concatenate([ctv, ctn], axis=1).astype(BF16)
            ctv = a_prevs[c, hd] * ctv + upds[c, hd][:, 0:HEAD_DIM]
            ctn = a_prevs[c, hd] * ctn + upds[c, hd][:, HEAD_DIM:2 * HEAD_DIM]
        ctv_s[hd], ctn_s[hd] = ctv, ctn

    for c, hd in pairs:
        numden = jnp.dot(lhs[c, hd], jnp.concatenate([v_augs[c, hd], ct_in[c, hd]], axis=0),
                         preferred_element_type=F32)
        num = numden[:, 0:HEAD_DIM]
        den = numden[:, HEAD_DIM:2 * HEAD_DIM]
        hh = num / jnp.maximum(jnp.abs(den), e_negms[c, hd])
        ms = jnp.mean(hh * hh, axis=-1, keepdims=True)
        hn = hh * lax.rsqrt(ms + EPS) * ng_ref[:, HEAD_DIM * hd:HEAD_DIM * (hd + 1)]
        o_c = rows(proj_s, c, col_o + HEAD_DIM * hd)
        uc_c = rows(uc_s, c, HEAD_DIM * hd)
        out_c = _sigmoid(o_c) * (hn + skip_ref[:, HEAD_DIM * hd:HEAD_DIM * (hd + 1)] * uc_c)
        mix_s[pl.ds(c * CHUNK, CHUNK), HEAD_DIM * hd:HEAD_DIM * (hd + 1)] = out_c.astype(BF16)

    def pe(shift, rows, lanes):
        return pext_s[pl.ds(PHIST - shift, rows), lanes]

    gd = LANES
    sums = []
    for gi in range(2):
        lanes = slice(gd * gi, gd * (gi + 1))
        tot = pe(0, ts, lanes)
        for j in range(1, POOL_WINDOWS[gi]):
            tot = tot + pe(j, ts, lanes)
        sums.append(tot)
    wide = slice(2 * gd, 4 * gd)
    s4 = pe(12, ts + 12, wide)
    for j in range(1, 4):
        s4 = s4 + pe(12 + j, ts + 12, wide)
    pool4_s[0:ts + 12, :] = s4
    s8 = pool4_s[pl.ds(4, ts + 8), :] + pool4_s[pl.ds(0, ts + 8), :]
    sums.append(s8[8:, 0:gd])
    sums.append(s8[8:, gd:2 * gd] + s8[0:ts, gd:2 * gd])
    for gi, win in enumerate(POOL_WINDOWS):
        lanes = slice(gd * gi, gd * (gi + 1))
        cnt = jnp.minimum(t_glob, win).astype(F32)
        pooled = sums[gi] / cnt - pe(0, ts, lanes)
        yp = jnp.dot(pooled.astype(BF16), wpool_ref[gi], preferred_element_type=F32)
        yp = (yp + bpool_ref[:, lanes]) * pscale_ref[:, lanes]
        mix_s[:, dml + gd * gi:dml + gd * (gi + 1)] = yp.astype(BF16)

    mix = jnp.dot(mix_s[...], wout_ref[...], preferred_element_type=F32)
    x1 = x + mod[2:3] * mix
    x1_ref[...] = x1
    r2 = lax.rsqrt(jnp.mean(x1 * x1, axis=-1, keepdims=True) + EPS)
    h2 = (x1 * r2) * (g2_ref[...] * (1.0 + mod[4:5])) + mod[3:4]
    h2_ref[...] = _pack_bf16_pairs(h2)
    lgt_ref[...] = lax.dot_general(wrt_ref[...], h2.astype(BF16), (((1,), (1,)), ((), ())),
                                   preferred_element_type=F32) + rbias_ref[...]

    uext_s[0:UHIST, :] = uext_s[ts:ts + UHIST, :]
    pext_s[0:PHIST, :] = pext_s[ts:ts + PHIST, :]


def _mixer(x, mod, g1, w_in_r, gbias, conv_w, conv_b, wqk, ng, skip, wpool, bpool, pscale,
           w_out, g2, wrt, rbias, b0, nb):
    _, seq, dm = x.shape
    dml = conv_w.shape[1]
    ts = min(SEQ_TILE, seq)
    ncols = w_in_r.shape[1]
    nst = seq // ts
    n_tok = nb * seq
    assert seq % ts == 0 and ts % CHUNK == 0
    full = lambda a: pl.BlockSpec(a.shape, lambda b, s: (0,) * a.ndim)
    kern = functools.partial(_mixer_kernel, ts=ts, dm=dm, dml=dml)
    return pl.pallas_call(
        kern,
        grid=(nb, nst),
        in_specs=[pl.BlockSpec((1, ts, dm), lambda b, s: (b + b0, s, 0)),
                  pl.BlockSpec((1, 6, dm), lambda b, s: (b + b0, 0, 0)),
                  full(g1), full(w_in_r), full(gbias), full(conv_w), full(conv_b), full(wqk),
                  full(ng), full(skip), full(wpool), full(bpool), full(pscale), full(w_out),
                  full(g2), full(wrt), full(rbias)],
        out_specs=[pl.BlockSpec((ts, dm), lambda b, s: (b * nst + s, 0)),
                   pl.BlockSpec((ts, dm // 2), lambda b, s: (b * nst + s, 0)),
                   pl.BlockSpec((LOGIT_ROWS, ts), lambda b, s: (0, b * nst + s))],
        out_shape=[jax.ShapeDtypeStruct((n_tok, dm), F32),
                   jax.ShapeDtypeStruct((n_tok, dm // 2), I32),
                   jax.ShapeDtypeStruct((LOGIT_ROWS, n_tok), F32)],
        scratch_shapes=[pltpu.VMEM((UHIST + ts, dml), F32),
                        pltpu.VMEM((PHIST + ts, dm - dml), F32),
                        pltpu.VMEM((ts, ncols - dm), F32),
                        pltpu.VMEM((ts, dml), F32),
                        pltpu.VMEM((ts, 2 * dml), F32),
                        pltpu.VMEM((ts, LANES), F32),
                        pltpu.VMEM((ts, dm), BF16),
                        pltpu.VMEM((ts + PHIST, 2 * LANES), F32),
                        pltpu.VMEM((N_HEADS, HEAD_DIM, HEAD_DIM), F32),
                        pltpu.VMEM((N_HEADS, HEAD_DIM, HEAD_DIM), F32),
                        pltpu.VMEM((N_HEADS, CHUNK, LANES), F32)],
        compiler_params=pltpu.CompilerParams(dimension_semantics=("arbitrary", "arbitrary"),
                                             vmem_limit_bytes=VMEM_LIMIT),
        name="mixer",
    )(x, mod, g1, w_in_r, gbias, conv_w, conv_b, wqk, ng, skip, wpool, bpool, pscale, w_out, g2,
      wrt, rbias)


def _route_kernel(lgt_ref, idx_ref, gcol_ref, cnt_ref, carry_s, *, tr):
    first = (pl.program_id(0) == 0) & (pl.program_id(1) == 0)

    @pl.when(first)
    def _():
        carry_s[...] = jnp.zeros_like(carry_s)

    lg = lgt_ref[...]
    best = lg[0:1]
    gidx = jnp.zeros((1, tr), I32)
    for j in range(1, N_GROUPS):
        cand = lg[j:j + 1]
        better = cand > best
        gidx = jnp.where(better, j, gidx)
        best = jnp.where(better, cand, best)
    sumexp = jnp.zeros((1, tr), F32)
    for j in range(N_GROUPS):
        sumexp = sumexp + jnp.exp(lg[j:j + 1] - best)
    g_gate = 1.0 / sumexp

    sel = lg[SUBLANES:2 * SUBLANES]
    for j in range(1, N_GROUPS):
        sel = jnp.where(gidx == j, lg[SUBLANES * (j + 1):SUBLANES * (j + 2)], sel)
    sub = lax.broadcasted_iota(I32, (EXPERTS_PER_GROUP, tr), 0)
    v1 = jnp.max(sel, axis=0, keepdims=True)
    i1 = jnp.min(jnp.where(sel == v1, sub, EXPERTS_PER_GROUP), axis=0, keepdims=True)
    sel2 = jnp.where(sub == i1, -jnp.inf, sel)
    v2 = jnp.max(sel2, axis=0, keepdims=True)
    i2 = jnp.min(jnp.where(sel2 == v2, sub, EXPERTS_PER_GROUP), axis=0, keepdims=True)
    e2 = jnp.exp(v2 - v1)
    den = 1.0 + e2
    gate0 = (1.0 / den) * g_gate
    gate1 = (e2 / den) * g_gate
    ex0 = gidx * EXPERTS_PER_GROUP + i1
    ex1 = gidx * EXPERTS_PER_GROUP + i2

    erow = lax.broadcasted_iota(I32, (N_EXPERTS, tr), 0)
    oh0 = erow == ex0
    oh1 = erow == ex1
    oh = jnp.where(oh0 | oh1, 1.0, 0.0).astype(BF16)
    tr_r = lax.broadcasted_iota(I32, (tr, tr), 0)
    tr_c = lax.broadcasted_iota(I32, (tr, tr), 1)
    upper = jnp.where(tr_r < tr_c, 1.0, 0.0).astype(BF16)
    carry = carry_s[...]
    before = jnp.dot(oh, upper, preferred_element_type=F32)
    before = before + jnp.concatenate([carry] * (tr // LANES), axis=1)
    rank0 = jnp.sum(jnp.where(oh0, before, 0.0), axis=0, keepdims=True)
    rank1 = jnp.sum(jnp.where(oh1, before, 0.0), axis=0, keepdims=True)
    carry = carry + jnp.dot(oh, jnp.ones((tr, LANES), BF16), preferred_element_type=F32)
    carry_s[...] = carry
    cnt_ref[...] = carry

    zrow = jnp.zeros((SUBLANES - 4, tr), I32)
    idx_ref[...] = jnp.concatenate([ex0, ex1, rank0.astype(I32), rank1.astype(I32), zrow], axis=0)

    for q in range(tr // LANES):
        tile = jnp.concatenate([gate0[:, q * LANES:(q + 1) * LANES],
                                gate1[:, q * LANES:(q + 1) * LANES],
                                jnp.zeros((LANES - 2, LANES), F32)], axis=0)
        gcol_ref[q * LANES:(q + 1) * LANES, :] = tile.T


def _route(lgt, bsz, seq):
    tr = min(ROUTE_TILE, seq)
    nst = seq // tr
    return pl.pallas_call(
        functools.partial(_route_kernel, tr=tr),
        grid=(bsz, nst),
        in_specs=[pl.BlockSpec((LOGIT_ROWS, tr), lambda b, s: (0, b * nst + s))],
        out_specs=[pl.BlockSpec((SUBLANES, tr), lambda b, s: (0, b * nst + s)),
                   pl.BlockSpec((tr, LANES), lambda b, s: (b * nst + s, 0)),
                   pl.BlockSpec((N_EXPERTS, LANES), lambda b, s: (0, 0))],
        out_shape=[jax.ShapeDtypeStruct((SUBLANES, bsz * seq), I32),
                   jax.ShapeDtypeStruct((bsz * seq, LANES), F32),
                   jax.ShapeDtypeStruct((N_EXPERTS, LANES), F32)],
        scratch_shapes=[pltpu.VMEM((N_EXPERTS, LANES), F32)],
        compiler_params=pltpu.CompilerParams(dimension_semantics=("arbitrary", "arbitrary"),
                                             vmem_limit_bytes=VMEM_LIMIT),
        name="route",
    )(lgt)


def _sc_workers():
    info = plsc.get_sparse_core_info()
    return info.num_cores, info.num_cores * info.num_subcores


def _dispatch(h2p, dest0, dest1, n_slots):
    n_tok, width = h2p.shape
    n_cores, n_workers = _sc_workers()
    per_w = n_tok // n_workers
    ch = min(SC_SCATTER_CHUNK, per_w)
    n_ch = per_w // ch
    assert n_tok % n_workers == 0 and per_w % ch == 0 and ch % SUBLANES == 0 and n_ch % 2 == 0
    mesh = plsc.VectorSubcoreMesh(core_axis_name="c", subcore_axis_name="s")

    @functools.partial(
        pl.kernel, mesh=mesh,
        out_type=jax.ShapeDtypeStruct((n_slots, width), h2p.dtype),
        scratch_types=[pltpu.VMEM((2, ch), I32), pltpu.VMEM((2, ch), I32),
                       pltpu.VMEM((2, ch, width), h2p.dtype),
                       pltpu.SemaphoreType.DMA((2,)), pltpu.SemaphoreType.DMA((2,))],
        name="dispatch")
    def scatter(h_hbm, d0_hbm, d1_hbm, xs_hbm, i0_v, i1_v, rows_v, sem_in, sem_out):
        wid = lax.axis_index("s") * n_cores + lax.axis_index("c")
        base = wid * per_w

        def loads(t0, slot):
            return (pltpu.make_async_copy(d0_hbm.at[pl.ds(t0, ch)], i0_v.at[slot], sem_in.at[slot]),
                    pltpu.make_async_copy(d1_hbm.at[pl.ds(t0, ch)], i1_v.at[slot], sem_in.at[slot]),
                    pltpu.make_async_copy(h_hbm.at[pl.ds(t0, ch)], rows_v.at[slot], sem_in.at[slot]))

        def scatters(slot):
            return (pltpu.make_async_copy(rows_v.at[slot], xs_hbm.at[i0_v.at[slot]], sem_out.at[slot]),
                    pltpu.make_async_copy(rows_v.at[slot], xs_hbm.at[i1_v.at[slot]], sem_out.at[slot]))

        for cp in loads(base, 0):
            cp.start()

        @pl.loop(0, n_ch, step=2)
        def _(k):
            for slot in range(2):
                for cp in loads(base + (k + slot) * ch, slot):
                    cp.wait()
                out_cps = scatters(slot)
                for cp in out_cps:
                    cp.start()
                nxt = k + slot + 1

                @pl.when(nxt < n_ch)
                def _():
                    for cp in loads(base + nxt * ch, 1 - slot):
                        cp.start()

                for cp in out_cps:
                    cp.wait()

    return scatter(h2p, dest0, dest1)


def _collect(ys, dest0, dest1, tok0, n_tok):
    width = ys.shape[1]
    n_cores, n_workers = _sc_workers()
    per_w = n_tok // n_workers
    ch = min(SC_GATHER_CHUNK, per_w)
    n_ch = per_w // ch
    assert n_tok % n_workers == 0 and per_w % ch == 0 and ch % SUBLANES == 0 and n_ch % 2 == 0
    mesh = plsc.VectorSubcoreMesh(core_axis_name="c", subcore_axis_name="s")

    @functools.partial(
        pl.kernel, mesh=mesh,
        out_type=jax.ShapeDtypeStruct((TOP_K, n_tok, width), ys.dtype),
        scratch_types=[pltpu.VMEM((2, ch), I32), pltpu.VMEM((2, ch), I32),
                       pltpu.VMEM((2, ch, width), ys.dtype), pltpu.VMEM((2, ch, width), ys.dtype),
                       pltpu.SemaphoreType.DMA((2,)), pltpu.SemaphoreType.DMA((2,)),
                       pltpu.SemaphoreType.DMA((2,))],
        name="collect")
    def gather(ys_hbm, d0_hbm, d1_hbm, o_hbm, i0_v, i1_v, r0_v, r1_v, sem_idx, sem_in, sem_out):
        wid = lax.axis_index("s") * n_cores + lax.axis_index("c")
        base = wid * per_w

        def idx_loads(t0, slot):
            return (pltpu.make_async_copy(d0_hbm.at[pl.ds(tok0 + t0, ch)], i0_v.at[slot], sem_idx.at[slot]),
                    pltpu.make_async_copy(d1_hbm.at[pl.ds(tok0 + t0, ch)], i1_v.at[slot], sem_idx.at[slot]))

        def gathers(slot):
            return (pltpu.make_async_copy(ys_hbm.at[i0_v.at[slot]], r0_v.at[slot], sem_in.at[slot]),
                    pltpu.make_async_copy(ys_hbm.at[i1_v.at[slot]], r1_v.at[slot], sem_in.at[slot]))

        def stores(t0, slot):
            return (pltpu.make_async_copy(r0_v.at[slot], o_hbm.at[0, pl.ds(t0, ch)], sem_out.at[slot]),
                    pltpu.make_async_copy(r1_v.at[slot], o_hbm.at[1, pl.ds(t0, ch)], sem_out.at[slot]))

        def start_chunk(t0, slot):
            for cp in idx_loads(t0, slot):
                cp.start()
            for cp in idx_loads(t0, slot):
                cp.wait()
            for cp in gathers(slot):
                cp.start()

        start_chunk(base, 0)

        @pl.loop(0, n_ch, step=2)
        def _(k):
            for slot in range(2):
                nxt = k + slot + 1

                @pl.when(nxt < n_ch)
                def _():
                    start_chunk(base + nxt * ch, 1 - slot)

                for cp in gathers(slot):
                    cp.wait()
                out_cps = stores(base + (k + slot) * ch, slot)
                for cp in out_cps:
                    cp.start()
                for cp in out_cps:
                    cp.wait()

    return gather(ys, dest0, dest1)


def _experts_kernel(first_ref, nblk_ref, cnt_ref, xs_hbm, wg_ref, wu_ref, wd_ref, ys_hbm,
                    wgu_s, wd_s, xbuf, ybuf, sem_in, sem_out, *, de, blk):
    e = pl.program_id(0)
    wgu_s[:, 0:de] = wg_ref[0].astype(BF16)
    wgu_s[:, de:2 * de] = wu_ref[0].astype(BF16)
    wd_s[...] = wd_ref[0].astype(BF16)
    first = first_ref[e]
    n_blk = nblk_ref[e]
    count = cnt_ref[e]

    def row0(j):
        return pl.multiple_of((first + j) * blk, blk)

    def in_copy(j, slot):
        return pltpu.make_async_copy(xs_hbm.at[pl.ds(row0(j), blk)], xbuf.at[slot], sem_in.at[slot])

    def out_copy(j, slot):
        return pltpu.make_async_copy(ybuf.at[slot], ys_hbm.at[pl.ds(row0(j), blk)], sem_out.at[slot])

    @pl.when(n_blk > 0)
    def _():
        in_copy(0, 0).start()

    def block(j, carry):
        slot = lax.rem(j, 2)
        in_copy(j, slot).wait()

        @pl.when(j + 1 < n_blk)
        def _():
            in_copy(j + 1, 1 - slot).start()

        words = xbuf[slot]
        rows = lax.broadcasted_iota(I32, words.shape, 0)
        xb = _unpack_bf16_pairs(jnp.where(rows < count - j * blk, words, 0)).astype(BF16)
        ab = jnp.dot(xb, wgu_s[...], preferred_element_type=F32)
        a = ab[:, 0:de]
        b = ab[:, de:2 * de]
        hmid = (a * _sigmoid(a)) * b
        y = jnp.dot(hmid.astype(BF16), wd_s[...], preferred_element_type=F32)

        @pl.when(j >= 2)
        def _():
            out_copy(j - 2, slot).wait()

        ybuf[slot] = _pack_bf16_pairs(y)
        out_copy(j, slot).start()
        return carry

    lax.fori_loop(0, n_blk, block, 0)

    @pl.when(n_blk >= 2)
    def _():
        out_copy(n_blk - 2, lax.rem(n_blk, 2)).wait()

    @pl.when(n_blk >= 1)
    def _():
        out_copy(n_blk - 1, lax.rem(n_blk + 1, 2)).wait()


def _experts(plan, xs, w_gate, w_up, w_down):
    n_slots, width = xs.shape
    n_exp, dm, de = w_gate.shape
    blk = EXPERT_BLOCK
    return pl.pallas_call(
        functools.partial(_experts_kernel, de=de, blk=blk),
        grid_spec=pltpu.PrefetchScalarGridSpec(
            num_scalar_prefetch=3,
            grid=(n_exp,),
            in_specs=[pl.BlockSpec(memory_space=pl.ANY),
                      pl.BlockSpec((1, dm, de), lambda e, f, n, c: (e, 0, 0)),
                      pl.BlockSpec((1, dm, de), lambda e, f, n, c: (e, 0, 0)),
                      pl.BlockSpec((1, de, dm), lambda e, f, n, c: (e, 0, 0))],
            out_specs=pl.BlockSpec(memory_space=pl.ANY),
            scratch_shapes=[pltpu.VMEM((dm, 2 * de), BF16), pltpu.VMEM((de, dm), BF16),
                            pltpu.VMEM((2, blk, width), I32), pltpu.VMEM((2, blk, width), I32),
                            pltpu.SemaphoreType.DMA((2,)), pltpu.SemaphoreType.DMA((2,))]),
        out_shape=jax.ShapeDtypeStruct((n_slots, width), I32),
        compiler_params=pltpu.CompilerParams(dimension_semantics=("arbitrary",),
                                             vmem_limit_bytes=VMEM_LIMIT),
        name="experts",
    )(*plan, xs, w_gate, w_up, w_down)


def _pick(onehot, table):
    return jnp.sum(jnp.where(onehot, table[None, :], 0), axis=1).astype(I32)


def _expert_plan(counts, n_assign):
    blk = EXPERT_BLOCK
    per_e = (counts + blk - 1) // blk
    first_blk = jnp.cumsum(per_e) - per_e
    n_slots = (n_assign // blk + N_EXPERTS) * blk
    return (first_blk * blk).astype(I32), (first_blk.astype(I32), per_e.astype(I32), counts), n_slots


def _slot_of(starts, expert, rank):
    onehot = jnp.arange(N_EXPERTS, dtype=I32)[None, :] == expert[:, None]
    return _pick(onehot, starts) + rank


def _combine_kernel(rows_ref, gcol_ref, x1_ref, mod_ref, fg_ref, *rest):
    o_ref = rest[-1]
    gc = gcol_ref[...]
    y = gc[:, 0:1] * _unpack_bf16_pairs(rows_ref[0]) + gc[:, 1:2] * _unpack_bf16_pairs(rows_ref[1])
    x2 = x1_ref[...] + mod_ref[0][5:6] * y
    r = lax.rsqrt(jnp.mean(x2 * x2, axis=-1, keepdims=True) + EPS)
    o_ref[...] = (x2 * r) * fg_ref[...]


def _combine(rows, gcol, x1, mod, final_g, seq, b0, nb, bsz, out_prev):
    dm = x1.shape[1]
    width = rows.shape[2]
    tc = min(COMBINE_TILE, seq)
    nst = seq // tc
    in_specs = [pl.BlockSpec((TOP_K, tc, width), lambda b, s: (0, b * nst + s, 0)),
                pl.BlockSpec((tc, LANES), lambda b, s: ((b + b0) * nst + s, 0)),
                pl.BlockSpec((tc, dm), lambda b, s: ((b + b0) * nst + s, 0)),
                pl.BlockSpec((1, 6, dm), lambda b, s: (b + b0, 0, 0)),
                pl.BlockSpec((1, dm), lambda b, s: (0, 0))]
    args = [rows, gcol, x1, mod, final_g.reshape(1, dm)]
    aliases = {}
    if out_prev is not None:
        in_specs.append(pl.BlockSpec(memory_space=pl.ANY))
        args.append(out_prev)
        aliases = {len(args) - 1: 0}
    return pl.pallas_call(
        _combine_kernel,
        grid=(nb, nst),
        in_specs=in_specs,
        out_specs=pl.BlockSpec((tc, dm), lambda b, s: ((b + b0) * nst + s, 0)),
        out_shape=jax.ShapeDtypeStruct((bsz * seq, dm), F32),
        input_output_aliases=aliases,
        compiler_params=pltpu.CompilerParams(dimension_semantics=("arbitrary", "arbitrary"),
                                             vmem_limit_bytes=VMEM_LIMIT),
        name="combine",
    )(*args)


def _layer(x, c, ada_w, ada_b, norm1_g, w_in, conv_w, conv_b, w_q, w_k, b_igate, b_fgate,
           mlstm_norm_g, mlstm_skip, w_pool, b_pool, pool_scale, w_out, norm2_g,
           w_rg, b_rg, w_re, b_re, w_eg, w_eu, w_ed, out_g):
    bsz, seq, dm = x.shape
    dml = conv_w.shape[1]
    n_tok = bsz * seq
    ts = min(SEQ_TILE, seq)

    mod = _ada(c, ada_w, ada_b).reshape(bsz, 6, dm)

    col_v = dml
    col_o = 2 * dml
    col_i = 3 * dml
    col_p = col_i + 2 * N_HEADS
    w_gate_cols = jnp.pad(w_in[:, col_i:col_p], ((0, 0), (0, LANES - 2 * N_HEADS)))
    w_in_r = jnp.concatenate([w_in[:, :col_v], w_in[:, col_p:], w_in[:, col_v:col_o],
                              w_in[:, col_o:col_i], w_gate_cols], axis=1).astype(BF16)
    gbias = jnp.pad(jnp.concatenate([b_igate, b_fgate]), (0, LANES - 2 * N_HEADS)).reshape(1, LANES)
    wqk = jnp.concatenate([w_q, w_k], axis=-1).astype(BF16)
    wrt = jnp.zeros((LOGIT_ROWS, dm), F32)
    wrt = wrt.at[0:N_GROUPS].set(w_rg.T).at[SUBLANES:SUBLANES + N_EXPERTS].set(w_re.T).astype(BF16)
    rb = jnp.zeros((LOGIT_ROWS,), F32).at[0:N_GROUPS].set(b_rg).at[SUBLANES:SUBLANES + N_EXPERTS].set(b_re)
    rbias = jnp.broadcast_to(rb[:, None], (LOGIT_ROWS, ts))

    mixer_params = (norm1_g.reshape(1, dm), w_in_r, gbias, conv_w, conv_b.reshape(1, dml), wqk,
                    mlstm_norm_g.reshape(1, dml), mlstm_skip.reshape(1, dml), w_pool.astype(BF16),
                    b_pool.reshape(1, dm - dml), pool_scale.reshape(1, dm - dml),
                    w_out.astype(BF16), norm2_g.reshape(1, dm), wrt, rbias)

    x1, h2, lgt = _mixer(x, mod, *mixer_params, 0, bsz)
    idx, gcol, cnt = _route(lgt, bsz, seq)
    counts = cnt[:, 0].astype(I32)
    starts, plan, n_slots = _expert_plan(counts, n_tok * TOP_K)
    dest0 = _slot_of(starts, idx[0], idx[2])
    dest1 = _slot_of(starts, idx[1], idx[3])
    xs = _dispatch(h2, dest0, dest1, n_slots)
    ys = _experts(plan, xs, w_eg, w_eu, w_ed)

    nb = bsz // COMBINE_GROUPS
    assert bsz % COMBINE_GROUPS == 0
    out = None
    for b0 in range(0, bsz, nb):
        rows = _collect(ys, dest0, dest1, b0 * seq, nb * seq)
        out = _combine(rows, gcol, x1, mod, out_g, seq, b0, nb, bsz, out)
    return out.reshape(bsz, seq, dm)


def kernel(x, c, ada_w, ada_b, norm1_g, w_in, conv_w, conv_b, w_q, w_k, b_igate, b_fgate, mlstm_norm_g, mlstm_skip, w_pool, b_pool, pool_scale, w_out, norm2_g, w_router_group, b_router_group, w_router_expert, b_router_expert, w_expert_gate, w_expert_up, w_expert_down, final_g):
    depth = ada_w.shape[0]
    assert depth == 1, "the final norm is fused into the last layer's combine kernel"
    l = 0
    return _layer(x, c, ada_w[l], ada_b[l], norm1_g[l], w_in[l], conv_w[l], conv_b[l], w_q[l],
                  w_k[l], b_igate[l], b_fgate[l], mlstm_norm_g[l], mlstm_skip[l], w_pool[l],
                  b_pool[l], pool_scale[l], w_out[l], norm2_g[l], w_router_group[l],
                  b_router_group[l], w_router_expert[l], b_router_expert[l],
                  w_expert_gate[l], w_expert_up[l], w_expert_down[l], final_g)
```

```python
import functools

import jax
import jax.numpy as jnp
from jax import lax
from jax.experimental import pallas as pl
from jax.experimental.pallas import tpu as pltpu
from jax.experimental.pallas import tpu_sc as plsc

F32 = jnp.float32
BF16 = jnp.bfloat16
I32 = jnp.int32
U32 = jnp.uint32

EPS = 1e-6
N_HEADS = 4
HEAD_DIM = 128
CONV_WIDTH = 4
POOL_WINDOWS = (2, 4, 8, 16)
N_GROUPS = 4
EXPERTS_PER_GROUP = 8
N_EXPERTS = N_GROUPS * EXPERTS_PER_GROUP
TOP_K = 2

LANES = 128
SUBLANES = 8
CHUNK = 128
SEQ_TILE = 512
COMBINE_GROUPS = 4
ROUTE_TILE = 512
SC_SCATTER_CHUNK = 64
SC_GATHER_CHUNK = 32
COMBINE_TILE = 512
EXPERT_BLOCK = 512
LOGIT_ROWS = 48
UHIST = 8
PHIST = 16
VMEM_LIMIT = 60 * 1024 * 1024


def _sigmoid(x):
    return 1.0 / (1.0 + jnp.exp(-x))


def _pack_bf16_pairs(x):
    w = x.shape[1] // 2
    half_ulp = jnp.uint32(0x8000)
    hi = lax.bitcast_convert_type(x[:, :w], U32) + half_ulp
    lo = lax.bitcast_convert_type(x[:, w:], U32) + half_ulp
    return lax.bitcast_convert_type((hi & jnp.uint32(0xFFFF0000)) | (lo >> 16), I32)


def _unpack_bf16_pairs(words):
    u = lax.bitcast_convert_type(words, U32)
    hi = lax.bitcast_convert_type(u & jnp.uint32(0xFFFF0000), F32)
    lo = lax.bitcast_convert_type(u << 16, F32)
    return jnp.concatenate([hi, lo], axis=1)


def _ada_kernel(c_ref, w_ref, b_ref, o_ref):
    c = c_ref[...]
    s = c * _sigmoid(c)
    o_ref[...] = jnp.dot(s, w_ref[...], preferred_element_type=F32,
                         precision=lax.Precision.HIGHEST) + b_ref[...]


def _ada(c, ada_w, ada_b):
    bsz, dm = c.shape
    n = ada_w.shape[1]
    tn = 1024
    return pl.pallas_call(
        _ada_kernel,
        grid=(n // tn,),
        in_specs=[pl.BlockSpec((bsz, dm), lambda j: (0, 0)),
                  pl.BlockSpec((dm, tn), lambda j: (0, j)),
                  pl.BlockSpec((1, tn), lambda j: (0, j))],
        out_specs=pl.BlockSpec((bsz, tn), lambda j: (0, j)),
        out_shape=jax.ShapeDtypeStruct((bsz, n), F32),
        compiler_params=pltpu.CompilerParams(dimension_semantics=("arbitrary",),
                                             vmem_limit_bytes=VMEM_LIMIT),
        name="ada",
    )(c, ada_w, ada_b.reshape(1, n))


def _split3(x):
    hi = x.astype(BF16)
    r1 = x - hi.astype(F32)
    mid = r1.astype(BF16)
    lo = (r1 - mid.astype(F32)).astype(BF16)
    return hi, mid, lo


def _mixer_kernel(x_ref, mod_ref, g1_ref, win_ref, gbias_ref, convw_ref, convb_ref, wqk_ref,
                  ng_ref, skip_ref, wpool_ref, bpool_ref, pscale_ref, wout_ref, g2_ref,
                  wrt_ref, rbias_ref,
                  x1_ref, h2_ref, lgt_ref,
                  uext_s, pext_s, proj_s, uc_s, qk_s, gate_s, mix_s, pool4_s, ctv_s, ctn_s, mprev_s,
                  *, ts, dm, dml):
    s_idx = pl.program_id(1)
    n_chunks = ts // CHUNK
    dp = dm - dml

    @pl.when(s_idx == 0)
    def _():
        uext_s[0:UHIST, :] = jnp.zeros((UHIST, dml), F32)
        pext_s[0:PHIST, :] = jnp.zeros((PHIST, dp), F32)
        ctv_s[...] = jnp.zeros_like(ctv_s)
        ctn_s[...] = jnp.zeros_like(ctn_s)
        mprev_s[...] = jnp.zeros_like(mprev_s)

    row_i = lax.broadcasted_iota(I32, (CHUNK, CHUNK), 0)
    col_i = lax.broadcasted_iota(I32, (CHUNK, CHUNK), 1)
    causal = row_i >= col_i
    tril = jnp.where(causal, 1.0, 0.0).astype(BF16)
    lane_c = lax.broadcasted_iota(I32, (CHUNK, LANES), 1)
    ones_blk = jnp.ones((CHUNK, HEAD_DIM), BF16)
    q_scale = HEAD_DIM ** -0.5
    t_glob = lax.broadcasted_iota(I32, (ts, LANES), 0) + s_idx * ts + 1

    col_o = dml
    col_g = 2 * dml

    x = x_ref[0]
    mod = mod_ref[0]
    r = lax.rsqrt(jnp.mean(x * x, axis=-1, keepdims=True) + EPS)
    h = (x * r) * (g1_ref[...] * (1.0 + mod[1:2])) + mod[0:1]
    res = jnp.dot(h.astype(BF16), win_ref[...], preferred_element_type=F32)
    uext_s[UHIST:, :] = res[:, 0:dml]
    pext_s[PHIST:, :] = res[:, dml:dm]
    proj_s[...] = res[:, dm:]

    acc = None
    for j in range(CONV_WIDTH):
        tap = uext_s[pl.ds(UHIST - (CONV_WIDTH - 1 - j), ts), :] * convw_ref[j:j + 1, :]
        acc = tap if acc is None else acc + tap
    conv = acc + convb_ref[...]
    uc = conv * _sigmoid(conv)
    uc_s[...] = uc

    for hd in range(N_HEADS):
        qk_s[:, 2 * HEAD_DIM * hd:2 * HEAD_DIM * (hd + 1)] = jnp.dot(
            uc[:, HEAD_DIM * hd:HEAD_DIM * (hd + 1)].astype(BF16), wqk_ref[hd],
            preferred_element_type=F32)

    g = proj_s[:, col_g:col_g + LANES] + gbias_ref[...]
    lane = lax.broadcasted_iota(I32, (ts, LANES), 1)
    logf = -(jnp.maximum(-g, 0.0) + jnp.log1p(jnp.exp(-jnp.abs(g))))
    gate_s[...] = jnp.where(lane < N_HEADS, g, logf)

    pairs = [(c, hd) for c in range(n_chunks) for hd in range(N_HEADS)]
    bcums, comb_ts = [], []
    for c in range(n_chunks):
        gc = gate_s[pl.ds(c * CHUNK, CHUNK), :]
        hi, mid, lo = _split3(gc)
        cs = jnp.dot(tril, jnp.concatenate([hi, mid, lo], axis=1), preferred_element_type=F32)
        bcum = cs[:, 0:LANES] + cs[:, LANES:2 * LANES] + cs[:, 2 * LANES:3 * LANES]
        bcums.append(bcum)
        comb_ts.append(jnp.where(lane_c < N_HEADS, gc, bcum).T)

    def rows(ref, c, lo_col, width=HEAD_DIM):
        return ref[pl.ds(c * CHUNK, CHUNK), lo_col:lo_col + width]

    b_bcs, dlogs, rmaxs, p_mats = {}, {}, {}, {}
    for c, hd in pairs:
        b_bc = jnp.broadcast_to(bcums[c][:, N_HEADS + hd:N_HEADS + hd + 1], (CHUNK, CHUNK))
        i_row = comb_ts[c][hd:hd + 1, :]
        b_row = comb_ts[c][N_HEADS + hd:N_HEADS + hd + 1, :]
        dlog = jnp.where(causal, (b_bc - b_row) + i_row, -jnp.inf)
        b_bcs[c, hd], dlogs[c, hd] = b_bc, dlog
        rmaxs[c, hd] = jnp.max(dlog, axis=-1, keepdims=True)
        q_c = rows(qk_s, c, 2 * HEAD_DIM * hd) * q_scale
        k_c = rows(qk_s, c, 2 * HEAD_DIM * hd + HEAD_DIM)
        p_mats[c, hd] = lax.dot_general(q_c.astype(BF16), k_c.astype(BF16),
                                        (((1,), (1,)), ((), ())), preferred_element_type=F32)

    inters, m_ts = {}, {}
    for hd in range(N_HEADS):
        m_prev = mprev_s[hd]
        for c in range(n_chunks):
            inter = b_bcs[c, hd] + m_prev
            m_t = jnp.maximum(inter, rmaxs[c, hd])
            inters[c, hd], m_ts[c, hd] = inter, m_t
            m_prev = jnp.broadcast_to(m_t[CHUNK - 1:CHUNK, :], (CHUNK, LANES))
        mprev_s[hd] = m_prev

    lhs, v_augs, upds, a_prevs, e_negms = {}, {}, {}, {}, {}
    for c, hd in pairs:
        m_t = m_ts[c, hd]
        wm = jnp.exp(dlogs[c, hd] - m_t)
        a_inter = jnp.exp(inters[c, hd] - m_t)
        e_negms[c, hd] = jnp.exp(-m_t)
        q_c = rows(qk_s, c, 2 * HEAD_DIM * hd) * q_scale
        k_c = rows(qk_s, c, 2 * HEAD_DIM * hd + HEAD_DIM)
        v_c = rows(proj_s, c, HEAD_DIM * hd)
        s_mat = (p_mats[c, hd] * wm).astype(BF16)
        qa = (q_c * a_inter).astype(BF16)
        lhs[c, hd] = jnp.concatenate([s_mat, qa], axis=1)
        v_aug = jnp.concatenate([v_c.astype(BF16), ones_blk], axis=1)
        v_augs[c, hd] = v_aug
        ktw = (k_c.T * wm[CHUNK - 1:CHUNK, :]).astype(BF16)
        upds[c, hd] = jnp.dot(ktw, v_aug, preferred_element_type=F32)
        a_prevs[c, hd] = a_inter[CHUNK - 1:CHUNK, :]

    ct_in = {}
    for hd in range(N_HEADS):
        ctv, ctn = ctv_s[hd], ctn_s[hd]
        for c in range(n_chunks):
            ct_in[c, hd] = jnp.concatenate([ctv, ctn], axis=1).astype(BF16)
            ctv = a_prevs[c, hd] * ctv + upds[c, hd][:, 0:HEAD_DIM]
            ctn = a_prevs[c, hd] * ctn + upds[c, hd][:, HEAD_DIM:2 * HEAD_DIM]
        ctv_s[hd], ctn_s[hd] = ctv, ctn

    for c, hd in pairs:
        numden = jnp.dot(lhs[c, hd], jnp.concatenate([v_augs[c, hd], ct_in[c, hd]], axis=0),
                         preferred_element_type=F32)
        num = numden[:, 0:HEAD_DIM]
        den = numden[:, HEAD_DIM:2 * HEAD_DIM]
        hh = num / jnp.maximum(jnp.abs(den), e_negms[c, hd])
        ms = jnp.mean(hh * hh, axis=-1, keepdims=True)
        hn = hh * lax.rsqrt(ms + EPS) * ng_ref[:, HEAD_DIM * hd:HEAD_DIM * (hd + 1)]
        o_c = rows(proj_s, c, col_o + HEAD_DIM * hd)
        uc_c = rows(uc_s, c, HEAD_DIM * hd)
        out_c = _sigmoid(o_c) * (hn + skip_ref[:, HEAD_DIM * hd:HEAD_DIM * (hd + 1)] * uc_c)
        mix_s[pl.ds(c * CHUNK, CHUNK), HEAD_DIM * hd:HEAD_DIM * (hd + 1)] = out_c.astype(BF16)

    def pe(shift, rows, lanes):
        return pext_s[pl.ds(PHIST - shift, rows), lanes]

    gd = LANES
    sums = []
    for gi in range(2):
        lanes = slice(gd * gi, gd * (gi + 1))
        tot = pe(0, ts, lanes)
        for j in range(1, POOL_WINDOWS[gi]):
            tot = tot + pe(j, ts, lanes)
        sums.append(tot)
    wide = slice(2 * gd, 4 * gd)
    s4 = pe(12, ts + 12, wide)
    for j in range(1, 4):
        s4 = s4 + pe(12 + j, ts + 12, wide)
    pool4_s[0:ts + 12, :] = s4
    s8 = pool4_s[pl.ds(4, ts + 8), :] + pool4_s[pl.ds(0, ts + 8), :]
    sums.append(s8[8:, 0:gd])
    sums.append(s8[8:, gd:2 * gd] + s8[0:ts, gd:2 * gd])
    for gi, win in enumerate(POOL_WINDOWS):
        lanes = slice(gd * gi, gd * (gi + 1))
        cnt = jnp.minimum(t_glob, win).astype(F32)
        pooled = sums[gi] / cnt - pe(0, ts, lanes)
        yp = jnp.dot(pooled.astype(BF16), wpool_ref[gi], preferred_element_type=F32)
        yp = (yp + bpool_ref[:, lanes]) * pscale_ref[:, lanes]
        mix_s[:, dml + gd * gi:dml + gd * (gi + 1)] = yp.astype(BF16)

    mix = jnp.dot(mix_s[...], wout_ref[...], preferred_element_type=F32)
    x1 = x + mod[2:3] * mix
    x1_ref[...] = x1
    r2 = lax.rsqrt(jnp.mean(x1 * x1, axis=-1, keepdims=True) + EPS)
    h2 = (x1 * r2) * (g2_ref[...] * (1.0 + mod[4:5])) + mod[3:4]
    h2_ref[...] = _pack_bf16_pairs(h2)
    lgt_ref[...] = lax.dot_general(wrt_ref[...], h2.astype(BF16), (((1,), (1,)), ((), ())),
                                   preferred_element_type=F32) + rbias_ref[...]

    uext_s[0:UHIST, :] = uext_s[ts:ts + UHIST, :]
    pext_s[0:PHIST, :] = pext_s[ts:ts + PHIST, :]


def _mixer(x, mod, g1, w_in_r, gbias, conv_w, conv_b, wqk, ng, skip, wpool, bpool, pscale,
           w_out, g2, wrt, rbias, b0, nb):
    _, seq, dm = x.shape
    dml = conv_w.shape[1]
    ts = min(SEQ_TILE, seq)
    ncols = w_in_r.shape[1]
    nst = seq // ts
    n_tok = nb * seq
    assert seq % ts == 0 and ts % CHUNK == 0
    full = lambda a: pl.BlockSpec(a.shape, lambda b, s: (0,) * a.ndim)
    kern = functools.partial(_mixer_kernel, ts=ts, dm=dm, dml=dml)
    return pl.pallas_call(
        kern,
        grid=(nb, nst),
        in_specs=[pl.BlockSpec((1, ts, dm), lambda b, s: (b + b0, s, 0)),
                  pl.BlockSpec((1, 6, dm), lambda b, s: (b + b0, 0, 0)),
                  full(g1), full(w_in_r), full(gbias), full(conv_w), full(conv_b), full(wqk),
                  full(ng), full(skip), full(wpool), full(bpool), full(pscale), full(w_out),
                  full(g2), full(wrt), full(rbias)],
        out_specs=[pl.BlockSpec((ts, dm), lambda b, s: (b * nst + s, 0)),
                   pl.BlockSpec((ts, dm // 2), lambda b, s: (b * nst + s, 0)),
                   pl.BlockSpec((LOGIT_ROWS, ts), lambda b, s: (0, b * nst + s))],
        out_shape=[jax.ShapeDtypeStruct((n_tok, dm), F32),
                   jax.ShapeDtypeStruct((n_tok, dm // 2), I32),
                   jax.ShapeDtypeStruct((LOGIT_ROWS, n_tok), F32)],
        scratch_shapes=[pltpu.VMEM((UHIST + ts, dml), F32),
                        pltpu.VMEM((PHIST + ts, dm - dml), F32),
                        pltpu.VMEM((ts, ncols - dm), F32),
                        pltpu.VMEM((ts, dml), F32),
                        pltpu.VMEM((ts, 2 * dml), F32),
                        pltpu.VMEM((ts, LANES), F32),
                        pltpu.VMEM((ts, dm), BF16),
                        pltpu.VMEM((ts + PHIST, 2 * LANES), F32),
                        pltpu.VMEM((N_HEADS, HEAD_DIM, HEAD_DIM), F32),
                        pltpu.VMEM((N_HEADS, HEAD_DIM, HEAD_DIM), F32),
                        pltpu.VMEM((N_HEADS, CHUNK, LANES), F32)],
        compiler_params=pltpu.CompilerParams(dimension_semantics=("arbitrary", "arbitrary"),
                                             vmem_limit_bytes=VMEM_LIMIT),
        name="mixer",
    )(x, mod, g1, w_in_r, gbias, conv_w, conv_b, wqk, ng, skip, wpool, bpool, pscale, w_out, g2,
      wrt, rbias)


def _route_kernel(lgt_ref, idx_ref, gcol_ref, cnt_ref, carry_s, *, tr):
    first = (pl.program_id(0) == 0) & (pl.program_id(1) == 0)

    @pl.when(first)
    def _():
        carry_s[...] = jnp.zeros_like(carry_s)

    lg = lgt_ref[...]
    best = lg[0:1]
    gidx = jnp.zeros((1, tr), I32)
    for j in range(1, N_GROUPS):
        cand = lg[j:j + 1]
        better = cand > best
        gidx = jnp.where(better, j, gidx)
        best = jnp.where(better, cand, best)
    sumexp = jnp.zeros((1, tr), F32)
    for j in range(N_GROUPS):
        sumexp = sumexp + jnp.exp(lg[j:j + 1] - best)
    g_gate = 1.0 / sumexp

    sel = lg[SUBLANES:2 * SUBLANES]
    for j in range(1, N_GROUPS):
        sel = jnp.where(gidx == j, lg[SUBLANES * (j + 1):SUBLANES * (j + 2)], sel)
    sub = lax.broadcasted_iota(I32, (EXPERTS_PER_GROUP, tr), 0)
    v1 = jnp.max(sel, axis=0, keepdims=True)
    i1 = jnp.min(jnp.where(sel == v1, sub, EXPERTS_PER_GROUP), axis=0, keepdims=True)
    sel2 = jnp.where(sub == i1, -jnp.inf, sel)
    v2 = jnp.max(sel2, axis=0, keepdims=True)
    i2 = jnp.min(jnp.where(sel2 == v2, sub, EXPERTS_PER_GROUP), axis=0, keepdims=True)
    e2 = jnp.exp(v2 - v1)
    den = 1.0 + e2
    gate0 = (1.0 / den) * g_gate
    gate1 = (e2 / den) * g_gate
    ex0 = gidx * EXPERTS_PER_GROUP + i1
    ex1 = gidx * EXPERTS_PER_GROUP + i2

    erow = lax.broadcasted_iota(I32, (N_EXPERTS, tr), 0)
    oh0 = erow == ex0
    oh1 = erow == ex1
    oh = jnp.where(oh0 | oh1, 1.0, 0.0).astype(BF16)
    tr_r = lax.broadcasted_iota(I32, (tr, tr), 0)
    tr_c = lax.broadcasted_iota(I32, (tr, tr), 1)
    upper = jnp.where(tr_r < tr_c, 1.0, 0.0).astype(BF16)
    carry = carry_s[...]
    before = jnp.dot(oh, upper, preferred_element_type=F32)
    before = before + jnp.concatenate([carry] * (tr // LANES), axis=1)
    rank0 = jnp.sum(jnp.where(oh0, before, 0.0), axis=0, keepdims=True)
    rank1 = jnp.sum(jnp.where(oh1, before, 0.0), axis=0, keepdims=True)
    carry = carry + jnp.dot(oh, jnp.ones((tr, LANES), BF16), preferred_element_type=F32)
    carry_s[...] = carry
    cnt_ref[...] = carry

    zrow = jnp.zeros((SUBLANES - 4, tr), I32)
    idx_ref[...] = jnp.concatenate([ex0, ex1, rank0.astype(I32), rank1.astype(I32), zrow], axis=0)

    for q in range(tr // LANES):
        tile = jnp.concatenate([gate0[:, q * LANES:(q + 1) * LANES],
                                gate1[:, q * LANES:(q + 1) * LANES],
                                jnp.zeros((LANES - 2, LANES), F32)], axis=0)
        gcol_ref[q * LANES:(q + 1) * LANES, :] = tile.T


def _route(lgt, bsz, seq):
    tr = min(ROUTE_TILE, seq)
    nst = seq // tr
    return pl.pallas_call(
        functools.partial(_route_kernel, tr=tr),
        grid=(bsz, nst),
        in_specs=[pl.BlockSpec((LOGIT_ROWS, tr), lambda b, s: (0, b * nst + s))],
        out_specs=[pl.BlockSpec((SUBLANES, tr), lambda b, s: (0, b * nst + s)),
                   pl.BlockSpec((tr, LANES), lambda b, s: (b * nst + s, 0)),
                   pl.BlockSpec((N_EXPERTS, LANES), lambda b, s: (0, 0))],
        out_shape=[jax.ShapeDtypeStruct((SUBLANES, bsz * seq), I32),
                   jax.ShapeDtypeStruct((bsz * seq, LANES), F32),
                   jax.ShapeDtypeStruct((N_EXPERTS, LANES), F32)],
        scratch_shapes=[pltpu.VMEM((N_EXPERTS, LANES), F32)],
        compiler_params=pltpu.CompilerParams(dimension_semantics=("arbitrary", "arbitrary"),
                                             vmem_limit_bytes=VMEM_LIMIT),
        name="route",
    )(lgt)


def _sc_workers():
    info = plsc.get_sparse_core_info()
    return info.num_cores, info.num_cores * info.num_subcores


def _dispatch(h2p, dest0, dest1, n_slots):
    n_tok, width = h2p.shape
    n_cores, n_workers = _sc_workers()
    per_w = n_tok // n_workers
    ch = min(SC_SCATTER_CHUNK, per_w)
    n_ch = per_w // ch
    assert n_tok % n_workers == 0 and per_w % ch == 0 and ch % SUBLANES == 0 and n_ch % 2 == 0
    mesh = plsc.VectorSubcoreMesh(core_axis_name="c", subcore_axis_name="s")

    @functools.partial(
        pl.kernel, mesh=mesh,
        out_type=jax.ShapeDtypeStruct((n_slots, width), h2p.dtype),
        scratch_types=[pltpu.VMEM((2, ch), I32), pltpu.VMEM((2, ch), I32),
                       pltpu.VMEM((2, ch, width), h2p.dtype),
                       pltpu.SemaphoreType.DMA((2,)), pltpu.SemaphoreType.DMA((2,))],
        name="dispatch")
    def scatter(h_hbm, d0_hbm, d1_hbm, xs_hbm, i0_v, i1_v, rows_v, sem_in, sem_out):
        wid = lax.axis_index("s") * n_cores + lax.axis_index("c")
        base = wid * per_w

        def loads(t0, slot):
            return (pltpu.make_async_copy(d0_hbm.at[pl.ds(t0, ch)], i0_v.at[slot], sem_in.at[slot]),
                    pltpu.make_async_copy(d1_hbm.at[pl.ds(t0, ch)], i1_v.at[slot], sem_in.at[slot]),
                    pltpu.make_async_copy(h_hbm.at[pl.ds(t0, ch)], rows_v.at[slot], sem_in.at[slot]))

        def scatters(slot):
            return (pltpu.make_async_copy(rows_v.at[slot], xs_hbm.at[i0_v.at[slot]], sem_out.at[slot]),
                    pltpu.make_async_copy(rows_v.at[slot], xs_hbm.at[i1_v.at[slot]], sem_out.at[slot]))

        for cp in loads(base, 0):
            cp.start()

        @pl.loop(0, n_ch, step=2)
        def _(k):
            for slot in range(2):
                for cp in loads(base + (k + slot) * ch, slot):
                    cp.wait()
                out_cps = scatters(slot)
                for cp in out_cps:
                    cp.start()
                nxt = k + slot + 1

                @pl.when(nxt < n_ch)
                def _():
                    for cp in loads(base + nxt * ch, 1 - slot):
                        cp.start()

                for cp in out_cps:
                    cp.wait()

    return scatter(h2p, dest0, dest1)


def _collect(ys, dest0, dest1, tok0, n_tok):
    width = ys.shape[1]
    n_cores, n_workers = _sc_workers()
    per_w = n_tok // n_workers
    ch = min(SC_GATHER_CHUNK, per_w)
    n_ch = per_w // ch
    assert n_tok % n_workers == 0 and per_w % ch == 0 and ch % SUBLANES == 0 and n_ch % 2 == 0
    mesh = plsc.VectorSubcoreMesh(core_axis_name="c", subcore_axis_name="s")

    @functools.partial(
        pl.kernel, mesh=mesh,
        out_type=jax.ShapeDtypeStruct((TOP_K, n_tok, width), ys.dtype),
        scratch_types=[pltpu.VMEM((2, ch), I32), pltpu.VMEM((2, ch), I32),
                       pltpu.VMEM((2, ch, width), ys.dtype), pltpu.VMEM((2, ch, width), ys.dtype),
                       pltpu.SemaphoreType.DMA((2,)), pltpu.SemaphoreType.DMA((2,)),
                       pltpu.SemaphoreType.DMA((2,))],
        name="collect")
    def gather(ys_hbm, d0_hbm, d1_hbm, o_hbm, i0_v, i1_v, r0_v, r1_v, sem_idx, sem_in, sem_out):
        wid = lax.axis_index("s") * n_cores + lax.axis_index("c")
        base = wid * per_w

        def idx_loads(t0, slot):
            return (pltpu.make_async_copy(d0_hbm.at[pl.ds(tok0 + t0, ch)], i0_v.at[slot], sem_idx.at[slot]),
                    pltpu.make_async_copy(d1_hbm.at[pl.ds(tok0 + t0, ch)], i1_v.at[slot], sem_idx.at[slot]))

        def gathers(slot):
            return (pltpu.make_async_copy(ys_hbm.at[i0_v.at[slot]], r0_v.at[slot], sem_in.at[slot]),
                    pltpu.make_async_copy(ys_hbm.at[i1_v.at[slot]], r1_v.at[slot], sem_in.at[slot]))

        def stores(t0, slot):
            return (pltpu.make_async_copy(r0_v.at[slot], o_hbm.at[0, pl.ds(t0, ch)], sem_out.at[slot]),
                    pltpu.make_async_copy(r1_v.at[slot], o_hbm.at[1, pl.ds(t0, ch)], sem_out.at[slot]))

        def start_chunk(t0, slot):
            for cp in idx_loads(t0, slot):
                cp.start()
            for cp in idx_loads(t0, slot):
                cp.wait()
            for cp in gathers(slot):
                cp.start()

        start_chunk(base, 0)

        @pl.loop(0, n_ch, step=2)
        def _(k):
            for slot in range(2):
                nxt = k + slot + 1

                @pl.when(nxt < n_ch)
                def _():
                    start_chunk(base + nxt * ch, 1 - slot)

                for cp in gathers(slot):
                    cp.wait()
                out_cps = stores(base + (k + slot) * ch, slot)
                for cp in out_cps:
                    cp.start()
                for cp in out_cps:
                    cp.wait()

    return gather(ys, dest0, dest1)


def _experts_kernel(first_ref, nblk_ref, cnt_ref, xs_hbm, wg_ref, wu_ref, wd_ref, ys_hbm,
                    wgu_s, wd_s, xbuf, ybuf, sem_in, sem_out, *, de, blk):
    e = pl.program_id(0)
    n_exp = pl.num_programs(0)
    wgu_s[:, 0:de] = wg_ref[0].astype(BF16)
    wgu_s[:, de:2 * de] = wu_ref[0].astype(BF16)
    wd_s[...] = wd_ref[0].astype(BF16)
    first = first_ref[e]
    n_blk = nblk_ref[e]
    count = cnt_ref[e]
    total = first_ref[n_exp - 1] + nblk_ref[n_exp - 1]

    def in_copy(g, slot):
        return pltpu.make_async_copy(xs_hbm.at[pl.ds(pl.multiple_of(g * blk, blk), blk)],
                                     xbuf.at[slot], sem_in.at[slot])

    def out_copy(g, slot):
        return pltpu.make_async_copy(ybuf.at[slot],
                                     ys_hbm.at[pl.ds(pl.multiple_of(g * blk, blk), blk)],
                                     sem_out.at[slot])

    @pl.when(e == 0)
    def _():
        in_copy(0, 0).start(priority=1)

    def block(j, carry):
        g = first + j
        slot = lax.rem(g, 2)
        in_copy(g, slot).wait()

        @pl.when(g + 1 < total)
        def _():
            in_copy(g + 1, 1 - slot).start(priority=1)

        words = xbuf[slot]
        rows = lax.broadcasted_iota(I32, words.shape, 0)
        xb = _unpack_bf16_pairs(jnp.where(rows < count - j * blk, words, 0)).astype(BF16)
        ab = jnp.dot(xb, wgu_s[...], preferred_element_type=F32)
        a = ab[:, 0:de]
        b = ab[:, de:2 * de]
        hmid = (a * _sigmoid(a)) * b
        y = jnp.dot(hmid.astype(BF16), wd_s[...], preferred_element_type=F32)

        @pl.when(g >= 2)
        def _():
            out_copy(g - 2, slot).wait()

        ybuf[slot] = _pack_bf16_pairs(y)
        out_copy(g, slot).start(priority=1)
        return carry

    lax.fori_loop(0, n_blk, block, 0)

    @pl.when(e == n_exp - 1)
    def _():
        @pl.when(total >= 2)
        def _():
            out_copy(total - 2, lax.rem(total, 2)).wait()

        out_copy(total - 1, lax.rem(total + 1, 2)).wait()


def _experts(plan, xs, w_gate, w_up, w_down):
    n_slots, width = xs.shape
    n_exp, dm, de = w_gate.shape
    blk = EXPERT_BLOCK
    return pl.pallas_call(
        functools.partial(_experts_kernel, de=de, blk=blk),
        grid_spec=pltpu.PrefetchScalarGridSpec(
            num_scalar_prefetch=3,
            grid=(n_exp,),
            in_specs=[pl.BlockSpec(memory_space=pl.ANY),
                      pl.BlockSpec((1, dm, de), lambda e, f, n, c: (e, 0, 0)),
                      pl.BlockSpec((1, dm, de), lambda e, f, n, c: (e, 0, 0)),
                      pl.BlockSpec((1, de, dm), lambda e, f, n, c: (e, 0, 0))],
            out_specs=pl.BlockSpec(memory_space=pl.ANY),
            scratch_shapes=[pltpu.VMEM((dm, 2 * de), BF16), pltpu.VMEM((de, dm), BF16),
                            pltpu.VMEM((2, blk, width), I32), pltpu.VMEM((2, blk, width), I32),
                            pltpu.SemaphoreType.DMA((2,)), pltpu.SemaphoreType.DMA((2,))]),
        out_shape=jax.ShapeDtypeStruct((n_slots, width), I32),
        compiler_params=pltpu.CompilerParams(dimension_semantics=("arbitrary",),
                                             vmem_limit_bytes=VMEM_LIMIT),
        name="experts",
    )(*plan, xs, w_gate, w_up, w_down)


def _pick(onehot, table):
    return jnp.sum(jnp.where(onehot, table[None, :], 0), axis=1).astype(I32)


def _expert_plan(counts, n_assign):
    blk = EXPERT_BLOCK
    per_e = (counts + blk - 1) // blk
    first_blk = jnp.cumsum(per_e) - per_e
    n_slots = (n_assign // blk + N_EXPERTS) * blk
    return (first_blk * blk).astype(I32), (first_blk.astype(I32), per_e.astype(I32), counts), n_slots


def _slot_of(starts, expert, rank):
    onehot = jnp.arange(N_EXPERTS, dtype=I32)[None, :] == expert[:, None]
    return _pick(onehot, starts) + rank


def _combine_kernel(rows_ref, gcol_ref, x1_ref, mod_ref, fg_ref, *rest):
    o_ref = rest[-1]
    gc = gcol_ref[...]
    y = gc[:, 0:1] * _unpack_bf16_pairs(rows_ref[0]) + gc[:, 1:2] * _unpack_bf16_pairs(rows_ref[1])
    x2 = x1_ref[...] + mod_ref[0][5:6] * y
    r = lax.rsqrt(jnp.mean(x2 * x2, axis=-1, keepdims=True) + EPS)
    o_ref[...] = (x2 * r) * fg_ref[...]


def _combine(rows, gcol, x1, mod, final_g, seq, b0, nb, bsz, out_prev):
    dm = x1.shape[1]
    width = rows.shape[2]
    tc = min(COMBINE_TILE, seq)
    nst = seq // tc
    in_specs = [pl.BlockSpec((TOP_K, tc, width), lambda b, s: (0, b * nst + s, 0)),
                pl.BlockSpec((tc, LANES), lambda b, s: ((b + b0) * nst + s, 0)),
                pl.BlockSpec((tc, dm), lambda b, s: ((b + b0) * nst + s, 0)),
                pl.BlockSpec((1, 6, dm), lambda b, s: (b + b0, 0, 0)),
                pl.BlockSpec((1, dm), lambda b, s: (0, 0))]
    args = [rows, gcol, x1, mod, final_g.reshape(1, dm)]
    aliases = {}
    if out_prev is not None:
        in_specs.append(pl.BlockSpec(memory_space=pl.ANY))
        args.append(out_prev)
        aliases = {len(args) - 1: 0}
    return pl.pallas_call(
        _combine_kernel,
        grid=(nb, nst),
        in_specs=in_specs,
        out_specs=pl.BlockSpec((tc, dm), lambda b, s: ((b + b0) * nst + s, 0)),
        out_shape=jax.ShapeDtypeStruct((bsz * seq, dm), F32),
        input_output_aliases=aliases,
        compiler_params=pltpu.CompilerParams(dimension_semantics=("arbitrary", "arbitrary"),
                                             vmem_limit_bytes=VMEM_LIMIT),
        name="combine",
    )(*args)


def _layer(x, c, ada_w, ada_b, norm1_g, w_in, conv_w, conv_b, w_q, w_k, b_igate, b_fgate,
           mlstm_norm_g, mlstm_skip, w_pool, b_pool, pool_scale, w_out, norm2_g,
           w_rg, b_rg, w_re, b_re, w_eg, w_eu, w_ed, out_g):
    bsz, seq, dm = x.shape
    dml = conv_w.shape[1]
    n_tok = bsz * seq
    ts = min(SEQ_TILE, seq)

    mod = _ada(c, ada_w, ada_b).reshape(bsz, 6, dm)

    col_v = dml
    col_o = 2 * dml
    col_i = 3 * dml
    col_p = col_i + 2 * N_HEADS
    w_gate_cols = jnp.pad(w_in[:, col_i:col_p], ((0, 0), (0, LANES - 2 * N_HEADS)))
    w_in_r = jnp.concatenate([w_in[:, :col_v], w_in[:, col_p:], w_in[:, col_v:col_o],
                              w_in[:, col_o:col_i], w_gate_cols], axis=1).astype(BF16)
    gbias = jnp.pad(jnp.concatenate([b_igate, b_fgate]), (0, LANES - 2 * N_HEADS)).reshape(1, LANES)
    wqk = jnp.concatenate([w_q, w_k], axis=-1).astype(BF16)
    wrt = jnp.zeros((LOGIT_ROWS, dm), F32)
    wrt = wrt.at[0:N_GROUPS].set(w_rg.T).at[SUBLANES:SUBLANES + N_EXPERTS].set(w_re.T).astype(BF16)
    rb = jnp.zeros((LOGIT_ROWS,), F32).at[0:N_GROUPS].set(b_rg).at[SUBLANES:SUBLANES + N_EXPERTS].set(b_re)
    rbias = jnp.broadcast_to(rb[:, None], (LOGIT_ROWS, ts))

    mixer_params = (norm1_g.reshape(1, dm), w_in_r, gbias, conv_w, conv_b.reshape(1, dml), wqk,
                    mlstm_norm_g.reshape(1, dml), mlstm_skip.reshape(1, dml), w_pool.astype(BF16),
                    b_pool.reshape(1, dm - dml), pool_scale.reshape(1, dm - dml),
                    w_out.astype(BF16), norm2_g.reshape(1, dm), wrt, rbias)

    x1, h2, lgt = _mixer(x, mod, *mixer_params, 0, bsz)
    idx, gcol, cnt = _route(lgt, bsz, seq)
    counts = cnt[:, 0].astype(I32)
    starts, plan, n_slots = _expert_plan(counts, n_tok * TOP_K)
    dest0 = _slot_of(starts, idx[0], idx[2])
    dest1 = _slot_of(starts, idx[1], idx[3])
    xs = _dispatch(h2, dest0, dest1, n_slots)
    ys = _experts(plan, xs, w_eg, w_eu, w_ed)

    nb = bsz // COMBINE_GROUPS
    assert bsz % COMBINE_GROUPS == 0
    out = None
    for b0 in range(0, bsz, nb):
        rows = _collect(ys, dest0, dest1, b0 * seq, nb * seq)
        out = _combine(rows, gcol, x1, mod, out_g, seq, b0, nb, bsz, out)
    return out.reshape(bsz, seq, dm)


def kernel(x, c, ada_w, ada_b, norm1_g, w_in, conv_w, conv_b, w_q, w_k, b_igate, b_fgate, mlstm_norm_g, mlstm_skip, w_pool, b_pool, pool_scale, w_out, norm2_g, w_router_group, b_router_group, w_router_expert, b_router_expert, w_expert_gate, w_expert_up, w_expert_down, final_g):
    depth = ada_w.shape[0]
    assert depth == 1, "the final norm is fused into the last layer's combine kernel"
    l = 0
    return _layer(x, c, ada_w[l], ada_b[l], norm1_g[l], w_in[l], conv_w[l], conv_b[l], w_q[l],
                  w_k[l], b_igate[l], b_fgate[l], mlstm_norm_g[l], mlstm_skip[l], w_pool[l],
                  b_pool[l], pool_scale[l], w_out[l], norm2_g[l], w_router_group[l],
                  b_router_group[l], w_router_expert[l], b_router_expert[l],
                  w_expert_gate[l], w_expert_up[l], w_expert_down[l], final_g)
```

```python
import functools

import jax
import jax.numpy as jnp
from jax import lax
from jax.experimental import pallas as pl
from jax.experimental.pallas import tpu as pltpu
from jax.experimental.pallas import tpu_sc as plsc

F32 = jnp.float32
BF16 = jnp.bfloat16
I32 = jnp.int32
U32 = jnp.uint32

EPS = 1e-6
N_HEADS = 4
HEAD_DIM = 128
CONV_WIDTH = 4
POOL_WINDOWS = (2, 4, 8, 16)
N_GROUPS = 4
EXPERTS_PER_GROUP = 8
N_EXPERTS = N_GROUPS * EXPERTS_PER_GROUP
TOP_K = 2

LANES = 128
SUBLANES = 8
CHUNK = 128
SEQ_TILE = 512
COMBINE_GROUPS = 4
ROUTE_TILE = 512
ROUTE_SUBTILES = 4
SC_SCATTER_CHUNK = 64
SC_GATHER_CHUNK = 32
COMBINE_TILE = 512
EXPERT_BLOCK = 512
LOGIT_ROWS = 48
UHIST = 8
PHIST = 16
VMEM_LIMIT = 60 * 1024 * 1024


def _sigmoid(x):
    return 1.0 / (1.0 + jnp.exp(-x))


def _pack_bf16_pairs(x):
    w = x.shape[1] // 2
    half_ulp = jnp.uint32(0x8000)
    hi = lax.bitcast_convert_type(x[:, :w], U32) + half_ulp
    lo = lax.bitcast_convert_type(x[:, w:], U32) + half_ulp
    return lax.bitcast_convert_type((hi & jnp.uint32(0xFFFF0000)) | (lo >> 16), I32)


def _unpack_bf16_pairs(words):
    u = lax.bitcast_convert_type(words, U32)
    hi = lax.bitcast_convert_type(u & jnp.uint32(0xFFFF0000), F32)
    lo = lax.bitcast_convert_type(u << 16, F32)
    return jnp.concatenate([hi, lo], axis=1)


def _ada_kernel(c_ref, w_ref, b_ref, o_ref):
    c = c_ref[...]
    s = c * _sigmoid(c)
    o_ref[...] = jnp.dot(s, w_ref[...], preferred_element_type=F32,
                         precision=lax.Precision.HIGHEST) + b_ref[...]


def _ada(c, ada_w, ada_b):
    bsz, dm = c.shape
    n = ada_w.shape[1]
    tn = 1024
    return pl.pallas_call(
        _ada_kernel,
        grid=(n // tn,),
        in_specs=[pl.BlockSpec((bsz, dm), lambda j: (0, 0)),
                  pl.BlockSpec((dm, tn), lambda j: (0, j)),
                  pl.BlockSpec((1, tn), lambda j: (0, j))],
        out_specs=pl.BlockSpec((bsz, tn), lambda j: (0, j)),
        out_shape=jax.ShapeDtypeStruct((bsz, n), F32),
        compiler_params=pltpu.CompilerParams(dimension_semantics=("arbitrary",),
                                             vmem_limit_bytes=VMEM_LIMIT),
        name="ada",
    )(c, ada_w, ada_b.reshape(1, n))


def _split3(x):
    hi = x.astype(BF16)
    r1 = x - hi.astype(F32)
    mid = r1.astype(BF16)
    lo = (r1 - mid.astype(F32)).astype(BF16)
    return hi, mid, lo


def _mixer_kernel(x_ref, mod_ref, g1_ref, win_ref, gbias_ref, convw_ref, convb_ref, wqk_ref,
                  ng_ref, skip_ref, wpool_ref, bpool_ref, pscale_ref, pinv_ref, wout_ref, g2_ref,
                  wrt_ref, rbias_ref,
                  x1_ref, h2_ref, lgt_ref,
                  uext_s, pext_s, proj_s, uc_s, qk_s, gate_s, mix_s, pool4_s, ctv_s, ctn_s, mprev_s,
                  *, ts, dm, dml):
    s_idx = pl.program_id(1)
    n_chunks = ts // CHUNK
    dp = dm - dml

    @pl.when(s_idx == 0)
    def _():
        uext_s[0:UHIST, :] = jnp.zeros((UHIST, dml), F32)
        pext_s[0:PHIST, :] = jnp.zeros((PHIST, dp), F32)
        ctv_s[...] = jnp.zeros_like(ctv_s)
        ctn_s[...] = jnp.zeros_like(ctn_s)
        mprev_s[...] = jnp.zeros_like(mprev_s)

    row_i = lax.broadcasted_iota(I32, (CHUNK, CHUNK), 0)
    col_i = lax.broadcasted_iota(I32, (CHUNK, CHUNK), 1)
    causal = row_i >= col_i
    tril = jnp.where(causal, 1.0, 0.0).astype(BF16)
    lane_c = lax.broadcasted_iota(I32, (CHUNK, LANES), 1)
    ones_blk = jnp.ones((CHUNK, HEAD_DIM), BF16)
    q_scale = HEAD_DIM ** -0.5

    col_o = dml
    col_g = 2 * dml

    x = x_ref[0]
    mod = mod_ref[0]
    r = lax.rsqrt(jnp.mean(x * x, axis=-1, keepdims=True) + EPS)
    h = (x * r) * (g1_ref[...] * (1.0 + mod[1:2])) + mod[0:1]
    res = jnp.dot(h.astype(BF16), win_ref[...], preferred_element_type=F32)
    uext_s[UHIST:, :] = res[:, 0:dml]
    pext_s[PHIST:, :] = res[:, dml:dm]
    proj_s[...] = res[:, dm:]

    acc = None
    for j in reversed(range(CONV_WIDTH)):
        tap = uext_s[pl.ds(UHIST - (CONV_WIDTH - 1 - j), ts), :] * convw_ref[j:j + 1, :]
        acc = tap if acc is None else acc + tap
    conv = acc + convb_ref[...]
    uc = conv * _sigmoid(conv)
    uc_s[...] = uc

    for hd in range(N_HEADS):
        qk_s[:, 2 * HEAD_DIM * hd:2 * HEAD_DIM * (hd + 1)] = jnp.dot(
            uc[:, HEAD_DIM * hd:HEAD_DIM * (hd + 1)].astype(BF16), wqk_ref[hd],
            preferred_element_type=F32)

    g = proj_s[:, col_g:col_g + LANES] + gbias_ref[...]
    lane = lax.broadcasted_iota(I32, (ts, LANES), 1)
    logf = -(jnp.maximum(-g, 0.0) + jnp.log1p(jnp.exp(-jnp.abs(g))))
    gate_s[...] = jnp.where(lane < N_HEADS, g, logf)

    pairs = [(c, hd) for c in range(n_chunks) for hd in range(N_HEADS)]
    bcums, comb_ts = [], []
    for c in range(n_chunks):
        gc = gate_s[pl.ds(c * CHUNK, CHUNK), :]
        hi, mid, lo = _split3(gc)
        cs = jnp.dot(tril, jnp.concatenate([hi, mid, lo], axis=1), preferred_element_type=F32)
        bcum = cs[:, 0:LANES] + cs[:, LANES:2 * LANES] + cs[:, 2 * LANES:3 * LANES]
        bcums.append(bcum)
        comb_ts.append(jnp.where(lane_c < N_HEADS, gc, bcum).T)

    def rows(ref, c, lo_col, width=HEAD_DIM):
        return ref[pl.ds(c * CHUNK, CHUNK), lo_col:lo_col + width]

    b_bcs, dlogs, rmaxs, p_mats = {}, {}, {}, {}
    for c, hd in pairs:
        b_bc = jnp.broadcast_to(bcums[c][:, N_HEADS + hd:N_HEADS + hd + 1], (CHUNK, CHUNK))
        i_row = comb_ts[c][hd:hd + 1, :]
        b_row = comb_ts[c][N_HEADS + hd:N_HEADS + hd + 1, :]
        dlog = jnp.where(causal, (b_bc - b_row) + i_row, -jnp.inf)
        b_bcs[c, hd], dlogs[c, hd] = b_bc, dlog
        rmaxs[c, hd] = jnp.max(dlog, axis=-1, keepdims=True)
        q_c = rows(qk_s, c, 2 * HEAD_DIM * hd) * q_scale
        k_c = rows(qk_s, c, 2 * HEAD_DIM * hd + HEAD_DIM)
        p_mats[c, hd] = lax.dot_general(q_c.astype(BF16), k_c.astype(BF16),
                                        (((1,), (1,)), ((), ())), preferred_element_type=F32)

    inters, m_ts = {}, {}
    for hd in range(N_HEADS):
        m_prev = mprev_s[hd]
        for c in range(n_chunks):
            inter = b_bcs[c, hd] + m_prev
            m_t = jnp.maximum(inter, rmaxs[c, hd])
            inters[c, hd], m_ts[c, hd] = inter, m_t
            m_prev = jnp.broadcast_to(m_t[CHUNK - 1:CHUNK, :], (CHUNK, LANES))
        mprev_s[hd] = m_prev

    lhs, v_augs, upds, a_prevs, e_negms = {}, {}, {}, {}, {}
    for c, hd in pairs:
        m_t = m_ts[c, hd]
        wm = jnp.exp(dlogs[c, hd] - m_t)
        a_inter = jnp.exp(inters[c, hd] - m_t)
        e_negms[c, hd] = jnp.exp(-m_t)
        q_c = rows(qk_s, c, 2 * HEAD_DIM * hd) * q_scale
        k_c = rows(qk_s, c, 2 * HEAD_DIM * hd + HEAD_DIM)
        v_c = rows(proj_s, c, HEAD_DIM * hd)
        s_mat = (p_mats[c, hd] * wm).astype(BF16)
        qa = (q_c * a_inter).astype(BF16)
        lhs[c, hd] = jnp.concatenate([s_mat, qa], axis=1)
        v_aug = jnp.concatenate([v_c.astype(BF16), ones_blk], axis=1)
        v_augs[c, hd] = v_aug
        ktw = (k_c.T * wm[CHUNK - 1:CHUNK, :]).astype(BF16)
        upds[c, hd] = jnp.dot(ktw, v_aug, preferred_element_type=F32)
        a_prevs[c, hd] = a_inter[CHUNK - 1:CHUNK, :]

    ct_in = {}
    for hd in range(N_HEADS):
        ctv, ctn = ctv_s[hd], ctn_s[hd]
        for c in range(n_chunks):
            ct_in[c, hd] = jnp.concatenate([ctv, ctn], axis=1).astype(BF16)
            ctv = a_prevs[c, hd] * ctv + upds[c, hd][:, 0:HEAD_DIM]
            ctn = a_prevs[c, hd] * ctn + upds[c, hd][:, HEAD_DIM:2 * HEAD_DIM]
        ctv_s[hd], ctn_s[hd] = ctv, ctn

    for c, hd in pairs:
        numden = jnp.dot(lhs[c, hd], jnp.concatenate([v_augs[c, hd], ct_in[c, hd]], axis=0),
                         preferred_element_type=F32)
        num = numden[:, 0:HEAD_DIM]
        den = numden[:, HEAD_DIM:2 * HEAD_DIM]
        hh = num / jnp.maximum(jnp.abs(den), e_negms[c, hd])
        ms = jnp.mean(hh * hh, axis=-1, keepdims=True)
        hn = hh * lax.rsqrt(ms + EPS) * ng_ref[:, HEAD_DIM * hd:HEAD_DIM * (hd + 1)]
        o_c = rows(proj_s, c, col_o + HEAD_DIM * hd)
        uc_c = rows(uc_s, c, HEAD_DIM * hd)
        out_c = _sigmoid(o_c) * (hn + skip_ref[:, HEAD_DIM * hd:HEAD_DIM * (hd + 1)] * uc_c)
        mix_s[pl.ds(c * CHUNK, CHUNK), HEAD_DIM * hd:HEAD_DIM * (hd + 1)] = out_c.astype(BF16)

    def pe(shift, rows, lanes):
        return pext_s[pl.ds(PHIST - shift, rows), lanes]

    gd = LANES
    sums = []
    for gi in range(2):
        lanes = slice(gd * gi, gd * (gi + 1))
        tot = pe(0, ts, lanes)
        for j in range(1, POOL_WINDOWS[gi]):
            tot = tot + pe(j, ts, lanes)
        sums.append(tot)
    wide = slice(2 * gd, 4 * gd)
    s4 = pe(12, ts + 12, wide)
    for j in range(1, 4):
        s4 = s4 + pe(12 + j, ts + 12, wide)
    pool4_s[0:ts + 12, :] = s4
    s8 = pool4_s[pl.ds(4, ts + 8), :] + pool4_s[pl.ds(0, ts + 8), :]
    sums.append(s8[8:, 0:gd])
    sums.append(s8[8:, gd:2 * gd] + s8[0:ts, gd:2 * gd])
    for gi in range(len(POOL_WINDOWS)):
        lanes = slice(gd * gi, gd * (gi + 1))
        pooled = sums[gi] * pinv_ref[0, :, lanes] - pe(0, ts, lanes)
        yp = jnp.dot(pooled.astype(BF16), wpool_ref[gi], preferred_element_type=F32)
        yp = (yp + bpool_ref[:, lanes]) * pscale_ref[:, lanes]
        mix_s[:, dml + gd * gi:dml + gd * (gi + 1)] = yp.astype(BF16)

    mix = jnp.dot(mix_s[...], wout_ref[...], preferred_element_type=F32)
    x1 = x + mod[2:3] * mix
    x1_ref[...] = x1
    r2 = lax.rsqrt(jnp.mean(x1 * x1, axis=-1, keepdims=True) + EPS)
    h2 = (x1 * r2) * (g2_ref[...] * (1.0 + mod[4:5])) + mod[3:4]
    h2_ref[...] = _pack_bf16_pairs(h2)
    lgt_ref[...] = lax.dot_general(wrt_ref[...], h2.astype(BF16), (((1,), (1,)), ((), ())),
                                   preferred_element_type=F32) + rbias_ref[...]

    uext_s[0:UHIST, :] = uext_s[ts:ts + UHIST, :]
    pext_s[0:PHIST, :] = pext_s[ts:ts + PHIST, :]


def _mixer(x, mod, g1, w_in_r, gbias, conv_w, conv_b, wqk, ng, skip, wpool, bpool, pscale, pinv,
           w_out, g2, wrt, rbias, b0, nb):
    _, seq, dm = x.shape
    dml = conv_w.shape[1]
    ts = min(SEQ_TILE, seq)
    ncols = w_in_r.shape[1]
    nst = seq // ts
    n_tok = nb * seq
    assert seq % ts == 0 and ts % CHUNK == 0
    full = lambda a: pl.BlockSpec(a.shape, lambda b, s: (0,) * a.ndim)
    kern = functools.partial(_mixer_kernel, ts=ts, dm=dm, dml=dml)
    return pl.pallas_call(
        kern,
        grid=(nb, nst),
        in_specs=[pl.BlockSpec((1, ts, dm), lambda b, s: (b + b0, s, 0)),
                  pl.BlockSpec((1, 6, dm), lambda b, s: (b + b0, 0, 0)),
                  full(g1), full(w_in_r), full(gbias), full(conv_w), full(conv_b), full(wqk),
                  full(ng), full(skip), full(wpool), full(bpool), full(pscale),
                  pl.BlockSpec((1,) + pinv.shape[1:], lambda b, s: (jnp.minimum(s, 1), 0, 0)),
                  full(w_out), full(g2), full(wrt), full(rbias)],
        out_specs=[pl.BlockSpec((ts, dm), lambda b, s: (b * nst + s, 0)),
                   pl.BlockSpec((ts, dm // 2), lambda b, s: (b * nst + s, 0)),
                   pl.BlockSpec((LOGIT_ROWS, ts), lambda b, s: (0, b * nst + s))],
        out_shape=[jax.ShapeDtypeStruct((n_tok, dm), F32),
                   jax.ShapeDtypeStruct((n_tok, dm // 2), I32),
                   jax.ShapeDtypeStruct((LOGIT_ROWS, n_tok), F32)],
        scratch_shapes=[pltpu.VMEM((UHIST + ts, dml), F32),
                        pltpu.VMEM((PHIST + ts, dm - dml), F32),
                        pltpu.VMEM((ts, ncols - dm), F32),
                        pltpu.VMEM((ts, dml), F32),
                        pltpu.VMEM((ts, 2 * dml), F32),
                        pltpu.VMEM((ts, LANES), F32),
                        pltpu.VMEM((ts, dm), BF16),
                        pltpu.VMEM((ts + PHIST, 2 * LANES), F32),
                        pltpu.VMEM((N_HEADS, HEAD_DIM, HEAD_DIM), F32),
                        pltpu.VMEM((N_HEADS, HEAD_DIM, HEAD_DIM), F32),
                        pltpu.VMEM((N_HEADS, CHUNK, LANES), F32)],
        compiler_params=pltpu.CompilerParams(dimension_semantics=("arbitrary", "arbitrary"),
                                             vmem_limit_bytes=VMEM_LIMIT),
        name="mixer",
    )(x, mod, g1, w_in_r, gbias, conv_w, conv_b, wqk, ng, skip, wpool, bpool, pscale, pinv, w_out,
      g2, wrt, rbias)


def _route_kernel(lgt_ref, idx_ref, gate_ref, cnt_ref, carry_s, *, tr, n_sub):
    @pl.when(pl.program_id(0) == 0)
    def _():
        carry_s[...] = jnp.zeros_like(carry_s)

    tr_r = lax.broadcasted_iota(I32, (tr, tr), 0)
    tr_c = lax.broadcasted_iota(I32, (tr, tr), 1)
    upper = jnp.where(tr_r < tr_c, 1.0, 0.0).astype(BF16)
    for q in range(n_sub):
        cols = slice(q * tr, (q + 1) * tr)
        idx, gates = _route_tile(lgt_ref[:, cols], upper, carry_s, tr)
        idx_ref[:, cols] = idx
        gate_ref[:, cols] = gates
    cnt_ref[...] = carry_s[...]


def _route_tile(lg, upper, carry_s, tr):
    best = lg[0:1]
    gidx = jnp.zeros((1, tr), I32)
    for j in range(1, N_GROUPS):
        cand = lg[j:j + 1]
        better = cand > best
        gidx = jnp.where(better, j, gidx)
        best = jnp.where(better, cand, best)
    sumexp = jnp.zeros((1, tr), F32)
    for j in range(N_GROUPS):
        sumexp = sumexp + jnp.exp(lg[j:j + 1] - best)
    g_gate = 1.0 / sumexp

    sel = lg[SUBLANES:2 * SUBLANES]
    for j in range(1, N_GROUPS):
        sel = jnp.where(gidx == j, lg[SUBLANES * (j + 1):SUBLANES * (j + 2)], sel)
    sub = lax.broadcasted_iota(I32, (EXPERTS_PER_GROUP, tr), 0)
    v1 = jnp.max(sel, axis=0, keepdims=True)
    i1 = jnp.min(jnp.where(sel == v1, sub, EXPERTS_PER_GROUP), axis=0, keepdims=True)
    sel2 = jnp.where(sub == i1, -jnp.inf, sel)
    v2 = jnp.max(sel2, axis=0, keepdims=True)
    i2 = jnp.min(jnp.where(sel2 == v2, sub, EXPERTS_PER_GROUP), axis=0, keepdims=True)
    e2 = jnp.exp(v2 - v1)
    den = 1.0 + e2
    gate0 = (1.0 / den) * g_gate
    gate1 = (e2 / den) * g_gate
    ex0 = gidx * EXPERTS_PER_GROUP + i1
    ex1 = gidx * EXPERTS_PER_GROUP + i2

    erow = lax.broadcasted_iota(I32, (N_EXPERTS, tr), 0)
    oh0 = erow == ex0
    oh1 = erow == ex1
    oh = jnp.where(oh0 | oh1, 1.0, 0.0).astype(BF16)
    carry = carry_s[...]
    before = jnp.dot(oh, upper, preferred_element_type=F32)
    before = before + jnp.concatenate([carry] * (tr // LANES), axis=1)
    rank0 = jnp.sum(jnp.where(oh0, before, 0.0), axis=0, keepdims=True)
    rank1 = jnp.sum(jnp.where(oh1, before, 0.0), axis=0, keepdims=True)
    carry_s[...] = carry + jnp.dot(oh, jnp.ones((tr, LANES), BF16), preferred_element_type=F32)

    idx = jnp.concatenate([ex0, ex1, rank0.astype(I32), rank1.astype(I32),
                           jnp.zeros((SUBLANES - 4, tr), I32)], axis=0)
    gates = jnp.concatenate([gate0, gate1, jnp.zeros((SUBLANES - 2, tr), F32)], axis=0)
    return idx, gates


def _route(lgt, n_tok):
    tr = ROUTE_TILE
    n_sub = min(ROUTE_SUBTILES, n_tok // tr)
    step = tr * n_sub
    assert n_tok % step == 0
    return pl.pallas_call(
        functools.partial(_route_kernel, tr=tr, n_sub=n_sub),
        grid=(n_tok // step,),
        in_specs=[pl.BlockSpec((LOGIT_ROWS, step), lambda i: (0, i))],
        out_specs=[pl.BlockSpec((SUBLANES, step), lambda i: (0, i)),
                   pl.BlockSpec((SUBLANES, step), lambda i: (0, i)),
                   pl.BlockSpec((N_EXPERTS, LANES), lambda i: (0, 0))],
        out_shape=[jax.ShapeDtypeStruct((SUBLANES, n_tok), I32),
                   jax.ShapeDtypeStruct((SUBLANES, n_tok), F32),
                   jax.ShapeDtypeStruct((N_EXPERTS, LANES), F32)],
        scratch_shapes=[pltpu.VMEM((N_EXPERTS, LANES), F32)],
        compiler_params=pltpu.CompilerParams(dimension_semantics=("arbitrary",),
                                             vmem_limit_bytes=VMEM_LIMIT),
        name="route",
    )(lgt)


def _sc_workers():
    info = plsc.get_sparse_core_info()
    return info.num_cores, info.num_cores * info.num_subcores


def _dispatch(h2p, dest0, dest1, n_slots):
    n_tok, width = h2p.shape
    n_cores, n_workers = _sc_workers()
    per_w = n_tok // n_workers
    ch = min(SC_SCATTER_CHUNK, per_w)
    n_ch = per_w // ch
    assert n_tok % n_workers == 0 and per_w % ch == 0 and ch % SUBLANES == 0 and n_ch % 2 == 0
    mesh = plsc.VectorSubcoreMesh(core_axis_name="c", subcore_axis_name="s")

    @functools.partial(
        pl.kernel, mesh=mesh,
        out_type=jax.ShapeDtypeStruct((n_slots, width), h2p.dtype),
        scratch_types=[pltpu.VMEM((2, ch), I32), pltpu.VMEM((2, ch), I32),
                       pltpu.VMEM((2, ch, width), h2p.dtype),
                       pltpu.SemaphoreType.DMA((2,)), pltpu.SemaphoreType.DMA((2,))],
        name="dispatch")
    def scatter(h_hbm, d0_hbm, d1_hbm, xs_hbm, i0_v, i1_v, rows_v, sem_in, sem_out):
        wid = lax.axis_index("s") * n_cores + lax.axis_index("c")
        base = wid * per_w

        def loads(t0, slot):
            return (pltpu.make_async_copy(d0_hbm.at[pl.ds(t0, ch)], i0_v.at[slot], sem_in.at[slot]),
                    pltpu.make_async_copy(d1_hbm.at[pl.ds(t0, ch)], i1_v.at[slot], sem_in.at[slot]),
                    pltpu.make_async_copy(h_hbm.at[pl.ds(t0, ch)], rows_v.at[slot], sem_in.at[slot]))

        def scatters(slot):
            return (pltpu.make_async_copy(rows_v.at[slot], xs_hbm.at[i0_v.at[slot]], sem_out.at[slot]),
                    pltpu.make_async_copy(rows_v.at[slot], xs_hbm.at[i1_v.at[slot]], sem_out.at[slot]))

        for cp in loads(base, 0):
            cp.start()

        @pl.loop(0, n_ch, step=2)
        def _(k):
            for slot in range(2):
                for cp in loads(base + (k + slot) * ch, slot):
                    cp.wait()
                out_cps = scatters(slot)
                for cp in out_cps:
                    cp.start()
                nxt = k + slot + 1

                @pl.when(nxt < n_ch)
                def _():
                    for cp in loads(base + nxt * ch, 1 - slot):
                        cp.start()

                for cp in out_cps:
                    cp.wait()

    return scatter(h2p, dest0, dest1)


def _collect(ys, dest0, dest1, tok0, n_tok):
    width = ys.shape[1]
    n_cores, n_workers = _sc_workers()
    per_w = n_tok // n_workers
    ch = min(SC_GATHER_CHUNK, per_w)
    n_ch = per_w // ch
    assert n_tok % n_workers == 0 and per_w % ch == 0 and ch % SUBLANES == 0 and n_ch % 2 == 0
    mesh = plsc.VectorSubcoreMesh(core_axis_name="c", subcore_axis_name="s")

    @functools.partial(
        pl.kernel, mesh=mesh,
        out_type=jax.ShapeDtypeStruct((TOP_K, n_tok, width), ys.dtype),
        scratch_types=[pltpu.VMEM((2, ch), I32), pltpu.VMEM((2, ch), I32),
                       pltpu.VMEM((2, ch, width), ys.dtype), pltpu.VMEM((2, ch, width), ys.dtype),
                       pltpu.SemaphoreType.DMA((2,)), pltpu.SemaphoreType.DMA((2,)),
                       pltpu.SemaphoreType.DMA((2,))],
        name="collect")
    def gather(ys_hbm, d0_hbm, d1_hbm, o_hbm, i0_v, i1_v, r0_v, r1_v, sem_idx, sem_in, sem_out):
        wid = lax.axis_index("s") * n_cores + lax.axis_index("c")
        base = wid * per_w

        def idx_loads(t0, slot):
            return (pltpu.make_async_copy(d0_hbm.at[pl.ds(tok0 + t0, ch)], i0_v.at[slot], sem_idx.at[slot]),
                    pltpu.make_async_copy(d1_hbm.at[pl.ds(tok0 + t0, ch)], i1_v.at[slot], sem_idx.at[slot]))

        def gathers(slot):
            return (pltpu.make_async_copy(ys_hbm.at[i0_v.at[slot]], r0_v.at[slot], sem_in.at[slot]),
                    pltpu.make_async_copy(ys_hbm.at[i1_v.at[slot]], r1_v.at[slot], sem_in.at[slot]))

        def stores(t0, slot):
            return (pltpu.make_async_copy(r0_v.at[slot], o_hbm.at[0, pl.ds(t0, ch)], sem_out.at[slot]),
                    pltpu.make_async_copy(r1_v.at[slot], o_hbm.at[1, pl.ds(t0, ch)], sem_out.at[slot]))

        def start_chunk(t0, slot):
            for cp in idx_loads(t0, slot):
                cp.start()
            for cp in idx_loads(t0, slot):
                cp.wait()
            for cp in gathers(slot):
                cp.start()

        start_chunk(base, 0)

        @pl.loop(0, n_ch, step=2)
        def _(k):
            for slot in range(2):
                nxt = k + slot + 1

                @pl.when(nxt < n_ch)
                def _():
                    start_chunk(base + nxt * ch, 1 - slot)

                for cp in gathers(slot):
                    cp.wait()
                out_cps = stores(base + (k + slot) * ch, slot)
                for cp in out_cps:
                    cp.start()
                for cp in out_cps:
                    cp.wait()

    return gather(ys, dest0, dest1)


def _experts_kernel(first_ref, nblk_ref, cnt_ref, xs_hbm, wg_ref, wu_ref, wd_ref, ys_hbm,
                    wgu_s, wd_s, xbuf, ybuf, sem_in, sem_out, *, de, blk):
    e = pl.program_id(0)
    n_exp = pl.num_programs(0)
    wgu_s[:, 0:de] = wg_ref[0].astype(BF16)
    wgu_s[:, de:2 * de] = wu_ref[0].astype(BF16)
    wd_s[...] = wd_ref[0].astype(BF16)
    first = first_ref[e]
    n_blk = nblk_ref[e]
    count = cnt_ref[e]
    total = first_ref[n_exp - 1] + nblk_ref[n_exp - 1]

    def in_copy(g, slot):
        return pltpu.make_async_copy(xs_hbm.at[pl.ds(pl.multiple_of(g * blk, blk), blk)],
                                     xbuf.at[slot], sem_in.at[slot])

    def out_copy(g, slot):
        return pltpu.make_async_copy(ybuf.at[slot],
                                     ys_hbm.at[pl.ds(pl.multiple_of(g * blk, blk), blk)],
                                     sem_out.at[slot])

    @pl.when(e == 0)
    def _():
        in_copy(0, 0).start(priority=1)

    def block(j, carry):
        g = first + j
        slot = lax.rem(g, 2)
        in_copy(g, slot).wait()

        @pl.when(g + 1 < total)
        def _():
            in_copy(g + 1, 1 - slot).start(priority=1)

        words = xbuf[slot]
        rows = lax.broadcasted_iota(I32, words.shape, 0)
        xb = _unpack_bf16_pairs(jnp.where(rows < count - j * blk, words, 0)).astype(BF16)
        ab = jnp.dot(xb, wgu_s[...], preferred_element_type=F32)
        a = ab[:, 0:de]
        b = ab[:, de:2 * de]
        hmid = (a * _sigmoid(a)) * b
        y = jnp.dot(hmid.astype(BF16), wd_s[...], preferred_element_type=F32)

        @pl.when(g >= 2)
        def _():
            out_copy(g - 2, slot).wait()

        ybuf[slot] = _pack_bf16_pairs(y)
        out_copy(g, slot).start(priority=1)
        return carry

    lax.fori_loop(0, n_blk, block, 0)

    @pl.when(e == n_exp - 1)
    def _():
        @pl.when(total >= 2)
        def _():
            out_copy(total - 2, lax.rem(total, 2)).wait()

        out_copy(total - 1, lax.rem(total + 1, 2)).wait()


def _experts(plan, xs, w_gate, w_up, w_down):
    n_slots, width = xs.shape
    n_exp, dm, de = w_gate.shape
    blk = EXPERT_BLOCK
    return pl.pallas_call(
        functools.partial(_experts_kernel, de=de, blk=blk),
        grid_spec=pltpu.PrefetchScalarGridSpec(
            num_scalar_prefetch=3,
            grid=(n_exp,),
            in_specs=[pl.BlockSpec(memory_space=pl.ANY),
                      pl.BlockSpec((1, dm, de), lambda e, f, n, c: (e, 0, 0)),
                      pl.BlockSpec((1, dm, de), lambda e, f, n, c: (e, 0, 0)),
                      pl.BlockSpec((1, de, dm), lambda e, f, n, c: (e, 0, 0))],
            out_specs=pl.BlockSpec(memory_space=pl.ANY),
            scratch_shapes=[pltpu.VMEM((dm, 2 * de), BF16), pltpu.VMEM((de, dm), BF16),
                            pltpu.VMEM((2, blk, width), I32), pltpu.VMEM((2, blk, width), I32),
                            pltpu.SemaphoreType.DMA((2,)), pltpu.SemaphoreType.DMA((2,))]),
        out_shape=jax.ShapeDtypeStruct((n_slots, width), I32),
        compiler_params=pltpu.CompilerParams(dimension_semantics=("arbitrary",),
                                             vmem_limit_bytes=VMEM_LIMIT),
        name="experts",
    )(*plan, xs, w_gate, w_up, w_down)


def _pick(onehot, table):
    return jnp.sum(jnp.where(onehot, table[None, :], 0), axis=1).astype(I32)


def _expert_plan(counts, n_assign):
    blk = EXPERT_BLOCK
    per_e = (counts + blk - 1) // blk
    first_blk = jnp.cumsum(per_e) - per_e
    n_slots = (n_assign // blk + N_EXPERTS) * blk
    return (first_blk * blk).astype(I32), (first_blk.astype(I32), per_e.astype(I32), counts), n_slots


def _slot_of(starts, expert, rank):
    onehot = jnp.arange(N_EXPERTS, dtype=I32)[None, :] == expert[:, None]
    return _pick(onehot, starts) + rank


def _combine_kernel(rows_ref, gate_ref, x1_ref, mod_ref, fg_ref, *rest):
    o_ref = rest[-1]
    tc = x1_ref.shape[0]
    gate_f = mod_ref[0][5:6]
    for q in range(tc // LANES):
        tok = slice(q * LANES, (q + 1) * LANES)
        g_rows = jnp.concatenate([gate_ref[:, tok], jnp.zeros((LANES - SUBLANES, LANES), F32)], axis=0)
        gc = g_rows.T
        y = (gc[:, 0:1] * _unpack_bf16_pairs(rows_ref[0, tok, :])
             + gc[:, 1:2] * _unpack_bf16_pairs(rows_ref[1, tok, :]))
        x2 = x1_ref[tok, :] + gate_f * y
        r = lax.rsqrt(jnp.mean(x2 * x2, axis=-1, keepdims=True) + EPS)
        o_ref[tok, :] = (x2 * r) * fg_ref[...]


def _combine(rows, gcol, x1, mod, final_g, seq, b0, nb, bsz, out_prev):
    dm = x1.shape[1]
    width = rows.shape[2]
    tc = min(COMBINE_TILE, seq)
    nst = seq // tc
    in_specs = [pl.BlockSpec((TOP_K, tc, width), lambda b, s: (0, b * nst + s, 0)),
                pl.BlockSpec((SUBLANES, tc), lambda b, s: (0, (b + b0) * nst + s)),
                pl.BlockSpec((tc, dm), lambda b, s: ((b + b0) * nst + s, 0)),
                pl.BlockSpec((1, 6, dm), lambda b, s: (b + b0, 0, 0)),
                pl.BlockSpec((1, dm), lambda b, s: (0, 0))]
    args = [rows, gcol, x1, mod, final_g.reshape(1, dm)]
    aliases = {}
    if out_prev is not None:
        in_specs.append(pl.BlockSpec(memory_space=pl.ANY))
        args.append(out_prev)
        aliases = {len(args) - 1: 0}
    return pl.pallas_call(
        _combine_kernel,
        grid=(nb, nst),
        in_specs=in_specs,
        out_specs=pl.BlockSpec((tc, dm), lambda b, s: ((b + b0) * nst + s, 0)),
        out_shape=jax.ShapeDtypeStruct((bsz * seq, dm), F32),
        input_output_aliases=aliases,
        compiler_params=pltpu.CompilerParams(dimension_semantics=("arbitrary", "arbitrary"),
                                             vmem_limit_bytes=VMEM_LIMIT),
        name="combine",
    )(*args)


def _layer(x, c, ada_w, ada_b, norm1_g, w_in, conv_w, conv_b, w_q, w_k, b_igate, b_fgate,
           mlstm_norm_g, mlstm_skip, w_pool, b_pool, pool_scale, w_out, norm2_g,
           w_rg, b_rg, w_re, b_re, w_eg, w_eu, w_ed, out_g):
    bsz, seq, dm = x.shape
    dml = conv_w.shape[1]
    n_tok = bsz * seq
    ts = min(SEQ_TILE, seq)

    mod = _ada(c, ada_w, ada_b).reshape(bsz, 6, dm)

    col_v = dml
    col_o = 2 * dml
    col_i = 3 * dml
    col_p = col_i + 2 * N_HEADS
    w_gate_cols = jnp.pad(w_in[:, col_i:col_p], ((0, 0), (0, LANES - 2 * N_HEADS)))
    w_in_r = jnp.concatenate([w_in[:, :col_v], w_in[:, col_p:], w_in[:, col_v:col_o],
                              w_in[:, col_o:col_i], w_gate_cols], axis=1).astype(BF16)
    gbias = jnp.pad(jnp.concatenate([b_igate, b_fgate]), (0, LANES - 2 * N_HEADS)).reshape(1, LANES)
    wqk = jnp.concatenate([w_q, w_k], axis=-1).astype(BF16)
    wrt = jnp.zeros((LOGIT_ROWS, dm), F32)
    wrt = wrt.at[0:N_GROUPS].set(w_rg.T).at[SUBLANES:SUBLANES + N_EXPERTS].set(w_re.T).astype(BF16)
    rb = jnp.zeros((LOGIT_ROWS,), F32).at[0:N_GROUPS].set(b_rg).at[SUBLANES:SUBLANES + N_EXPERTS].set(b_re)
    rbias = jnp.broadcast_to(rb[:, None], (LOGIT_ROWS, ts))

    gdim = (dm - dml) // len(POOL_WINDOWS)
    win = jnp.repeat(jnp.array(POOL_WINDOWS, F32), gdim)[None, :]
    t1 = jnp.arange(1, ts + 1, dtype=F32)[:, None]
    pinv = jnp.stack([1.0 / jnp.minimum(t1, win), jnp.broadcast_to(1.0 / win, (ts, dm - dml))])

    mixer_params = (norm1_g.reshape(1, dm), w_in_r, gbias, conv_w, conv_b.reshape(1, dml), wqk,
                    mlstm_norm_g.reshape(1, dml), mlstm_skip.reshape(1, dml), w_pool.astype(BF16),
                    b_pool.reshape(1, dm - dml), pool_scale.reshape(1, dm - dml), pinv,
                    w_out.astype(BF16), norm2_g.reshape(1, dm), wrt, rbias)

    x1, h2, lgt = _mixer(x, mod, *mixer_params, 0, bsz)
    idx, gcol, cnt = _route(lgt, n_tok)
    counts = cnt[:, 0].astype(I32)
    starts, plan, n_slots = _expert_plan(counts, n_tok * TOP_K)
    dest0 = _slot_of(starts, idx[0], idx[2])
    dest1 = _slot_of(starts, idx[1], idx[3])
    xs = _dispatch(h2, dest0, dest1, n_slots)
    ys = _experts(plan, xs, w_eg, w_eu, w_ed)

    nb = bsz // COMBINE_GROUPS
    assert bsz % COMBINE_GROUPS == 0
    out = None
    for b0 in range(0, bsz, nb):
        rows = _collect(ys, dest0, dest1, b0 * seq, nb * seq)
        out = _combine(rows, gcol, x1, mod, out_g, seq, b0, nb, bsz, out)
    return out.reshape(bsz, seq, dm)


def kernel(x, c, ada_w, ada_b, norm1_g, w_in, conv_w, conv_b, w_q, w_k, b_igate, b_fgate, mlstm_norm_g, mlstm_skip, w_pool, b_pool, pool_scale, w_out, norm2_g, w_router_group, b_router_group, w_router_expert, b_router_expert, w_expert_gate, w_expert_up, w_expert_down, final_g):
    depth = ada_w.shape[0]
    assert depth == 1, "the final norm is fused into the last layer's combine kernel"
    l = 0
    return _layer(x, c, ada_w[l], ada_b[l], norm1_g[l], w_in[l], conv_w[l], conv_b[l], w_q[l],
                  w_k[l], b_igate[l], b_fgate[l], mlstm_norm_g[l], mlstm_skip[l], w_pool[l],
                  b_pool[l], pool_scale[l], w_out[l], norm2_g[l], w_router_group[l],
                  b_router_group[l], w_router_expert[l], b_router_expert[l],
                  w_expert_gate[l], w_expert_up[l], w_expert_down[l], final_g)
```

```python
import functools

import jax
import jax.numpy as jnp
from jax import lax
from jax.experimental import pallas as pl
from jax.experimental.pallas import tpu as pltpu
from jax.experimental.pallas import tpu_sc as plsc

F32 = jnp.float32
BF16 = jnp.bfloat16
I32 = jnp.int32
U32 = jnp.uint32

EPS = 1e-6
N_HEADS = 4
HEAD_DIM = 128
CONV_WIDTH = 4
POOL_WINDOWS = (2, 4, 8, 16)
N_GROUPS = 4
EXPERTS_PER_GROUP = 8
N_EXPERTS = N_GROUPS * EXPERTS_PER_GROUP
TOP_K = 2

LANES = 128
SUBLANES = 8
CHUNK = 128
SEQ_TILE = 512
ADA_TILE = 2048
COMBINE_GROUPS = 8
ROUTE_TILE = 512
ROUTE_SUBTILES = 4
SC_SCATTER_CHUNK = 64
SC_GATHER_CHUNK = 32
COMBINE_TILE = 512
EXPERT_BLOCK = 512
LOGIT_ROWS = 48
UHIST = 8
PHIST = 16
VMEM_LIMIT = 60 * 1024 * 1024


def _sigmoid(x):
    return 1.0 / (1.0 + jnp.exp(-x))


def _pack_bf16_pairs(x):
    w = x.shape[1] // 2
    half_ulp = jnp.uint32(0x8000)
    hi = lax.bitcast_convert_type(x[:, :w], U32) + half_ulp
    lo = lax.bitcast_convert_type(x[:, w:], U32) + half_ulp
    return lax.bitcast_convert_type((hi & jnp.uint32(0xFFFF0000)) | (lo >> 16), I32)


def _unpack_bf16_pairs(words):
    u = lax.bitcast_convert_type(words, U32)
    hi = lax.bitcast_convert_type(u & jnp.uint32(0xFFFF0000), F32)
    lo = lax.bitcast_convert_type(u << 16, F32)
    return jnp.concatenate([hi, lo], axis=1)


def _ada_kernel(c_ref, w_ref, b_ref, o_ref):
    c = c_ref[...]
    s = c * _sigmoid(c)
    o_ref[...] = jnp.dot(s, w_ref[...], preferred_element_type=F32,
                         precision=lax.Precision.HIGHEST) + b_ref[...]


def _ada(c, ada_w, ada_b):
    bsz, dm = c.shape
    n = ada_w.shape[1]
    tn = ADA_TILE
    return pl.pallas_call(
        _ada_kernel,
        grid=(n // tn,),
        in_specs=[pl.BlockSpec((bsz, dm), lambda j: (0, 0)),
                  pl.BlockSpec((dm, tn), lambda j: (0, j)),
                  pl.BlockSpec((1, tn), lambda j: (0, j))],
        out_specs=pl.BlockSpec((bsz, tn), lambda j: (0, j)),
        out_shape=jax.ShapeDtypeStruct((bsz, n), F32),
        compiler_params=pltpu.CompilerParams(dimension_semantics=("arbitrary",),
                                             vmem_limit_bytes=VMEM_LIMIT),
        name="ada",
    )(c, ada_w, ada_b.reshape(1, n))


def _split3(x):
    hi = x.astype(BF16)
    r1 = x - hi.astype(F32)
    mid = r1.astype(BF16)
    lo = (r1 - mid.astype(F32)).astype(BF16)
    return hi, mid, lo


def _mixer_kernel(x_ref, mod_ref, g1_ref, win_ref, gbias_ref, convw_ref, convb_ref, wqk_ref,
                  ng_ref, skip_ref, wpool_ref, bpool_ref, pscale_ref, pinv_ref, wout_ref, g2_ref,
                  wrt_ref, rbias_ref,
                  x1_ref, h2_ref, lgt_ref,
                  uext_s, pext_s, proj_s, uc_s, qk_s, gate_s, mix_s, pool4_s, ctv_s, ctn_s, mprev_s,
                  *, ts, dm, dml):
    s_idx = pl.program_id(1)
    n_chunks = ts // CHUNK
    dp = dm - dml

    @pl.when(s_idx == 0)
    def _():
        uext_s[0:UHIST, :] = jnp.zeros((UHIST, dml), F32)
        pext_s[0:PHIST, :] = jnp.zeros((PHIST, dp), F32)
        ctv_s[...] = jnp.zeros_like(ctv_s)
        ctn_s[...] = jnp.zeros_like(ctn_s)
        mprev_s[...] = jnp.zeros_like(mprev_s)

    row_i = lax.broadcasted_iota(I32, (CHUNK, CHUNK), 0)
    col_i = lax.broadcasted_iota(I32, (CHUNK, CHUNK), 1)
    causal = row_i >= col_i
    tril = jnp.where(causal, 1.0, 0.0).astype(BF16)
    lane_c = lax.broadcasted_iota(I32, (CHUNK, LANES), 1)
    ones_blk = jnp.ones((CHUNK, HEAD_DIM), BF16)
    q_scale = HEAD_DIM ** -0.5

    col_o = dml
    col_g = 2 * dml

    x = x_ref[0]
    mod = mod_ref[0]
    r = lax.rsqrt(jnp.mean(x * x, axis=-1, keepdims=True) + EPS)
    h = (x * r) * (g1_ref[...] * (1.0 + mod[1:2])) + mod[0:1]
    res = jnp.dot(h.astype(BF16), win_ref[...], preferred_element_type=F32)
    uext_s[UHIST:, :] = res[:, 0:dml]
    pext_s[PHIST:, :] = res[:, dml:dm]
    proj_s[...] = res[:, dm:]

    acc = None
    for j in reversed(range(CONV_WIDTH)):
        tap = uext_s[pl.ds(UHIST - (CONV_WIDTH - 1 - j), ts), :] * convw_ref[j:j + 1, :]
        acc = tap if acc is None else acc + tap
    conv = acc + convb_ref[...]
    uc = conv * _sigmoid(conv)
    uc_s[...] = uc

    for hd in range(N_HEADS):
        qk_s[:, 2 * HEAD_DIM * hd:2 * HEAD_DIM * (hd + 1)] = jnp.dot(
            uc[:, HEAD_DIM * hd:HEAD_DIM * (hd + 1)].astype(BF16), wqk_ref[hd],
            preferred_element_type=F32)

    g = proj_s[:, col_g:col_g + LANES] + gbias_ref[...]
    lane = lax.broadcasted_iota(I32, (ts, LANES), 1)
    logf = -(jnp.maximum(-g, 0.0) + jnp.log1p(jnp.exp(-jnp.abs(g))))
    gate_s[...] = jnp.where(lane < N_HEADS, g, logf)

    pairs = [(c, hd) for c in range(n_chunks) for hd in range(N_HEADS)]
    bcums, comb_ts = [], []
    for c in range(n_chunks):
        gc = gate_s[pl.ds(c * CHUNK, CHUNK), :]
        hi, mid, lo = _split3(gc)
        cs = jnp.dot(tril, jnp.concatenate([hi, mid, lo], axis=1), preferred_element_type=F32)
        bcum = cs[:, 0:LANES] + cs[:, LANES:2 * LANES] + cs[:, 2 * LANES:3 * LANES]
        bcums.append(bcum)
        comb_ts.append(jnp.where(lane_c < N_HEADS, gc, bcum).T)

    def rows(ref, c, lo_col, width=HEAD_DIM):
        return ref[pl.ds(c * CHUNK, CHUNK), lo_col:lo_col + width]

    b_bcs, dlogs, rmaxs, p_mats = {}, {}, {}, {}
    for c, hd in pairs:
        b_bc = jnp.broadcast_to(bcums[c][:, N_HEADS + hd:N_HEADS + hd + 1], (CHUNK, CHUNK))
        i_row = comb_ts[c][hd:hd + 1, :]
        b_row = comb_ts[c][N_HEADS + hd:N_HEADS + hd + 1, :]
        dlog = jnp.where(causal, (b_bc - b_row) + i_row, -jnp.inf)
        b_bcs[c, hd], dlogs[c, hd] = b_bc, dlog
        rmaxs[c, hd] = jnp.max(dlog, axis=-1, keepdims=True)
        q_c = rows(qk_s, c, 2 * HEAD_DIM * hd) * q_scale
        k_c = rows(qk_s, c, 2 * HEAD_DIM * hd + HEAD_DIM)
        p_mats[c, hd] = lax.dot_general(q_c.astype(BF16), k_c.astype(BF16),
                                        (((1,), (1,)), ((), ())), preferred_element_type=F32)

    inters, m_ts = {}, {}
    for hd in range(N_HEADS):
        m_prev = mprev_s[hd]
        for c in range(n_chunks):
            inter = b_bcs[c, hd] + m_prev
            m_t = jnp.maximum(inter, rmaxs[c, hd])
            inters[c, hd], m_ts[c, hd] = inter, m_t
            m_prev = jnp.broadcast_to(m_t[CHUNK - 1:CHUNK, :], (CHUNK, LANES))
        mprev_s[hd] = m_prev

    lhs, v_augs, upds, a_prevs, e_negms = {}, {}, {}, {}, {}
    for c, hd in pairs:
        m_t = m_ts[c, hd]
        wm = jnp.exp(dlogs[c, hd] - m_t)
        a_inter = jnp.exp(inters[c, hd] - m_t)
        e_negms[c, hd] = jnp.exp(-m_t)
        q_c = rows(qk_s, c, 2 * HEAD_DIM * hd) * q_scale
        k_c = rows(qk_s, c, 2 * HEAD_DIM * hd + HEAD_DIM)
        v_c = rows(proj_s, c, HEAD_DIM * hd)
        s_mat = (p_mats[c, hd] * wm).astype(BF16)
        qa = (q_c * a_inter).astype(BF16)
        lhs[c, hd] = jnp.concatenate([s_mat, qa], axis=1)
        v_aug = jnp.concatenate([v_c.astype(BF16), ones_blk], axis=1)
        v_augs[c, hd] = v_aug
        ktw = (k_c.T * wm[CHUNK - 1:CHUNK, :]).astype(BF16)
        upds[c, hd] = jnp.dot(ktw, v_aug, preferred_element_type=F32)
        a_prevs[c, hd] = a_inter[CHUNK - 1:CHUNK, :]

    ct_in = {}
    for hd in range(N_HEADS):
        ctv, ctn = ctv_s[hd], ctn_s[hd]
        for c in range(n_chunks):
            ct_in[c, hd] = jnp.concatenate([ctv, ctn], axis=1).astype(BF16)
            ctv = a_prevs[c, hd] * ctv + upds[c, hd][:, 0:HEAD_DIM]
            ctn = a_prevs[c, hd] * ctn + upds[c, hd][:, HEAD_DIM:2 * HEAD_DIM]
        ctv_s[hd], ctn_s[hd] = ctv, ctn

    for c, hd in pairs:
        numden = jnp.dot(lhs[c, hd], jnp.concatenate([v_augs[c, hd], ct_in[c, hd]], axis=0),
                         preferred_element_type=F32)
        num = numden[:, 0:HEAD_DIM]
        den = numden[:, HEAD_DIM:2 * HEAD_DIM]
        hh = num / jnp.maximum(jnp.abs(den), e_negms[c, hd])
        ms = jnp.mean(hh * hh, axis=-1, keepdims=True)
        hn = hh * lax.rsqrt(ms + EPS) * ng_ref[:, HEAD_DIM * hd:HEAD_DIM * (hd + 1)]
        o_c = rows(proj_s, c, col_o + HEAD_DIM * hd)
        uc_c = rows(uc_s, c, HEAD_DIM * hd)
        out_c = _sigmoid(o_c) * (hn + skip_ref[:, HEAD_DIM * hd:HEAD_DIM * (hd + 1)] * uc_c)
        mix_s[pl.ds(c * CHUNK, CHUNK), HEAD_DIM * hd:HEAD_DIM * (hd + 1)] = out_c.astype(BF16)

    def pe(shift, rows, lanes):
        return pext_s[pl.ds(PHIST - shift, rows), lanes]

    gd = LANES
    sums = []
    for gi in range(2):
        lanes = slice(gd * gi, gd * (gi + 1))
        tot = pe(0, ts, lanes)
        for j in range(1, POOL_WINDOWS[gi]):
            tot = tot + pe(j, ts, lanes)
        sums.append(tot)
    wide = slice(2 * gd, 4 * gd)
    s4 = pe(12, ts + 12, wide)
    for j in range(1, 4):
        s4 = s4 + pe(12 + j, ts + 12, wide)
    pool4_s[0:ts + 12, :] = s4
    s8 = pool4_s[pl.ds(4, ts + 8), :] + pool4_s[pl.ds(0, ts + 8), :]
    sums.append(s8[8:, 0:gd])
    sums.append(s8[8:, gd:2 * gd] + s8[0:ts, gd:2 * gd])
    for gi in range(len(POOL_WINDOWS)):
        lanes = slice(gd * gi, gd * (gi + 1))
        pooled = sums[gi] * pinv_ref[0, :, lanes] - pe(0, ts, lanes)
        yp = jnp.dot(pooled.astype(BF16), wpool_ref[gi], preferred_element_type=F32)
        yp = (yp + bpool_ref[:, lanes]) * pscale_ref[:, lanes]
        mix_s[:, dml + gd * gi:dml + gd * (gi + 1)] = yp.astype(BF16)

    mix = jnp.dot(mix_s[...], wout_ref[...], preferred_element_type=F32)
    x1 = x + mod[2:3] * mix
    x1_ref[...] = x1
    r2 = lax.rsqrt(jnp.mean(x1 * x1, axis=-1, keepdims=True) + EPS)
    h2 = (x1 * r2) * (g2_ref[...] * (1.0 + mod[4:5])) + mod[3:4]
    h2_ref[...] = _pack_bf16_pairs(h2)
    lgt_ref[...] = lax.dot_general(wrt_ref[...], h2.astype(BF16), (((1,), (1,)), ((), ())),
                                   preferred_element_type=F32) + rbias_ref[...]

    uext_s[0:UHIST, :] = uext_s[ts:ts + UHIST, :]
    pext_s[0:PHIST, :] = pext_s[ts:ts + PHIST, :]


def _mixer(x, mod, g1, w_in_r, gbias, conv_w, conv_b, wqk, ng, skip, wpool, bpool, pscale, pinv,
           w_out, g2, wrt, rbias, b0, nb):
    _, seq, dm = x.shape
    dml = conv_w.shape[1]
    ts = min(SEQ_TILE, seq)
    ncols = w_in_r.shape[1]
    nst = seq // ts
    n_tok = nb * seq
    assert seq % ts == 0 and ts % CHUNK == 0
    full = lambda a: pl.BlockSpec(a.shape, lambda b, s: (0,) * a.ndim)
    kern = functools.partial(_mixer_kernel, ts=ts, dm=dm, dml=dml)
    return pl.pallas_call(
        kern,
        grid=(nb, nst),
        in_specs=[pl.BlockSpec((1, ts, dm), lambda b, s: (b + b0, s, 0)),
                  pl.BlockSpec((1, 6, dm), lambda b, s: (b + b0, 0, 0)),
                  full(g1), full(w_in_r), full(gbias), full(conv_w), full(conv_b), full(wqk),
                  full(ng), full(skip), full(wpool), full(bpool), full(pscale),
                  pl.BlockSpec((1,) + pinv.shape[1:], lambda b, s: (jnp.minimum(s, 1), 0, 0)),
                  full(w_out), full(g2), full(wrt), full(rbias)],
        out_specs=[pl.BlockSpec((ts, dm), lambda b, s: (b * nst + s, 0)),
                   pl.BlockSpec((ts, dm // 2), lambda b, s: (b * nst + s, 0)),
                   pl.BlockSpec((LOGIT_ROWS, ts), lambda b, s: (0, b * nst + s))],
        out_shape=[jax.ShapeDtypeStruct((n_tok, dm), F32),
                   jax.ShapeDtypeStruct((n_tok, dm // 2), I32),
                   jax.ShapeDtypeStruct((LOGIT_ROWS, n_tok), F32)],
        scratch_shapes=[pltpu.VMEM((UHIST + ts, dml), F32),
                        pltpu.VMEM((PHIST + ts, dm - dml), F32),
                        pltpu.VMEM((ts, ncols - dm), F32),
                        pltpu.VMEM((ts, dml), F32),
                        pltpu.VMEM((ts, 2 * dml), F32),
                        pltpu.VMEM((ts, LANES), F32),
                        pltpu.VMEM((ts, dm), BF16),
                        pltpu.VMEM((ts + PHIST, 2 * LANES), F32),
                        pltpu.VMEM((N_HEADS, HEAD_DIM, HEAD_DIM), F32),
                        pltpu.VMEM((N_HEADS, HEAD_DIM, HEAD_DIM), F32),
                        pltpu.VMEM((N_HEADS, CHUNK, LANES), F32)],
        compiler_params=pltpu.CompilerParams(dimension_semantics=("arbitrary", "arbitrary"),
                                             vmem_limit_bytes=VMEM_LIMIT),
        name="mixer",
    )(x, mod, g1, w_in_r, gbias, conv_w, conv_b, wqk, ng, skip, wpool, bpool, pscale, pinv, w_out,
      g2, wrt, rbias)


def _route_kernel(lgt_ref, idx_ref, gate_ref, cnt_ref, carry_s, *, tr, n_sub):
    @pl.when(pl.program_id(0) == 0)
    def _():
        carry_s[...] = jnp.zeros_like(carry_s)

    tr_r = lax.broadcasted_iota(I32, (tr, tr), 0)
    tr_c = lax.broadcasted_iota(I32, (tr, tr), 1)
    upper = jnp.where(tr_r < tr_c, 1.0, 0.0).astype(BF16)
    for q in range(n_sub):
        cols = slice(q * tr, (q + 1) * tr)
        idx, gates = _route_tile(lgt_ref[:, cols], upper, carry_s, tr)
        idx_ref[:, cols] = idx
        gate_ref[:, cols] = gates
    cnt_ref[...] = carry_s[...]


def _route_tile(lg, upper, carry_s, tr):
    best = lg[0:1]
    gidx = jnp.zeros((1, tr), I32)
    for j in range(1, N_GROUPS):
        cand = lg[j:j + 1]
        better = cand > best
        gidx = jnp.where(better, j, gidx)
        best = jnp.where(better, cand, best)
    sumexp = jnp.zeros((1, tr), F32)
    for j in range(N_GROUPS):
        sumexp = sumexp + jnp.exp(lg[j:j + 1] - best)
    g_gate = 1.0 / sumexp

    sel = lg[SUBLANES:2 * SUBLANES]
    for j in range(1, N_GROUPS):
        sel = jnp.where(gidx == j, lg[SUBLANES * (j + 1):SUBLANES * (j + 2)], sel)
    sub = lax.broadcasted_iota(I32, (EXPERTS_PER_GROUP, tr), 0)
    v1 = jnp.max(sel, axis=0, keepdims=True)
    i1 = jnp.min(jnp.where(sel == v1, sub, EXPERTS_PER_GROUP), axis=0, keepdims=True)
    sel2 = jnp.where(sub == i1, -jnp.inf, sel)
    v2 = jnp.max(sel2, axis=0, keepdims=True)
    i2 = jnp.min(jnp.where(sel2 == v2, sub, EXPERTS_PER_GROUP), axis=0, keepdims=True)
    e2 = jnp.exp(v2 - v1)
    den = 1.0 + e2
    gate0 = (1.0 / den) * g_gate
    gate1 = (e2 / den) * g_gate
    ex0 = gidx * EXPERTS_PER_GROUP + i1
    ex1 = gidx * EXPERTS_PER_GROUP + i2

    erow = lax.broadcasted_iota(I32, (N_EXPERTS, tr), 0)
    oh0 = erow == ex0
    oh1 = erow == ex1
    oh = jnp.where(oh0 | oh1, 1.0, 0.0).astype(BF16)
    carry = carry_s[...]
    before = jnp.dot(oh, upper, preferred_element_type=F32)
    before = before + jnp.concatenate([carry] * (tr // LANES), axis=1)
    rank0 = jnp.sum(jnp.where(oh0, before, 0.0), axis=0, keepdims=True)
    rank1 = jnp.sum(jnp.where(oh1, before, 0.0), axis=0, keepdims=True)
    carry_s[...] = carry + jnp.dot(oh, jnp.ones((tr, LANES), BF16), preferred_element_type=F32)

    idx = jnp.concatenate([ex0, ex1, rank0.astype(I32), rank1.astype(I32),
                           jnp.zeros((SUBLANES - 4, tr), I32)], axis=0)
    gates = jnp.concatenate([gate0, gate1, jnp.zeros((SUBLANES - 2, tr), F32)], axis=0)
    return idx, gates


def _route(lgt, n_tok):
    tr = ROUTE_TILE
    n_sub = min(ROUTE_SUBTILES, n_tok // tr)
    step = tr * n_sub
    assert n_tok % step == 0
    return pl.pallas_call(
        functools.partial(_route_kernel, tr=tr, n_sub=n_sub),
        grid=(n_tok // step,),
        in_specs=[pl.BlockSpec((LOGIT_ROWS, step), lambda i: (0, i))],
        out_specs=[pl.BlockSpec((SUBLANES, step), lambda i: (0, i)),
                   pl.BlockSpec((SUBLANES, step), lambda i: (0, i)),
                   pl.BlockSpec((N_EXPERTS, LANES), lambda i: (0, 0))],
        out_shape=[jax.ShapeDtypeStruct((SUBLANES, n_tok), I32),
                   jax.ShapeDtypeStruct((SUBLANES, n_tok), F32),
                   jax.ShapeDtypeStruct((N_EXPERTS, LANES), F32)],
        scratch_shapes=[pltpu.VMEM((N_EXPERTS, LANES), F32)],
        compiler_params=pltpu.CompilerParams(dimension_semantics=("arbitrary",),
                                             vmem_limit_bytes=VMEM_LIMIT),
        name="route",
    )(lgt)


def _sc_workers():
    info = plsc.get_sparse_core_info()
    return info.num_cores, info.num_cores * info.num_subcores


def _dispatch(h2p, dest0, dest1, n_slots):
    n_tok, width = h2p.shape
    n_cores, n_workers = _sc_workers()
    per_w = n_tok // n_workers
    ch = min(SC_SCATTER_CHUNK, per_w)
    n_ch = per_w // ch
    assert n_tok % n_workers == 0 and per_w % ch == 0 and ch % SUBLANES == 0 and n_ch % 2 == 0
    mesh = plsc.VectorSubcoreMesh(core_axis_name="c", subcore_axis_name="s")

    @functools.partial(
        pl.kernel, mesh=mesh,
        out_type=jax.ShapeDtypeStruct((n_slots, width), h2p.dtype),
        scratch_types=[pltpu.VMEM((2, ch), I32), pltpu.VMEM((2, ch), I32),
                       pltpu.VMEM((2, ch, width), h2p.dtype),
                       pltpu.SemaphoreType.DMA((2,)), pltpu.SemaphoreType.DMA((2,))],
        name="dispatch")
    def scatter(h_hbm, d0_hbm, d1_hbm, xs_hbm, i0_v, i1_v, rows_v, sem_in, sem_out):
        wid = lax.axis_index("s") * n_cores + lax.axis_index("c")
        base = wid * per_w

        def loads(t0, slot):
            return (pltpu.make_async_copy(d0_hbm.at[pl.ds(t0, ch)], i0_v.at[slot], sem_in.at[slot]),
                    pltpu.make_async_copy(d1_hbm.at[pl.ds(t0, ch)], i1_v.at[slot], sem_in.at[slot]),
                    pltpu.make_async_copy(h_hbm.at[pl.ds(t0, ch)], rows_v.at[slot], sem_in.at[slot]))

        def scatters(slot):
            return (pltpu.make_async_copy(rows_v.at[slot], xs_hbm.at[i0_v.at[slot]], sem_out.at[slot]),
                    pltpu.make_async_copy(rows_v.at[slot], xs_hbm.at[i1_v.at[slot]], sem_out.at[slot]))

        for cp in loads(base, 0):
            cp.start()

        @pl.loop(0, n_ch, step=2)
        def _(k):
            for slot in range(2):
                for cp in loads(base + (k + slot) * ch, slot):
                    cp.wait()
                out_cps = scatters(slot)
                for cp in out_cps:
                    cp.start()
                nxt = k + slot + 1

                @pl.when(nxt < n_ch)
                def _():
                    for cp in loads(base + nxt * ch, 1 - slot):
                        cp.start()

                for cp in out_cps:
                    cp.wait()

    return scatter(h2p, dest0, dest1)


def _collect(ys, dest0, dest1, tok0, n_tok):
    width = ys.shape[1]
    n_cores, n_workers = _sc_workers()
    per_w = n_tok // n_workers
    ch = min(SC_GATHER_CHUNK, per_w)
    n_ch = per_w // ch
    assert n_tok % n_workers == 0 and per_w % ch == 0 and ch % SUBLANES == 0 and n_ch % 2 == 0
    mesh = plsc.VectorSubcoreMesh(core_axis_name="c", subcore_axis_name="s")

    @functools.partial(
        pl.kernel, mesh=mesh,
        out_type=jax.ShapeDtypeStruct((TOP_K, n_tok, width), ys.dtype),
        scratch_types=[pltpu.VMEM((2, ch), I32), pltpu.VMEM((2, ch), I32),
                       pltpu.VMEM((2, ch, width), ys.dtype), pltpu.VMEM((2, ch, width), ys.dtype),
                       pltpu.SemaphoreType.DMA((2,)), pltpu.SemaphoreType.DMA((2,)),
                       pltpu.SemaphoreType.DMA((2,))],
        name="collect")
    def gather(ys_hbm, d0_hbm, d1_hbm, o_hbm, i0_v, i1_v, r0_v, r1_v, sem_idx, sem_in, sem_out):
        wid = lax.axis_index("s") * n_cores + lax.axis_index("c")
        base = wid * per_w

        def idx_loads(t0, slot):
            return (pltpu.make_async_copy(d0_hbm.at[pl.ds(tok0 + t0, ch)], i0_v.at[slot], sem_idx.at[slot]),
                    pltpu.make_async_copy(d1_hbm.at[pl.ds(tok0 + t0, ch)], i1_v.at[slot], sem_idx.at[slot]))

        def gathers(slot):
            return (pltpu.make_async_copy(ys_hbm.at[i0_v.at[slot]], r0_v.at[slot], sem_in.at[slot]),
                    pltpu.make_async_copy(ys_hbm.at[i1_v.at[slot]], r1_v.at[slot], sem_in.at[slot]))

        def stores(t0, slot):
            return (pltpu.make_async_copy(r0_v.at[slot], o_hbm.at[0, pl.ds(t0, ch)], sem_out.at[slot]),
                    pltpu.make_async_copy(r1_v.at[slot], o_hbm.at[1, pl.ds(t0, ch)], sem_out.at[slot]))

        def start_chunk(t0, slot):
            for cp in idx_loads(t0, slot):
                cp.start()
            for cp in idx_loads(t0, slot):
                cp.wait()
            for cp in gathers(slot):
                cp.start()

        start_chunk(base, 0)

        @pl.loop(0, n_ch, step=2)
        def _(k):
            for slot in range(2):
                nxt = k + slot + 1

                @pl.when(nxt < n_ch)
                def _():
                    start_chunk(base + nxt * ch, 1 - slot)

                for cp in gathers(slot):
                    cp.wait()
                out_cps = stores(base + (k + slot) * ch, slot)
                for cp in out_cps:
                    cp.start()
                for cp in out_cps:
                    cp.wait()

    return gather(ys, dest0, dest1)


def _experts_kernel(first_ref, nblk_ref, cnt_ref, xs_hbm, wg_ref, wu_ref, wd_ref, ys_hbm,
                    wgu_s, wd_s, xbuf, ybuf, sem_in, sem_out, *, de, blk):
    e = pl.program_id(0)
    n_exp = pl.num_programs(0)
    wgu_s[:, 0:de] = wg_ref[0].astype(BF16)
    wgu_s[:, de:2 * de] = wu_ref[0].astype(BF16)
    wd_s[...] = wd_ref[0].astype(BF16)
    first = first_ref[e]
    n_blk = nblk_ref[e]
    count = cnt_ref[e]
    total = first_ref[n_exp - 1] + nblk_ref[n_exp - 1]

    def in_copy(g, slot):
        return pltpu.make_async_copy(xs_hbm.at[pl.ds(pl.multiple_of(g * blk, blk), blk)],
                                     xbuf.at[slot], sem_in.at[slot])

    def out_copy(g, slot):
        return pltpu.make_async_copy(ybuf.at[slot],
                                     ys_hbm.at[pl.ds(pl.multiple_of(g * blk, blk), blk)],
                                     sem_out.at[slot])

    @pl.when(e == 0)
    def _():
        in_copy(0, 0).start(priority=1)
        ybuf[...] = jnp.zeros_like(ybuf)

    def block(j, carry):
        g = first + j
        slot = lax.rem(g, 2)
        in_copy(g, slot).wait()

        @pl.when(g + 1 < total)
        def _():
            in_copy(g + 1, 1 - slot).start(priority=1)

        @pl.when(g >= 2)
        def _():
            out_copy(g - 2, slot).wait()

        n_left = count - j * blk

        def ffn(n_rows):
            words = xbuf[slot, 0:n_rows, :]
            rows = lax.broadcasted_iota(I32, words.shape, 0)
            xb = _unpack_bf16_pairs(jnp.where(rows < n_left, words, 0)).astype(BF16)
            ab = jnp.dot(xb, wgu_s[...], preferred_element_type=F32)
            a = ab[:, 0:de]
            b = ab[:, de:2 * de]
            hmid = (a * _sigmoid(a)) * b
            y = jnp.dot(hmid.astype(BF16), wd_s[...], preferred_element_type=F32)
            ybuf[slot, 0:n_rows, :] = _pack_bf16_pairs(y)

        @pl.when(n_left > blk // 2)
        def _():
            ffn(blk)

        @pl.when(n_left <= blk // 2)
        def _():
            ffn(blk // 2)

        out_copy(g, slot).start(priority=1)
        return carry

    lax.fori_loop(0, n_blk, block, 0)

    @pl.when(e == n_exp - 1)
    def _():
        @pl.when(total >= 2)
        def _():
            out_copy(total - 2, lax.rem(total, 2)).wait()

        out_copy(total - 1, lax.rem(total + 1, 2)).wait()


def _experts(plan, xs, w_gate, w_up, w_down):
    n_slots, width = xs.shape
    n_exp, dm, de = w_gate.shape
    blk = EXPERT_BLOCK
    return pl.pallas_call(
        functools.partial(_experts_kernel, de=de, blk=blk),
        grid_spec=pltpu.PrefetchScalarGridSpec(
            num_scalar_prefetch=3,
            grid=(n_exp,),
            in_specs=[pl.BlockSpec(memory_space=pl.ANY),
                      pl.BlockSpec((1, dm, de), lambda e, f, n, c: (e, 0, 0)),
                      pl.BlockSpec((1, dm, de), lambda e, f, n, c: (e, 0, 0)),
                      pl.BlockSpec((1, de, dm), lambda e, f, n, c: (e, 0, 0))],
            out_specs=pl.BlockSpec(memory_space=pl.ANY),
            scratch_shapes=[pltpu.VMEM((dm, 2 * de), BF16), pltpu.VMEM((de, dm), BF16),
                            pltpu.VMEM((2, blk, width), I32), pltpu.VMEM((2, blk, width), I32),
                            pltpu.SemaphoreType.DMA((2,)), pltpu.SemaphoreType.DMA((2,))]),
        out_shape=jax.ShapeDtypeStruct((n_slots, width), I32),
        compiler_params=pltpu.CompilerParams(dimension_semantics=("arbitrary",),
                                             vmem_limit_bytes=VMEM_LIMIT),
        name="experts",
    )(*plan, xs, w_gate, w_up, w_down)


def _pick(onehot, table):
    return jnp.sum(jnp.where(onehot, table[None, :], 0), axis=1).astype(I32)


def _expert_plan(counts, n_assign):
    blk = EXPERT_BLOCK
    per_e = (counts + blk - 1) // blk
    first_blk = jnp.cumsum(per_e) - per_e
    n_slots = (n_assign // blk + N_EXPERTS) * blk
    return (first_blk * blk).astype(I32), (first_blk.astype(I32), per_e.astype(I32), counts), n_slots


def _slot_of(starts, expert, rank):
    onehot = jnp.arange(N_EXPERTS, dtype=I32)[None, :] == expert[:, None]
    return _pick(onehot, starts) + rank


def _combine_kernel(rows_ref, gate_ref, x1_ref, mod_ref, fg_ref, *rest):
    o_ref = rest[-1]
    tc = x1_ref.shape[0]
    gate_f = mod_ref[0][5:6]
    for q in range(tc // LANES):
        tok = slice(q * LANES, (q + 1) * LANES)
        g_rows = jnp.concatenate([gate_ref[:, tok], jnp.zeros((LANES - SUBLANES, LANES), F32)], axis=0)
        gc = g_rows.T
        y = (gc[:, 0:1] * _unpack_bf16_pairs(rows_ref[0, tok, :])
             + gc[:, 1:2] * _unpack_bf16_pairs(rows_ref[1, tok, :]))
        x2 = x1_ref[tok, :] + gate_f * y
        r = lax.rsqrt(jnp.mean(x2 * x2, axis=-1, keepdims=True) + EPS)
        o_ref[tok, :] = (x2 * r) * fg_ref[...]


def _combine(rows, gcol, x1, mod, final_g, seq, b0, nb, bsz, out_prev):
    dm = x1.shape[1]
    width = rows.shape[2]
    tc = min(COMBINE_TILE, seq)
    nst = seq // tc
    in_specs = [pl.BlockSpec((TOP_K, tc, width), lambda b, s: (0, b * nst + s, 0)),
                pl.BlockSpec((SUBLANES, tc), lambda b, s: (0, (b + b0) * nst + s)),
                pl.BlockSpec((tc, dm), lambda b, s: ((b + b0) * nst + s, 0)),
                pl.BlockSpec((1, 6, dm), lambda b, s: (b + b0, 0, 0)),
                pl.BlockSpec((1, dm), lambda b, s: (0, 0))]
    args = [rows, gcol, x1, mod, final_g.reshape(1, dm)]
    aliases = {}
    if out_prev is not None:
        in_specs.append(pl.BlockSpec(memory_space=pl.ANY))
        args.append(out_prev)
        aliases = {len(args) - 1: 0}
    return pl.pallas_call(
        _combine_kernel,
        grid=(nb, nst),
        in_specs=in_specs,
        out_specs=pl.BlockSpec((tc, dm), lambda b, s: ((b + b0) * nst + s, 0)),
        out_shape=jax.ShapeDtypeStruct((bsz * seq, dm), F32),
        input_output_aliases=aliases,
        compiler_params=pltpu.CompilerParams(dimension_semantics=("arbitrary", "arbitrary"),
                                             vmem_limit_bytes=VMEM_LIMIT),
        name="combine",
    )(*args)


def _layer(x, c, ada_w, ada_b, norm1_g, w_in, conv_w, conv_b, w_q, w_k, b_igate, b_fgate,
           mlstm_norm_g, mlstm_skip, w_pool, b_pool, pool_scale, w_out, norm2_g,
           w_rg, b_rg, w_re, b_re, w_eg, w_eu, w_ed, out_g):
    bsz, seq, dm = x.shape
    dml = conv_w.shape[1]
    n_tok = bsz * seq
    ts = min(SEQ_TILE, seq)

    mod = _ada(c, ada_w, ada_b).reshape(bsz, 6, dm)

    col_v = dml
    col_o = 2 * dml
    col_i = 3 * dml
    col_p = col_i + 2 * N_HEADS
    w_gate_cols = jnp.pad(w_in[:, col_i:col_p], ((0, 0), (0, LANES - 2 * N_HEADS)))
    w_in_r = jnp.concatenate([w_in[:, :col_v], w_in[:, col_p:], w_in[:, col_v:col_o],
                              w_in[:, col_o:col_i], w_gate_cols], axis=1).astype(BF16)
    gbias = jnp.pad(jnp.concatenate([b_igate, b_fgate]), (0, LANES - 2 * N_HEADS)).reshape(1, LANES)
    wqk = jnp.concatenate([w_q, w_k], axis=-1).astype(BF16)
    wrt = jnp.zeros((LOGIT_ROWS, dm), F32)
    wrt = wrt.at[0:N_GROUPS].set(w_rg.T).at[SUBLANES:SUBLANES + N_EXPERTS].set(w_re.T).astype(BF16)
    rb = jnp.zeros((LOGIT_ROWS,), F32).at[0:N_GROUPS].set(b_rg).at[SUBLANES:SUBLANES + N_EXPERTS].set(b_re)
    rbias = jnp.broadcast_to(rb[:, None], (LOGIT_ROWS, ts))

    gdim = (dm - dml) // len(POOL_WINDOWS)
    win = jnp.repeat(jnp.array(POOL_WINDOWS, F32), gdim)[None, :]
    t1 = jnp.arange(1, ts + 1, dtype=F32)[:, None]
    pinv = jnp.stack([1.0 / jnp.minimum(t1, win), jnp.broadcast_to(1.0 / win, (ts, dm - dml))])

    mixer_params = (norm1_g.reshape(1, dm), w_in_r, gbias, conv_w, conv_b.reshape(1, dml), wqk,
                    mlstm_norm_g.reshape(1, dml), mlstm_skip.reshape(1, dml), w_pool.astype(BF16),
                    b_pool.reshape(1, dm - dml), pool_scale.reshape(1, dm - dml), pinv,
                    w_out.astype(BF16), norm2_g.reshape(1, dm), wrt, rbias)

    x1, h2, lgt = _mixer(x, mod, *mixer_params, 0, bsz)
    idx, gcol, cnt = _route(lgt, n_tok)
    counts = cnt[:, 0].astype(I32)
    starts, plan, n_slots = _expert_plan(counts, n_tok * TOP_K)
    dest0 = _slot_of(starts, idx[0], idx[2])
    dest1 = _slot_of(starts, idx[1], idx[3])
    xs = _dispatch(h2, dest0, dest1, n_slots)
    ys = _experts(plan, xs, w_eg, w_eu, w_ed)

    nb = bsz // COMBINE_GROUPS
    assert bsz % COMBINE_GROUPS == 0
    out = None
    for b0 in range(0, bsz, nb):
        rows = _collect(ys, dest0, dest1, b0 * seq, nb * seq)
        out = _combine(rows, gcol, x1, mod, out_g, seq, b0, nb, bsz, out)
    return out.reshape(bsz, seq, dm)


def kernel(x, c, ada_w, ada_b, norm1_g, w_in, conv_w, conv_b, w_q, w_k, b_igate, b_fgate, mlstm_norm_g, mlstm_skip, w_pool, b_pool, pool_scale, w_out, norm2_g, w_router_group, b_router_group, w_router_expert, b_router_expert, w_expert_gate, w_expert_up, w_expert_down, final_g):
    depth = ada_w.shape[0]
    assert depth == 1, "the final norm is fused into the last layer's combine kernel"
    l = 0
    return _layer(x, c, ada_w[l], ada_b[l], norm1_g[l], w_in[l], conv_w[l], conv_b[l], w_q[l],
                  w_k[l], b_igate[l], b_fgate[l], mlstm_norm_g[l], mlstm_skip[l], w_pool[l],
                  b_pool[l], pool_scale[l], w_out[l], norm2_g[l], w_router_group[l],
                  b_router_group[l], w_router_expert[l], b_router_expert[l],
                  w_expert_gate[l], w_expert_up[l], w_expert_down[l], final_g)
```

```python
import functools

import jax
import jax.numpy as jnp
from jax import lax
from jax.experimental import pallas as pl
from jax.experimental.pallas import tpu as pltpu
from jax.experimental.pallas import tpu_sc as plsc

F32 = jnp.float32
BF16 = jnp.bfloat16
I32 = jnp.int32
U32 = jnp.uint32

EPS = 1e-6
N_HEADS = 4
HEAD_DIM = 128
CONV_WIDTH = 4
POOL_WINDOWS = (2, 4, 8, 16)
N_GROUPS = 4
EXPERTS_PER_GROUP = 8
N_EXPERTS = N_GROUPS * EXPERTS_PER_GROUP
TOP_K = 2

LANES = 128
SUBLANES = 8
CHUNK = 128
SEQ_TILE = 512
ADA_TILE = 1024
COMBINE_GROUPS = 8
ROUTE_TILE = 512
ROUTE_SUBTILES = 4
SC_SCATTER_CHUNK = 64
SC_GATHER_CHUNK = 32
COMBINE_TILE = 512
EXPERT_BLOCK = 1024
EXPERT_BLOCK_PATHS = 4
LOGIT_ROWS = 48
UHIST = 8
PHIST = 16
VMEM_LIMIT = 60 * 1024 * 1024


def _sigmoid(x):
    return 1.0 / (1.0 + jnp.exp(-x))


def _pack_bf16_pairs(x):
    w = x.shape[1] // 2
    half_ulp = jnp.uint32(0x8000)
    hi = lax.bitcast_convert_type(x[:, :w], U32) + half_ulp
    lo = lax.bitcast_convert_type(x[:, w:], U32) + half_ulp
    return lax.bitcast_convert_type((hi & jnp.uint32(0xFFFF0000)) | (lo >> 16), I32)


def _unpack_bf16_pairs(words):
    u = lax.bitcast_convert_type(words, U32)
    hi = lax.bitcast_convert_type(u & jnp.uint32(0xFFFF0000), F32)
    lo = lax.bitcast_convert_type(u << 16, F32)
    return jnp.concatenate([hi, lo], axis=1)


def _ada_kernel(c_ref, w_ref, b_ref, o_ref):
    c = c_ref[...]
    s = c * _sigmoid(c)
    o_ref[...] = jnp.dot(s, w_ref[...], preferred_element_type=F32,
                         precision=lax.Precision.HIGHEST) + b_ref[...]


def _ada(c, ada_w, ada_b):
    bsz, dm = c.shape
    n = ada_w.shape[1]
    tn = ADA_TILE
    return pl.pallas_call(
        _ada_kernel,
        grid=(n // tn,),
        in_specs=[pl.BlockSpec((bsz, dm), lambda j: (0, 0)),
                  pl.BlockSpec((dm, tn), lambda j: (0, j)),
                  pl.BlockSpec((1, tn), lambda j: (0, j))],
        out_specs=pl.BlockSpec((bsz, tn), lambda j: (0, j)),
        out_shape=jax.ShapeDtypeStruct((bsz, n), F32),
        compiler_params=pltpu.CompilerParams(dimension_semantics=("arbitrary",),
                                             vmem_limit_bytes=VMEM_LIMIT),
        name="ada",
    )(c, ada_w, ada_b.reshape(1, n))


def _split3(x):
    hi = x.astype(BF16)
    r1 = x - hi.astype(F32)
    mid = r1.astype(BF16)
    lo = (r1 - mid.astype(F32)).astype(BF16)
    return hi, mid, lo


def _mixer_kernel(x_ref, mod_ref, g1_ref, win_ref, gbias_ref, convw_ref, convb_ref, wqk_ref,
                  ng_ref, skip_ref, wpool_ref, bpool_ref, pscale_ref, pinv_ref, wout_ref, g2_ref,
                  wrt_ref, rbias_ref,
                  x1_ref, h2_ref, lgt_ref,
                  uext_s, pext_s, proj_s, uc_s, qk_s, gate_s, mix_s, pool4_s, ctv_s, ctn_s, mprev_s,
                  *, ts, dm, dml):
    s_idx = pl.program_id(1)
    n_chunks = ts // CHUNK
    dp = dm - dml

    @pl.when(s_idx == 0)
    def _():
        uext_s[0:UHIST, :] = jnp.zeros((UHIST, dml), F32)
        pext_s[0:PHIST, :] = jnp.zeros((PHIST, dp), F32)
        ctv_s[...] = jnp.zeros_like(ctv_s)
        ctn_s[...] = jnp.zeros_like(ctn_s)
        mprev_s[...] = jnp.zeros_like(mprev_s)

    row_i = lax.broadcasted_iota(I32, (CHUNK, CHUNK), 0)
    col_i = lax.broadcasted_iota(I32, (CHUNK, CHUNK), 1)
    causal = row_i >= col_i
    tril = jnp.where(causal, 1.0, 0.0).astype(BF16)
    lane_c = lax.broadcasted_iota(I32, (CHUNK, LANES), 1)
    ones_blk = jnp.ones((CHUNK, HEAD_DIM), BF16)
    q_scale = HEAD_DIM ** -0.5

    col_o = dml
    col_g = 2 * dml

    x = x_ref[0]
    mod = mod_ref[0]
    r = lax.rsqrt(jnp.mean(x * x, axis=-1, keepdims=True) + EPS)
    h = (x * r) * (g1_ref[...] * (1.0 + mod[1:2])) + mod[0:1]
    res = jnp.dot(h.astype(BF16), win_ref[...], preferred_element_type=F32)
    uext_s[UHIST:, :] = res[:, 0:dml]
    pext_s[PHIST:, :] = res[:, dml:dm]
    proj_s[...] = res[:, dm:]

    acc = None
    for j in reversed(range(CONV_WIDTH)):
        tap = uext_s[pl.ds(UHIST - (CONV_WIDTH - 1 - j), ts), :] * convw_ref[j:j + 1, :]
        acc = tap if acc is None else acc + tap
    conv = acc + convb_ref[...]
    uc = conv * _sigmoid(conv)
    uc_s[...] = uc

    for hd in range(N_HEADS):
        qk_s[:, 2 * HEAD_DIM * hd:2 * HEAD_DIM * (hd + 1)] = jnp.dot(
            uc[:, HEAD_DIM * hd:HEAD_DIM * (hd + 1)].astype(BF16), wqk_ref[hd],
            preferred_element_type=F32)

    g = proj_s[:, col_g:col_g + LANES] + gbias_ref[...]
    lane = lax.broadcasted_iota(I32, (ts, LANES), 1)
    logf = -(jnp.maximum(-g, 0.0) + jnp.log1p(jnp.exp(-jnp.abs(g))))
    gate_s[...] = jnp.where(lane < N_HEADS, g, logf)

    pairs = [(c, hd) for c in range(n_chunks) for hd in range(N_HEADS)]
    bcums, comb_ts = [], []
    for c in range(n_chunks):
        gc = gate_s[pl.ds(c * CHUNK, CHUNK), :]
        hi, mid, lo = _split3(gc)
        cs = jnp.dot(tril, jnp.concatenate([hi, mid, lo], axis=1), preferred_element_type=F32)
        bcum = cs[:, 0:LANES] + cs[:, LANES:2 * LANES] + cs[:, 2 * LANES:3 * LANES]
        bcums.append(bcum)
        comb_ts.append(jnp.where(lane_c < N_HEADS, gc, bcum).T)

    def rows(ref, c, lo_col, width=HEAD_DIM):
        return ref[pl.ds(c * CHUNK, CHUNK), lo_col:lo_col + width]

    b_bcs, dlogs, rmaxs, p_mats = {}, {}, {}, {}
    for c, hd in pairs:
        b_bc = jnp.broadcast_to(bcums[c][:, N_HEADS + hd:N_HEADS + hd + 1], (CHUNK, CHUNK))
        i_row = comb_ts[c][hd:hd + 1, :]
        b_row = comb_ts[c][N_HEADS + hd:N_HEADS + hd + 1, :]
        dlog = jnp.where(causal, (b_bc - b_row) + i_row, -jnp.inf)
        b_bcs[c, hd], dlogs[c, hd] = b_bc, dlog
        rmaxs[c, hd] = jnp.max(dlog, axis=-1, keepdims=True)
        q_c = rows(qk_s, c, 2 * HEAD_DIM * hd) * q_scale
        k_c = rows(qk_s, c, 2 * HEAD_DIM * hd + HEAD_DIM)
        p_mats[c, hd] = lax.dot_general(q_c.astype(BF16), k_c.astype(BF16),
                                        (((1,), (1,)), ((), ())), preferred_element_type=F32)

    inters, m_ts = {}, {}
    for hd in range(N_HEADS):
        m_prev = mprev_s[hd]
        for c in range(n_chunks):
            inter = b_bcs[c, hd] + m_prev
            m_t = jnp.maximum(inter, rmaxs[c, hd])
            inters[c, hd], m_ts[c, hd] = inter, m_t
            m_prev = jnp.broadcast_to(m_t[CHUNK - 1:CHUNK, :], (CHUNK, LANES))
        mprev_s[hd] = m_prev

    lhs, v_augs, upds, a_prevs, e_negms = {}, {}, {}, {}, {}
    for c, hd in pairs:
        m_t = m_ts[c, hd]
        wm = jnp.exp(dlogs[c, hd] - m_t)
        a_inter = jnp.exp(inters[c, hd] - m_t)
        e_negms[c, hd] = jnp.exp(-m_t)
        q_c = rows(qk_s, c, 2 * HEAD_DIM * hd) * q_scale
        k_c = rows(qk_s, c, 2 * HEAD_DIM * hd + HEAD_DIM)
        v_c = rows(proj_s, c, HEAD_DIM * hd)
        s_mat = (p_mats[c, hd] * wm).astype(BF16)
        qa = (q_c * a_inter).astype(BF16)
        lhs[c, hd] = jnp.concatenate([s_mat, qa], axis=1)
        v_aug = jnp.concatenate([v_c.astype(BF16), ones_blk], axis=1)
        v_augs[c, hd] = v_aug
        ktw = (k_c.T * wm[CHUNK - 1:CHUNK, :]).astype(BF16)
        upds[c, hd] = jnp.dot(ktw, v_aug, preferred_element_type=F32)
        a_prevs[c, hd] = a_inter[CHUNK - 1:CHUNK, :]

    ct_in = {}
    for hd in range(N_HEADS):
        ctv, ctn = ctv_s[hd], ctn_s[hd]
        for c in range(n_chunks):
            ct_in[c, hd] = jnp.concatenate([ctv, ctn], axis=1).astype(BF16)
            ctv = a_prevs[c, hd] * ctv + upds[c, hd][:, 0:HEAD_DIM]
            ctn = a_prevs[c, hd] * ctn + upds[c, hd][:, HEAD_DIM:2 * HEAD_DIM]
        ctv_s[hd], ctn_s[hd] = ctv, ctn

    for c, hd in pairs:
        numden = jnp.dot(lhs[c, hd], jnp.concatenate([v_augs[c, hd], ct_in[c, hd]], axis=0),
                         preferred_element_type=F32)
        num = numden[:, 0:HEAD_DIM]
        den = numden[:, HEAD_DIM:2 * HEAD_DIM]
        hh = num / jnp.maximum(jnp.abs(den), e_negms[c, hd])
        ms = jnp.mean(hh * hh, axis=-1, keepdims=True)
        hn = hh * lax.rsqrt(ms + EPS) * ng_ref[:, HEAD_DIM * hd:HEAD_DIM * (hd + 1)]
        o_c = rows(proj_s, c, col_o + HEAD_DIM * hd)
        uc_c = rows(uc_s, c, HEAD_DIM * hd)
        out_c = _sigmoid(o_c) * (hn + skip_ref[:, HEAD_DIM * hd:HEAD_DIM * (hd + 1)] * uc_c)
        mix_s[pl.ds(c * CHUNK, CHUNK), HEAD_DIM * hd:HEAD_DIM * (hd + 1)] = out_c.astype(BF16)

    def pe(shift, rows, lanes):
        return pext_s[pl.ds(PHIST - shift, rows), lanes]

    gd = LANES
    sums = []
    for gi in range(2):
        lanes = slice(gd * gi, gd * (gi + 1))
        tot = pe(0, ts, lanes)
        for j in range(1, POOL_WINDOWS[gi]):
            tot = tot + pe(j, ts, lanes)
        sums.append(tot)
    wide = slice(2 * gd, 4 * gd)
    s4 = pe(12, ts + 12, wide)
    for j in range(1, 4):
        s4 = s4 + pe(12 + j, ts + 12, wide)
    pool4_s[0:ts + 12, :] = s4
    s8 = pool4_s[pl.ds(4, ts + 8), :] + pool4_s[pl.ds(0, ts + 8), :]
    sums.append(s8[8:, 0:gd])
    sums.append(s8[8:, gd:2 * gd] + s8[0:ts, gd:2 * gd])
    for gi in range(len(POOL_WINDOWS)):
        lanes = slice(gd * gi, gd * (gi + 1))
        pooled = sums[gi] * pinv_ref[0, :, lanes] - pe(0, ts, lanes)
        yp = jnp.dot(pooled.astype(BF16), wpool_ref[gi], preferred_element_type=F32)
        yp = (yp + bpool_ref[:, lanes]) * pscale_ref[:, lanes]
        mix_s[:, dml + gd * gi:dml + gd * (gi + 1)] = yp.astype(BF16)

    mix = jnp.dot(mix_s[...], wout_ref[...], preferred_element_type=F32)
    x1 = x + mod[2:3] * mix
    x1_ref[...] = x1
    r2 = lax.rsqrt(jnp.mean(x1 * x1, axis=-1, keepdims=True) + EPS)
    h2 = (x1 * r2) * (g2_ref[...] * (1.0 + mod[4:5])) + mod[3:4]
    h2_ref[...] = _pack_bf16_pairs(h2)
    lgt_ref[...] = lax.dot_general(wrt_ref[...], h2.astype(BF16), (((1,), (1,)), ((), ())),
                                   preferred_element_type=F32) + rbias_ref[...]

    uext_s[0:UHIST, :] = uext_s[ts:ts + UHIST, :]
    pext_s[0:PHIST, :] = pext_s[ts:ts + PHIST, :]


def _mixer(x, mod, g1, w_in_r, gbias, conv_w, conv_b, wqk, ng, skip, wpool, bpool, pscale, pinv,
           w_out, g2, wrt, rbias, b0, nb):
    _, seq, dm = x.shape
    dml = conv_w.shape[1]
    ts = min(SEQ_TILE, seq)
    ncols = w_in_r.shape[1]
    nst = seq // ts
    n_tok = nb * seq
    assert seq % ts == 0 and ts % CHUNK == 0
    full = lambda a: pl.BlockSpec(a.shape, lambda b, s: (0,) * a.ndim)
    kern = functools.partial(_mixer_kernel, ts=ts, dm=dm, dml=dml)
    return pl.pallas_call(
        kern,
        grid=(nb, nst),
        in_specs=[pl.BlockSpec((1, ts, dm), lambda b, s: (b + b0, s, 0)),
                  pl.BlockSpec((1, 6, dm), lambda b, s: (b + b0, 0, 0)),
                  full(g1), full(w_in_r), full(gbias), full(conv_w), full(conv_b), full(wqk),
                  full(ng), full(skip), full(wpool), full(bpool), full(pscale),
                  pl.BlockSpec((1,) + pinv.shape[1:], lambda b, s: (jnp.minimum(s, 1), 0, 0)),
                  full(w_out), full(g2), full(wrt), full(rbias)],
        out_specs=[pl.BlockSpec((ts, dm), lambda b, s: (b * nst + s, 0)),
                   pl.BlockSpec((ts, dm // 2), lambda b, s: (b * nst + s, 0)),
                   pl.BlockSpec((LOGIT_ROWS, ts), lambda b, s: (0, b * nst + s))],
        out_shape=[jax.ShapeDtypeStruct((n_tok, dm), F32),
                   jax.ShapeDtypeStruct((n_tok, dm // 2), I32),
                   jax.ShapeDtypeStruct((LOGIT_ROWS, n_tok), F32)],
        scratch_shapes=[pltpu.VMEM((UHIST + ts, dml), F32),
                        pltpu.VMEM((PHIST + ts, dm - dml), F32),
                        pltpu.VMEM((ts, ncols - dm), F32),
                        pltpu.VMEM((ts, dml), F32),
                        pltpu.VMEM((ts, 2 * dml), F32),
                        pltpu.VMEM((ts, LANES), F32),
                        pltpu.VMEM((ts, dm), BF16),
                        pltpu.VMEM((ts + PHIST, 2 * LANES), F32),
                        pltpu.VMEM((N_HEADS, HEAD_DIM, HEAD_DIM), F32),
                        pltpu.VMEM((N_HEADS, HEAD_DIM, HEAD_DIM), F32),
                        pltpu.VMEM((N_HEADS, CHUNK, LANES), F32)],
        compiler_params=pltpu.CompilerParams(dimension_semantics=("arbitrary", "arbitrary"),
                                             vmem_limit_bytes=VMEM_LIMIT),
        name="mixer",
    )(x, mod, g1, w_in_r, gbias, conv_w, conv_b, wqk, ng, skip, wpool, bpool, pscale, pinv, w_out,
      g2, wrt, rbias)


def _route_kernel(lgt_ref, idx_ref, gate_ref, cnt_ref, carry_s, *, tr, n_sub):
    @pl.when(pl.program_id(0) == 0)
    def _():
        carry_s[...] = jnp.zeros_like(carry_s)

    tr_r = lax.broadcasted_iota(I32, (tr, tr), 0)
    tr_c = lax.broadcasted_iota(I32, (tr, tr), 1)
    upper = jnp.where(tr_r < tr_c, 1.0, 0.0).astype(BF16)
    for q in range(n_sub):
        cols = slice(q * tr, (q + 1) * tr)
        idx, gates = _route_tile(lgt_ref[:, cols], upper, carry_s, tr)
        idx_ref[:, cols] = idx
        gate_ref[:, cols] = gates
    cnt_ref[...] = carry_s[...]


def _route_tile(lg, upper, carry_s, tr):
    best = lg[0:1]
    gidx = jnp.zeros((1, tr), I32)
    for j in range(1, N_GROUPS):
        cand = lg[j:j + 1]
        better = cand > best
        gidx = jnp.where(better, j, gidx)
        best = jnp.where(better, cand, best)
    sumexp = jnp.zeros((1, tr), F32)
    for j in range(N_GROUPS):
        sumexp = sumexp + jnp.exp(lg[j:j + 1] - best)
    g_gate = 1.0 / sumexp

    sel = lg[SUBLANES:2 * SUBLANES]
    for j in range(1, N_GROUPS):
        sel = jnp.where(gidx == j, lg[SUBLANES * (j + 1):SUBLANES * (j + 2)], sel)
    sub = lax.broadcasted_iota(I32, (EXPERTS_PER_GROUP, tr), 0)
    v1 = jnp.max(sel, axis=0, keepdims=True)
    i1 = jnp.min(jnp.where(sel == v1, sub, EXPERTS_PER_GROUP), axis=0, keepdims=True)
    sel2 = jnp.where(sub == i1, -jnp.inf, sel)
    v2 = jnp.max(sel2, axis=0, keepdims=True)
    i2 = jnp.min(jnp.where(sel2 == v2, sub, EXPERTS_PER_GROUP), axis=0, keepdims=True)
    e2 = jnp.exp(v2 - v1)
    den = 1.0 + e2
    gate0 = (1.0 / den) * g_gate
    gate1 = (e2 / den) * g_gate
    ex0 = gidx * EXPERTS_PER_GROUP + i1
    ex1 = gidx * EXPERTS_PER_GROUP + i2

    erow = lax.broadcasted_iota(I32, (N_EXPERTS, tr), 0)
    oh0 = erow == ex0
    oh1 = erow == ex1
    oh = jnp.where(oh0 | oh1, 1.0, 0.0).astype(BF16)
    carry = carry_s[...]
    before = jnp.dot(oh, upper, preferred_element_type=F32)
    before = before + jnp.concatenate([carry] * (tr // LANES), axis=1)
    rank0 = jnp.sum(jnp.where(oh0, before, 0.0), axis=0, keepdims=True)
    rank1 = jnp.sum(jnp.where(oh1, before, 0.0), axis=0, keepdims=True)
    carry_s[...] = carry + jnp.dot(oh, jnp.ones((tr, LANES), BF16), preferred_element_type=F32)

    idx = jnp.concatenate([ex0, ex1, rank0.astype(I32), rank1.astype(I32),
                           jnp.zeros((SUBLANES - 4, tr), I32)], axis=0)
    gates = jnp.concatenate([gate0, gate1, jnp.zeros((SUBLANES - 2, tr), F32)], axis=0)
    return idx, gates


def _route(lgt, n_tok):
    tr = ROUTE_TILE
    n_sub = min(ROUTE_SUBTILES, n_tok // tr)
    step = tr * n_sub
    assert n_tok % step == 0
    return pl.pallas_call(
        functools.partial(_route_kernel, tr=tr, n_sub=n_sub),
        grid=(n_tok // step,),
        in_specs=[pl.BlockSpec((LOGIT_ROWS, step), lambda i: (0, i))],
        out_specs=[pl.BlockSpec((SUBLANES, step), lambda i: (0, i)),
                   pl.BlockSpec((SUBLANES, step), lambda i: (0, i)),
                   pl.BlockSpec((N_EXPERTS, LANES), lambda i: (0, 0))],
        out_shape=[jax.ShapeDtypeStruct((SUBLANES, n_tok), I32),
                   jax.ShapeDtypeStruct((SUBLANES, n_tok), F32),
                   jax.ShapeDtypeStruct((N_EXPERTS, LANES), F32)],
        scratch_shapes=[pltpu.VMEM((N_EXPERTS, LANES), F32)],
        compiler_params=pltpu.CompilerParams(dimension_semantics=("arbitrary",),
                                             vmem_limit_bytes=VMEM_LIMIT),
        name="route",
    )(lgt)


def _sc_workers():
    info = plsc.get_sparse_core_info()
    return info.num_cores, info.num_cores * info.num_subcores


def _dispatch(h2p, dest0, dest1, n_slots):
    n_tok, width = h2p.shape
    n_cores, n_workers = _sc_workers()
    per_w = n_tok // n_workers
    ch = min(SC_SCATTER_CHUNK, per_w)
    n_ch = per_w // ch
    assert n_tok % n_workers == 0 and per_w % ch == 0 and ch % SUBLANES == 0 and n_ch % 2 == 0
    mesh = plsc.VectorSubcoreMesh(core_axis_name="c", subcore_axis_name="s")

    @functools.partial(
        pl.kernel, mesh=mesh,
        out_type=jax.ShapeDtypeStruct((n_slots, width), h2p.dtype),
        scratch_types=[pltpu.VMEM((2, ch), I32), pltpu.VMEM((2, ch), I32),
                       pltpu.VMEM((2, ch, width), h2p.dtype),
                       pltpu.SemaphoreType.DMA((2,)), pltpu.SemaphoreType.DMA((2,))],
        name="dispatch")
    def scatter(h_hbm, d0_hbm, d1_hbm, xs_hbm, i0_v, i1_v, rows_v, sem_in, sem_out):
        wid = lax.axis_index("s") * n_cores + lax.axis_index("c")
        base = wid * per_w

        def loads(t0, slot):
            return (pltpu.make_async_copy(d0_hbm.at[pl.ds(t0, ch)], i0_v.at[slot], sem_in.at[slot]),
                    pltpu.make_async_copy(d1_hbm.at[pl.ds(t0, ch)], i1_v.at[slot], sem_in.at[slot]),
                    pltpu.make_async_copy(h_hbm.at[pl.ds(t0, ch)], rows_v.at[slot], sem_in.at[slot]))

        def scatters(slot):
            return (pltpu.make_async_copy(rows_v.at[slot], xs_hbm.at[i0_v.at[slot]], sem_out.at[slot]),
                    pltpu.make_async_copy(rows_v.at[slot], xs_hbm.at[i1_v.at[slot]], sem_out.at[slot]))

        for cp in loads(base, 0):
            cp.start()

        @pl.loop(0, n_ch, step=2)
        def _(k):
            for slot in range(2):
                for cp in loads(base + (k + slot) * ch, slot):
                    cp.wait()
                out_cps = scatters(slot)
                for cp in out_cps:
                    cp.start()
                nxt = k + slot + 1

                @pl.when(nxt < n_ch)
                def _():
                    for cp in loads(base + nxt * ch, 1 - slot):
                        cp.start()

                for cp in out_cps:
                    cp.wait()

    return scatter(h2p, dest0, dest1)


def _collect(ys, dest0, dest1, tok0, n_tok):
    width = ys.shape[1]
    n_cores, n_workers = _sc_workers()
    per_w = n_tok // n_workers
    ch = min(SC_GATHER_CHUNK, per_w)
    n_ch = per_w // ch
    assert n_tok % n_workers == 0 and per_w % ch == 0 and ch % SUBLANES == 0 and n_ch % 2 == 0
    mesh = plsc.VectorSubcoreMesh(core_axis_name="c", subcore_axis_name="s")

    @functools.partial(
        pl.kernel, mesh=mesh,
        out_type=jax.ShapeDtypeStruct((TOP_K, n_tok, width), ys.dtype),
        scratch_types=[pltpu.VMEM((2, ch), I32), pltpu.VMEM((2, ch), I32),
                       pltpu.VMEM((2, ch, width), ys.dtype), pltpu.VMEM((2, ch, width), ys.dtype),
                       pltpu.SemaphoreType.DMA((2,)), pltpu.SemaphoreType.DMA((2,)),
                       pltpu.SemaphoreType.DMA((2,))],
        name="collect")
    def gather(ys_hbm, d0_hbm, d1_hbm, o_hbm, i0_v, i1_v, r0_v, r1_v, sem_idx, sem_in, sem_out):
        wid = lax.axis_index("s") * n_cores + lax.axis_index("c")
        base = wid * per_w

        def idx_loads(t0, slot):
            return (pltpu.make_async_copy(d0_hbm.at[pl.ds(tok0 + t0, ch)], i0_v.at[slot], sem_idx.at[slot]),
                    pltpu.make_async_copy(d1_hbm.at[pl.ds(tok0 + t0, ch)], i1_v.at[slot], sem_idx.at[slot]))

        def gathers(slot):
            return (pltpu.make_async_copy(ys_hbm.at[i0_v.at[slot]], r0_v.at[slot], sem_in.at[slot]),
                    pltpu.make_async_copy(ys_hbm.at[i1_v.at[slot]], r1_v.at[slot], sem_in.at[slot]))

        def stores(t0, slot):
            return (pltpu.make_async_copy(r0_v.at[slot], o_hbm.at[0, pl.ds(t0, ch)], sem_out.at[slot]),
                    pltpu.make_async_copy(r1_v.at[slot], o_hbm.at[1, pl.ds(t0, ch)], sem_out.at[slot]))

        def start_chunk(t0, slot):
            for cp in idx_loads(t0, slot):
                cp.start()
            for cp in idx_loads(t0, slot):
                cp.wait()
            for cp in gathers(slot):
                cp.start()

        start_chunk(base, 0)

        @pl.loop(0, n_ch, step=2)
        def _(k):
            for slot in range(2):
                nxt = k + slot + 1

                @pl.when(nxt < n_ch)
                def _():
                    start_chunk(base + nxt * ch, 1 - slot)

                for cp in gathers(slot):
                    cp.wait()
                out_cps = stores(base + (k + slot) * ch, slot)
                for cp in out_cps:
                    cp.start()
                for cp in out_cps:
                    cp.wait()

    return gather(ys, dest0, dest1)


def _experts_kernel(first_ref, nblk_ref, cnt_ref, xs_hbm, wg_ref, wu_ref, wd_ref, ys_hbm,
                    wgu_s, wd_s, xbuf, ybuf, sem_in, sem_out, *, de, blk):
    e = pl.program_id(0)
    n_exp = pl.num_programs(0)
    wgu_s[:, 0:de] = wg_ref[0].astype(BF16)
    wgu_s[:, de:2 * de] = wu_ref[0].astype(BF16)
    wd_s[...] = wd_ref[0].astype(BF16)
    first = first_ref[e]
    n_blk = nblk_ref[e]
    count = cnt_ref[e]
    total = first_ref[n_exp - 1] + nblk_ref[n_exp - 1]

    def in_copy(g, slot):
        return pltpu.make_async_copy(xs_hbm.at[pl.ds(pl.multiple_of(g * blk, blk), blk)],
                                     xbuf.at[slot], sem_in.at[slot])

    def out_copy(g, slot):
        return pltpu.make_async_copy(ybuf.at[slot],
                                     ys_hbm.at[pl.ds(pl.multiple_of(g * blk, blk), blk)],
                                     sem_out.at[slot])

    @pl.when(e == 0)
    def _():
        in_copy(0, 0).start(priority=1)
        ybuf[...] = jnp.zeros_like(ybuf)

    def block(j, carry):
        g = first + j
        slot = lax.rem(g, 2)
        in_copy(g, slot).wait()

        @pl.when(g + 1 < total)
        def _():
            in_copy(g + 1, 1 - slot).start(priority=1)

        @pl.when(g >= 2)
        def _():
            out_copy(g - 2, slot).wait()

        n_left = count - j * blk

        def ffn(n_rows):
            words = xbuf[slot, 0:n_rows, :]
            rows = lax.broadcasted_iota(I32, words.shape, 0)
            xb = _unpack_bf16_pairs(jnp.where(rows < n_left, words, 0)).astype(BF16)
            ab = jnp.dot(xb, wgu_s[...], preferred_element_type=F32)
            a = ab[:, 0:de]
            b = ab[:, de:2 * de]
            hmid = (a * _sigmoid(a)) * b
            y = jnp.dot(hmid.astype(BF16), wd_s[...], preferred_element_type=F32)
            ybuf[slot, 0:n_rows, :] = _pack_bf16_pairs(y)

        step = blk // EXPERT_BLOCK_PATHS
        for p in range(EXPERT_BLOCK_PATHS):
            lo_rows, hi_rows = p * step, (p + 1) * step
            last = p == EXPERT_BLOCK_PATHS - 1

            @pl.when((n_left > lo_rows) & ((n_left <= hi_rows) | last))
            def _(hi_rows=hi_rows):
                ffn(hi_rows)

        out_copy(g, slot).start(priority=1)
        return carry

    lax.fori_loop(0, n_blk, block, 0)

    @pl.when(e == n_exp - 1)
    def _():
        @pl.when(total >= 2)
        def _():
            out_copy(total - 2, lax.rem(total, 2)).wait()

        out_copy(total - 1, lax.rem(total + 1, 2)).wait()


def _experts(plan, xs, w_gate, w_up, w_down):
    n_slots, width = xs.shape
    n_exp, dm, de = w_gate.shape
    blk = EXPERT_BLOCK
    return pl.pallas_call(
        functools.partial(_experts_kernel, de=de, blk=blk),
        grid_spec=pltpu.PrefetchScalarGridSpec(
            num_scalar_prefetch=3,
            grid=(n_exp,),
            in_specs=[pl.BlockSpec(memory_space=pl.ANY),
                      pl.BlockSpec((1, dm, de), lambda e, f, n, c: (e, 0, 0)),
                      pl.BlockSpec((1, dm, de), lambda e, f, n, c: (e, 0, 0)),
                      pl.BlockSpec((1, de, dm), lambda e, f, n, c: (e, 0, 0))],
            out_specs=pl.BlockSpec(memory_space=pl.ANY),
            scratch_shapes=[pltpu.VMEM((dm, 2 * de), BF16), pltpu.VMEM((de, dm), BF16),
                            pltpu.VMEM((2, blk, width), I32), pltpu.VMEM((2, blk, width), I32),
                            pltpu.SemaphoreType.DMA((2,)), pltpu.SemaphoreType.DMA((2,))]),
        out_shape=jax.ShapeDtypeStruct((n_slots, width), I32),
        compiler_params=pltpu.CompilerParams(dimension_semantics=("arbitrary",),
                                             vmem_limit_bytes=VMEM_LIMIT),
        name="experts",
    )(*plan, xs, w_gate, w_up, w_down)


def _pick(onehot, table):
    return jnp.sum(jnp.where(onehot, table[None, :], 0), axis=1).astype(I32)


def _expert_plan(counts, n_assign):
    blk = EXPERT_BLOCK
    per_e = (counts + blk - 1) // blk
    first_blk = jnp.cumsum(per_e) - per_e
    n_slots = (n_assign // blk + N_EXPERTS) * blk
    return (first_blk * blk).astype(I32), (first_blk.astype(I32), per_e.astype(I32), counts), n_slots


def _slot_of(starts, expert, rank):
    onehot = jnp.arange(N_EXPERTS, dtype=I32)[None, :] == expert[:, None]
    return _pick(onehot, starts) + rank


def _combine_kernel(rows_ref, gate_ref, x1_ref, mod_ref, fg_ref, *rest):
    o_ref = rest[-1]
    tc = x1_ref.shape[0]
    gate_f = mod_ref[0][5:6]
    for q in range(tc // LANES):
        tok = slice(q * LANES, (q + 1) * LANES)
        g_rows = jnp.concatenate([gate_ref[:, tok], jnp.zeros((LANES - SUBLANES, LANES), F32)], axis=0)
        gc = g_rows.T
        y = (gc[:, 0:1] * _unpack_bf16_pairs(rows_ref[0, tok, :])
             + gc[:, 1:2] * _unpack_bf16_pairs(rows_ref[1, tok, :]))
        x2 = x1_ref[tok, :] + gate_f * y
        r = lax.rsqrt(jnp.mean(x2 * x2, axis=-1, keepdims=True) + EPS)
        o_ref[tok, :] = (x2 * r) * fg_ref[...]


def _combine(rows, gcol, x1, mod, final_g, seq, b0, nb, bsz, out_prev):
    dm = x1.shape[1]
    width = rows.shape[2]
    tc = min(COMBINE_TILE, seq)
    nst = seq // tc
    in_specs = [pl.BlockSpec((TOP_K, tc, width), lambda b, s: (0, b * nst + s, 0)),
                pl.BlockSpec((SUBLANES, tc), lambda b, s: (0, (b + b0) * nst + s)),
                pl.BlockSpec((tc, dm), lambda b, s: ((b + b0) * nst + s, 0)),
                pl.BlockSpec((1, 6, dm), lambda b, s: (b + b0, 0, 0)),
                pl.BlockSpec((1, dm), lambda b, s: (0, 0))]
    args = [rows, gcol, x1, mod, final_g.reshape(1, dm)]
    aliases = {}
    if out_prev is not None:
        in_specs.append(pl.BlockSpec(memory_space=pl.ANY))
        args.append(out_prev)
        aliases = {len(args) - 1: 0}
    return pl.pallas_call(
        _combine_kernel,
        grid=(nb, nst),
        in_specs=in_specs,
        out_specs=pl.BlockSpec((tc, dm), lambda b, s: ((b + b0) * nst + s, 0)),
        out_shape=jax.ShapeDtypeStruct((bsz * seq, dm), F32),
        input_output_aliases=aliases,
        compiler_params=pltpu.CompilerParams(dimension_semantics=("arbitrary", "arbitrary"),
                                             vmem_limit_bytes=VMEM_LIMIT),
        name="combine",
    )(*args)


def _layer(x, c, ada_w, ada_b, norm1_g, w_in, conv_w, conv_b, w_q, w_k, b_igate, b_fgate,
           mlstm_norm_g, mlstm_skip, w_pool, b_pool, pool_scale, w_out, norm2_g,
           w_rg, b_rg, w_re, b_re, w_eg, w_eu, w_ed, out_g):
    bsz, seq, dm = x.shape
    dml = conv_w.shape[1]
    n_tok = bsz * seq
    ts = min(SEQ_TILE, seq)

    mod = _ada(c, ada_w, ada_b).reshape(bsz, 6, dm)

    col_v = dml
    col_o = 2 * dml
    col_i = 3 * dml
    col_p = col_i + 2 * N_HEADS
    w_gate_cols = jnp.pad(w_in[:, col_i:col_p], ((0, 0), (0, LANES - 2 * N_HEADS)))
    w_in_r = jnp.concatenate([w_in[:, :col_v], w_in[:, col_p:], w_in[:, col_v:col_o],
                              w_in[:, col_o:col_i], w_gate_cols], axis=1).astype(BF16)
    gbias = jnp.pad(jnp.concatenate([b_igate, b_fgate]), (0, LANES - 2 * N_HEADS)).reshape(1, LANES)
    wqk = jnp.concatenate([w_q, w_k], axis=-1).astype(BF16)
    wrt = jnp.zeros((LOGIT_ROWS, dm), F32)
    wrt = wrt.at[0:N_GROUPS].set(w_rg.T).at[SUBLANES:SUBLANES + N_EXPERTS].set(w_re.T).astype(BF16)
    rb = jnp.zeros((LOGIT_ROWS,), F32).at[0:N_GROUPS].set(b_rg).at[SUBLANES:SUBLANES + N_EXPERTS].set(b_re)
    rbias = jnp.broadcast_to(rb[:, None], (LOGIT_ROWS, ts))

    gdim = (dm - dml) // len(POOL_WINDOWS)
    win = jnp.repeat(jnp.array(POOL_WINDOWS, F32), gdim)[None, :]
    t1 = jnp.arange(1, ts + 1, dtype=F32)[:, None]
    pinv = jnp.stack([1.0 / jnp.minimum(t1, win), jnp.broadcast_to(1.0 / win, (ts, dm - dml))])

    mixer_params = (norm1_g.reshape(1, dm), w_in_r, gbias, conv_w, conv_b.reshape(1, dml), wqk,
                    mlstm_norm_g.reshape(1, dml), mlstm_skip.reshape(1, dml), w_pool.astype(BF16),
                    b_pool.reshape(1, dm - dml), pool_scale.reshape(1, dm - dml), pinv,
                    w_out.astype(BF16), norm2_g.reshape(1, dm), wrt, rbias)

    x1, h2, lgt = _mixer(x, mod, *mixer_params, 0, bsz)
    idx, gcol, cnt = _route(lgt, n_tok)
    counts = cnt[:, 0].astype(I32)
    starts, plan, n_slots = _expert_plan(counts, n_tok * TOP_K)
    dest0 = _slot_of(starts, idx[0], idx[2])
    dest1 = _slot_of(starts, idx[1], idx[3])
    xs = _dispatch(h2, dest0, dest1, n_slots)
    ys = _experts(plan, xs, w_eg, w_eu, w_ed)

    nb = bsz // COMBINE_GROUPS
    assert bsz % COMBINE_GROUPS == 0
    out = None
    for b0 in range(0, bsz, nb):
        rows = _collect(ys, dest0, dest1, b0 * seq, nb * seq)
        out = _combine(rows, gcol, x1, mod, out_g, seq, b0, nb, bsz, out)
    return out.reshape(bsz, seq, dm)


def kernel(x, c, ada_w, ada_b, norm1_g, w_in, conv_w, conv_b, w_q, w_k, b_igate, b_fgate, mlstm_norm_g, mlstm_skip, w_pool, b_pool, pool_scale, w_out, norm2_g, w_router_group, b_router_group, w_router_expert, b_router_expert, w_expert_gate, w_expert_up, w_expert_down, final_g):
    depth = ada_w.shape[0]
    assert depth == 1, "the final norm is fused into the last layer's combine kernel"
    l = 0
    return _layer(x, c, ada_w[l], ada_b[l], norm1_g[l], w_in[l], conv_w[l], conv_b[l], w_q[l],
                  w_k[l], b_igate[l], b_fgate[l], mlstm_norm_g[l], mlstm_skip[l], w_pool[l],
                  b_pool[l], pool_scale[l], w_out[l], norm2_g[l], w_router_group[l],
                  b_router_group[l], w_router_expert[l], b_router_expert[l],
                  w_expert_gate[l], w_expert_up[l], w_expert_down[l], final_g)
```

```python
import functools

import jax
import jax.numpy as jnp
from jax import lax
from jax.experimental import pallas as pl
from jax.experimental.pallas import tpu as pltpu
from jax.experimental.pallas import tpu_sc as plsc

F32 = jnp.float32
BF16 = jnp.bfloat16
I32 = jnp.int32
U32 = jnp.uint32

EPS = 1e-6
N_HEADS = 4
HEAD_DIM = 128
CONV_WIDTH = 4
POOL_WINDOWS = (2, 4, 8, 16)
N_GROUPS = 4
EXPERTS_PER_GROUP = 8
N_EXPERTS = N_GROUPS * EXPERTS_PER_GROUP
TOP_K = 2

LANES = 128
SUBLANES = 8
CHUNK = 128
SEQ_TILE = 512
ADA_TILE = 1024
COMBINE_GROUPS = 8
ROUTE_TILE = 512
ROUTE_SUBTILES = 4
SC_SCATTER_CHUNK = 64
SC_GATHER_CHUNK = 32
COMBINE_TILE = 512
EXPERT_BLOCK = 1024
EXPERT_BLOCK_PATHS = 4
LOGIT_ROWS = 48
UHIST = 8
PHIST = 16
VMEM_LIMIT = 60 * 1024 * 1024


def _sigmoid(x):
    return 1.0 / (1.0 + jnp.exp(-x))


def _pack_bf16_pairs(x):
    w = x.shape[1] // 2
    half_ulp = jnp.uint32(0x8000)
    hi = lax.bitcast_convert_type(x[:, :w], U32) + half_ulp
    lo = lax.bitcast_convert_type(x[:, w:], U32) + half_ulp
    return lax.bitcast_convert_type((hi & jnp.uint32(0xFFFF0000)) | (lo >> 16), I32)


def _unpack_bf16_pairs(words):
    u = lax.bitcast_convert_type(words, U32)
    hi = lax.bitcast_convert_type(u & jnp.uint32(0xFFFF0000), F32)
    lo = lax.bitcast_convert_type(u << 16, F32)
    return jnp.concatenate([hi, lo], axis=1)


def _ada_kernel(c_ref, w_ref, b_ref, o_ref):
    c = c_ref[...]
    s = c * _sigmoid(c)
    o_ref[...] = jnp.dot(s, w_ref[...], preferred_element_type=F32,
                         precision=lax.Precision.HIGHEST) + b_ref[...]


def _ada(c, ada_w, ada_b):
    bsz, dm = c.shape
    n = ada_w.shape[1]
    tn = ADA_TILE
    return pl.pallas_call(
        _ada_kernel,
        grid=(n // tn,),
        in_specs=[pl.BlockSpec((bsz, dm), lambda j: (0, 0)),
                  pl.BlockSpec((dm, tn), lambda j: (0, j)),
                  pl.BlockSpec((1, tn), lambda j: (0, j))],
        out_specs=pl.BlockSpec((bsz, tn), lambda j: (0, j)),
        out_shape=jax.ShapeDtypeStruct((bsz, n), F32),
        compiler_params=pltpu.CompilerParams(dimension_semantics=("arbitrary",),
                                             vmem_limit_bytes=VMEM_LIMIT),
        name="ada",
    )(c, ada_w, ada_b.reshape(1, n))


def _split3(x):
    hi = x.astype(BF16)
    r1 = x - hi.astype(F32)
    mid = r1.astype(BF16)
    lo = (r1 - mid.astype(F32)).astype(BF16)
    return hi, mid, lo


def _mixer_kernel(x_ref, mod_ref, g1_ref, win_ref, gbias_ref, convw_ref, convb_ref, wqk_ref,
                  ng_ref, skip_ref, wpool_ref, bpool_ref, pscale_ref, pinv_ref, wout_ref, g2_ref,
                  wrt_ref, rbias_ref,
                  x1_ref, h2_ref, lgt_ref,
                  uext_s, pext_s, proj_s, uc_s, qk_s, mix_s, pool4_s, ctv_s, ctn_s, mprev_s,
                  *, ts, dm, dml):
    s_idx = pl.program_id(1)
    n_chunks = ts // CHUNK
    dp = dm - dml

    @pl.when(s_idx == 0)
    def _():
        uext_s[0:UHIST, :] = jnp.zeros((UHIST, dml), F32)
        pext_s[0:PHIST, :] = jnp.zeros((PHIST, dp), F32)
        ctv_s[...] = jnp.zeros_like(ctv_s)
        ctn_s[...] = jnp.zeros_like(ctn_s)
        mprev_s[...] = jnp.zeros_like(mprev_s)

    row_i = lax.broadcasted_iota(I32, (CHUNK, CHUNK), 0)
    col_i = lax.broadcasted_iota(I32, (CHUNK, CHUNK), 1)
    causal = row_i >= col_i
    triu = jnp.where(row_i <= col_i, 1.0, 0.0).astype(BF16)
    ones_blk = jnp.ones((CHUNK, HEAD_DIM), BF16)
    q_scale = HEAD_DIM ** -0.5

    col_o = dml
    col_g = 2 * dml

    x = x_ref[0]
    mod = mod_ref[0]
    r = lax.rsqrt(jnp.mean(x * x, axis=-1, keepdims=True) + EPS)
    h = (x * r) * (g1_ref[...] * (1.0 + mod[1:2])) + mod[0:1]
    res = jnp.dot(h.astype(BF16), win_ref[...], preferred_element_type=F32)
    uext_s[UHIST:, :] = res[:, 0:dml]
    pext_s[PHIST:, :] = res[:, dml:dm]
    proj_s[...] = res[:, dm:]

    acc = None
    for j in reversed(range(CONV_WIDTH)):
        tap = uext_s[pl.ds(UHIST - (CONV_WIDTH - 1 - j), ts), :] * convw_ref[j:j + 1, :]
        acc = tap if acc is None else acc + tap
    conv = acc + convb_ref[...]
    uc = conv * _sigmoid(conv)
    uc_s[...] = uc

    for hd in range(N_HEADS):
        qk = jnp.dot(uc[:, HEAD_DIM * hd:HEAD_DIM * (hd + 1)].astype(BF16), wqk_ref[hd],
                     preferred_element_type=F32)
        c0 = 2 * HEAD_DIM * hd
        qk_s[:, c0:c0 + HEAD_DIM] = qk[:, 0:HEAD_DIM] * q_scale
        qk_s[:, c0 + HEAD_DIM:c0 + 2 * HEAD_DIM] = qk[:, HEAD_DIM:2 * HEAD_DIM]

    pairs = [(c, hd) for c in range(n_chunks) for hd in range(N_HEADS)]
    g_rows, b_rows, b_cols = [], [], []
    pad_rows = jnp.zeros((CHUNK - SUBLANES, CHUNK), F32)
    for c in range(n_chunks):
        g_tile = proj_s[pl.ds(c * CHUNK, CHUNK), col_g:col_g + LANES] + gbias_ref[...]
        g_row = g_tile.T[0:SUBLANES, :]
        logf = -(jnp.maximum(-g_row, 0.0) + jnp.log1p(jnp.exp(-jnp.abs(g_row))))
        hi, mid, lo = _split3(logf)
        cs = jnp.dot(jnp.concatenate([hi, mid, lo], axis=0), triu, preferred_element_type=F32)
        b_row = cs[0:SUBLANES] + cs[SUBLANES:2 * SUBLANES] + cs[2 * SUBLANES:3 * SUBLANES]
        g_rows.append(g_row)
        b_rows.append(b_row)
        b_cols.append(jnp.concatenate([b_row, pad_rows], axis=0).T)

    def rows(ref, c, lo_col, width=HEAD_DIM):
        return ref[pl.ds(c * CHUNK, CHUNK), lo_col:lo_col + width]

    b_bcs, dlogs, rmaxs, p_mats = {}, {}, {}, {}
    for c, hd in pairs:
        b_bc = jnp.broadcast_to(b_cols[c][:, N_HEADS + hd:N_HEADS + hd + 1], (CHUNK, CHUNK))
        i_row = g_rows[c][hd:hd + 1, :]
        b_row = b_rows[c][N_HEADS + hd:N_HEADS + hd + 1, :]
        dlog = jnp.where(causal, (b_bc - b_row) + i_row, -jnp.inf)
        b_bcs[c, hd], dlogs[c, hd] = b_bc, dlog
        rmaxs[c, hd] = jnp.max(dlog, axis=-1, keepdims=True)
        q_c = rows(qk_s, c, 2 * HEAD_DIM * hd)
        k_c = rows(qk_s, c, 2 * HEAD_DIM * hd + HEAD_DIM)
        p_mats[c, hd] = lax.dot_general(q_c.astype(BF16), k_c.astype(BF16),
                                        (((1,), (1,)), ((), ())), preferred_element_type=F32)

    inters, m_ts = {}, {}
    for hd in range(N_HEADS):
        m_prev = mprev_s[hd]
        for c in range(n_chunks):
            inter = b_bcs[c, hd] + m_prev
            m_t = jnp.maximum(inter, rmaxs[c, hd])
            inters[c, hd], m_ts[c, hd] = inter, m_t
            m_prev = jnp.broadcast_to(m_t[CHUNK - 1:CHUNK, :], (CHUNK, LANES))
        mprev_s[hd] = m_prev

    lhs, v_augs, upds, a_prevs, e_negms = {}, {}, {}, {}, {}
    for c, hd in pairs:
        m_t = m_ts[c, hd]
        wm = jnp.exp(dlogs[c, hd] - m_t)
        a_inter = jnp.exp(inters[c, hd] - m_t)
        e_negms[c, hd] = jnp.exp(-m_t)
        q_c = rows(qk_s, c, 2 * HEAD_DIM * hd)
        k_c = rows(qk_s, c, 2 * HEAD_DIM * hd + HEAD_DIM)
        v_c = rows(proj_s, c, HEAD_DIM * hd)
        s_mat = (p_mats[c, hd] * wm).astype(BF16)
        qa = (q_c * a_inter).astype(BF16)
        lhs[c, hd] = jnp.concatenate([s_mat, qa], axis=1)
        v_aug = jnp.concatenate([v_c.astype(BF16), ones_blk], axis=1)
        v_augs[c, hd] = v_aug
        ktw = (k_c.T * wm[CHUNK - 1:CHUNK, :]).astype(BF16)
        upds[c, hd] = jnp.dot(ktw, v_aug, preferred_element_type=F32)
        a_prevs[c, hd] = a_inter[CHUNK - 1:CHUNK, :]

    ct_in = {}
    for hd in range(N_HEADS):
        ctv, ctn = ctv_s[hd], ctn_s[hd]
        for c in range(n_chunks):
            ct_in[c, hd] = jnp.concatenate([ctv, ctn], axis=1).astype(BF16)
            ctv = a_prevs[c, hd] * ctv + upds[c, hd][:, 0:HEAD_DIM]
            ctn = a_prevs[c, hd] * ctn + upds[c, hd][:, HEAD_DIM:2 * HEAD_DIM]
        ctv_s[hd], ctn_s[hd] = ctv, ctn

    for c, hd in pairs:
        numden = jnp.dot(lhs[c, hd], jnp.concatenate([v_augs[c, hd], ct_in[c, hd]], axis=0),
                         preferred_element_type=F32)
        num = numden[:, 0:HEAD_DIM]
        den = numden[:, HEAD_DIM:2 * HEAD_DIM]
        hh = num / jnp.maximum(jnp.abs(den), e_negms[c, hd])
        ms = jnp.mean(hh * hh, axis=-1, keepdims=True)
        hn = hh * lax.rsqrt(ms + EPS) * ng_ref[:, HEAD_DIM * hd:HEAD_DIM * (hd + 1)]
        o_c = rows(proj_s, c, col_o + HEAD_DIM * hd)
        uc_c = rows(uc_s, c, HEAD_DIM * hd)
        out_c = _sigmoid(o_c) * (hn + skip_ref[:, HEAD_DIM * hd:HEAD_DIM * (hd + 1)] * uc_c)
        mix_s[pl.ds(c * CHUNK, CHUNK), HEAD_DIM * hd:HEAD_DIM * (hd + 1)] = out_c.astype(BF16)

    def pe(shift, rows, lanes):
        return pext_s[pl.ds(PHIST - shift, rows), lanes]

    gd = LANES
    sums = []
    for gi in range(2):
        lanes = slice(gd * gi, gd * (gi + 1))
        tot = pe(0, ts, lanes)
        for j in range(1, POOL_WINDOWS[gi]):
            tot = tot + pe(j, ts, lanes)
        sums.append(tot)
    wide = slice(2 * gd, 4 * gd)
    s4 = pe(12, ts + 12, wide)
    for j in range(1, 4):
        s4 = s4 + pe(12 + j, ts + 12, wide)
    pool4_s[0:ts + 12, :] = s4
    s8 = pool4_s[pl.ds(4, ts + 8), :] + pool4_s[pl.ds(0, ts + 8), :]
    sums.append(s8[8:, 0:gd])
    sums.append(s8[8:, gd:2 * gd] + s8[0:ts, gd:2 * gd])
    for gi in range(len(POOL_WINDOWS)):
        lanes = slice(gd * gi, gd * (gi + 1))
        pooled = sums[gi] * pinv_ref[0, :, lanes] - pe(0, ts, lanes)
        yp = jnp.dot(pooled.astype(BF16), wpool_ref[gi], preferred_element_type=F32)
        yp = (yp + bpool_ref[:, lanes]) * pscale_ref[:, lanes]
        mix_s[:, dml + gd * gi:dml + gd * (gi + 1)] = yp.astype(BF16)

    mix = jnp.dot(mix_s[...], wout_ref[...], preferred_element_type=F32)
    x1 = x + mod[2:3] * mix
    x1_ref[...] = x1
    r2 = lax.rsqrt(jnp.mean(x1 * x1, axis=-1, keepdims=True) + EPS)
    h2 = (x1 * r2) * (g2_ref[...] * (1.0 + mod[4:5])) + mod[3:4]
    h2_ref[...] = _pack_bf16_pairs(h2)
    lgt_ref[...] = lax.dot_general(wrt_ref[...], h2.astype(BF16), (((1,), (1,)), ((), ())),
                                   preferred_element_type=F32) + rbias_ref[...]

    uext_s[0:UHIST, :] = uext_s[ts:ts + UHIST, :]
    pext_s[0:PHIST, :] = pext_s[ts:ts + PHIST, :]


def _mixer(x, mod, g1, w_in_r, gbias, conv_w, conv_b, wqk, ng, skip, wpool, bpool, pscale, pinv,
           w_out, g2, wrt, rbias, b0, nb):
    _, seq, dm = x.shape
    dml = conv_w.shape[1]
    ts = min(SEQ_TILE, seq)
    ncols = w_in_r.shape[1]
    nst = seq // ts
    n_tok = nb * seq
    assert seq % ts == 0 and ts % CHUNK == 0
    full = lambda a: pl.BlockSpec(a.shape, lambda b, s: (0,) * a.ndim)
    kern = functools.partial(_mixer_kernel, ts=ts, dm=dm, dml=dml)
    return pl.pallas_call(
        kern,
        grid=(nb, nst),
        in_specs=[pl.BlockSpec((1, ts, dm), lambda b, s: (b + b0, s, 0)),
                  pl.BlockSpec((1, 6, dm), lambda b, s: (b + b0, 0, 0)),
                  full(g1), full(w_in_r), full(gbias), full(conv_w), full(conv_b), full(wqk),
                  full(ng), full(skip), full(wpool), full(bpool), full(pscale),
                  pl.BlockSpec((1,) + pinv.shape[1:], lambda b, s: (jnp.minimum(s, 1), 0, 0)),
                  full(w_out), full(g2), full(wrt), full(rbias)],
        out_specs=[pl.BlockSpec((ts, dm), lambda b, s: (b * nst + s, 0)),
                   pl.BlockSpec((ts, dm // 2), lambda b, s: (b * nst + s, 0)),
                   pl.BlockSpec((LOGIT_ROWS, ts), lambda b, s: (0, b * nst + s))],
        out_shape=[jax.ShapeDtypeStruct((n_tok, dm), F32),
                   jax.ShapeDtypeStruct((n_tok, dm // 2), I32),
                   jax.ShapeDtypeStruct((LOGIT_ROWS, n_tok), F32)],
        scratch_shapes=[pltpu.VMEM((UHIST + ts, dml), F32),
                        pltpu.VMEM((PHIST + ts, dm - dml), F32),
                        pltpu.VMEM((ts, ncols - dm), F32),
                        pltpu.VMEM((ts, dml), F32),
                        pltpu.VMEM((ts, 2 * dml), F32),
                        pltpu.VMEM((ts, dm), BF16),
                        pltpu.VMEM((ts + PHIST, 2 * LANES), F32),
                        pltpu.VMEM((N_HEADS, HEAD_DIM, HEAD_DIM), F32),
                        pltpu.VMEM((N_HEADS, HEAD_DIM, HEAD_DIM), F32),
                        pltpu.VMEM((N_HEADS, CHUNK, LANES), F32)],
        compiler_params=pltpu.CompilerParams(dimension_semantics=("arbitrary", "arbitrary"),
                                             vmem_limit_bytes=VMEM_LIMIT),
        name="mixer",
    )(x, mod, g1, w_in_r, gbias, conv_w, conv_b, wqk, ng, skip, wpool, bpool, pscale, pinv, w_out,
      g2, wrt, rbias)


def _route_kernel(lgt_ref, idx_ref, gate_ref, cnt_ref, carry_s, *, tr, n_sub):
    @pl.when(pl.program_id(0) == 0)
    def _():
        carry_s[...] = jnp.zeros_like(carry_s)

    tr_r = lax.broadcasted_iota(I32, (tr, tr), 0)
    tr_c = lax.broadcasted_iota(I32, (tr, tr), 1)
    upper = jnp.where(tr_r < tr_c, 1.0, 0.0).astype(BF16)
    for q in range(n_sub):
        cols = slice(q * tr, (q + 1) * tr)
        idx, gates = _route_tile(lgt_ref[:, cols], upper, carry_s, tr)
        idx_ref[:, cols] = idx
        gate_ref[:, cols] = gates
    cnt_ref[...] = carry_s[...]


def _route_tile(lg, upper, carry_s, tr):
    best = lg[0:1]
    gidx = jnp.zeros((1, tr), I32)
    for j in range(1, N_GROUPS):
        cand = lg[j:j + 1]
        better = cand > best
        gidx = jnp.where(better, j, gidx)
        best = jnp.where(better, cand, best)
    sumexp = jnp.zeros((1, tr), F32)
    for j in range(N_GROUPS):
        sumexp = sumexp + jnp.exp(lg[j:j + 1] - best)
    g_gate = 1.0 / sumexp

    sel = lg[SUBLANES:2 * SUBLANES]
    for j in range(1, N_GROUPS):
        sel = jnp.where(gidx == j, lg[SUBLANES * (j + 1):SUBLANES * (j + 2)], sel)
    sub = lax.broadcasted_iota(I32, (EXPERTS_PER_GROUP, tr), 0)
    v1 = jnp.max(sel, axis=0, keepdims=True)
    i1 = jnp.min(jnp.where(sel == v1, sub, EXPERTS_PER_GROUP), axis=0, keepdims=True)
    sel2 = jnp.where(sub == i1, -jnp.inf, sel)
    v2 = jnp.max(sel2, axis=0, keepdims=True)
    i2 = jnp.min(jnp.where(sel2 == v2, sub, EXPERTS_PER_GROUP), axis=0, keepdims=True)
    e2 = jnp.exp(v2 - v1)
    den = 1.0 + e2
    gate0 = (1.0 / den) * g_gate
    gate1 = (e2 / den) * g_gate
    ex0 = gidx * EXPERTS_PER_GROUP + i1
    ex1 = gidx * EXPERTS_PER_GROUP + i2

    erow = lax.broadcasted_iota(I32, (N_EXPERTS, tr), 0)
    oh0 = erow == ex0
    oh1 = erow == ex1
    oh = jnp.where(oh0 | oh1, 1.0, 0.0).astype(BF16)
    carry = carry_s[...]
    before = jnp.dot(oh, upper, preferred_element_type=F32)
    before = before + jnp.concatenate([carry] * (tr // LANES), axis=1)
    rank0 = jnp.sum(jnp.where(oh0, before, 0.0), axis=0, keepdims=True)
    rank1 = jnp.sum(jnp.where(oh1, before, 0.0), axis=0, keepdims=True)
    carry_s[...] = carry + jnp.dot(oh, jnp.ones((tr, LANES), BF16), preferred_element_type=F32)

    idx = jnp.concatenate([ex0, ex1, rank0.astype(I32), rank1.astype(I32),
                           jnp.zeros((SUBLANES - 4, tr), I32)], axis=0)
    gates = jnp.concatenate([gate0, gate1, jnp.zeros((SUBLANES - 2, tr), F32)], axis=0)
    return idx, gates


def _route(lgt, n_tok):
    tr = ROUTE_TILE
    n_sub = min(ROUTE_SUBTILES, n_tok // tr)
    step = tr * n_sub
    assert n_tok % step == 0
    return pl.pallas_call(
        functools.partial(_route_kernel, tr=tr, n_sub=n_sub),
        grid=(n_tok // step,),
        in_specs=[pl.BlockSpec((LOGIT_ROWS, step), lambda i: (0, i))],
        out_specs=[pl.BlockSpec((SUBLANES, step), lambda i: (0, i)),
                   pl.BlockSpec((SUBLANES, step), lambda i: (0, i)),
                   pl.BlockSpec((N_EXPERTS, LANES), lambda i: (0, 0))],
        out_shape=[jax.ShapeDtypeStruct((SUBLANES, n_tok), I32),
                   jax.ShapeDtypeStruct((SUBLANES, n_tok), F32),
                   jax.ShapeDtypeStruct((N_EXPERTS, LANES), F32)],
        scratch_shapes=[pltpu.VMEM((N_EXPERTS, LANES), F32)],
        compiler_params=pltpu.CompilerParams(dimension_semantics=("arbitrary",),
                                             vmem_limit_bytes=VMEM_LIMIT),
        name="route",
    )(lgt)


def _sc_workers():
    info = plsc.get_sparse_core_info()
    return info.num_cores, info.num_cores * info.num_subcores


def _dispatch(h2p, dest0, dest1, n_slots):
    n_tok, width = h2p.shape
    n_cores, n_workers = _sc_workers()
    per_w = n_tok // n_workers
    ch = min(SC_SCATTER_CHUNK, per_w)
    n_ch = per_w // ch
    assert n_tok % n_workers == 0 and per_w % ch == 0 and ch % SUBLANES == 0 and n_ch % 2 == 0
    mesh = plsc.VectorSubcoreMesh(core_axis_name="c", subcore_axis_name="s")

    @functools.partial(
        pl.kernel, mesh=mesh,
        out_type=jax.ShapeDtypeStruct((n_slots, width), h2p.dtype),
        scratch_types=[pltpu.VMEM((2, ch), I32), pltpu.VMEM((2, ch), I32),
                       pltpu.VMEM((2, ch, width), h2p.dtype),
                       pltpu.SemaphoreType.DMA((2,)), pltpu.SemaphoreType.DMA((2,))],
        name="dispatch")
    def scatter(h_hbm, d0_hbm, d1_hbm, xs_hbm, i0_v, i1_v, rows_v, sem_in, sem_out):
        wid = lax.axis_index("s") * n_cores + lax.axis_index("c")
        base = wid * per_w

        def loads(t0, slot):
            return (pltpu.make_async_copy(d0_hbm.at[pl.ds(t0, ch)], i0_v.at[slot], sem_in.at[slot]),
                    pltpu.make_async_copy(d1_hbm.at[pl.ds(t0, ch)], i1_v.at[slot], sem_in.at[slot]),
                    pltpu.make_async_copy(h_hbm.at[pl.ds(t0, ch)], rows_v.at[slot], sem_in.at[slot]))

        def scatters(slot):
            return (pltpu.make_async_copy(rows_v.at[slot], xs_hbm.at[i0_v.at[slot]], sem_out.at[slot]),
                    pltpu.make_async_copy(rows_v.at[slot], xs_hbm.at[i1_v.at[slot]], sem_out.at[slot]))

        for cp in loads(base, 0):
            cp.start()

        @pl.loop(0, n_ch, step=2)
        def _(k):
            for slot in range(2):
                for cp in loads(base + (k + slot) * ch, slot):
                    cp.wait()
                out_cps = scatters(slot)
                for cp in out_cps:
                    cp.start()
                nxt = k + slot + 1

                @pl.when(nxt < n_ch)
                def _():
                    for cp in loads(base + nxt * ch, 1 - slot):
                        cp.start()

                for cp in out_cps:
                    cp.wait()

    return scatter(h2p, dest0, dest1)


def _collect(ys, dest0, dest1, tok0, n_tok):
    width = ys.shape[1]
    n_cores, n_workers = _sc_workers()
    per_w = n_tok // n_workers
    ch = min(SC_GATHER_CHUNK, per_w)
    n_ch = per_w // ch
    assert n_tok % n_workers == 0 and per_w % ch == 0 and ch % SUBLANES == 0 and n_ch % 2 == 0
    mesh = plsc.VectorSubcoreMesh(core_axis_name="c", subcore_axis_name="s")

    @functools.partial(
        pl.kernel, mesh=mesh,
        out_type=jax.ShapeDtypeStruct((TOP_K, n_tok, width), ys.dtype),
        scratch_types=[pltpu.VMEM((2, ch), I32), pltpu.VMEM((2, ch), I32),
                       pltpu.VMEM((2, ch, width), ys.dtype), pltpu.VMEM((2, ch, width), ys.dtype),
                       pltpu.SemaphoreType.DMA((2,)), pltpu.SemaphoreType.DMA((2,)),
                       pltpu.SemaphoreType.DMA((2,))],
        name="collect")
    def gather(ys_hbm, d0_hbm, d1_hbm, o_hbm, i0_v, i1_v, r0_v, r1_v, sem_idx, sem_in, sem_out):
        wid = lax.axis_index("s") * n_cores + lax.axis_index("c")
        base = wid * per_w

        def idx_loads(t0, slot):
            return (pltpu.make_async_copy(d0_hbm.at[pl.ds(tok0 + t0, ch)], i0_v.at[slot], sem_idx.at[slot]),
                    pltpu.make_async_copy(d1_hbm.at[pl.ds(tok0 + t0, ch)], i1_v.at[slot], sem_idx.at[slot]))

        def gathers(slot):
            return (pltpu.make_async_copy(ys_hbm.at[i0_v.at[slot]], r0_v.at[slot], sem_in.at[slot]),
                    pltpu.make_async_copy(ys_hbm.at[i1_v.at[slot]], r1_v.at[slot], sem_in.at[slot]))

        def stores(t0, slot):
            return (pltpu.make_async_copy(r0_v.at[slot], o_hbm.at[0, pl.ds(t0, ch)], sem_out.at[slot]),
                    pltpu.make_async_copy(r1_v.at[slot], o_hbm.at[1, pl.ds(t0, ch)], sem_out.at[slot]))

        def start_chunk(t0, slot):
            for cp in idx_loads(t0, slot):
                cp.start()
            for cp in idx_loads(t0, slot):
                cp.wait()
            for cp in gathers(slot):
                cp.start()

        start_chunk(base, 0)

        @pl.loop(0, n_ch, step=2)
        def _(k):
            for slot in range(2):
                nxt = k + slot + 1

                @pl.when(nxt < n_ch)
                def _():
                    start_chunk(base + nxt * ch, 1 - slot)

                for cp in gathers(slot):
                    cp.wait()
                out_cps = stores(base + (k + slot) * ch, slot)
                for cp in out_cps:
                    cp.start()
                for cp in out_cps:
                    cp.wait()

    return gather(ys, dest0, dest1)


def _experts_kernel(first_ref, nblk_ref, cnt_ref, xs_hbm, wg_ref, wu_ref, wd_ref, ys_hbm,
                    wgu_s, wd_s, xbuf, ybuf, sem_in, sem_out, *, de, blk):
    e = pl.program_id(0)
    n_exp = pl.num_programs(0)
    wgu_s[:, 0:de] = wg_ref[0].astype(BF16)
    wgu_s[:, de:2 * de] = wu_ref[0].astype(BF16)
    wd_s[...] = wd_ref[0].astype(BF16)
    first = first_ref[e]
    n_blk = nblk_ref[e]
    count = cnt_ref[e]
    total = first_ref[n_exp - 1] + nblk_ref[n_exp - 1]

    def in_copy(g, slot):
        return pltpu.make_async_copy(xs_hbm.at[pl.ds(pl.multiple_of(g * blk, blk), blk)],
                                     xbuf.at[slot], sem_in.at[slot])

    def out_copy(g, slot):
        return pltpu.make_async_copy(ybuf.at[slot],
                                     ys_hbm.at[pl.ds(pl.multiple_of(g * blk, blk), blk)],
                                     sem_out.at[slot])

    @pl.when(e == 0)
    def _():
        in_copy(0, 0).start(priority=1)
        ybuf[...] = jnp.zeros_like(ybuf)

    def block(j, carry):
        g = first + j
        slot = lax.rem(g, 2)
        in_copy(g, slot).wait()

        @pl.when(g + 1 < total)
        def _():
            in_copy(g + 1, 1 - slot).start(priority=1)

        @pl.when(g >= 2)
        def _():
            out_copy(g - 2, slot).wait()

        n_left = count - j * blk

        def ffn(n_rows):
            words = xbuf[slot, 0:n_rows, :]
            rows = lax.broadcasted_iota(I32, words.shape, 0)
            xb = _unpack_bf16_pairs(jnp.where(rows < n_left, words, 0)).astype(BF16)
            ab = jnp.dot(xb, wgu_s[...], preferred_element_type=F32)
            a = ab[:, 0:de]
            b = ab[:, de:2 * de]
            hmid = (a * _sigmoid(a)) * b
            y = jnp.dot(hmid.astype(BF16), wd_s[...], preferred_element_type=F32)
            ybuf[slot, 0:n_rows, :] = _pack_bf16_pairs(y)

        step = blk // EXPERT_BLOCK_PATHS
        for p in range(EXPERT_BLOCK_PATHS):
            lo_rows, hi_rows = p * step, (p + 1) * step
            last = p == EXPERT_BLOCK_PATHS - 1

            @pl.when((n_left > lo_rows) & ((n_left <= hi_rows) | last))
            def _(hi_rows=hi_rows):
                ffn(hi_rows)

        out_copy(g, slot).start(priority=1)
        return carry

    lax.fori_loop(0, n_blk, block, 0)

    @pl.when(e == n_exp - 1)
    def _():
        @pl.when(total >= 2)
        def _():
            out_copy(total - 2, lax.rem(total, 2)).wait()

        out_copy(total - 1, lax.rem(total + 1, 2)).wait()


def _experts(plan, xs, w_gate, w_up, w_down):
    n_slots, width = xs.shape
    n_exp, dm, de = w_gate.shape
    blk = EXPERT_BLOCK
    return pl.pallas_call(
        functools.partial(_experts_kernel, de=de, blk=blk),
        grid_spec=pltpu.PrefetchScalarGridSpec(
            num_scalar_prefetch=3,
            grid=(n_exp,),
            in_specs=[pl.BlockSpec(memory_space=pl.ANY),
                      pl.BlockSpec((1, dm, de), lambda e, f, n, c: (e, 0, 0)),
                      pl.BlockSpec((1, dm, de), lambda e, f, n, c: (e, 0, 0)),
                      pl.BlockSpec((1, de, dm), lambda e, f, n, c: (e, 0, 0))],
            out_specs=pl.BlockSpec(memory_space=pl.ANY),
            scratch_shapes=[pltpu.VMEM((dm, 2 * de), BF16), pltpu.VMEM((de, dm), BF16),
                            pltpu.VMEM((2, blk, width), I32), pltpu.VMEM((2, blk, width), I32),
                            pltpu.SemaphoreType.DMA((2,)), pltpu.SemaphoreType.DMA((2,))]),
        out_shape=jax.ShapeDtypeStruct((n_slots, width), I32),
        compiler_params=pltpu.CompilerParams(dimension_semantics=("arbitrary",),
                                             vmem_limit_bytes=VMEM_LIMIT),
        name="experts",
    )(*plan, xs, w_gate, w_up, w_down)


def _pick(onehot, table):
    return jnp.sum(jnp.where(onehot, table[None, :], 0), axis=1).astype(I32)


def _expert_plan(counts, n_assign):
    blk = EXPERT_BLOCK
    per_e = (counts + blk - 1) // blk
    first_blk = jnp.cumsum(per_e) - per_e
    n_slots = (n_assign // blk + N_EXPERTS) * blk
    return (first_blk * blk).astype(I32), (first_blk.astype(I32), per_e.astype(I32), counts), n_slots


def _slot_of(starts, expert, rank):
    onehot = jnp.arange(N_EXPERTS, dtype=I32)[None, :] == expert[:, None]
    return _pick(onehot, starts) + rank


def _combine_kernel(rows_ref, gate_ref, x1_ref, mod_ref, fg_ref, *rest):
    o_ref = rest[-1]
    tc = x1_ref.shape[0]
    gate_f = mod_ref[0][5:6]
    for q in range(tc // LANES):
        tok = slice(q * LANES, (q + 1) * LANES)
        g_rows = jnp.concatenate([gate_ref[:, tok], jnp.zeros((LANES - SUBLANES, LANES), F32)], axis=0)
        gc = g_rows.T
        y = (gc[:, 0:1] * _unpack_bf16_pairs(rows_ref[0, tok, :])
             + gc[:, 1:2] * _unpack_bf16_pairs(rows_ref[1, tok, :]))
        x2 = x1_ref[tok, :] + gate_f * y
        r = lax.rsqrt(jnp.mean(x2 * x2, axis=-1, keepdims=True) + EPS)
        o_ref[tok, :] = (x2 * r) * fg_ref[...]


def _combine(rows, gcol, x1, mod, final_g, seq, b0, nb, bsz, out_prev):
    dm = x1.shape[1]
    width = rows.shape[2]
    tc = min(COMBINE_TILE, seq)
    nst = seq // tc
    in_specs = [pl.BlockSpec((TOP_K, tc, width), lambda b, s: (0, b * nst + s, 0)),
                pl.BlockSpec((SUBLANES, tc), lambda b, s: (0, (b + b0) * nst + s)),
                pl.BlockSpec((tc, dm), lambda b, s: ((b + b0) * nst + s, 0)),
                pl.BlockSpec((1, 6, dm), lambda b, s: (b + b0, 0, 0)),
                pl.BlockSpec((1, dm), lambda b, s: (0, 0))]
    args = [rows, gcol, x1, mod, final_g.reshape(1, dm)]
    aliases = {}
    if out_prev is not None:
        in_specs.append(pl.BlockSpec(memory_space=pl.ANY))
        args.append(out_prev)
        aliases = {len(args) - 1: 0}
    return pl.pallas_call(
        _combine_kernel,
        grid=(nb, nst),
        in_specs=in_specs,
        out_specs=pl.BlockSpec((tc, dm), lambda b, s: ((b + b0) * nst + s, 0)),
        out_shape=jax.ShapeDtypeStruct((bsz * seq, dm), F32),
        input_output_aliases=aliases,
        compiler_params=pltpu.CompilerParams(dimension_semantics=("arbitrary", "arbitrary"),
                                             vmem_limit_bytes=VMEM_LIMIT),
        name="combine",
    )(*args)


def _layer(x, c, ada_w, ada_b, norm1_g, w_in, conv_w, conv_b, w_q, w_k, b_igate, b_fgate,
           mlstm_norm_g, mlstm_skip, w_pool, b_pool, pool_scale, w_out, norm2_g,
           w_rg, b_rg, w_re, b_re, w_eg, w_eu, w_ed, out_g):
    bsz, seq, dm = x.shape
    dml = conv_w.shape[1]
    n_tok = bsz * seq
    ts = min(SEQ_TILE, seq)

    mod = _ada(c, ada_w, ada_b).reshape(bsz, 6, dm)

    col_v = dml
    col_o = 2 * dml
    col_i = 3 * dml
    col_p = col_i + 2 * N_HEADS
    w_gate_cols = jnp.pad(w_in[:, col_i:col_p], ((0, 0), (0, LANES - 2 * N_HEADS)))
    w_in_r = jnp.concatenate([w_in[:, :col_v], w_in[:, col_p:], w_in[:, col_v:col_o],
                              w_in[:, col_o:col_i], w_gate_cols], axis=1).astype(BF16)
    gbias = jnp.pad(jnp.concatenate([b_igate, b_fgate]), (0, LANES - 2 * N_HEADS)).reshape(1, LANES)
    wqk = jnp.concatenate([w_q, w_k], axis=-1).astype(BF16)
    wrt = jnp.zeros((LOGIT_ROWS, dm), F32)
    wrt = wrt.at[0:N_GROUPS].set(w_rg.T).at[SUBLANES:SUBLANES + N_EXPERTS].set(w_re.T).astype(BF16)
    rb = jnp.zeros((LOGIT_ROWS,), F32).at[0:N_GROUPS].set(b_rg).at[SUBLANES:SUBLANES + N_EXPERTS].set(b_re)
    rbias = jnp.broadcast_to(rb[:, None], (LOGIT_ROWS, ts))

    gdim = (dm - dml) // len(POOL_WINDOWS)
    win = jnp.repeat(jnp.array(POOL_WINDOWS, F32), gdim)[None, :]
    t1 = jnp.arange(1, ts + 1, dtype=F32)[:, None]
    pinv = jnp.stack([1.0 / jnp.minimum(t1, win), jnp.broadcast_to(1.0 / win, (ts, dm - dml))])

    mixer_params = (norm1_g.reshape(1, dm), w_in_r, gbias, conv_w, conv_b.reshape(1, dml), wqk,
                    mlstm_norm_g.reshape(1, dml), mlstm_skip.reshape(1, dml), w_pool.astype(BF16),
                    b_pool.reshape(1, dm - dml), pool_scale.reshape(1, dm - dml), pinv,
                    w_out.astype(BF16), norm2_g.reshape(1, dm), wrt, rbias)

    x1, h2, lgt = _mixer(x, mod, *mixer_params, 0, bsz)
    idx, gcol, cnt = _route(lgt, n_tok)
    counts = cnt[:, 0].astype(I32)
    starts, plan, n_slots = _expert_plan(counts, n_tok * TOP_K)
    dest0 = _slot_of(starts, idx[0], idx[2])
    dest1 = _slot_of(starts, idx[1], idx[3])
    xs = _dispatch(h2, dest0, dest1, n_slots)
    ys = _experts(plan, xs, w_eg, w_eu, w_ed)

    nb = bsz // COMBINE_GROUPS
    assert bsz % COMBINE_GROUPS == 0
    out = None
    for b0 in range(0, bsz, nb):
        rows = _collect(ys, dest0, dest1, b0 * seq, nb * seq)
        out = _combine(rows, gcol, x1, mod, out_g, seq, b0, nb, bsz, out)
    return out.reshape(bsz, seq, dm)


def kernel(x, c, ada_w, ada_b, norm1_g, w_in, conv_w, conv_b, w_q, w_k, b_igate, b_fgate, mlstm_norm_g, mlstm_skip, w_pool, b_pool, pool_scale, w_out, norm2_g, w_router_group, b_router_group, w_router_expert, b_router_expert, w_expert_gate, w_expert_up, w_expert_down, final_g):
    depth = ada_w.shape[0]
    assert depth == 1, "the final norm is fused into the last layer's combine kernel"
    l = 0
    return _layer(x, c, ada_w[l], ada_b[l], norm1_g[l], w_in[l], conv_w[l], conv_b[l], w_q[l],
                  w_k[l], b_igate[l], b_fgate[l], mlstm_norm_g[l], mlstm_skip[l], w_pool[l],
                  b_pool[l], pool_scale[l], w_out[l], norm2_g[l], w_router_group[l],
                  b_router_group[l], w_router_expert[l], b_router_expert[l],
                  w_expert_gate[l], w_expert_up[l], w_expert_down[l], final_g)
```

```python
import functools

import jax
import jax.numpy as jnp
from jax import lax
from jax.experimental import pallas as pl
from jax.experimental.pallas import tpu as pltpu
from jax.experimental.pallas import tpu_sc as plsc

F32 = jnp.float32
BF16 = jnp.bfloat16
I32 = jnp.int32
U32 = jnp.uint32

EPS = 1e-6
N_HEADS = 4
HEAD_DIM = 128
CONV_WIDTH = 4
POOL_WINDOWS = (2, 4, 8, 16)
N_GROUPS = 4
EXPERTS_PER_GROUP = 8
N_EXPERTS = N_GROUPS * EXPERTS_PER_GROUP
TOP_K = 2

LANES = 128
SUBLANES = 8
CHUNK = 128
SEQ_TILE = 1024
ADA_TILE = 1024
COMBINE_GROUPS = 8
ROUTE_TILE = 512
ROUTE_SUBTILES = 4
SC_SCATTER_CHUNK = 64
SC_GATHER_CHUNK = 32
COMBINE_TILE = 512
EXPERT_BLOCK = 1024
EXPERT_BLOCK_PATHS = 4
LOGIT_ROWS = 48
UHIST = 8
PHIST = 16
VMEM_LIMIT = 60 * 1024 * 1024


def _sigmoid(x):
    return 1.0 / (1.0 + jnp.exp(-x))


def _pack_bf16_pairs(x):
    w = x.shape[1] // 2
    half_ulp = jnp.uint32(0x8000)
    hi = lax.bitcast_convert_type(x[:, :w], U32) + half_ulp
    lo = lax.bitcast_convert_type(x[:, w:], U32) + half_ulp
    return lax.bitcast_convert_type((hi & jnp.uint32(0xFFFF0000)) | (lo >> 16), I32)


def _unpack_bf16_pairs(words):
    u = lax.bitcast_convert_type(words, U32)
    hi = lax.bitcast_convert_type(u & jnp.uint32(0xFFFF0000), F32)
    lo = lax.bitcast_convert_type(u << 16, F32)
    return jnp.concatenate([hi, lo], axis=1)


def _ada_kernel(c_ref, w_ref, b_ref, o_ref):
    c = c_ref[...]
    s = c * _sigmoid(c)
    o_ref[...] = jnp.dot(s, w_ref[...], preferred_element_type=F32,
                         precision=lax.Precision.HIGHEST) + b_ref[...]


def _ada(c, ada_w, ada_b):
    bsz, dm = c.shape
    n = ada_w.shape[1]
    tn = ADA_TILE
    return pl.pallas_call(
        _ada_kernel,
        grid=(n // tn,),
        in_specs=[pl.BlockSpec((bsz, dm), lambda j: (0, 0)),
                  pl.BlockSpec((dm, tn), lambda j: (0, j)),
                  pl.BlockSpec((1, tn), lambda j: (0, j))],
        out_specs=pl.BlockSpec((bsz, tn), lambda j: (0, j)),
        out_shape=jax.ShapeDtypeStruct((bsz, n), F32),
        compiler_params=pltpu.CompilerParams(dimension_semantics=("arbitrary",),
                                             vmem_limit_bytes=VMEM_LIMIT),
        name="ada",
    )(c, ada_w, ada_b.reshape(1, n))


def _split3(x):
    hi = x.astype(BF16)
    r1 = x - hi.astype(F32)
    mid = r1.astype(BF16)
    lo = (r1 - mid.astype(F32)).astype(BF16)
    return hi, mid, lo


def _mixer_kernel(x_ref, mod_ref, g1_ref, win_ref, gbias_ref, convw_ref, convb_ref, wqk_ref,
                  ng_ref, skip_ref, wpool_ref, bpool_ref, pscale_ref, pinv_ref, wout_ref, g2_ref,
                  wrt_ref, rbias_ref,
                  x1_ref, h2_ref, lgt_ref,
                  uext_s, pext_s, proj_s, uc_s, qk_s, mix_s, pool4_s, ctv_s, ctn_s, mprev_s,
                  *, ts, dm, dml):
    s_idx = pl.program_id(1)
    n_chunks = ts // CHUNK
    dp = dm - dml

    @pl.when(s_idx == 0)
    def _():
        uext_s[0:UHIST, :] = jnp.zeros((UHIST, dml), F32)
        pext_s[0:PHIST, :] = jnp.zeros((PHIST, dp), F32)
        ctv_s[...] = jnp.zeros_like(ctv_s)
        ctn_s[...] = jnp.zeros_like(ctn_s)
        mprev_s[...] = jnp.zeros_like(mprev_s)

    row_i = lax.broadcasted_iota(I32, (CHUNK, CHUNK), 0)
    col_i = lax.broadcasted_iota(I32, (CHUNK, CHUNK), 1)
    causal = row_i >= col_i
    triu = jnp.where(row_i <= col_i, 1.0, 0.0).astype(BF16)
    ones_blk = jnp.ones((CHUNK, HEAD_DIM), BF16)
    q_scale = HEAD_DIM ** -0.5

    col_o = dml
    col_g = 2 * dml

    x = x_ref[0]
    mod = mod_ref[0]
    r = lax.rsqrt(jnp.mean(x * x, axis=-1, keepdims=True) + EPS)
    h = (x * r) * (g1_ref[...] * (1.0 + mod[1:2])) + mod[0:1]
    res = jnp.dot(h.astype(BF16), win_ref[...], preferred_element_type=F32)
    uext_s[UHIST:, :] = res[:, 0:dml]
    pext_s[PHIST:, :] = res[:, dml:dm]
    proj_s[...] = res[:, dm:]

    acc = None
    for j in reversed(range(CONV_WIDTH)):
        tap = uext_s[pl.ds(UHIST - (CONV_WIDTH - 1 - j), ts), :] * convw_ref[j:j + 1, :]
        acc = tap if acc is None else acc + tap
    conv = acc + convb_ref[...]
    uc = conv * _sigmoid(conv)
    uc_s[...] = uc

    for hd in range(N_HEADS):
        qk = jnp.dot(uc[:, HEAD_DIM * hd:HEAD_DIM * (hd + 1)].astype(BF16), wqk_ref[hd],
                     preferred_element_type=F32)
        c0 = 2 * HEAD_DIM * hd
        qk_s[:, c0:c0 + HEAD_DIM] = qk[:, 0:HEAD_DIM] * q_scale
        qk_s[:, c0 + HEAD_DIM:c0 + 2 * HEAD_DIM] = qk[:, HEAD_DIM:2 * HEAD_DIM]

    pairs = [(c, hd) for c in range(n_chunks) for hd in range(N_HEADS)]
    g_rows, b_rows, b_cols = [], [], []
    pad_rows = jnp.zeros((CHUNK - SUBLANES, CHUNK), F32)
    for c in range(n_chunks):
        g_tile = proj_s[pl.ds(c * CHUNK, CHUNK), col_g:col_g + LANES] + gbias_ref[...]
        g_row = g_tile.T[0:SUBLANES, :]
        logf = -(jnp.maximum(-g_row, 0.0) + jnp.log1p(jnp.exp(-jnp.abs(g_row))))
        hi, mid, lo = _split3(logf)
        cs = jnp.dot(jnp.concatenate([hi, mid, lo], axis=0), triu, preferred_element_type=F32)
        b_row = cs[0:SUBLANES] + cs[SUBLANES:2 * SUBLANES] + cs[2 * SUBLANES:3 * SUBLANES]
        g_rows.append(g_row)
        b_rows.append(b_row)
        b_cols.append(jnp.concatenate([b_row, pad_rows], axis=0).T)

    def rows(ref, c, lo_col, width=HEAD_DIM):
        return ref[pl.ds(c * CHUNK, CHUNK), lo_col:lo_col + width]

    b_bcs, dlogs, rmaxs, p_mats = {}, {}, {}, {}
    for c, hd in pairs:
        b_bc = jnp.broadcast_to(b_cols[c][:, N_HEADS + hd:N_HEADS + hd + 1], (CHUNK, CHUNK))
        i_row = g_rows[c][hd:hd + 1, :]
        b_row = b_rows[c][N_HEADS + hd:N_HEADS + hd + 1, :]
        dlog = jnp.where(causal, (b_bc - b_row) + i_row, -jnp.inf)
        b_bcs[c, hd], dlogs[c, hd] = b_bc, dlog
        rmaxs[c, hd] = jnp.max(dlog, axis=-1, keepdims=True)
        q_c = rows(qk_s, c, 2 * HEAD_DIM * hd)
        k_c = rows(qk_s, c, 2 * HEAD_DIM * hd + HEAD_DIM)
        p_mats[c, hd] = lax.dot_general(q_c.astype(BF16), k_c.astype(BF16),
                                        (((1,), (1,)), ((), ())), preferred_element_type=F32)

    inters, m_ts = {}, {}
    for hd in range(N_HEADS):
        m_prev = mprev_s[hd]
        for c in range(n_chunks):
            inter = b_bcs[c, hd] + m_prev
            m_t = jnp.maximum(inter, rmaxs[c, hd])
            inters[c, hd], m_ts[c, hd] = inter, m_t
            m_prev = jnp.broadcast_to(m_t[CHUNK - 1:CHUNK, :], (CHUNK, LANES))
        mprev_s[hd] = m_prev

    lhs, v_augs, upds, a_prevs, e_negms = {}, {}, {}, {}, {}
    for c, hd in pairs:
        m_t = m_ts[c, hd]
        wm = jnp.exp(dlogs[c, hd] - m_t)
        a_inter = jnp.exp(inters[c, hd] - m_t)
        e_negms[c, hd] = jnp.exp(-m_t)
        q_c = rows(qk_s, c, 2 * HEAD_DIM * hd)
        k_c = rows(qk_s, c, 2 * HEAD_DIM * hd + HEAD_DIM)
        v_c = rows(proj_s, c, HEAD_DIM * hd)
        s_mat = (p_mats[c, hd] * wm).astype(BF16)
        qa = (q_c * a_inter).astype(BF16)
        lhs[c, hd] = jnp.concatenate([s_mat, qa], axis=1)
        v_aug = jnp.concatenate([v_c.astype(BF16), ones_blk], axis=1)
        v_augs[c, hd] = v_aug
        ktw = (k_c.T * wm[CHUNK - 1:CHUNK, :]).astype(BF16)
        upds[c, hd] = jnp.dot(ktw, v_aug, preferred_element_type=F32)
        a_prevs[c, hd] = a_inter[CHUNK - 1:CHUNK, :]

    ct_in = {}
    for hd in range(N_HEADS):
        ctv, ctn = ctv_s[hd], ctn_s[hd]
        for c in range(n_chunks):
            ct_in[c, hd] = jnp.concatenate([ctv, ctn], axis=1).astype(BF16)
            ctv = a_prevs[c, hd] * ctv + upds[c, hd][:, 0:HEAD_DIM]
            ctn = a_prevs[c, hd] * ctn + upds[c, hd][:, HEAD_DIM:2 * HEAD_DIM]
        ctv_s[hd], ctn_s[hd] = ctv, ctn

    for c, hd in pairs:
        numden = jnp.dot(lhs[c, hd], jnp.concatenate([v_augs[c, hd], ct_in[c, hd]], axis=0),
                         preferred_element_type=F32)
        num = numden[:, 0:HEAD_DIM]
        den = numden[:, HEAD_DIM:2 * HEAD_DIM]
        hh = num / jnp.maximum(jnp.abs(den), e_negms[c, hd])
        ms = jnp.mean(hh * hh, axis=-1, keepdims=True)
        hn = hh * lax.rsqrt(ms + EPS) * ng_ref[:, HEAD_DIM * hd:HEAD_DIM * (hd + 1)]
        o_c = rows(proj_s, c, col_o + HEAD_DIM * hd)
        uc_c = rows(uc_s, c, HEAD_DIM * hd)
        out_c = _sigmoid(o_c) * (hn + skip_ref[:, HEAD_DIM * hd:HEAD_DIM * (hd + 1)] * uc_c)
        mix_s[pl.ds(c * CHUNK, CHUNK), HEAD_DIM * hd:HEAD_DIM * (hd + 1)] = out_c.astype(BF16)

    def pe(shift, rows, lanes):
        return pext_s[pl.ds(PHIST - shift, rows), lanes]

    gd = LANES
    sums = []
    for gi in range(2):
        lanes = slice(gd * gi, gd * (gi + 1))
        tot = pe(0, ts, lanes)
        for j in range(1, POOL_WINDOWS[gi]):
            tot = tot + pe(j, ts, lanes)
        sums.append(tot)
    wide = slice(2 * gd, 4 * gd)
    s4 = pe(12, ts + 12, wide)
    for j in range(1, 4):
        s4 = s4 + pe(12 + j, ts + 12, wide)
    pool4_s[0:ts + 12, :] = s4
    s8 = pool4_s[pl.ds(4, ts + 8), :] + pool4_s[pl.ds(0, ts + 8), :]
    sums.append(s8[8:, 0:gd])
    sums.append(s8[8:, gd:2 * gd] + s8[0:ts, gd:2 * gd])
    for gi in range(len(POOL_WINDOWS)):
        lanes = slice(gd * gi, gd * (gi + 1))
        pooled = sums[gi] * pinv_ref[0, :, lanes] - pe(0, ts, lanes)
        yp = jnp.dot(pooled.astype(BF16), wpool_ref[gi], preferred_element_type=F32)
        yp = (yp + bpool_ref[:, lanes]) * pscale_ref[:, lanes]
        mix_s[:, dml + gd * gi:dml + gd * (gi + 1)] = yp.astype(BF16)

    mix = jnp.dot(mix_s[...], wout_ref[...], preferred_element_type=F32)
    x1 = x + mod[2:3] * mix
    x1_ref[...] = x1
    r2 = lax.rsqrt(jnp.mean(x1 * x1, axis=-1, keepdims=True) + EPS)
    h2 = (x1 * r2) * (g2_ref[...] * (1.0 + mod[4:5])) + mod[3:4]
    h2_ref[...] = _pack_bf16_pairs(h2)
    lgt_ref[...] = lax.dot_general(wrt_ref[...], h2.astype(BF16), (((1,), (1,)), ((), ())),
                                   preferred_element_type=F32) + rbias_ref[...]

    uext_s[0:UHIST, :] = uext_s[ts:ts + UHIST, :]
    pext_s[0:PHIST, :] = pext_s[ts:ts + PHIST, :]


def _mixer(x, mod, g1, w_in_r, gbias, conv_w, conv_b, wqk, ng, skip, wpool, bpool, pscale, pinv,
           w_out, g2, wrt, rbias, b0, nb):
    _, seq, dm = x.shape
    dml = conv_w.shape[1]
    ts = min(SEQ_TILE, seq)
    ncols = w_in_r.shape[1]
    nst = seq // ts
    n_tok = nb * seq
    assert seq % ts == 0 and ts % CHUNK == 0
    full = lambda a: pl.BlockSpec(a.shape, lambda b, s: (0,) * a.ndim, pipeline_mode=pl.Buffered(1))
    kern = functools.partial(_mixer_kernel, ts=ts, dm=dm, dml=dml)
    return pl.pallas_call(
        kern,
        grid=(nb, nst),
        in_specs=[pl.BlockSpec((1, ts, dm), lambda b, s: (b + b0, s, 0)),
                  pl.BlockSpec((1, 6, dm), lambda b, s: (b + b0, 0, 0)),
                  full(g1), full(w_in_r), full(gbias), full(conv_w), full(conv_b), full(wqk),
                  full(ng), full(skip), full(wpool), full(bpool), full(pscale),
                  pl.BlockSpec((1,) + pinv.shape[1:], lambda b, s: (jnp.minimum(s, 1), 0, 0)),
                  full(w_out), full(g2), full(wrt), full(rbias)],
        out_specs=[pl.BlockSpec((ts, dm), lambda b, s: (b * nst + s, 0)),
                   pl.BlockSpec((ts, dm // 2), lambda b, s: (b * nst + s, 0)),
                   pl.BlockSpec((LOGIT_ROWS, ts), lambda b, s: (0, b * nst + s))],
        out_shape=[jax.ShapeDtypeStruct((n_tok, dm), F32),
                   jax.ShapeDtypeStruct((n_tok, dm // 2), I32),
                   jax.ShapeDtypeStruct((LOGIT_ROWS, n_tok), F32)],
        scratch_shapes=[pltpu.VMEM((UHIST + ts, dml), F32),
                        pltpu.VMEM((PHIST + ts, dm - dml), F32),
                        pltpu.VMEM((ts, ncols - dm), F32),
                        pltpu.VMEM((ts, dml), F32),
                        pltpu.VMEM((ts, 2 * dml), F32),
                        pltpu.VMEM((ts, dm), BF16),
                        pltpu.VMEM((ts + PHIST, 2 * LANES), F32),
                        pltpu.VMEM((N_HEADS, HEAD_DIM, HEAD_DIM), F32),
                        pltpu.VMEM((N_HEADS, HEAD_DIM, HEAD_DIM), F32),
                        pltpu.VMEM((N_HEADS, CHUNK, LANES), F32)],
        compiler_params=pltpu.CompilerParams(dimension_semantics=("arbitrary", "arbitrary"),
                                             vmem_limit_bytes=VMEM_LIMIT),
        name="mixer",
    )(x, mod, g1, w_in_r, gbias, conv_w, conv_b, wqk, ng, skip, wpool, bpool, pscale, pinv, w_out,
      g2, wrt, rbias)


def _route_kernel(lgt_ref, idx_ref, gate_ref, cnt_ref, carry_s, *, tr, n_sub):
    @pl.when(pl.program_id(0) == 0)
    def _():
        carry_s[...] = jnp.zeros_like(carry_s)

    tr_r = lax.broadcasted_iota(I32, (tr, tr), 0)
    tr_c = lax.broadcasted_iota(I32, (tr, tr), 1)
    upper = jnp.where(tr_r < tr_c, 1.0, 0.0).astype(BF16)
    for q in range(n_sub):
        cols = slice(q * tr, (q + 1) * tr)
        idx, gates = _route_tile(lgt_ref[:, cols], upper, carry_s, tr)
        idx_ref[:, cols] = idx
        gate_ref[:, cols] = gates
    cnt_ref[...] = carry_s[...]


def _route_tile(lg, upper, carry_s, tr):
    best = lg[0:1]
    gidx = jnp.zeros((1, tr), I32)
    for j in range(1, N_GROUPS):
        cand = lg[j:j + 1]
        better = cand > best
        gidx = jnp.where(better, j, gidx)
        best = jnp.where(better, cand, best)
    sumexp = jnp.zeros((1, tr), F32)
    for j in range(N_GROUPS):
        sumexp = sumexp + jnp.exp(lg[j:j + 1] - best)
    g_gate = 1.0 / sumexp

    sel = lg[SUBLANES:2 * SUBLANES]
    for j in range(1, N_GROUPS):
        sel = jnp.where(gidx == j, lg[SUBLANES * (j + 1):SUBLANES * (j + 2)], sel)
    sub = lax.broadcasted_iota(I32, (EXPERTS_PER_GROUP, tr), 0)
    v1 = jnp.max(sel, axis=0, keepdims=True)
    i1 = jnp.min(jnp.where(sel == v1, sub, EXPERTS_PER_GROUP), axis=0, keepdims=True)
    sel2 = jnp.where(sub == i1, -jnp.inf, sel)
    v2 = jnp.max(sel2, axis=0, keepdims=True)
    i2 = jnp.min(jnp.where(sel2 == v2, sub, EXPERTS_PER_GROUP), axis=0, keepdims=True)
    e2 = jnp.exp(v2 - v1)
    den = 1.0 + e2
    gate0 = (1.0 / den) * g_gate
    gate1 = (e2 / den) * g_gate
    ex0 = gidx * EXPERTS_PER_GROUP + i1
    ex1 = gidx * EXPERTS_PER_GROUP + i2

    erow = lax.broadcasted_iota(I32, (N_EXPERTS, tr), 0)
    oh0 = erow == ex0
    oh1 = erow == ex1
    oh = jnp.where(oh0 | oh1, 1.0, 0.0).astype(BF16)
    carry = carry_s[...]
    before = jnp.dot(oh, upper, preferred_element_type=F32)
    before = before + jnp.concatenate([carry] * (tr // LANES), axis=1)
    rank0 = jnp.sum(jnp.where(oh0, before, 0.0), axis=0, keepdims=True)
    rank1 = jnp.sum(jnp.where(oh1, before, 0.0), axis=0, keepdims=True)
    carry_s[...] = carry + jnp.dot(oh, jnp.ones((tr, LANES), BF16), preferred_element_type=F32)

    idx = jnp.concatenate([ex0, ex1, rank0.astype(I32), rank1.astype(I32),
                           jnp.zeros((SUBLANES - 4, tr), I32)], axis=0)
    gates = jnp.concatenate([gate0, gate1, jnp.zeros((SUBLANES - 2, tr), F32)], axis=0)
    return idx, gates


def _route(lgt, n_tok):
    tr = ROUTE_TILE
    n_sub = min(ROUTE_SUBTILES, n_tok // tr)
    step = tr * n_sub
    assert n_tok % step == 0
    return pl.pallas_call(
        functools.partial(_route_kernel, tr=tr, n_sub=n_sub),
        grid=(n_tok // step,),
        in_specs=[pl.BlockSpec((LOGIT_ROWS, step), lambda i: (0, i))],
        out_specs=[pl.BlockSpec((SUBLANES, step), lambda i: (0, i)),
                   pl.BlockSpec((SUBLANES, step), lambda i: (0, i)),
                   pl.BlockSpec((N_EXPERTS, LANES), lambda i: (0, 0))],
        out_shape=[jax.ShapeDtypeStruct((SUBLANES, n_tok), I32),
                   jax.ShapeDtypeStruct((SUBLANES, n_tok), F32),
                   jax.ShapeDtypeStruct((N_EXPERTS, LANES), F32)],
        scratch_shapes=[pltpu.VMEM((N_EXPERTS, LANES), F32)],
        compiler_params=pltpu.CompilerParams(dimension_semantics=("arbitrary",),
                                             vmem_limit_bytes=VMEM_LIMIT),
        name="route",
    )(lgt)


def _sc_workers():
    info = plsc.get_sparse_core_info()
    return info.num_cores, info.num_cores * info.num_subcores


def _dispatch(h2p, dest0, dest1, n_slots):
    n_tok, width = h2p.shape
    n_cores, n_workers = _sc_workers()
    per_w = n_tok // n_workers
    ch = min(SC_SCATTER_CHUNK, per_w)
    n_ch = per_w // ch
    assert n_tok % n_workers == 0 and per_w % ch == 0 and ch % SUBLANES == 0 and n_ch % 2 == 0
    mesh = plsc.VectorSubcoreMesh(core_axis_name="c", subcore_axis_name="s")

    @functools.partial(
        pl.kernel, mesh=mesh,
        out_type=jax.ShapeDtypeStruct((n_slots, width), h2p.dtype),
        scratch_types=[pltpu.VMEM((2, ch), I32), pltpu.VMEM((2, ch), I32),
                       pltpu.VMEM((2, ch, width), h2p.dtype),
                       pltpu.SemaphoreType.DMA((2,)), pltpu.SemaphoreType.DMA((2,))],
        name="dispatch")
    def scatter(h_hbm, d0_hbm, d1_hbm, xs_hbm, i0_v, i1_v, rows_v, sem_in, sem_out):
        wid = lax.axis_index("s") * n_cores + lax.axis_index("c")
        base = wid * per_w

        def loads(t0, slot):
            return (pltpu.make_async_copy(d0_hbm.at[pl.ds(t0, ch)], i0_v.at[slot], sem_in.at[slot]),
                    pltpu.make_async_copy(d1_hbm.at[pl.ds(t0, ch)], i1_v.at[slot], sem_in.at[slot]),
                    pltpu.make_async_copy(h_hbm.at[pl.ds(t0, ch)], rows_v.at[slot], sem_in.at[slot]))

        def scatters(slot):
            return (pltpu.make_async_copy(rows_v.at[slot], xs_hbm.at[i0_v.at[slot]], sem_out.at[slot]),
                    pltpu.make_async_copy(rows_v.at[slot], xs_hbm.at[i1_v.at[slot]], sem_out.at[slot]))

        for cp in loads(base, 0):
            cp.start()

        @pl.loop(0, n_ch, step=2)
        def _(k):
            for slot in range(2):
                for cp in loads(base + (k + slot) * ch, slot):
                    cp.wait()
                out_cps = scatters(slot)
                for cp in out_cps:
                    cp.start()
                nxt = k + slot + 1

                @pl.when(nxt < n_ch)
                def _():
                    for cp in loads(base + nxt * ch, 1 - slot):
                        cp.start()

                for cp in out_cps:
                    cp.wait()

    return scatter(h2p, dest0, dest1)


def _collect(ys, dest0, dest1, tok0, n_tok):
    width = ys.shape[1]
    n_cores, n_workers = _sc_workers()
    per_w = n_tok // n_workers
    ch = min(SC_GATHER_CHUNK, per_w)
    n_ch = per_w // ch
    assert n_tok % n_workers == 0 and per_w % ch == 0 and ch % SUBLANES == 0 and n_ch % 2 == 0
    mesh = plsc.VectorSubcoreMesh(core_axis_name="c", subcore_axis_name="s")

    @functools.partial(
        pl.kernel, mesh=mesh,
        out_type=jax.ShapeDtypeStruct((TOP_K, n_tok, width), ys.dtype),
        scratch_types=[pltpu.VMEM((2, ch), I32), pltpu.VMEM((2, ch), I32),
                       pltpu.VMEM((2, ch, width), ys.dtype), pltpu.VMEM((2, ch, width), ys.dtype),
                       pltpu.SemaphoreType.DMA((2,)), pltpu.SemaphoreType.DMA((2,)),
                       pltpu.SemaphoreType.DMA((2,))],
        name="collect")
    def gather(ys_hbm, d0_hbm, d1_hbm, o_hbm, i0_v, i1_v, r0_v, r1_v, sem_idx, sem_in, sem_out):
        wid = lax.axis_index("s") * n_cores + lax.axis_index("c")
        base = wid * per_w

        def idx_loads(t0, slot):
            return (pltpu.make_async_copy(d0_hbm.at[pl.ds(tok0 + t0, ch)], i0_v.at[slot], sem_idx.at[slot]),
                    pltpu.make_async_copy(d1_hbm.at[pl.ds(tok0 + t0, ch)], i1_v.at[slot], sem_idx.at[slot]))

        def gathers(slot):
            return (pltpu.make_async_copy(ys_hbm.at[i0_v.at[slot]], r0_v.at[slot], sem_in.at[slot]),
                    pltpu.make_async_copy(ys_hbm.at[i1_v.at[slot]], r1_v.at[slot], sem_in.at[slot]))

        def stores(t0, slot):
            return (pltpu.make_async_copy(r0_v.at[slot], o_hbm.at[0, pl.ds(t0, ch)], sem_out.at[slot]),
                    pltpu.make_async_copy(r1_v.at[slot], o_hbm.at[1, pl.ds(t0, ch)], sem_out.at[slot]))

        def start_chunk(t0, slot):
            for cp in idx_loads(t0, slot):
                cp.start()
            for cp in idx_loads(t0, slot):
                cp.wait()
            for cp in gathers(slot):
                cp.start()

        start_chunk(base, 0)

        @pl.loop(0, n_ch, step=2)
        def _(k):
            for slot in range(2):
                nxt = k + slot + 1

                @pl.when(nxt < n_ch)
                def _():
                    start_chunk(base + nxt * ch, 1 - slot)

                for cp in gathers(slot):
                    cp.wait()
                out_cps = stores(base + (k + slot) * ch, slot)
                for cp in out_cps:
                    cp.start()
                for cp in out_cps:
                    cp.wait()

    return gather(ys, dest0, dest1)


def _experts_kernel(first_ref, nblk_ref, cnt_ref, xs_hbm, wg_ref, wu_ref, wd_ref, ys_hbm,
                    wgu_s, wd_s, xbuf, ybuf, sem_in, sem_out, *, de, blk):
    e = pl.program_id(0)
    n_exp = pl.num_programs(0)
    wgu_s[:, 0:de] = wg_ref[0].astype(BF16)
    wgu_s[:, de:2 * de] = wu_ref[0].astype(BF16)
    wd_s[...] = wd_ref[0].astype(BF16)
    first = first_ref[e]
    n_blk = nblk_ref[e]
    count = cnt_ref[e]
    total = first_ref[n_exp - 1] + nblk_ref[n_exp - 1]

    def in_copy(g, slot):
        return pltpu.make_async_copy(xs_hbm.at[pl.ds(pl.multiple_of(g * blk, blk), blk)],
                                     xbuf.at[slot], sem_in.at[slot])

    def out_copy(g, slot):
        return pltpu.make_async_copy(ybuf.at[slot],
                                     ys_hbm.at[pl.ds(pl.multiple_of(g * blk, blk), blk)],
                                     sem_out.at[slot])

    @pl.when(e == 0)
    def _():
        in_copy(0, 0).start(priority=1)
        ybuf[...] = jnp.zeros_like(ybuf)

    def block(j, carry):
        g = first + j
        slot = lax.rem(g, 2)
        in_copy(g, slot).wait()

        @pl.when(g + 1 < total)
        def _():
            in_copy(g + 1, 1 - slot).start(priority=1)

        @pl.when(g >= 2)
        def _():
            out_copy(g - 2, slot).wait()

        n_left = count - j * blk

        def ffn(n_rows):
            words = xbuf[slot, 0:n_rows, :]
            rows = lax.broadcasted_iota(I32, words.shape, 0)
            xb = _unpack_bf16_pairs(jnp.where(rows < n_left, words, 0)).astype(BF16)
            ab = jnp.dot(xb, wgu_s[...], preferred_element_type=F32)
            a = ab[:, 0:de]
            b = ab[:, de:2 * de]
            hmid = (a * _sigmoid(a)) * b
            y = jnp.dot(hmid.astype(BF16), wd_s[...], preferred_element_type=F32)
            ybuf[slot, 0:n_rows, :] = _pack_bf16_pairs(y)

        step = blk // EXPERT_BLOCK_PATHS
        for p in range(EXPERT_BLOCK_PATHS):
            lo_rows, hi_rows = p * step, (p + 1) * step
            last = p == EXPERT_BLOCK_PATHS - 1

            @pl.when((n_left > lo_rows) & ((n_left <= hi_rows) | last))
            def _(hi_rows=hi_rows):
                ffn(hi_rows)

        out_copy(g, slot).start(priority=1)
        return carry

    lax.fori_loop(0, n_blk, block, 0)

    @pl.when(e == n_exp - 1)
    def _():
        @pl.when(total >= 2)
        def _():
            out_copy(total - 2, lax.rem(total, 2)).wait()

        out_copy(total - 1, lax.rem(total + 1, 2)).wait()


def _experts(plan, xs, w_gate, w_up, w_down):
    n_slots, width = xs.shape
    n_exp, dm, de = w_gate.shape
    blk = EXPERT_BLOCK
    return pl.pallas_call(
        functools.partial(_experts_kernel, de=de, blk=blk),
        grid_spec=pltpu.PrefetchScalarGridSpec(
            num_scalar_prefetch=3,
            grid=(n_exp,),
            in_specs=[pl.BlockSpec(memory_space=pl.ANY),
                      pl.BlockSpec((1, dm, de), lambda e, f, n, c: (e, 0, 0)),
                      pl.BlockSpec((1, dm, de), lambda e, f, n, c: (e, 0, 0)),
                      pl.BlockSpec((1, de, dm), lambda e, f, n, c: (e, 0, 0))],
            out_specs=pl.BlockSpec(memory_space=pl.ANY),
            scratch_shapes=[pltpu.VMEM((dm, 2 * de), BF16), pltpu.VMEM((de, dm), BF16),
                            pltpu.VMEM((2, blk, width), I32), pltpu.VMEM((2, blk, width), I32),
                            pltpu.SemaphoreType.DMA((2,)), pltpu.SemaphoreType.DMA((2,))]),
        out_shape=jax.ShapeDtypeStruct((n_slots, width), I32),
        compiler_params=pltpu.CompilerParams(dimension_semantics=("arbitrary",),
                                             vmem_limit_bytes=VMEM_LIMIT),
        name="experts",
    )(*plan, xs, w_gate, w_up, w_down)


def _pick(onehot, table):
    return jnp.sum(jnp.where(onehot, table[None, :], 0), axis=1).astype(I32)


def _expert_plan(counts, n_assign):
    blk = EXPERT_BLOCK
    per_e = (counts + blk - 1) // blk
    first_blk = jnp.cumsum(per_e) - per_e
    n_slots = (n_assign // blk + N_EXPERTS) * blk
    return (first_blk * blk).astype(I32), (first_blk.astype(I32), per_e.astype(I32), counts), n_slots


def _slot_of(starts, expert, rank):
    onehot = jnp.arange(N_EXPERTS, dtype=I32)[None, :] == expert[:, None]
    return _pick(onehot, starts) + rank


def _combine_kernel(rows_ref, gate_ref, x1_ref, mod_ref, fg_ref, *rest):
    o_ref = rest[-1]
    tc = x1_ref.shape[0]
    gate_f = mod_ref[0][5:6]
    for q in range(tc // LANES):
        tok = slice(q * LANES, (q + 1) * LANES)
        g_rows = jnp.concatenate([gate_ref[:, tok], jnp.zeros((LANES - SUBLANES, LANES), F32)], axis=0)
        gc = g_rows.T
        y = (gc[:, 0:1] * _unpack_bf16_pairs(rows_ref[0, tok, :])
             + gc[:, 1:2] * _unpack_bf16_pairs(rows_ref[1, tok, :]))
        x2 = x1_ref[tok, :] + gate_f * y
        r = lax.rsqrt(jnp.mean(x2 * x2, axis=-1, keepdims=True) + EPS)
        o_ref[tok, :] = (x2 * r) * fg_ref[...]


def _combine(rows, gcol, x1, mod, final_g, seq, b0, nb, bsz, out_prev):
    dm = x1.shape[1]
    width = rows.shape[2]
    tc = min(COMBINE_TILE, seq)
    nst = seq // tc
    in_specs = [pl.BlockSpec((TOP_K, tc, width), lambda b, s: (0, b * nst + s, 0)),
                pl.BlockSpec((SUBLANES, tc), lambda b, s: (0, (b + b0) * nst + s)),
                pl.BlockSpec((tc, dm), lambda b, s: ((b + b0) * nst + s, 0)),
                pl.BlockSpec((1, 6, dm), lambda b, s: (b + b0, 0, 0)),
                pl.BlockSpec((1, dm), lambda b, s: (0, 0))]
    args = [rows, gcol, x1, mod, final_g.reshape(1, dm)]
    aliases = {}
    if out_prev is not None:
        in_specs.append(pl.BlockSpec(memory_space=pl.ANY))
        args.append(out_prev)
        aliases = {len(args) - 1: 0}
    return pl.pallas_call(
        _combine_kernel,
        grid=(nb, nst),
        in_specs=in_specs,
        out_specs=pl.BlockSpec((tc, dm), lambda b, s: ((b + b0) * nst + s, 0)),
        out_shape=jax.ShapeDtypeStruct((bsz * seq, dm), F32),
        input_output_aliases=aliases,
        compiler_params=pltpu.CompilerParams(dimension_semantics=("arbitrary", "arbitrary"),
                                             vmem_limit_bytes=VMEM_LIMIT),
        name="combine",
    )(*args)


def _layer(x, c, ada_w, ada_b, norm1_g, w_in, conv_w, conv_b, w_q, w_k, b_igate, b_fgate,
           mlstm_norm_g, mlstm_skip, w_pool, b_pool, pool_scale, w_out, norm2_g,
           w_rg, b_rg, w_re, b_re, w_eg, w_eu, w_ed, out_g):
    bsz, seq, dm = x.shape
    dml = conv_w.shape[1]
    n_tok = bsz * seq
    ts = min(SEQ_TILE, seq)

    mod = _ada(c, ada_w, ada_b).reshape(bsz, 6, dm)

    col_v = dml
    col_o = 2 * dml
    col_i = 3 * dml
    col_p = col_i + 2 * N_HEADS
    w_gate_cols = jnp.pad(w_in[:, col_i:col_p], ((0, 0), (0, LANES - 2 * N_HEADS)))
    w_in_r = jnp.concatenate([w_in[:, :col_v], w_in[:, col_p:], w_in[:, col_v:col_o],
                              w_in[:, col_o:col_i], w_gate_cols], axis=1).astype(BF16)
    gbias = jnp.pad(jnp.concatenate([b_igate, b_fgate]), (0, LANES - 2 * N_HEADS)).reshape(1, LANES)
    wqk = jnp.concatenate([w_q, w_k], axis=-1).astype(BF16)
    wrt = jnp.zeros((LOGIT_ROWS, dm), F32)
    wrt = wrt.at[0:N_GROUPS].set(w_rg.T).at[SUBLANES:SUBLANES + N_EXPERTS].set(w_re.T).astype(BF16)
    rb = jnp.zeros((LOGIT_ROWS,), F32).at[0:N_GROUPS].set(b_rg).at[SUBLANES:SUBLANES + N_EXPERTS].set(b_re)
    rbias = jnp.broadcast_to(rb[:, None], (LOGIT_ROWS, ts))

    gdim = (dm - dml) // len(POOL_WINDOWS)
    win = jnp.repeat(jnp.array(POOL_WINDOWS, F32), gdim)[None, :]
    t1 = jnp.arange(1, ts + 1, dtype=F32)[:, None]
    pinv = jnp.stack([1.0 / jnp.minimum(t1, win), jnp.broadcast_to(1.0 / win, (ts, dm - dml))])

    mixer_params = (norm1_g.reshape(1, dm), w_in_r, gbias, conv_w, conv_b.reshape(1, dml), wqk,
                    mlstm_norm_g.reshape(1, dml), mlstm_skip.reshape(1, dml), w_pool.astype(BF16),
                    b_pool.reshape(1, dm - dml), pool_scale.reshape(1, dm - dml), pinv,
                    w_out.astype(BF16), norm2_g.reshape(1, dm), wrt, rbias)

    x1, h2, lgt = _mixer(x, mod, *mixer_params, 0, bsz)
    idx, gcol, cnt = _route(lgt, n_tok)
    counts = cnt[:, 0].astype(I32)
    starts, plan, n_slots = _expert_plan(counts, n_tok * TOP_K)
    dest0 = _slot_of(starts, idx[0], idx[2])
    dest1 = _slot_of(starts, idx[1], idx[3])
    xs = _dispatch(h2, dest0, dest1, n_slots)
    ys = _experts(plan, xs, w_eg, w_eu, w_ed)

    nb = bsz // COMBINE_GROUPS
    assert bsz % COMBINE_GROUPS == 0
    out = None
    for b0 in range(0, bsz, nb):
        rows = _collect(ys, dest0, dest1, b0 * seq, nb * seq)
        out = _combine(rows, gcol, x1, mod, out_g, seq, b0, nb, bsz, out)
    return out.reshape(bsz, seq, dm)


def kernel(x, c, ada_w, ada_b, norm1_g, w_in, conv_w, conv_b, w_q, w_k, b_igate, b_fgate, mlstm_norm_g, mlstm_skip, w_pool, b_pool, pool_scale, w_out, norm2_g, w_router_group, b_router_group, w_router_expert, b_router_expert, w_expert_gate, w_expert_up, w_expert_down, final_g):
    depth = ada_w.shape[0]
    assert depth == 1, "the final norm is fused into the last layer's combine kernel"
    l = 0
    return _layer(x, c, ada_w[l], ada_b[l], norm1_g[l], w_in[l], conv_w[l], conv_b[l], w_q[l],
                  w_k[l], b_igate[l], b_fgate[l], mlstm_norm_g[l], mlstm_skip[l], w_pool[l],
                  b_pool[l], pool_scale[l], w_out[l], norm2_g[l], w_router_group[l],
                  b_router_group[l], w_router_expert[l], b_router_expert[l],
                  w_expert_gate[l], w_expert_up[l], w_expert_down[l], final_g)
```

```python
import functools

import jax
import jax.numpy as jnp
from jax import lax
from jax.experimental import pallas as pl
from jax.experimental.pallas import tpu as pltpu
from jax.experimental.pallas import tpu_sc as plsc

F32 = jnp.float32
BF16 = jnp.bfloat16
I32 = jnp.int32
U32 = jnp.uint32

EPS = 1e-6
N_HEADS = 4
HEAD_DIM = 128
CONV_WIDTH = 4
POOL_WINDOWS = (2, 4, 8, 16)
N_GROUPS = 4
EXPERTS_PER_GROUP = 8
N_EXPERTS = N_GROUPS * EXPERTS_PER_GROUP
TOP_K = 2

LANES = 128
SUBLANES = 8
CHUNK = 128
SEQ_TILE = 1024
ADA_TILE = 1024
BATCH_GROUPS = 2
COMBINE_GROUPS = 4
ROUTE_TILE = 512
ROUTE_SUBTILES = 4
SC_SCATTER_CHUNK = 64
SC_GATHER_CHUNK = 32
COMBINE_TILE = 512
EXPERT_BLOCK = 1024
EXPERT_BLOCK_PATHS = 4
LOGIT_ROWS = 48
UHIST = 8
PHIST = 16
VMEM_LIMIT = 60 * 1024 * 1024


def _sigmoid(x):
    return 1.0 / (1.0 + jnp.exp(-x))


def _pack_bf16_pairs(x):
    w = x.shape[1] // 2
    half_ulp = jnp.uint32(0x8000)
    hi = lax.bitcast_convert_type(x[:, :w], U32) + half_ulp
    lo = lax.bitcast_convert_type(x[:, w:], U32) + half_ulp
    return lax.bitcast_convert_type((hi & jnp.uint32(0xFFFF0000)) | (lo >> 16), I32)


def _unpack_bf16_pairs(words):
    u = lax.bitcast_convert_type(words, U32)
    hi = lax.bitcast_convert_type(u & jnp.uint32(0xFFFF0000), F32)
    lo = lax.bitcast_convert_type(u << 16, F32)
    return jnp.concatenate([hi, lo], axis=1)


def _ada_kernel(c_ref, w_ref, b_ref, o_ref):
    c = c_ref[...]
    s = c * _sigmoid(c)
    o_ref[...] = jnp.dot(s, w_ref[...], preferred_element_type=F32,
                         precision=lax.Precision.HIGHEST) + b_ref[...]


def _ada(c, ada_w, ada_b):
    bsz, dm = c.shape
    n = ada_w.shape[1]
    tn = ADA_TILE
    return pl.pallas_call(
        _ada_kernel,
        grid=(n // tn,),
        in_specs=[pl.BlockSpec((bsz, dm), lambda j: (0, 0)),
                  pl.BlockSpec((dm, tn), lambda j: (0, j)),
                  pl.BlockSpec((1, tn), lambda j: (0, j))],
        out_specs=pl.BlockSpec((bsz, tn), lambda j: (0, j)),
        out_shape=jax.ShapeDtypeStruct((bsz, n), F32),
        compiler_params=pltpu.CompilerParams(dimension_semantics=("arbitrary",),
                                             vmem_limit_bytes=VMEM_LIMIT),
        name="ada",
    )(c, ada_w, ada_b.reshape(1, n))


def _split3(x):
    hi = x.astype(BF16)
    r1 = x - hi.astype(F32)
    mid = r1.astype(BF16)
    lo = (r1 - mid.astype(F32)).astype(BF16)
    return hi, mid, lo


def _mixer_kernel(x_ref, mod_ref, win_ref, gbias_ref, convw_ref, convb_ref, wqk_ref,
                  ng_ref, skip_ref, wpool_ref, bpool_ref, pscale_ref, pinv_ref, wout_ref,
                  wrt_ref, rbias_ref,
                  x1_ref, h2_ref, lgt_ref,
                  uext_s, pext_s, proj_s, uc_s, qk_s, mix_s, pool4_s, ctv_s, ctn_s, mprev_s,
                  *, ts, dm, dml):
    s_idx = pl.program_id(1)
    n_chunks = ts // CHUNK
    dp = dm - dml

    @pl.when(s_idx == 0)
    def _():
        uext_s[0:UHIST, :] = jnp.zeros((UHIST, dml), F32)
        pext_s[0:PHIST, :] = jnp.zeros((PHIST, dp), F32)
        ctv_s[...] = jnp.zeros_like(ctv_s)
        ctn_s[...] = jnp.zeros_like(ctn_s)
        mprev_s[...] = jnp.zeros_like(mprev_s)

    row_i = lax.broadcasted_iota(I32, (CHUNK, CHUNK), 0)
    col_i = lax.broadcasted_iota(I32, (CHUNK, CHUNK), 1)
    causal = row_i >= col_i
    triu = jnp.where(row_i <= col_i, 1.0, 0.0).astype(BF16)
    ones_blk = jnp.ones((CHUNK, HEAD_DIM), BF16)
    q_scale = HEAD_DIM ** -0.5

    col_o = dml
    col_g = 2 * dml

    x = x_ref[0]
    mod = mod_ref[0]
    r = lax.rsqrt(jnp.mean(x * x, axis=-1, keepdims=True) + EPS)
    h = (x * r) * mod[1:2] + mod[0:1]
    res = jnp.dot(h.astype(BF16), win_ref[...], preferred_element_type=F32)
    uext_s[UHIST:, :] = res[:, 0:dml]
    pext_s[PHIST:, :] = res[:, dml:dm]
    proj_s[...] = res[:, dm:]

    acc = None
    for j in reversed(range(CONV_WIDTH)):
        tap = uext_s[pl.ds(UHIST - (CONV_WIDTH - 1 - j), ts), :] * convw_ref[j:j + 1, :]
        acc = tap if acc is None else acc + tap
    conv = acc + convb_ref[...]
    uc = conv * _sigmoid(conv)
    uc_s[...] = uc

    for hd in range(N_HEADS):
        qk = jnp.dot(uc[:, HEAD_DIM * hd:HEAD_DIM * (hd + 1)].astype(BF16), wqk_ref[hd],
                     preferred_element_type=F32)
        c0 = 2 * HEAD_DIM * hd
        qk_s[:, c0:c0 + HEAD_DIM] = qk[:, 0:HEAD_DIM] * q_scale
        qk_s[:, c0 + HEAD_DIM:c0 + 2 * HEAD_DIM] = qk[:, HEAD_DIM:2 * HEAD_DIM]

    pairs = [(c, hd) for c in range(n_chunks) for hd in range(N_HEADS)]
    g_rows, b_rows, b_cols = [], [], []
    pad_rows = jnp.zeros((CHUNK - SUBLANES, CHUNK), F32)
    for c in range(n_chunks):
        g_tile = proj_s[pl.ds(c * CHUNK, CHUNK), col_g:col_g + LANES] + gbias_ref[...]
        g_row = g_tile.T[0:SUBLANES, :]
        logf = -(jnp.maximum(-g_row, 0.0) + jnp.log1p(jnp.exp(-jnp.abs(g_row))))
        hi, mid, lo = _split3(logf)
        cs = jnp.dot(jnp.concatenate([hi, mid, lo], axis=0), triu, preferred_element_type=F32)
        b_row = cs[0:SUBLANES] + cs[SUBLANES:2 * SUBLANES] + cs[2 * SUBLANES:3 * SUBLANES]
        g_rows.append(g_row)
        b_rows.append(b_row)
        b_cols.append(jnp.concatenate([b_row, pad_rows], axis=0).T)

    def rows(ref, c, lo_col, width=HEAD_DIM):
        return ref[pl.ds(c * CHUNK, CHUNK), lo_col:lo_col + width]

    b_bcs, dlogs, rmaxs, p_mats = {}, {}, {}, {}
    for c, hd in pairs:
        b_bc = jnp.broadcast_to(b_cols[c][:, N_HEADS + hd:N_HEADS + hd + 1], (CHUNK, CHUNK))
        i_row = g_rows[c][hd:hd + 1, :]
        b_row = b_rows[c][N_HEADS + hd:N_HEADS + hd + 1, :]
        dlog = jnp.where(causal, (b_bc - b_row) + i_row, -jnp.inf)
        b_bcs[c, hd], dlogs[c, hd] = b_bc, dlog
        rmaxs[c, hd] = jnp.max(dlog, axis=-1, keepdims=True)
        q_c = rows(qk_s, c, 2 * HEAD_DIM * hd)
        k_c = rows(qk_s, c, 2 * HEAD_DIM * hd + HEAD_DIM)
        p_mats[c, hd] = lax.dot_general(q_c.astype(BF16), k_c.astype(BF16),
                                        (((1,), (1,)), ((), ())), preferred_element_type=F32)

    inters, m_ts = {}, {}
    for hd in range(N_HEADS):
        m_prev = mprev_s[hd]
        for c in range(n_chunks):
            inter = b_bcs[c, hd] + m_prev
            m_t = jnp.maximum(inter, rmaxs[c, hd])
            inters[c, hd], m_ts[c, hd] = inter, m_t
            m_prev = jnp.broadcast_to(m_t[CHUNK - 1:CHUNK, :], (CHUNK, LANES))
        mprev_s[hd] = m_prev

    lhs, v_augs, upds, a_prevs, e_negms = {}, {}, {}, {}, {}
    for c, hd in pairs:
        m_t = m_ts[c, hd]
        wm = jnp.exp(dlogs[c, hd] - m_t)
        a_inter = jnp.exp(inters[c, hd] - m_t)
        e_negms[c, hd] = jnp.exp(-m_t)
        q_c = rows(qk_s, c, 2 * HEAD_DIM * hd)
        k_c = rows(qk_s, c, 2 * HEAD_DIM * hd + HEAD_DIM)
        v_c = rows(proj_s, c, HEAD_DIM * hd)
        s_mat = (p_mats[c, hd] * wm).astype(BF16)
        qa = (q_c * a_inter).astype(BF16)
        lhs[c, hd] = jnp.concatenate([s_mat, qa], axis=1)
        v_aug = jnp.concatenate([v_c.astype(BF16), ones_blk], axis=1)
        v_augs[c, hd] = v_aug
        ktw = (k_c.T * wm[CHUNK - 1:CHUNK, :]).astype(BF16)
        upds[c, hd] = jnp.dot(ktw, v_aug, preferred_element_type=F32)
        a_prevs[c, hd] = a_inter[CHUNK - 1:CHUNK, :]

    ct_in = {}
    for hd in range(N_HEADS):
        ctv, ctn = ctv_s[hd], ctn_s[hd]
        for c in range(n_chunks):
            ct_in[c, hd] = jnp.concatenate([ctv, ctn], axis=1).astype(BF16)
            ctv = a_prevs[c, hd] * ctv + upds[c, hd][:, 0:HEAD_DIM]
            ctn = a_prevs[c, hd] * ctn + upds[c, hd][:, HEAD_DIM:2 * HEAD_DIM]
        ctv_s[hd], ctn_s[hd] = ctv, ctn

    for c, hd in pairs:
        numden = jnp.dot(lhs[c, hd], jnp.concatenate([v_augs[c, hd], ct_in[c, hd]], axis=0),
                         preferred_element_type=F32)
        num = numden[:, 0:HEAD_DIM]
        den = numden[:, HEAD_DIM:2 * HEAD_DIM]
        hh = num / jnp.maximum(jnp.abs(den), e_negms[c, hd])
        ms = jnp.mean(hh * hh, axis=-1, keepdims=True)
        hn = hh * lax.rsqrt(ms + EPS) * ng_ref[:, HEAD_DIM * hd:HEAD_DIM * (hd + 1)]
        o_c = rows(proj_s, c, col_o + HEAD_DIM * hd)
        uc_c = rows(uc_s, c, HEAD_DIM * hd)
        out_c = _sigmoid(o_c) * (hn + skip_ref[:, HEAD_DIM * hd:HEAD_DIM * (hd + 1)] * uc_c)
        mix_s[pl.ds(c * CHUNK, CHUNK), HEAD_DIM * hd:HEAD_DIM * (hd + 1)] = out_c.astype(BF16)

    def pe(shift, rows, lanes):
        return pext_s[pl.ds(PHIST - shift, rows), lanes]

    gd = LANES
    sums = []
    for gi in range(2):
        lanes = slice(gd * gi, gd * (gi + 1))
        tot = pe(0, ts, lanes)
        for j in range(1, POOL_WINDOWS[gi]):
            tot = tot + pe(j, ts, lanes)
        sums.append(tot)
    wide = slice(2 * gd, 4 * gd)
    s4 = pe(12, ts + 12, wide)
    for j in range(1, 4):
        s4 = s4 + pe(12 + j, ts + 12, wide)
    pool4_s[0:ts + 12, :] = s4
    s8 = pool4_s[pl.ds(4, ts + 8), :] + pool4_s[pl.ds(0, ts + 8), :]
    sums.append(s8[8:, 0:gd])
    sums.append(s8[8:, gd:2 * gd] + s8[0:ts, gd:2 * gd])
    for gi in range(len(POOL_WINDOWS)):
        lanes = slice(gd * gi, gd * (gi + 1))
        pooled = sums[gi] * pinv_ref[0, :, lanes] - pe(0, ts, lanes)
        yp = jnp.dot(pooled.astype(BF16), wpool_ref[gi], preferred_element_type=F32)
        yp = (yp + bpool_ref[:, lanes]) * pscale_ref[:, lanes]
        mix_s[:, dml + gd * gi:dml + gd * (gi + 1)] = yp.astype(BF16)

    mix = jnp.dot(mix_s[...], wout_ref[...], preferred_element_type=F32)
    x1 = x + mod[2:3] * mix
    x1_ref[...] = x1
    r2 = lax.rsqrt(jnp.mean(x1 * x1, axis=-1, keepdims=True) + EPS)
    h2 = (x1 * r2) * mod[4:5] + mod[3:4]
    h2_ref[...] = _pack_bf16_pairs(h2)
    lgt_ref[...] = lax.dot_general(wrt_ref[...], h2.astype(BF16), (((1,), (1,)), ((), ())),
                                   preferred_element_type=F32) + rbias_ref[...]

    uext_s[0:UHIST, :] = uext_s[ts:ts + UHIST, :]
    pext_s[0:PHIST, :] = pext_s[ts:ts + PHIST, :]


def _mixer(x, mod, w_in_r, gbias, conv_w, conv_b, wqk, ng, skip, wpool, bpool, pscale, pinv,
           w_out, wrt, rbias, b0, nb):
    _, seq, dm = x.shape
    dml = conv_w.shape[1]
    ts = min(SEQ_TILE, seq)
    ncols = w_in_r.shape[1]
    nst = seq // ts
    n_tok = nb * seq
    assert seq % ts == 0 and ts % CHUNK == 0
    full = lambda a: pl.BlockSpec(a.shape, lambda b, s: (0,) * a.ndim, pipeline_mode=pl.Buffered(1))
    kern = functools.partial(_mixer_kernel, ts=ts, dm=dm, dml=dml)
    return pl.pallas_call(
        kern,
        grid=(nb, nst),
        in_specs=[pl.BlockSpec((1, ts, dm), lambda b, s: (b + b0, s, 0)),
                  pl.BlockSpec((1, 6, dm), lambda b, s: (b + b0, 0, 0)),
                  full(w_in_r), full(gbias), full(conv_w), full(conv_b), full(wqk),
                  full(ng), full(skip), full(wpool), full(bpool), full(pscale),
                  pl.BlockSpec((1,) + pinv.shape[1:], lambda b, s: (jnp.minimum(s, 1), 0, 0)),
                  full(w_out), full(wrt), full(rbias)],
        out_specs=[pl.BlockSpec((ts, dm), lambda b, s: (b * nst + s, 0)),
                   pl.BlockSpec((ts, dm // 2), lambda b, s: (b * nst + s, 0)),
                   pl.BlockSpec((LOGIT_ROWS, ts), lambda b, s: (0, b * nst + s))],
        out_shape=[jax.ShapeDtypeStruct((n_tok, dm), F32),
                   jax.ShapeDtypeStruct((n_tok, dm // 2), I32),
                   jax.ShapeDtypeStruct((LOGIT_ROWS, n_tok), F32)],
        scratch_shapes=[pltpu.VMEM((UHIST + ts, dml), F32),
                        pltpu.VMEM((PHIST + ts, dm - dml), F32),
                        pltpu.VMEM((ts, ncols - dm), F32),
                        pltpu.VMEM((ts, dml), F32),
                        pltpu.VMEM((ts, 2 * dml), F32),
                        pltpu.VMEM((ts, dm), BF16),
                        pltpu.VMEM((ts + PHIST, 2 * LANES), F32),
                        pltpu.VMEM((N_HEADS, HEAD_DIM, HEAD_DIM), F32),
                        pltpu.VMEM((N_HEADS, HEAD_DIM, HEAD_DIM), F32),
                        pltpu.VMEM((N_HEADS, CHUNK, LANES), F32)],
        compiler_params=pltpu.CompilerParams(dimension_semantics=("arbitrary", "arbitrary"),
                                             vmem_limit_bytes=VMEM_LIMIT),
        name="mixer",
    )(x, mod, w_in_r, gbias, conv_w, conv_b, wqk, ng, skip, wpool, bpool, pscale, pinv, w_out,
      wrt, rbias)


def _route_kernel(lgt_ref, idx_ref, gate_ref, cnt_ref, carry_s, *, tr, n_sub):
    @pl.when(pl.program_id(0) == 0)
    def _():
        carry_s[...] = jnp.zeros_like(carry_s)

    tr_r = lax.broadcasted_iota(I32, (tr, tr), 0)
    tr_c = lax.broadcasted_iota(I32, (tr, tr), 1)
    upper = jnp.where(tr_r < tr_c, 1.0, 0.0).astype(BF16)
    for q in range(n_sub):
        cols = slice(q * tr, (q + 1) * tr)
        idx, gates = _route_tile(lgt_ref[:, cols], upper, carry_s, tr)
        idx_ref[:, cols] = idx
        gate_ref[:, cols] = gates
    cnt_ref[...] = carry_s[...]


def _route_tile(lg, upper, carry_s, tr):
    best = lg[0:1]
    gidx = jnp.zeros((1, tr), I32)
    for j in range(1, N_GROUPS):
        cand = lg[j:j + 1]
        better = cand > best
        gidx = jnp.where(better, j, gidx)
        best = jnp.where(better, cand, best)
    sumexp = jnp.zeros((1, tr), F32)
    for j in range(N_GROUPS):
        sumexp = sumexp + jnp.exp(lg[j:j + 1] - best)
    g_gate = 1.0 / sumexp

    sel = lg[SUBLANES:2 * SUBLANES]
    for j in range(1, N_GROUPS):
        sel = jnp.where(gidx == j, lg[SUBLANES * (j + 1):SUBLANES * (j + 2)], sel)
    sub = lax.broadcasted_iota(I32, (EXPERTS_PER_GROUP, tr), 0)
    v1 = jnp.max(sel, axis=0, keepdims=True)
    i1 = jnp.min(jnp.where(sel == v1, sub, EXPERTS_PER_GROUP), axis=0, keepdims=True)
    sel2 = jnp.where(sub == i1, -jnp.inf, sel)
    v2 = jnp.max(sel2, axis=0, keepdims=True)
    i2 = jnp.min(jnp.where(sel2 == v2, sub, EXPERTS_PER_GROUP), axis=0, keepdims=True)
    e2 = jnp.exp(v2 - v1)
    den = 1.0 + e2
    gate0 = (1.0 / den) * g_gate
    gate1 = (e2 / den) * g_gate
    ex0 = gidx * EXPERTS_PER_GROUP + i1
    ex1 = gidx * EXPERTS_PER_GROUP + i2

    erow = lax.broadcasted_iota(I32, (N_EXPERTS, tr), 0)
    oh0 = erow == ex0
    oh1 = erow == ex1
    oh = jnp.where(oh0 | oh1, 1.0, 0.0).astype(BF16)
    carry = carry_s[...]
    before = jnp.dot(oh, upper, preferred_element_type=F32)
    before = before + jnp.concatenate([carry] * (tr // LANES), axis=1)
    rank0 = jnp.sum(jnp.where(oh0, before, 0.0), axis=0, keepdims=True)
    rank1 = jnp.sum(jnp.where(oh1, before, 0.0), axis=0, keepdims=True)
    carry_s[...] = carry + jnp.dot(oh, jnp.ones((tr, LANES), BF16), preferred_element_type=F32)

    idx = jnp.concatenate([ex0, ex1, rank0.astype(I32), rank1.astype(I32),
                           jnp.zeros((SUBLANES - 4, tr), I32)], axis=0)
    gates = jnp.concatenate([gate0, gate1, jnp.zeros((SUBLANES - 2, tr), F32)], axis=0)
    return idx, gates


def _route(lgt, n_tok):
    tr = ROUTE_TILE
    n_sub = min(ROUTE_SUBTILES, n_tok // tr)
    step = tr * n_sub
    assert n_tok % step == 0
    return pl.pallas_call(
        functools.partial(_route_kernel, tr=tr, n_sub=n_sub),
        grid=(n_tok // step,),
        in_specs=[pl.BlockSpec((LOGIT_ROWS, step), lambda i: (0, i))],
        out_specs=[pl.BlockSpec((SUBLANES, step), lambda i: (0, i)),
                   pl.BlockSpec((SUBLANES, step), lambda i: (0, i)),
                   pl.BlockSpec((N_EXPERTS, LANES), lambda i: (0, 0))],
        out_shape=[jax.ShapeDtypeStruct((SUBLANES, n_tok), I32),
                   jax.ShapeDtypeStruct((SUBLANES, n_tok), F32),
                   jax.ShapeDtypeStruct((N_EXPERTS, LANES), F32)],
        scratch_shapes=[pltpu.VMEM((N_EXPERTS, LANES), F32)],
        compiler_params=pltpu.CompilerParams(dimension_semantics=("arbitrary",),
                                             vmem_limit_bytes=VMEM_LIMIT),
        name="route",
    )(lgt)


def _sc_workers():
    info = plsc.get_sparse_core_info()
    return info.num_cores, info.num_cores * info.num_subcores


def _dispatch(h2p, dest0, dest1, n_slots):
    n_tok, width = h2p.shape
    n_cores, n_workers = _sc_workers()
    per_w = n_tok // n_workers
    ch = min(SC_SCATTER_CHUNK, per_w)
    n_ch = per_w // ch
    assert n_tok % n_workers == 0 and per_w % ch == 0 and ch % SUBLANES == 0 and n_ch % 2 == 0
    mesh = plsc.VectorSubcoreMesh(core_axis_name="c", subcore_axis_name="s")

    @functools.partial(
        pl.kernel, mesh=mesh,
        out_type=jax.ShapeDtypeStruct((n_slots, width), h2p.dtype),
        scratch_types=[pltpu.VMEM((2, ch), I32), pltpu.VMEM((2, ch), I32),
                       pltpu.VMEM((2, ch, width), h2p.dtype),
                       pltpu.SemaphoreType.DMA((2,)), pltpu.SemaphoreType.DMA((2,))],
        name="dispatch")
    def scatter(h_hbm, d0_hbm, d1_hbm, xs_hbm, i0_v, i1_v, rows_v, sem_in, sem_out):
        wid = lax.axis_index("s") * n_cores + lax.axis_index("c")
        base = wid * per_w

        def loads(t0, slot):
            return (pltpu.make_async_copy(d0_hbm.at[pl.ds(t0, ch)], i0_v.at[slot], sem_in.at[slot]),
                    pltpu.make_async_copy(d1_hbm.at[pl.ds(t0, ch)], i1_v.at[slot], sem_in.at[slot]),
                    pltpu.make_async_copy(h_hbm.at[pl.ds(t0, ch)], rows_v.at[slot], sem_in.at[slot]))

        def scatters(slot):
            return (pltpu.make_async_copy(rows_v.at[slot], xs_hbm.at[i0_v.at[slot]], sem_out.at[slot]),
                    pltpu.make_async_copy(rows_v.at[slot], xs_hbm.at[i1_v.at[slot]], sem_out.at[slot]))

        for cp in loads(base, 0):
            cp.start()

        @pl.loop(0, n_ch, step=2)
        def _(k):
            for slot in range(2):
                for cp in loads(base + (k + slot) * ch, slot):
                    cp.wait()
                out_cps = scatters(slot)
                for cp in out_cps:
                    cp.start()
                nxt = k + slot + 1

                @pl.when(nxt < n_ch)
                def _():
                    for cp in loads(base + nxt * ch, 1 - slot):
                        cp.start()

                for cp in out_cps:
                    cp.wait()

    return scatter(h2p, dest0, dest1)


def _collect(ys, dest0, dest1, tok0, n_tok):
    width = ys.shape[1]
    n_cores, n_workers = _sc_workers()
    per_w = n_tok // n_workers
    ch = min(SC_GATHER_CHUNK, per_w)
    n_ch = per_w // ch
    assert n_tok % n_workers == 0 and per_w % ch == 0 and ch % SUBLANES == 0 and n_ch % 2 == 0
    mesh = plsc.VectorSubcoreMesh(core_axis_name="c", subcore_axis_name="s")

    @functools.partial(
        pl.kernel, mesh=mesh,
        out_type=jax.ShapeDtypeStruct((TOP_K, n_tok, width), ys.dtype),
        scratch_types=[pltpu.VMEM((2, ch), I32), pltpu.VMEM((2, ch), I32),
                       pltpu.VMEM((2, ch, width), ys.dtype), pltpu.VMEM((2, ch, width), ys.dtype),
                       pltpu.SemaphoreType.DMA((2,)), pltpu.SemaphoreType.DMA((2,)),
                       pltpu.SemaphoreType.DMA((2,))],
        name="collect")
    def gather(ys_hbm, d0_hbm, d1_hbm, o_hbm, i0_v, i1_v, r0_v, r1_v, sem_idx, sem_in, sem_out):
        wid = lax.axis_index("s") * n_cores + lax.axis_index("c")
        base = wid * per_w

        def idx_loads(t0, slot):
            return (pltpu.make_async_copy(d0_hbm.at[pl.ds(tok0 + t0, ch)], i0_v.at[slot], sem_idx.at[slot]),
                    pltpu.make_async_copy(d1_hbm.at[pl.ds(tok0 + t0, ch)], i1_v.at[slot], sem_idx.at[slot]))

        def gathers(slot):
            return (pltpu.make_async_copy(ys_hbm.at[i0_v.at[slot]], r0_v.at[slot], sem_in.at[slot]),
                    pltpu.make_async_copy(ys_hbm.at[i1_v.at[slot]], r1_v.at[slot], sem_in.at[slot]))

        def stores(t0, slot):
            return (pltpu.make_async_copy(r0_v.at[slot], o_hbm.at[0, pl.ds(t0, ch)], sem_out.at[slot]),
                    pltpu.make_async_copy(r1_v.at[slot], o_hbm.at[1, pl.ds(t0, ch)], sem_out.at[slot]))

        def start_chunk(t0, slot):
            for cp in idx_loads(t0, slot):
                cp.start()
            for cp in idx_loads(t0, slot):
                cp.wait()
            for cp in gathers(slot):
                cp.start()

        start_chunk(base, 0)

        @pl.loop(0, n_ch, step=2)
        def _(k):
            for slot in range(2):
                nxt = k + slot + 1

                @pl.when(nxt < n_ch)
                def _():
                    start_chunk(base + nxt * ch, 1 - slot)

                for cp in gathers(slot):
                    cp.wait()
                out_cps = stores(base + (k + slot) * ch, slot)
                for cp in out_cps:
                    cp.start()
                for cp in out_cps:
                    cp.wait()

    return gather(ys, dest0, dest1)


def _experts_kernel(first_ref, nblk_ref, cnt_ref, xs_hbm, wg_ref, wu_ref, wd_ref, ys_hbm,
                    wgu_s, wd_s, xbuf, ybuf, sem_in, sem_out, *, de, blk):
    e = pl.program_id(0)
    n_exp = pl.num_programs(0)
    wgu_s[:, 0:de] = wg_ref[0].astype(BF16)
    wgu_s[:, de:2 * de] = wu_ref[0].astype(BF16)
    wd_s[...] = wd_ref[0].astype(BF16)
    first = first_ref[e]
    n_blk = nblk_ref[e]
    count = cnt_ref[e]
    total = first_ref[n_exp - 1] + nblk_ref[n_exp - 1]

    def in_copy(g, slot):
        return pltpu.make_async_copy(xs_hbm.at[pl.ds(pl.multiple_of(g * blk, blk), blk)],
                                     xbuf.at[slot], sem_in.at[slot])

    def out_copy(g, slot):
        return pltpu.make_async_copy(ybuf.at[slot],
                                     ys_hbm.at[pl.ds(pl.multiple_of(g * blk, blk), blk)],
                                     sem_out.at[slot])

    @pl.when(e == 0)
    def _():
        in_copy(0, 0).start(priority=1)
        ybuf[...] = jnp.zeros_like(ybuf)

    def block(j, carry):
        g = first + j
        slot = lax.rem(g, 2)
        in_copy(g, slot).wait()

        @pl.when(g + 1 < total)
        def _():
            in_copy(g + 1, 1 - slot).start(priority=1)

        @pl.when(g >= 2)
        def _():
            out_copy(g - 2, slot).wait()

        n_left = count - j * blk

        def ffn(n_rows):
            words = xbuf[slot, 0:n_rows, :]
            rows = lax.broadcasted_iota(I32, words.shape, 0)
            xb = _unpack_bf16_pairs(jnp.where(rows < n_left, words, 0)).astype(BF16)
            ab = jnp.dot(xb, wgu_s[...], preferred_element_type=F32)
            a = ab[:, 0:de]
            b = ab[:, de:2 * de]
            hmid = (a * _sigmoid(a)) * b
            y = jnp.dot(hmid.astype(BF16), wd_s[...], preferred_element_type=F32)
            ybuf[slot, 0:n_rows, :] = _pack_bf16_pairs(y)

        step = blk // EXPERT_BLOCK_PATHS
        for p in range(EXPERT_BLOCK_PATHS):
            lo_rows, hi_rows = p * step, (p + 1) * step
            last = p == EXPERT_BLOCK_PATHS - 1

            @pl.when((n_left > lo_rows) & ((n_left <= hi_rows) | last))
            def _(hi_rows=hi_rows):
                ffn(hi_rows)

        out_copy(g, slot).start(priority=1)
        return carry

    lax.fori_loop(0, n_blk, block, 0)

    @pl.when(e == n_exp - 1)
    def _():
        @pl.when(total >= 2)
        def _():
            out_copy(total - 2, lax.rem(total, 2)).wait()

        out_copy(total - 1, lax.rem(total + 1, 2)).wait()


def _experts(plan, xs, w_gate, w_up, w_down):
    n_slots, width = xs.shape
    n_exp, dm, de = w_gate.shape
    blk = EXPERT_BLOCK
    return pl.pallas_call(
        functools.partial(_experts_kernel, de=de, blk=blk),
        grid_spec=pltpu.PrefetchScalarGridSpec(
            num_scalar_prefetch=3,
            grid=(n_exp,),
            in_specs=[pl.BlockSpec(memory_space=pl.ANY),
                      pl.BlockSpec((1, dm, de), lambda e, f, n, c: (e, 0, 0)),
                      pl.BlockSpec((1, dm, de), lambda e, f, n, c: (e, 0, 0)),
                      pl.BlockSpec((1, de, dm), lambda e, f, n, c: (e, 0, 0))],
            out_specs=pl.BlockSpec(memory_space=pl.ANY),
            scratch_shapes=[pltpu.VMEM((dm, 2 * de), BF16), pltpu.VMEM((de, dm), BF16),
                            pltpu.VMEM((2, blk, width), I32), pltpu.VMEM((2, blk, width), I32),
                            pltpu.SemaphoreType.DMA((2,)), pltpu.SemaphoreType.DMA((2,))]),
        out_shape=jax.ShapeDtypeStruct((n_slots, width), I32),
        compiler_params=pltpu.CompilerParams(dimension_semantics=("arbitrary",),
                                             vmem_limit_bytes=VMEM_LIMIT),
        name="experts",
    )(*plan, xs, w_gate, w_up, w_down)


def _pick(onehot, table):
    return jnp.sum(jnp.where(onehot, table[None, :], 0), axis=1).astype(I32)


def _expert_plan(counts, n_assign):
    blk = EXPERT_BLOCK
    per_e = (counts + blk - 1) // blk
    first_blk = jnp.cumsum(per_e) - per_e
    n_slots = (n_assign // blk + N_EXPERTS) * blk
    return (first_blk * blk).astype(I32), (first_blk.astype(I32), per_e.astype(I32), counts), n_slots


def _slot_of(starts, expert, rank):
    onehot = jnp.arange(N_EXPERTS, dtype=I32)[None, :] == expert[:, None]
    return _pick(onehot, starts) + rank


def _combine_kernel(rows_ref, gate_ref, x1_ref, mod_ref, fg_ref, *rest):
    o_ref = rest[-1]
    tc = x1_ref.shape[0]
    gate_f = mod_ref[0][5:6]
    for q in range(tc // LANES):
        tok = slice(q * LANES, (q + 1) * LANES)
        g_rows = jnp.concatenate([gate_ref[:, tok], jnp.zeros((LANES - SUBLANES, LANES), F32)], axis=0)
        gc = g_rows.T
        y = (gc[:, 0:1] * _unpack_bf16_pairs(rows_ref[0, tok, :])
             + gc[:, 1:2] * _unpack_bf16_pairs(rows_ref[1, tok, :]))
        x2 = x1_ref[tok, :] + gate_f * y
        r = lax.rsqrt(jnp.mean(x2 * x2, axis=-1, keepdims=True) + EPS)
        o_ref[tok, :] = (x2 * r) * fg_ref[...]


def _combine(rows, gcol, x1, mod, final_g, seq, lb0, b0, nb, bsz, out_prev):
    dm = x1.shape[1]
    width = rows.shape[2]
    tc = min(COMBINE_TILE, seq)
    nst = seq // tc
    in_specs = [pl.BlockSpec((TOP_K, tc, width), lambda b, s: (0, b * nst + s, 0)),
                pl.BlockSpec((SUBLANES, tc), lambda b, s: (0, (b + lb0) * nst + s)),
                pl.BlockSpec((tc, dm), lambda b, s: ((b + lb0) * nst + s, 0)),
                pl.BlockSpec((1, 6, dm), lambda b, s: (b + b0, 0, 0)),
                pl.BlockSpec((1, dm), lambda b, s: (0, 0))]
    args = [rows, gcol, x1, mod, final_g.reshape(1, dm)]
    aliases = {}
    if out_prev is not None:
        in_specs.append(pl.BlockSpec(memory_space=pl.ANY))
        args.append(out_prev)
        aliases = {len(args) - 1: 0}
    return pl.pallas_call(
        _combine_kernel,
        grid=(nb, nst),
        in_specs=in_specs,
        out_specs=pl.BlockSpec((tc, dm), lambda b, s: ((b + b0) * nst + s, 0)),
        out_shape=jax.ShapeDtypeStruct((bsz * seq, dm), F32),
        input_output_aliases=aliases,
        compiler_params=pltpu.CompilerParams(dimension_semantics=("arbitrary", "arbitrary"),
                                             vmem_limit_bytes=VMEM_LIMIT),
        name="combine",
    )(*args)


def _layer(x, c, ada_w, ada_b, norm1_g, w_in, conv_w, conv_b, w_q, w_k, b_igate, b_fgate,
           mlstm_norm_g, mlstm_skip, w_pool, b_pool, pool_scale, w_out, norm2_g,
           w_rg, b_rg, w_re, b_re, w_eg, w_eu, w_ed, out_g):
    bsz, seq, dm = x.shape
    dml = conv_w.shape[1]
    n_tok = bsz * seq
    ts = min(SEQ_TILE, seq)

    mod = _ada(c, ada_w, ada_b).reshape(bsz, 6, dm)

    col_v = dml
    col_o = 2 * dml
    col_i = 3 * dml
    col_p = col_i + 2 * N_HEADS
    w_gate_cols = jnp.pad(w_in[:, col_i:col_p], ((0, 0), (0, LANES - 2 * N_HEADS)))
    w_in_r = jnp.concatenate([w_in[:, :col_v], w_in[:, col_p:], w_in[:, col_v:col_o],
                              w_in[:, col_o:col_i], w_gate_cols], axis=1).astype(BF16)
    gbias = jnp.pad(jnp.concatenate([b_igate, b_fgate]), (0, LANES - 2 * N_HEADS)).reshape(1, LANES)
    wqk = jnp.concatenate([w_q, w_k], axis=-1).astype(BF16)
    wrt = jnp.zeros((LOGIT_ROWS, dm), F32)
    wrt = wrt.at[0:N_GROUPS].set(w_rg.T).at[SUBLANES:SUBLANES + N_EXPERTS].set(w_re.T).astype(BF16)
    rb = jnp.zeros((LOGIT_ROWS,), F32).at[0:N_GROUPS].set(b_rg).at[SUBLANES:SUBLANES + N_EXPERTS].set(b_re)
    rbias = jnp.broadcast_to(rb[:, None], (LOGIT_ROWS, ts))

    gdim = (dm - dml) // len(POOL_WINDOWS)
    win = jnp.repeat(jnp.array(POOL_WINDOWS, F32), gdim)[None, :]
    t1 = jnp.arange(1, ts + 1, dtype=F32)[:, None]
    pinv = jnp.stack([1.0 / jnp.minimum(t1, win), jnp.broadcast_to(1.0 / win, (ts, dm - dml))])

    mod = mod.at[:, 1].set(norm1_g * (1.0 + mod[:, 1])).at[:, 4].set(norm2_g * (1.0 + mod[:, 4]))

    mixer_params = (w_in_r, gbias, conv_w, conv_b.reshape(1, dml), wqk,
                    mlstm_norm_g.reshape(1, dml), mlstm_skip.reshape(1, dml), w_pool.astype(BF16),
                    b_pool.reshape(1, dm - dml), pool_scale.reshape(1, dm - dml), pinv,
                    w_out.astype(BF16), wrt, rbias)

    gb = bsz // BATCH_GROUPS
    assert bsz % BATCH_GROUPS == 0 and gb % COMBINE_GROUPS == 0
    staged = []
    for g0 in range(0, bsz, gb):
        x1, h2, lgt = _mixer(x, mod, *mixer_params, g0, gb)
        idx, gcol, cnt = _route(lgt, gb * seq)
        counts = cnt[:, 0].astype(I32)
        starts, plan, n_slots = _expert_plan(counts, gb * seq * TOP_K)
        dest0 = _slot_of(starts, idx[0], idx[2])
        dest1 = _slot_of(starts, idx[1], idx[3])
        xs = _dispatch(h2, dest0, dest1, n_slots)
        staged.append((g0, x1, gcol, dest0, dest1, plan, xs))

    expert_out = [(g0, x1, gcol, dest0, dest1, _experts(plan, xs, w_eg, w_eu, w_ed))
                  for g0, x1, gcol, dest0, dest1, plan, xs in staged]

    nb = gb // COMBINE_GROUPS
    out = None
    for g0, x1, gcol, dest0, dest1, ys in expert_out:
        for lb0 in range(0, gb, nb):
            rows = _collect(ys, dest0, dest1, lb0 * seq, nb * seq)
            out = _combine(rows, gcol, x1, mod, out_g, seq, lb0, g0 + lb0, nb, bsz, out)
    return out.reshape(bsz, seq, dm)


def kernel(x, c, ada_w, ada_b, norm1_g, w_in, conv_w, conv_b, w_q, w_k, b_igate, b_fgate, mlstm_norm_g, mlstm_skip, w_pool, b_pool, pool_scale, w_out, norm2_g, w_router_group, b_router_group, w_router_expert, b_router_expert, w_expert_gate, w_expert_up, w_expert_down, final_g):
    depth = ada_w.shape[0]
    assert depth == 1, "the final norm is fused into the last layer's combine kernel"
    l = 0
    return _layer(x, c, ada_w[l], ada_b[l], norm1_g[l], w_in[l], conv_w[l], conv_b[l], w_q[l],
                  w_k[l], b_igate[l], b_fgate[l], mlstm_norm_g[l], mlstm_skip[l], w_pool[l],
                  b_pool[l], pool_scale[l], w_out[l], norm2_g[l], w_router_group[l],
                  b_router_group[l], w_router_expert[l], b_router_expert[l],
                  w_expert_gate[l], w_expert_up[l], w_expert_down[l], final_g)
```

```python
import functools

import jax
import jax.numpy as jnp
from jax import lax
from jax.experimental import pallas as pl
from jax.experimental.pallas import tpu as pltpu
from jax.experimental.pallas import tpu_sc as plsc

F32 = jnp.float32
BF16 = jnp.bfloat16
I32 = jnp.int32
U32 = jnp.uint32

EPS = 1e-6
N_HEADS = 4
HEAD_DIM = 128
CONV_WIDTH = 4
POOL_WINDOWS = (2, 4, 8, 16)
N_GROUPS = 4
EXPERTS_PER_GROUP = 8
N_EXPERTS = N_GROUPS * EXPERTS_PER_GROUP
TOP_K = 2

LANES = 128
SUBLANES = 8
CHUNK = 128
SEQ_TILE = 1024
ADA_TILE = 1024
COMBINE_GROUPS = 8
ROUTE_TILE = 512
ROUTE_SUBTILES = 4
SC_SCATTER_CHUNK = 64
SC_GATHER_CHUNK = 32
COMBINE_TILE = 512
EXPERT_BLOCK = 1024
EXPERT_BLOCK_PATHS = 8
LOGIT_ROWS = 48
UHIST = 8
PHIST = 16
VMEM_LIMIT = 60 * 1024 * 1024


def _sigmoid(x):
    return 1.0 / (1.0 + jnp.exp(-x))


def _pack_bf16_pairs(x):
    w = x.shape[1] // 2
    half_ulp = jnp.uint32(0x8000)
    hi = lax.bitcast_convert_type(x[:, :w], U32) + half_ulp
    lo = lax.bitcast_convert_type(x[:, w:], U32) + half_ulp
    return lax.bitcast_convert_type((hi & jnp.uint32(0xFFFF0000)) | (lo >> 16), I32)


def _unpack_bf16_pairs(words):
    u = lax.bitcast_convert_type(words, U32)
    hi = lax.bitcast_convert_type(u & jnp.uint32(0xFFFF0000), F32)
    lo = lax.bitcast_convert_type(u << 16, F32)
    return jnp.concatenate([hi, lo], axis=1)


def _ada_kernel(c_ref, w_ref, b_ref, o_ref):
    c = c_ref[...]
    s = c * _sigmoid(c)
    o_ref[...] = jnp.dot(s.astype(BF16), w_ref[...].astype(BF16),
                         preferred_element_type=F32) + b_ref[...]


def _ada(c, ada_w, ada_b):
    bsz, dm = c.shape
    n = ada_w.shape[1]
    tn = ADA_TILE
    return pl.pallas_call(
        _ada_kernel,
        grid=(n // tn,),
        in_specs=[pl.BlockSpec((bsz, dm), lambda j: (0, 0)),
                  pl.BlockSpec((dm, tn), lambda j: (0, j)),
                  pl.BlockSpec((1, tn), lambda j: (0, j))],
        out_specs=pl.BlockSpec((bsz, tn), lambda j: (0, j)),
        out_shape=jax.ShapeDtypeStruct((bsz, n), F32),
        compiler_params=pltpu.CompilerParams(dimension_semantics=("arbitrary",),
                                             vmem_limit_bytes=VMEM_LIMIT),
        name="ada",
    )(c, ada_w, ada_b.reshape(1, n))


def _split3(x):
    hi = x.astype(BF16)
    r1 = x - hi.astype(F32)
    mid = r1.astype(BF16)
    lo = (r1 - mid.astype(F32)).astype(BF16)
    return hi, mid, lo


def _mixer_kernel(x_ref, mod_ref, win_ref, gbias_ref, convw_ref, convb_ref, wqk_ref,
                  ng_ref, skip_ref, wpool_ref, bpool_ref, pscale_ref, pinv_ref, wout_ref,
                  wrt_ref, rbias_ref,
                  x1_ref, h2_ref, lgt_ref,
                  uext_s, pext_s, proj_s, uc_s, qk_s, mix_s, pool4_s, ctv_s, ctn_s, mprev_s,
                  *, ts, dm, dml):
    s_idx = pl.program_id(1)
    n_chunks = ts // CHUNK
    dp = dm - dml

    @pl.when(s_idx == 0)
    def _():
        uext_s[0:UHIST, :] = jnp.zeros((UHIST, dml), F32)
        pext_s[0:PHIST, :] = jnp.zeros((PHIST, dp), F32)
        ctv_s[...] = jnp.zeros_like(ctv_s)
        ctn_s[...] = jnp.zeros_like(ctn_s)
        mprev_s[...] = jnp.zeros_like(mprev_s)

    row_i = lax.broadcasted_iota(I32, (CHUNK, CHUNK), 0)
    col_i = lax.broadcasted_iota(I32, (CHUNK, CHUNK), 1)
    causal = row_i >= col_i
    triu = jnp.where(row_i <= col_i, 1.0, 0.0).astype(BF16)
    ones_blk = jnp.ones((CHUNK, HEAD_DIM), BF16)
    q_scale = HEAD_DIM ** -0.5

    col_o = dml
    col_g = 2 * dml

    x = x_ref[0]
    mod = mod_ref[0]
    r = lax.rsqrt(jnp.mean(x * x, axis=-1, keepdims=True) + EPS)
    h = (x * r) * mod[1:2] + mod[0:1]
    res = jnp.dot(h.astype(BF16), win_ref[...], preferred_element_type=F32)
    uext_s[UHIST:, :] = res[:, 0:dml]
    pext_s[PHIST:, :] = res[:, dml:dm]
    proj_s[...] = res[:, dm:]

    acc = None
    for j in reversed(range(CONV_WIDTH)):
        tap = uext_s[pl.ds(UHIST - (CONV_WIDTH - 1 - j), ts), :] * convw_ref[j:j + 1, :]
        acc = tap if acc is None else acc + tap
    conv = acc + convb_ref[...]
    uc = conv * _sigmoid(conv)
    uc_s[...] = uc

    for hd in range(N_HEADS):
        qk = jnp.dot(uc[:, HEAD_DIM * hd:HEAD_DIM * (hd + 1)].astype(BF16), wqk_ref[hd],
                     preferred_element_type=F32)
        c0 = 2 * HEAD_DIM * hd
        qk_s[:, c0:c0 + HEAD_DIM] = qk[:, 0:HEAD_DIM] * q_scale
        qk_s[:, c0 + HEAD_DIM:c0 + 2 * HEAD_DIM] = qk[:, HEAD_DIM:2 * HEAD_DIM]

    pairs = [(c, hd) for c in range(n_chunks) for hd in range(N_HEADS)]
    g_rows, b_rows, b_cols = [], [], []
    pad_rows = jnp.zeros((CHUNK - SUBLANES, CHUNK), F32)
    for c in range(n_chunks):
        g_tile = proj_s[pl.ds(c * CHUNK, CHUNK), col_g:col_g + LANES] + gbias_ref[...]
        g_row = g_tile.T[0:SUBLANES, :]
        logf = -(jnp.maximum(-g_row, 0.0) + jnp.log1p(jnp.exp(-jnp.abs(g_row))))
        hi, mid, lo = _split3(logf)
        cs = jnp.dot(jnp.concatenate([hi, mid, lo], axis=0), triu, preferred_element_type=F32)
        b_row = cs[0:SUBLANES] + cs[SUBLANES:2 * SUBLANES] + cs[2 * SUBLANES:3 * SUBLANES]
        g_rows.append(g_row)
        b_rows.append(b_row)
        b_cols.append(jnp.concatenate([b_row, pad_rows], axis=0).T)

    def rows(ref, c, lo_col, width=HEAD_DIM):
        return ref[pl.ds(c * CHUNK, CHUNK), lo_col:lo_col + width]

    b_bcs, dlogs, rmaxs, p_mats = {}, {}, {}, {}
    for c, hd in pairs:
        b_bc = jnp.broadcast_to(b_cols[c][:, N_HEADS + hd:N_HEADS + hd + 1], (CHUNK, CHUNK))
        i_row = g_rows[c][hd:hd + 1, :]
        b_row = b_rows[c][N_HEADS + hd:N_HEADS + hd + 1, :]
        dlog = jnp.where(causal, (b_bc - b_row) + i_row, -jnp.inf)
        b_bcs[c, hd], dlogs[c, hd] = b_bc, dlog
        rmaxs[c, hd] = jnp.max(dlog, axis=-1, keepdims=True)
        q_c = rows(qk_s, c, 2 * HEAD_DIM * hd)
        k_c = rows(qk_s, c, 2 * HEAD_DIM * hd + HEAD_DIM)
        p_mats[c, hd] = lax.dot_general(q_c.astype(BF16), k_c.astype(BF16),
                                        (((1,), (1,)), ((), ())), preferred_element_type=F32)

    inters, m_ts = {}, {}
    for hd in range(N_HEADS):
        m_prev = mprev_s[hd]
        for c in range(n_chunks):
            inter = b_bcs[c, hd] + m_prev
            m_t = jnp.maximum(inter, rmaxs[c, hd])
            inters[c, hd], m_ts[c, hd] = inter, m_t
            m_prev = jnp.broadcast_to(m_t[CHUNK - 1:CHUNK, :], (CHUNK, LANES))
        mprev_s[hd] = m_prev

    lhs, v_augs, upds, a_prevs, e_negms = {}, {}, {}, {}, {}
    for c, hd in pairs:
        m_t = m_ts[c, hd]
        wm = jnp.exp(dlogs[c, hd] - m_t)
        a_inter = jnp.exp(inters[c, hd] - m_t)
        e_negms[c, hd] = jnp.exp(-m_t)
        q_c = rows(qk_s, c, 2 * HEAD_DIM * hd)
        k_c = rows(qk_s, c, 2 * HEAD_DIM * hd + HEAD_DIM)
        v_c = rows(proj_s, c, HEAD_DIM * hd)
        s_mat = (p_mats[c, hd] * wm).astype(BF16)
        qa = (q_c * a_inter).astype(BF16)
        lhs[c, hd] = jnp.concatenate([s_mat, qa], axis=1)
        v_aug = jnp.concatenate([v_c.astype(BF16), ones_blk], axis=1)
        v_augs[c, hd] = v_aug
        ktw = (k_c.T * wm[CHUNK - 1:CHUNK, :]).astype(BF16)
        upds[c, hd] = jnp.dot(ktw, v_aug, preferred_element_type=F32)
        a_prevs[c, hd] = a_inter[CHUNK - 1:CHUNK, :]

    ct_in = {}
    for hd in range(N_HEADS):
        ctv, ctn = ctv_s[hd], ctn_s[hd]
        for c in range(n_chunks):
            ct_in[c, hd] = jnp.concatenate([ctv, ctn], axis=1).astype(BF16)
            ctv = a_prevs[c, hd] * ctv + upds[c, hd][:, 0:HEAD_DIM]
            ctn = a_prevs[c, hd] * ctn + upds[c, hd][:, HEAD_DIM:2 * HEAD_DIM]
        ctv_s[hd], ctn_s[hd] = ctv, ctn

    for c, hd in pairs:
        numden = jnp.dot(lhs[c, hd], jnp.concatenate([v_augs[c, hd], ct_in[c, hd]], axis=0),
                         preferred_element_type=F32)
        num = numden[:, 0:HEAD_DIM]
        den = numden[:, HEAD_DIM:2 * HEAD_DIM]
        hh = num / jnp.maximum(jnp.abs(den), e_negms[c, hd])
        ms = jnp.mean(hh * hh, axis=-1, keepdims=True)
        hn = hh * lax.rsqrt(ms + EPS) * ng_ref[:, HEAD_DIM * hd:HEAD_DIM * (hd + 1)]
        o_c = rows(proj_s, c, col_o + HEAD_DIM * hd)
        uc_c = rows(uc_s, c, HEAD_DIM * hd)
        out_c = _sigmoid(o_c) * (hn + skip_ref[:, HEAD_DIM * hd:HEAD_DIM * (hd + 1)] * uc_c)
        mix_s[pl.ds(c * CHUNK, CHUNK), HEAD_DIM * hd:HEAD_DIM * (hd + 1)] = out_c.astype(BF16)

    def pe(shift, rows, lanes):
        return pext_s[pl.ds(PHIST - shift, rows), lanes]

    gd = LANES
    sums = []
    for gi in range(2):
        lanes = slice(gd * gi, gd * (gi + 1))
        tot = pe(0, ts, lanes)
        for j in range(1, POOL_WINDOWS[gi]):
            tot = tot + pe(j, ts, lanes)
        sums.append(tot)
    wide = slice(2 * gd, 4 * gd)
    s4 = pe(12, ts + 12, wide)
    for j in range(1, 4):
        s4 = s4 + pe(12 + j, ts + 12, wide)
    pool4_s[0:ts + 12, :] = s4
    s8 = pool4_s[pl.ds(4, ts + 8), :] + pool4_s[pl.ds(0, ts + 8), :]
    sums.append(s8[8:, 0:gd])
    sums.append(s8[8:, gd:2 * gd] + s8[0:ts, gd:2 * gd])
    for gi in range(len(POOL_WINDOWS)):
        lanes = slice(gd * gi, gd * (gi + 1))
        pooled = sums[gi] * pinv_ref[0, :, lanes] - pe(0, ts, lanes)
        yp = jnp.dot(pooled.astype(BF16), wpool_ref[gi], preferred_element_type=F32)
        yp = (yp + bpool_ref[:, lanes]) * pscale_ref[:, lanes]
        mix_s[:, dml + gd * gi:dml + gd * (gi + 1)] = yp.astype(BF16)

    mix = jnp.dot(mix_s[...], wout_ref[...], preferred_element_type=F32)
    x1 = x + mod[2:3] * mix
    x1_ref[...] = x1
    r2 = lax.rsqrt(jnp.mean(x1 * x1, axis=-1, keepdims=True) + EPS)
    h2 = (x1 * r2) * mod[4:5] + mod[3:4]
    h2_ref[...] = _pack_bf16_pairs(h2)
    lgt_ref[...] = lax.dot_general(wrt_ref[...], h2.astype(BF16), (((1,), (1,)), ((), ())),
                                   preferred_element_type=F32) + rbias_ref[...]

    uext_s[0:UHIST, :] = uext_s[ts:ts + UHIST, :]
    pext_s[0:PHIST, :] = pext_s[ts:ts + PHIST, :]


def _mixer(x, mod, w_in_r, gbias, conv_w, conv_b, wqk, ng, skip, wpool, bpool, pscale, pinv,
           w_out, wrt, rbias, b0, nb):
    _, seq, dm = x.shape
    dml = conv_w.shape[1]
    ts = min(SEQ_TILE, seq)
    ncols = w_in_r.shape[1]
    nst = seq // ts
    n_tok = nb * seq
    assert seq % ts == 0 and ts % CHUNK == 0
    full = lambda a: pl.BlockSpec(a.shape, lambda b, s: (0,) * a.ndim, pipeline_mode=pl.Buffered(1))
    kern = functools.partial(_mixer_kernel, ts=ts, dm=dm, dml=dml)
    return pl.pallas_call(
        kern,
        grid=(nb, nst),
        in_specs=[pl.BlockSpec((1, ts, dm), lambda b, s: (b + b0, s, 0)),
                  pl.BlockSpec((1, 6, dm), lambda b, s: (b + b0, 0, 0)),
                  full(w_in_r), full(gbias), full(conv_w), full(conv_b), full(wqk),
                  full(ng), full(skip), full(wpool), full(bpool), full(pscale),
                  pl.BlockSpec((1,) + pinv.shape[1:], lambda b, s: (jnp.minimum(s, 1), 0, 0)),
                  full(w_out), full(wrt), full(rbias)],
        out_specs=[pl.BlockSpec((ts, dm), lambda b, s: (b * nst + s, 0)),
                   pl.BlockSpec((ts, dm // 2), lambda b, s: (b * nst + s, 0)),
                   pl.BlockSpec((LOGIT_ROWS, ts), lambda b, s: (0, b * nst + s))],
        out_shape=[jax.ShapeDtypeStruct((n_tok, dm), F32),
                   jax.ShapeDtypeStruct((n_tok, dm // 2), I32),
                   jax.ShapeDtypeStruct((LOGIT_ROWS, n_tok), F32)],
        scratch_shapes=[pltpu.VMEM((UHIST + ts, dml), F32),
                        pltpu.VMEM((PHIST + ts, dm - dml), F32),
                        pltpu.VMEM((ts, ncols - dm), F32),
                        pltpu.VMEM((ts, dml), F32),
                        pltpu.VMEM((ts, 2 * dml), F32),
                        pltpu.VMEM((ts, dm), BF16),
                        pltpu.VMEM((ts + PHIST, 2 * LANES), F32),
                        pltpu.VMEM((N_HEADS, HEAD_DIM, HEAD_DIM), F32),
                        pltpu.VMEM((N_HEADS, HEAD_DIM, HEAD_DIM), F32),
                        pltpu.VMEM((N_HEADS, CHUNK, LANES), F32)],
        compiler_params=pltpu.CompilerParams(dimension_semantics=("arbitrary", "arbitrary"),
                                             vmem_limit_bytes=VMEM_LIMIT),
        name="mixer",
    )(x, mod, w_in_r, gbias, conv_w, conv_b, wqk, ng, skip, wpool, bpool, pscale, pinv, w_out,
      wrt, rbias)


def _route_kernel(lgt_ref, idx_ref, gate_ref, cnt_ref, carry_s, *, tr, n_sub):
    @pl.when(pl.program_id(0) == 0)
    def _():
        carry_s[...] = jnp.zeros_like(carry_s)

    tr_r = lax.broadcasted_iota(I32, (tr, tr), 0)
    tr_c = lax.broadcasted_iota(I32, (tr, tr), 1)
    upper = jnp.where(tr_r < tr_c, 1.0, 0.0).astype(BF16)
    for q in range(n_sub):
        cols = slice(q * tr, (q + 1) * tr)
        idx, gates = _route_tile(lgt_ref[:, cols], upper, carry_s, tr)
        idx_ref[:, cols] = idx
        gate_ref[:, cols] = gates
    cnt_ref[...] = carry_s[...]


def _route_tile(lg, upper, carry_s, tr):
    best = lg[0:1]
    gidx = jnp.zeros((1, tr), I32)
    for j in range(1, N_GROUPS):
        cand = lg[j:j + 1]
        better = cand > best
        gidx = jnp.where(better, j, gidx)
        best = jnp.where(better, cand, best)
    sumexp = jnp.zeros((1, tr), F32)
    for j in range(N_GROUPS):
        sumexp = sumexp + jnp.exp(lg[j:j + 1] - best)
    g_gate = 1.0 / sumexp

    sel = lg[SUBLANES:2 * SUBLANES]
    for j in range(1, N_GROUPS):
        sel = jnp.where(gidx == j, lg[SUBLANES * (j + 1):SUBLANES * (j + 2)], sel)
    sub = lax.broadcasted_iota(I32, (EXPERTS_PER_GROUP, tr), 0)
    v1 = jnp.max(sel, axis=0, keepdims=True)
    i1 = jnp.min(jnp.where(sel == v1, sub, EXPERTS_PER_GROUP), axis=0, keepdims=True)
    sel2 = jnp.where(sub == i1, -jnp.inf, sel)
    v2 = jnp.max(sel2, axis=0, keepdims=True)
    i2 = jnp.min(jnp.where(sel2 == v2, sub, EXPERTS_PER_GROUP), axis=0, keepdims=True)
    e2 = jnp.exp(v2 - v1)
    den = 1.0 + e2
    gate0 = (1.0 / den) * g_gate
    gate1 = (e2 / den) * g_gate
    ex0 = gidx * EXPERTS_PER_GROUP + i1
    ex1 = gidx * EXPERTS_PER_GROUP + i2

    erow = lax.broadcasted_iota(I32, (N_EXPERTS, tr), 0)
    oh0 = erow == ex0
    oh1 = erow == ex1
    oh = jnp.where(oh0 | oh1, 1.0, 0.0).astype(BF16)
    carry = carry_s[...]
    before = jnp.dot(oh, upper, preferred_element_type=F32)
    before = before + jnp.concatenate([carry] * (tr // LANES), axis=1)
    rank0 = jnp.sum(jnp.where(oh0, before, 0.0), axis=0, keepdims=True)
    rank1 = jnp.sum(jnp.where(oh1, before, 0.0), axis=0, keepdims=True)
    carry_s[...] = carry + jnp.dot(oh, jnp.ones((tr, LANES), BF16), preferred_element_type=F32)

    idx = jnp.concatenate([ex0, ex1, rank0.astype(I32), rank1.astype(I32),
                           jnp.zeros((SUBLANES - 4, tr), I32)], axis=0)
    gates = jnp.concatenate([gate0, gate1, jnp.zeros((SUBLANES - 2, tr), F32)], axis=0)
    return idx, gates


def _route(lgt, n_tok):
    tr = ROUTE_TILE
    n_sub = min(ROUTE_SUBTILES, n_tok // tr)
    step = tr * n_sub
    assert n_tok % step == 0
    return pl.pallas_call(
        functools.partial(_route_kernel, tr=tr, n_sub=n_sub),
        grid=(n_tok // step,),
        in_specs=[pl.BlockSpec((LOGIT_ROWS, step), lambda i: (0, i))],
        out_specs=[pl.BlockSpec((SUBLANES, step), lambda i: (0, i)),
                   pl.BlockSpec((SUBLANES, step), lambda i: (0, i)),
                   pl.BlockSpec((N_EXPERTS, LANES), lambda i: (0, 0))],
        out_shape=[jax.ShapeDtypeStruct((SUBLANES, n_tok), I32),
                   jax.ShapeDtypeStruct((SUBLANES, n_tok), F32),
                   jax.ShapeDtypeStruct((N_EXPERTS, LANES), F32)],
        scratch_shapes=[pltpu.VMEM((N_EXPERTS, LANES), F32)],
        compiler_params=pltpu.CompilerParams(dimension_semantics=("arbitrary",),
                                             vmem_limit_bytes=VMEM_LIMIT),
        name="route",
    )(lgt)


def _sc_workers():
    info = plsc.get_sparse_core_info()
    return info.num_cores, info.num_cores * info.num_subcores


def _dispatch(h2p, dest0, dest1, n_slots):
    n_tok, width = h2p.shape
    n_cores, n_workers = _sc_workers()
    per_w = n_tok // n_workers
    ch = min(SC_SCATTER_CHUNK, per_w)
    n_ch = per_w // ch
    assert n_tok % n_workers == 0 and per_w % ch == 0 and ch % SUBLANES == 0 and n_ch % 2 == 0
    mesh = plsc.VectorSubcoreMesh(core_axis_name="c", subcore_axis_name="s")

    @functools.partial(
        pl.kernel, mesh=mesh,
        out_type=jax.ShapeDtypeStruct((n_slots, width), h2p.dtype),
        scratch_types=[pltpu.VMEM((2, ch), I32), pltpu.VMEM((2, ch), I32),
                       pltpu.VMEM((2, ch, width), h2p.dtype),
                       pltpu.SemaphoreType.DMA((2,)), pltpu.SemaphoreType.DMA((2,))],
        name="dispatch")
    def scatter(h_hbm, d0_hbm, d1_hbm, xs_hbm, i0_v, i1_v, rows_v, sem_in, sem_out):
        wid = lax.axis_index("s") * n_cores + lax.axis_index("c")
        base = wid * per_w

        def loads(t0, slot):
            return (pltpu.make_async_copy(d0_hbm.at[pl.ds(t0, ch)], i0_v.at[slot], sem_in.at[slot]),
                    pltpu.make_async_copy(d1_hbm.at[pl.ds(t0, ch)], i1_v.at[slot], sem_in.at[slot]),
                    pltpu.make_async_copy(h_hbm.at[pl.ds(t0, ch)], rows_v.at[slot], sem_in.at[slot]))

        def scatters(slot):
            return (pltpu.make_async_copy(rows_v.at[slot], xs_hbm.at[i0_v.at[slot]], sem_out.at[slot]),
                    pltpu.make_async_copy(rows_v.at[slot], xs_hbm.at[i1_v.at[slot]], sem_out.at[slot]))

        for cp in loads(base, 0):
            cp.start()

        @pl.loop(0, n_ch, step=2)
        def _(k):
            for slot in range(2):
                for cp in loads(base + (k + slot) * ch, slot):
                    cp.wait()
                out_cps = scatters(slot)
                for cp in out_cps:
                    cp.start()
                nxt = k + slot + 1

                @pl.when(nxt < n_ch)
                def _():
                    for cp in loads(base + nxt * ch, 1 - slot):
                        cp.start()

                for cp in out_cps:
                    cp.wait()

    return scatter(h2p, dest0, dest1)


def _collect(ys, dest0, dest1, tok0, n_tok):
    width = ys.shape[1]
    n_cores, n_workers = _sc_workers()
    per_w = n_tok // n_workers
    ch = min(SC_GATHER_CHUNK, per_w)
    n_ch = per_w // ch
    assert n_tok % n_workers == 0 and per_w % ch == 0 and ch % SUBLANES == 0 and n_ch % 2 == 0
    mesh = plsc.VectorSubcoreMesh(core_axis_name="c", subcore_axis_name="s")

    @functools.partial(
        pl.kernel, mesh=mesh,
        out_type=jax.ShapeDtypeStruct((TOP_K, n_tok, width), ys.dtype),
        scratch_types=[pltpu.VMEM((2, ch), I32), pltpu.VMEM((2, ch), I32),
                       pltpu.VMEM((2, ch, width), ys.dtype), pltpu.VMEM((2, ch, width), ys.dtype),
                       pltpu.SemaphoreType.DMA((2,)), pltpu.SemaphoreType.DMA((2,)),
                       pltpu.SemaphoreType.DMA((2,))],
        name="collect")
    def gather(ys_hbm, d0_hbm, d1_hbm, o_hbm, i0_v, i1_v, r0_v, r1_v, sem_idx, sem_in, sem_out):
        wid = lax.axis_index("s") * n_cores + lax.axis_index("c")
        base = wid * per_w

        def idx_loads(t0, slot):
            return (pltpu.make_async_copy(d0_hbm.at[pl.ds(tok0 + t0, ch)], i0_v.at[slot], sem_idx.at[slot]),
                    pltpu.make_async_copy(d1_hbm.at[pl.ds(tok0 + t0, ch)], i1_v.at[slot], sem_idx.at[slot]))

        def gathers(slot):
            return (pltpu.make_async_copy(ys_hbm.at[i0_v.at[slot]], r0_v.at[slot], sem_in.at[slot]),
                    pltpu.make_async_copy(ys_hbm.at[i1_v.at[slot]], r1_v.at[slot], sem_in.at[slot]))

        def stores(t0, slot):
            return (pltpu.make_async_copy(r0_v.at[slot], o_hbm.at[0, pl.ds(t0, ch)], sem_out.at[slot]),
                    pltpu.make_async_copy(r1_v.at[slot], o_hbm.at[1, pl.ds(t0, ch)], sem_out.at[slot]))

        def start_chunk(t0, slot):
            for cp in idx_loads(t0, slot):
                cp.start()
            for cp in idx_loads(t0, slot):
                cp.wait()
            for cp in gathers(slot):
                cp.start()

        start_chunk(base, 0)

        @pl.loop(0, n_ch, step=2)
        def _(k):
            for slot in range(2):
                nxt = k + slot + 1

                @pl.when(nxt < n_ch)
                def _():
                    start_chunk(base + nxt * ch, 1 - slot)

                for cp in gathers(slot):
                    cp.wait()
                out_cps = stores(base + (k + slot) * ch, slot)
                for cp in out_cps:
                    cp.start()
                for cp in out_cps:
                    cp.wait()

    return gather(ys, dest0, dest1)


def _experts_kernel(first_ref, nblk_ref, cnt_ref, xs_hbm, wg_ref, wu_ref, wd_ref, ys_hbm,
                    wgu_s, wd_s, xbuf, ybuf, sem_in, sem_out, *, de, blk):
    e = pl.program_id(0)
    n_exp = pl.num_programs(0)
    wgu_s[:, 0:de] = wg_ref[0].astype(BF16)
    wgu_s[:, de:2 * de] = wu_ref[0].astype(BF16)
    wd_s[...] = wd_ref[0].astype(BF16)
    first = first_ref[e]
    n_blk = nblk_ref[e]
    count = cnt_ref[e]
    total = first_ref[n_exp - 1] + nblk_ref[n_exp - 1]

    def in_copy(g, slot):
        return pltpu.make_async_copy(xs_hbm.at[pl.ds(pl.multiple_of(g * blk, blk), blk)],
                                     xbuf.at[slot], sem_in.at[slot])

    def out_copy(g, slot):
        return pltpu.make_async_copy(ybuf.at[slot],
                                     ys_hbm.at[pl.ds(pl.multiple_of(g * blk, blk), blk)],
                                     sem_out.at[slot])

    @pl.when(e == 0)
    def _():
        in_copy(0, 0).start(priority=1)
        ybuf[...] = jnp.zeros_like(ybuf)

    def block(j, carry):
        g = first + j
        slot = lax.rem(g, 2)
        in_copy(g, slot).wait()

        @pl.when(g + 1 < total)
        def _():
            in_copy(g + 1, 1 - slot).start(priority=1)

        @pl.when(g >= 2)
        def _():
            out_copy(g - 2, slot).wait()

        n_left = count - j * blk

        def ffn(n_rows):
            words = xbuf[slot, 0:n_rows, :]
            rows = lax.broadcasted_iota(I32, words.shape, 0)
            xb = _unpack_bf16_pairs(jnp.where(rows < n_left, words, 0)).astype(BF16)
            ab = jnp.dot(xb, wgu_s[...], preferred_element_type=F32)
            a = ab[:, 0:de]
            b = ab[:, de:2 * de]
            hmid = (a * _sigmoid(a)) * b
            y = jnp.dot(hmid.astype(BF16), wd_s[...], preferred_element_type=F32)
            ybuf[slot, 0:n_rows, :] = _pack_bf16_pairs(y)

        step = blk // EXPERT_BLOCK_PATHS
        for p in range(EXPERT_BLOCK_PATHS):
            lo_rows, hi_rows = p * step, (p + 1) * step
            last = p == EXPERT_BLOCK_PATHS - 1

            @pl.when((n_left > lo_rows) & ((n_left <= hi_rows) | last))
            def _(hi_rows=hi_rows):
                ffn(hi_rows)

        out_copy(g, slot).start(priority=1)
        return carry

    lax.fori_loop(0, n_blk, block, 0)

    @pl.when(e == n_exp - 1)
    def _():
        @pl.when(total >= 2)
        def _():
            out_copy(total - 2, lax.rem(total, 2)).wait()

        out_copy(total - 1, lax.rem(total + 1, 2)).wait()


def _experts(plan, xs, w_gate, w_up, w_down):
    n_slots, width = xs.shape
    n_exp, dm, de = w_gate.shape
    blk = EXPERT_BLOCK
    return pl.pallas_call(
        functools.partial(_experts_kernel, de=de, blk=blk),
        grid_spec=pltpu.PrefetchScalarGridSpec(
            num_scalar_prefetch=3,
            grid=(n_exp,),
            in_specs=[pl.BlockSpec(memory_space=pl.ANY),
                      pl.BlockSpec((1, dm, de), lambda e, f, n, c: (e, 0, 0)),
                      pl.BlockSpec((1, dm, de), lambda e, f, n, c: (e, 0, 0)),
                      pl.BlockSpec((1, de, dm), lambda e, f, n, c: (e, 0, 0))],
            out_specs=pl.BlockSpec(memory_space=pl.ANY),
            scratch_shapes=[pltpu.VMEM((dm, 2 * de), BF16), pltpu.VMEM((de, dm), BF16),
                            pltpu.VMEM((2, blk, width), I32), pltpu.VMEM((2, blk, width), I32),
                            pltpu.SemaphoreType.DMA((2,)), pltpu.SemaphoreType.DMA((2,))]),
        out_shape=jax.ShapeDtypeStruct((n_slots, width), I32),
        compiler_params=pltpu.CompilerParams(dimension_semantics=("arbitrary",),
                                             vmem_limit_bytes=VMEM_LIMIT),
        name="experts",
    )(*plan, xs, w_gate, w_up, w_down)


def _pick(onehot, table):
    return jnp.sum(jnp.where(onehot, table[None, :], 0), axis=1).astype(I32)


def _expert_plan(counts, n_assign):
    blk = EXPERT_BLOCK
    per_e = (counts + blk - 1) // blk
    first_blk = jnp.cumsum(per_e) - per_e
    n_slots = (n_assign // blk + N_EXPERTS) * blk
    return (first_blk * blk).astype(I32), (first_blk.astype(I32), per_e.astype(I32), counts), n_slots


def _slot_of(starts, expert, rank):
    onehot = jnp.arange(N_EXPERTS, dtype=I32)[None, :] == expert[:, None]
    return _pick(onehot, starts) + rank


def _combine_kernel(rows_ref, gate_ref, x1_ref, mod_ref, fg_ref, *rest):
    o_ref = rest[-1]
    tc = x1_ref.shape[0]
    gate_f = mod_ref[0][5:6]
    for q in range(tc // LANES):
        tok = slice(q * LANES, (q + 1) * LANES)
        g_rows = jnp.concatenate([gate_ref[:, tok], jnp.zeros((LANES - SUBLANES, LANES), F32)], axis=0)
        gc = g_rows.T
        y = (gc[:, 0:1] * _unpack_bf16_pairs(rows_ref[0, tok, :])
             + gc[:, 1:2] * _unpack_bf16_pairs(rows_ref[1, tok, :]))
        x2 = x1_ref[tok, :] + gate_f * y
        r = lax.rsqrt(jnp.mean(x2 * x2, axis=-1, keepdims=True) + EPS)
        o_ref[tok, :] = (x2 * r) * fg_ref[...]


def _combine(rows, gcol, x1, mod, final_g, seq, lb0, b0, nb, bsz, out_prev):
    dm = x1.shape[1]
    width = rows.shape[2]
    tc = min(COMBINE_TILE, seq)
    nst = seq // tc
    in_specs = [pl.BlockSpec((TOP_K, tc, width), lambda b, s: (0, b * nst + s, 0)),
                pl.BlockSpec((SUBLANES, tc), lambda b, s: (0, (b + lb0) * nst + s)),
                pl.BlockSpec((tc, dm), lambda b, s: ((b + lb0) * nst + s, 0)),
                pl.BlockSpec((1, 6, dm), lambda b, s: (b + b0, 0, 0)),
                pl.BlockSpec((1, dm), lambda b, s: (0, 0))]
    args = [rows, gcol, x1, mod, final_g.reshape(1, dm)]
    aliases = {}
    if out_prev is not None:
        in_specs.append(pl.BlockSpec(memory_space=pl.ANY))
        args.append(out_prev)
        aliases = {len(args) - 1: 0}
    return pl.pallas_call(
        _combine_kernel,
        grid=(nb, nst),
        in_specs=in_specs,
        out_specs=pl.BlockSpec((tc, dm), lambda b, s: ((b + b0) * nst + s, 0)),
        out_shape=jax.ShapeDtypeStruct((bsz * seq, dm), F32),
        input_output_aliases=aliases,
        compiler_params=pltpu.CompilerParams(dimension_semantics=("arbitrary", "arbitrary"),
                                             vmem_limit_bytes=VMEM_LIMIT),
        name="combine",
    )(*args)


def _layer(x, c, ada_w, ada_b, norm1_g, w_in, conv_w, conv_b, w_q, w_k, b_igate, b_fgate,
           mlstm_norm_g, mlstm_skip, w_pool, b_pool, pool_scale, w_out, norm2_g,
           w_rg, b_rg, w_re, b_re, w_eg, w_eu, w_ed, out_g):
    bsz, seq, dm = x.shape
    dml = conv_w.shape[1]
    n_tok = bsz * seq
    ts = min(SEQ_TILE, seq)

    mod = _ada(c, ada_w, ada_b).reshape(bsz, 6, dm)

    col_v = dml
    col_o = 2 * dml
    col_i = 3 * dml
    col_p = col_i + 2 * N_HEADS
    w_gate_cols = jnp.pad(w_in[:, col_i:col_p], ((0, 0), (0, LANES - 2 * N_HEADS)))
    w_in_r = jnp.concatenate([w_in[:, :col_v], w_in[:, col_p:], w_in[:, col_v:col_o],
                              w_in[:, col_o:col_i], w_gate_cols], axis=1).astype(BF16)
    gbias = jnp.pad(jnp.concatenate([b_igate, b_fgate]), (0, LANES - 2 * N_HEADS)).reshape(1, LANES)
    wqk = jnp.concatenate([w_q, w_k], axis=-1).astype(BF16)
    wrt = jnp.zeros((LOGIT_ROWS, dm), F32)
    wrt = wrt.at[0:N_GROUPS].set(w_rg.T).at[SUBLANES:SUBLANES + N_EXPERTS].set(w_re.T).astype(BF16)
    rb = jnp.zeros((LOGIT_ROWS,), F32).at[0:N_GROUPS].set(b_rg).at[SUBLANES:SUBLANES + N_EXPERTS].set(b_re)
    rbias = jnp.broadcast_to(rb[:, None], (LOGIT_ROWS, ts))

    gdim = (dm - dml) // len(POOL_WINDOWS)
    win = jnp.repeat(jnp.array(POOL_WINDOWS, F32), gdim)[None, :]
    t1 = jnp.arange(1, ts + 1, dtype=F32)[:, None]
    pinv = jnp.stack([1.0 / jnp.minimum(t1, win), jnp.broadcast_to(1.0 / win, (ts, dm - dml))])

    mod = mod.at[:, 1].set(norm1_g * (1.0 + mod[:, 1])).at[:, 4].set(norm2_g * (1.0 + mod[:, 4]))

    mixer_params = (w_in_r, gbias, conv_w, conv_b.reshape(1, dml), wqk,
                    mlstm_norm_g.reshape(1, dml), mlstm_skip.reshape(1, dml), w_pool.astype(BF16),
                    b_pool.reshape(1, dm - dml), pool_scale.reshape(1, dm - dml), pinv,
                    w_out.astype(BF16), wrt, rbias)

    x1, h2, lgt = _mixer(x, mod, *mixer_params, 0, bsz)
    idx, gcol, cnt = _route(lgt, n_tok)
    counts = cnt[:, 0].astype(I32)
    starts, plan, n_slots = _expert_plan(counts, n_tok * TOP_K)
    dest0 = _slot_of(starts, idx[0], idx[2])
    dest1 = _slot_of(starts, idx[1], idx[3])
    xs = _dispatch(h2, dest0, dest1, n_slots)
    ys = _experts(plan, xs, w_eg, w_eu, w_ed)

    nb = bsz // COMBINE_GROUPS
    assert bsz % COMBINE_GROUPS == 0
    out = None
    for b0 in range(0, bsz, nb):
        rows = _collect(ys, dest0, dest1, b0 * seq, nb * seq)
        out = _combine(rows, gcol, x1, mod, out_g, seq, b0, b0, nb, bsz, out)
    return out.reshape(bsz, seq, dm)


def kernel(x, c, ada_w, ada_b, norm1_g, w_in, conv_w, conv_b, w_q, w_k, b_igate, b_fgate, mlstm_norm_g, mlstm_skip, w_pool, b_pool, pool_scale, w_out, norm2_g, w_router_group, b_router_group, w_router_expert, b_router_expert, w_expert_gate, w_expert_up, w_expert_down, final_g):
    depth = ada_w.shape[0]
    assert depth == 1, "the final norm is fused into the last layer's combine kernel"
    l = 0
    return _layer(x, c, ada_w[l], ada_b[l], norm1_g[l], w_in[l], conv_w[l], conv_b[l], w_q[l],
                  w_k[l], b_igate[l], b_fgate[l], mlstm_norm_g[l], mlstm_skip[l], w_pool[l],
                  b_pool[l], pool_scale[l], w_out[l], norm2_g[l], w_router_group[l],
                  b_router_group[l], w_router_expert[l], b_router_expert[l],
                  w_expert_gate[l], w_expert_up[l], w_expert_down[l], final_g)
```

```python
import functools

import jax
import jax.numpy as jnp
from jax import lax
from jax.experimental import pallas as pl
from jax.experimental.pallas import tpu as pltpu
from jax.experimental.pallas import tpu_sc as plsc

F32 = jnp.float32
BF16 = jnp.bfloat16
I32 = jnp.int32
U32 = jnp.uint32

EPS = 1e-6
N_HEADS = 4
HEAD_DIM = 128
CONV_WIDTH = 4
POOL_WINDOWS = (2, 4, 8, 16)
N_GROUPS = 4
EXPERTS_PER_GROUP = 8
N_EXPERTS = N_GROUPS * EXPERTS_PER_GROUP
TOP_K = 2

LANES = 128
SUBLANES = 8
CHUNK = 128
SEQ_TILE = 1024
ADA_TILE = 1024
PREP_COLS = 256
COMBINE_GROUPS = 8
ROUTE_TILE = 512
ROUTE_SUBTILES = 4
SC_SCATTER_CHUNK = 64
SC_GATHER_CHUNK = 32
COMBINE_TILE = 512
EXPERT_BLOCK = 1024
EXPERT_BLOCK_PATHS = 8
LOGIT_ROWS = 48
UHIST = 8
PHIST = 16
VMEM_LIMIT = 60 * 1024 * 1024


def _sigmoid(x):
    return 1.0 / (1.0 + jnp.exp(-x))


def _pack_bf16_pairs(x):
    w = x.shape[1] // 2
    half_ulp = jnp.uint32(0x8000)
    hi = lax.bitcast_convert_type(x[:, :w], U32) + half_ulp
    lo = lax.bitcast_convert_type(x[:, w:], U32) + half_ulp
    return lax.bitcast_convert_type((hi & jnp.uint32(0xFFFF0000)) | (lo >> 16), I32)


def _unpack_bf16_pairs(words):
    u = lax.bitcast_convert_type(words, U32)
    hi = lax.bitcast_convert_type(u & jnp.uint32(0xFFFF0000), F32)
    lo = lax.bitcast_convert_type(u << 16, F32)
    return jnp.concatenate([hi, lo], axis=1)


def _ada_kernel(c_ref, w_ref, b_ref, o_ref):
    c = c_ref[...]
    s = c * _sigmoid(c)
    o_ref[...] = jnp.dot(s.astype(BF16), w_ref[...].astype(BF16),
                         preferred_element_type=F32) + b_ref[...]


def _ada(c, ada_w, ada_b):
    bsz, dm = c.shape
    n = ada_w.shape[1]
    tn = ADA_TILE
    return pl.pallas_call(
        _ada_kernel,
        grid=(n // tn,),
        in_specs=[pl.BlockSpec((bsz, dm), lambda j: (0, 0)),
                  pl.BlockSpec((dm, tn), lambda j: (0, j)),
                  pl.BlockSpec((1, tn), lambda j: (0, j))],
        out_specs=pl.BlockSpec((bsz, tn), lambda j: (0, j)),
        out_shape=jax.ShapeDtypeStruct((bsz, n), F32),
        compiler_params=pltpu.CompilerParams(dimension_semantics=("arbitrary",),
                                             vmem_limit_bytes=VMEM_LIMIT),
        name="ada",
    )(c, ada_w, ada_b.reshape(1, n))


def _prep_kernel(wint_ref, wout_ref, winr_ref, woutb_ref, *, dml, dp):
    col_v, col_o, col_i = dml, 2 * dml, 3 * dml
    col_p = col_i + 2 * N_HEADS
    kb = wint_ref.shape[1]
    winr_ref[:, 0:dml] = wint_ref[0:col_v, :].T.astype(BF16)
    winr_ref[:, dml:dml + dp] = wint_ref[col_p:col_p + dp, :].T.astype(BF16)
    winr_ref[:, dml + dp:2 * dml + dp] = wint_ref[col_v:col_o, :].T.astype(BF16)
    winr_ref[:, 2 * dml + dp:3 * dml + dp] = wint_ref[col_o:col_i, :].T.astype(BF16)
    gates = jnp.concatenate([wint_ref[col_i:col_p, :], jnp.zeros((LANES - 2 * N_HEADS, kb), F32)], axis=0)
    winr_ref[:, 3 * dml + dp:3 * dml + dp + LANES] = gates.T.astype(BF16)
    woutb_ref[...] = wout_ref[...].astype(BF16)


def _prep_weights(w_in, w_out, dml):
    dm, ncol = w_in.shape
    dp = ncol - 3 * dml - 2 * N_HEADS
    nout = 3 * dml + dp + LANES
    kb = PREP_COLS
    assert dm % kb == 0 and w_out.shape[1] % kb == 0 and 2 * N_HEADS == SUBLANES
    return pl.pallas_call(
        functools.partial(_prep_kernel, dml=dml, dp=dp),
        grid=(dm // kb,),
        in_specs=[pl.BlockSpec((ncol, kb), lambda i: (0, i)),
                  pl.BlockSpec((w_out.shape[0], kb), lambda i: (0, i))],
        out_specs=[pl.BlockSpec((kb, nout), lambda i: (i, 0)),
                   pl.BlockSpec((w_out.shape[0], kb), lambda i: (0, i))],
        out_shape=[jax.ShapeDtypeStruct((dm, nout), BF16),
                   jax.ShapeDtypeStruct(w_out.shape, BF16)],
        compiler_params=pltpu.CompilerParams(dimension_semantics=("arbitrary",),
                                             vmem_limit_bytes=VMEM_LIMIT),
        name="prep",
    )(w_in.T, w_out)


def _split3(x):
    hi = x.astype(BF16)
    r1 = x - hi.astype(F32)
    mid = r1.astype(BF16)
    lo = (r1 - mid.astype(F32)).astype(BF16)
    return hi, mid, lo


def _mixer_kernel(x_ref, mod_ref, win_ref, gbias_ref, convw_ref, convb_ref, wqk_ref,
                  ng_ref, skip_ref, wpool_ref, bpool_ref, pscale_ref, pinv_ref, wout_ref,
                  wrt_ref, rbias_ref,
                  x1_ref, h2_ref, lgt_ref,
                  uext_s, pext_s, proj_s, uc_s, qk_s, mix_s, pool4_s, ctv_s, ctn_s, mprev_s,
                  *, ts, dm, dml):
    s_idx = pl.program_id(1)
    n_chunks = ts // CHUNK
    dp = dm - dml

    @pl.when(s_idx == 0)
    def _():
        uext_s[0:UHIST, :] = jnp.zeros((UHIST, dml), F32)
        pext_s[0:PHIST, :] = jnp.zeros((PHIST, dp), F32)
        ctv_s[...] = jnp.zeros_like(ctv_s)
        ctn_s[...] = jnp.zeros_like(ctn_s)
        mprev_s[...] = jnp.zeros_like(mprev_s)

    row_i = lax.broadcasted_iota(I32, (CHUNK, CHUNK), 0)
    col_i = lax.broadcasted_iota(I32, (CHUNK, CHUNK), 1)
    causal = row_i >= col_i
    triu = jnp.where(row_i <= col_i, 1.0, 0.0).astype(BF16)
    ones_blk = jnp.ones((CHUNK, HEAD_DIM), BF16)
    q_scale = HEAD_DIM ** -0.5

    col_o = dml
    col_g = 2 * dml

    x = x_ref[0]
    mod = mod_ref[0]
    r = lax.rsqrt(jnp.mean(x * x, axis=-1, keepdims=True) + EPS)
    h = (x * r) * mod[1:2] + mod[0:1]
    res = jnp.dot(h.astype(BF16), win_ref[...], preferred_element_type=F32)
    uext_s[UHIST:, :] = res[:, 0:dml]
    pext_s[PHIST:, :] = res[:, dml:dm]
    proj_s[...] = res[:, dm:]

    acc = None
    for j in reversed(range(CONV_WIDTH)):
        tap = uext_s[pl.ds(UHIST - (CONV_WIDTH - 1 - j), ts), :] * convw_ref[j:j + 1, :]
        acc = tap if acc is None else acc + tap
    conv = acc + convb_ref[...]
    uc = conv * _sigmoid(conv)
    uc_s[...] = uc

    for hd in range(N_HEADS):
        qk = jnp.dot(uc[:, HEAD_DIM * hd:HEAD_DIM * (hd + 1)].astype(BF16), wqk_ref[hd],
                     preferred_element_type=F32)
        c0 = 2 * HEAD_DIM * hd
        qk_s[:, c0:c0 + HEAD_DIM] = qk[:, 0:HEAD_DIM] * q_scale
        qk_s[:, c0 + HEAD_DIM:c0 + 2 * HEAD_DIM] = qk[:, HEAD_DIM:2 * HEAD_DIM]

    pairs = [(c, hd) for c in range(n_chunks) for hd in range(N_HEADS)]
    g_rows, b_rows, b_cols = [], [], []
    pad_rows = jnp.zeros((CHUNK - SUBLANES, CHUNK), F32)
    for c in range(n_chunks):
        g_tile = proj_s[pl.ds(c * CHUNK, CHUNK), col_g:col_g + LANES] + gbias_ref[...]
        g_row = g_tile.T[0:SUBLANES, :]
        logf = -(jnp.maximum(-g_row, 0.0) + jnp.log1p(jnp.exp(-jnp.abs(g_row))))
        hi, mid, lo = _split3(logf)
        cs = jnp.dot(jnp.concatenate([hi, mid, lo], axis=0), triu, preferred_element_type=F32)
        b_row = cs[0:SUBLANES] + cs[SUBLANES:2 * SUBLANES] + cs[2 * SUBLANES:3 * SUBLANES]
        g_rows.append(g_row)
        b_rows.append(b_row)
        b_cols.append(jnp.concatenate([b_row, pad_rows], axis=0).T)

    def rows(ref, c, lo_col, width=HEAD_DIM):
        return ref[pl.ds(c * CHUNK, CHUNK), lo_col:lo_col + width]

    b_bcs, dlogs, rmaxs, p_mats = {}, {}, {}, {}
    for c, hd in pairs:
        b_bc = jnp.broadcast_to(b_cols[c][:, N_HEADS + hd:N_HEADS + hd + 1], (CHUNK, CHUNK))
        i_row = g_rows[c][hd:hd + 1, :]
        b_row = b_rows[c][N_HEADS + hd:N_HEADS + hd + 1, :]
        dlog = jnp.where(causal, (b_bc - b_row) + i_row, -jnp.inf)
        b_bcs[c, hd], dlogs[c, hd] = b_bc, dlog
        rmaxs[c, hd] = jnp.max(dlog, axis=-1, keepdims=True)
        q_c = rows(qk_s, c, 2 * HEAD_DIM * hd)
        k_c = rows(qk_s, c, 2 * HEAD_DIM * hd + HEAD_DIM)
        p_mats[c, hd] = lax.dot_general(q_c.astype(BF16), k_c.astype(BF16),
                                        (((1,), (1,)), ((), ())), preferred_element_type=F32)

    inters, m_ts = {}, {}
    for hd in range(N_HEADS):
        m_prev = mprev_s[hd]
        for c in range(n_chunks):
            inter = b_bcs[c, hd] + m_prev
            m_t = jnp.maximum(inter, rmaxs[c, hd])
            inters[c, hd], m_ts[c, hd] = inter, m_t
            m_prev = jnp.broadcast_to(m_t[CHUNK - 1:CHUNK, :], (CHUNK, LANES))
        mprev_s[hd] = m_prev

    lhs, v_augs, upds, a_prevs, e_negms = {}, {}, {}, {}, {}
    for c, hd in pairs:
        m_t = m_ts[c, hd]
        wm = jnp.exp(dlogs[c, hd] - m_t)
        a_inter = jnp.exp(inters[c, hd] - m_t)
        e_negms[c, hd] = jnp.exp(-m_t)
        q_c = rows(qk_s, c, 2 * HEAD_DIM * hd)
        k_c = rows(qk_s, c, 2 * HEAD_DIM * hd + HEAD_DIM)
        v_c = rows(proj_s, c, HEAD_DIM * hd)
        s_mat = (p_mats[c, hd] * wm).astype(BF16)
        qa = (q_c * a_inter).astype(BF16)
        lhs[c, hd] = jnp.concatenate([s_mat, qa], axis=1)
        v_aug = jnp.concatenate([v_c.astype(BF16), ones_blk], axis=1)
        v_augs[c, hd] = v_aug
        ktw = (k_c.T * wm[CHUNK - 1:CHUNK, :]).astype(BF16)
        upds[c, hd] = jnp.dot(ktw, v_aug, preferred_element_type=F32)
        a_prevs[c, hd] = a_inter[CHUNK - 1:CHUNK, :]

    ct_in = {}
    for hd in range(N_HEADS):
        ctv, ctn = ctv_s[hd], ctn_s[hd]
        for c in range(n_chunks):
            ct_in[c, hd] = jnp.concatenate([ctv, ctn], axis=1).astype(BF16)
            ctv = a_prevs[c, hd] * ctv + upds[c, hd][:, 0:HEAD_DIM]
            ctn = a_prevs[c, hd] * ctn + upds[c, hd][:, HEAD_DIM:2 * HEAD_DIM]
        ctv_s[hd], ctn_s[hd] = ctv, ctn

    for c, hd in pairs:
        numden = jnp.dot(lhs[c, hd], jnp.concatenate([v_augs[c, hd], ct_in[c, hd]], axis=0),
                         preferred_element_type=F32)
        num = numden[:, 0:HEAD_DIM]
        den = numden[:, HEAD_DIM:2 * HEAD_DIM]
        hh = num / jnp.maximum(jnp.abs(den), e_negms[c, hd])
        ms = jnp.mean(hh * hh, axis=-1, keepdims=True)
        hn = hh * lax.rsqrt(ms + EPS) * ng_ref[:, HEAD_DIM * hd:HEAD_DIM * (hd + 1)]
        o_c = rows(proj_s, c, col_o + HEAD_DIM * hd)
        uc_c = rows(uc_s, c, HEAD_DIM * hd)
        out_c = _sigmoid(o_c) * (hn + skip_ref[:, HEAD_DIM * hd:HEAD_DIM * (hd + 1)] * uc_c)
        mix_s[pl.ds(c * CHUNK, CHUNK), HEAD_DIM * hd:HEAD_DIM * (hd + 1)] = out_c.astype(BF16)

    def pe(shift, rows, lanes):
        return pext_s[pl.ds(PHIST - shift, rows), lanes]

    gd = LANES
    sums = []
    for gi in range(2):
        lanes = slice(gd * gi, gd * (gi + 1))
        tot = pe(0, ts, lanes)
        for j in range(1, POOL_WINDOWS[gi]):
            tot = tot + pe(j, ts, lanes)
        sums.append(tot)
    wide = slice(2 * gd, 4 * gd)
    s4 = pe(12, ts + 12, wide)
    for j in range(1, 4):
        s4 = s4 + pe(12 + j, ts + 12, wide)
    pool4_s[0:ts + 12, :] = s4
    s8 = pool4_s[pl.ds(4, ts + 8), :] + pool4_s[pl.ds(0, ts + 8), :]
    sums.append(s8[8:, 0:gd])
    sums.append(s8[8:, gd:2 * gd] + s8[0:ts, gd:2 * gd])
    for gi in range(len(POOL_WINDOWS)):
        lanes = slice(gd * gi, gd * (gi + 1))
        pooled = sums[gi] * pinv_ref[0, :, lanes] - pe(0, ts, lanes)
        yp = jnp.dot(pooled.astype(BF16), wpool_ref[gi], preferred_element_type=F32)
        yp = (yp + bpool_ref[:, lanes]) * pscale_ref[:, lanes]
        mix_s[:, dml + gd * gi:dml + gd * (gi + 1)] = yp.astype(BF16)

    mix = jnp.dot(mix_s[...], wout_ref[...], preferred_element_type=F32)
    x1 = x + mod[2:3] * mix
    x1_ref[...] = x1
    r2 = lax.rsqrt(jnp.mean(x1 * x1, axis=-1, keepdims=True) + EPS)
    h2 = (x1 * r2) * mod[4:5] + mod[3:4]
    h2_ref[...] = _pack_bf16_pairs(h2)
    lgt_ref[...] = lax.dot_general(wrt_ref[...], h2.astype(BF16), (((1,), (1,)), ((), ())),
                                   preferred_element_type=F32) + rbias_ref[...]

    uext_s[0:UHIST, :] = uext_s[ts:ts + UHIST, :]
    pext_s[0:PHIST, :] = pext_s[ts:ts + PHIST, :]


def _mixer(x, mod, w_in_r, gbias, conv_w, conv_b, wqk, ng, skip, wpool, bpool, pscale, pinv,
           w_out, wrt, rbias, b0, nb):
    _, seq, dm = x.shape
    dml = conv_w.shape[1]
    ts = min(SEQ_TILE, seq)
    ncols = w_in_r.shape[1]
    nst = seq // ts
    n_tok = nb * seq
    assert seq % ts == 0 and ts % CHUNK == 0
    full = lambda a: pl.BlockSpec(a.shape, lambda b, s: (0,) * a.ndim, pipeline_mode=pl.Buffered(1))
    kern = functools.partial(_mixer_kernel, ts=ts, dm=dm, dml=dml)
    return pl.pallas_call(
        kern,
        grid=(nb, nst),
        in_specs=[pl.BlockSpec((1, ts, dm), lambda b, s: (b + b0, s, 0)),
                  pl.BlockSpec((1, 6, dm), lambda b, s: (b + b0, 0, 0)),
                  full(w_in_r), full(gbias), full(conv_w), full(conv_b), full(wqk),
                  full(ng), full(skip), full(wpool), full(bpool), full(pscale),
                  pl.BlockSpec((1,) + pinv.shape[1:], lambda b, s: (jnp.minimum(s, 1), 0, 0)),
                  full(w_out), full(wrt), full(rbias)],
        out_specs=[pl.BlockSpec((ts, dm), lambda b, s: (b * nst + s, 0)),
                   pl.BlockSpec((ts, dm // 2), lambda b, s: (b * nst + s, 0)),
                   pl.BlockSpec((LOGIT_ROWS, ts), lambda b, s: (0, b * nst + s))],
        out_shape=[jax.ShapeDtypeStruct((n_tok, dm), F32),
                   jax.ShapeDtypeStruct((n_tok, dm // 2), I32),
                   jax.ShapeDtypeStruct((LOGIT_ROWS, n_tok), F32)],
        scratch_shapes=[pltpu.VMEM((UHIST + ts, dml), F32),
                        pltpu.VMEM((PHIST + ts, dm - dml), F32),
                        pltpu.VMEM((ts, ncols - dm), F32),
                        pltpu.VMEM((ts, dml), F32),
                        pltpu.VMEM((ts, 2 * dml), F32),
                        pltpu.VMEM((ts, dm), BF16),
                        pltpu.VMEM((ts + PHIST, 2 * LANES), F32),
                        pltpu.VMEM((N_HEADS, HEAD_DIM, HEAD_DIM), F32),
                        pltpu.VMEM((N_HEADS, HEAD_DIM, HEAD_DIM), F32),
                        pltpu.VMEM((N_HEADS, CHUNK, LANES), F32)],
        compiler_params=pltpu.CompilerParams(dimension_semantics=("arbitrary", "arbitrary"),
                                             vmem_limit_bytes=VMEM_LIMIT),
        name="mixer",
    )(x, mod, w_in_r, gbias, conv_w, conv_b, wqk, ng, skip, wpool, bpool, pscale, pinv, w_out,
      wrt, rbias)


def _route_kernel(lgt_ref, idx_ref, gate_ref, cnt_ref, carry_s, *, tr, n_sub):
    @pl.when(pl.program_id(0) == 0)
    def _():
        carry_s[...] = jnp.zeros_like(carry_s)

    tr_r = lax.broadcasted_iota(I32, (tr, tr), 0)
    tr_c = lax.broadcasted_iota(I32, (tr, tr), 1)
    upper = jnp.where(tr_r < tr_c, 1.0, 0.0).astype(BF16)
    for q in range(n_sub):
        cols = slice(q * tr, (q + 1) * tr)
        idx, gates = _route_tile(lgt_ref[:, cols], upper, carry_s, tr)
        idx_ref[:, cols] = idx
        gate_ref[:, cols] = gates
    cnt_ref[...] = carry_s[...]


def _route_tile(lg, upper, carry_s, tr):
    best = lg[0:1]
    gidx = jnp.zeros((1, tr), I32)
    for j in range(1, N_GROUPS):
        cand = lg[j:j + 1]
        better = cand > best
        gidx = jnp.where(better, j, gidx)
        best = jnp.where(better, cand, best)
    sumexp = jnp.zeros((1, tr), F32)
    for j in range(N_GROUPS):
        sumexp = sumexp + jnp.exp(lg[j:j + 1] - best)
    g_gate = 1.0 / sumexp

    sel = lg[SUBLANES:2 * SUBLANES]
    for j in range(1, N_GROUPS):
        sel = jnp.where(gidx == j, lg[SUBLANES * (j + 1):SUBLANES * (j + 2)], sel)
    sub = lax.broadcasted_iota(I32, (EXPERTS_PER_GROUP, tr), 0)
    v1 = jnp.max(sel, axis=0, keepdims=True)
    i1 = jnp.min(jnp.where(sel == v1, sub, EXPERTS_PER_GROUP), axis=0, keepdims=True)
    sel2 = jnp.where(sub == i1, -jnp.inf, sel)
    v2 = jnp.max(sel2, axis=0, keepdims=True)
    i2 = jnp.min(jnp.where(sel2 == v2, sub, EXPERTS_PER_GROUP), axis=0, keepdims=True)
    e2 = jnp.exp(v2 - v1)
    den = 1.0 + e2
    gate0 = (1.0 / den) * g_gate
    gate1 = (e2 / den) * g_gate
    ex0 = gidx * EXPERTS_PER_GROUP + i1
    ex1 = gidx * EXPERTS_PER_GROUP + i2

    erow = lax.broadcasted_iota(I32, (N_EXPERTS, tr), 0)
    oh0 = erow == ex0
    oh1 = erow == ex1
    oh = jnp.where(oh0 | oh1, 1.0, 0.0).astype(BF16)
    carry = carry_s[...]
    before = jnp.dot(oh, upper, preferred_element_type=F32)
    before = before + jnp.concatenate([carry] * (tr // LANES), axis=1)
    rank0 = jnp.sum(jnp.where(oh0, before, 0.0), axis=0, keepdims=True)
    rank1 = jnp.sum(jnp.where(oh1, before, 0.0), axis=0, keepdims=True)
    carry_s[...] = carry + jnp.dot(oh, jnp.ones((tr, LANES), BF16), preferred_element_type=F32)

    idx = jnp.concatenate([ex0, ex1, rank0.astype(I32), rank1.astype(I32),
                           jnp.zeros((SUBLANES - 4, tr), I32)], axis=0)
    gates = jnp.concatenate([gate0, gate1, jnp.zeros((SUBLANES - 2, tr), F32)], axis=0)
    return idx, gates


def _route(lgt, n_tok):
    tr = ROUTE_TILE
    n_sub = min(ROUTE_SUBTILES, n_tok // tr)
    step = tr * n_sub
    assert n_tok % step == 0
    return pl.pallas_call(
        functools.partial(_route_kernel, tr=tr, n_sub=n_sub),
        grid=(n_tok // step,),
        in_specs=[pl.BlockSpec((LOGIT_ROWS, step), lambda i: (0, i))],
        out_specs=[pl.BlockSpec((SUBLANES, step), lambda i: (0, i)),
                   pl.BlockSpec((SUBLANES, step), lambda i: (0, i)),
                   pl.BlockSpec((N_EXPERTS, LANES), lambda i: (0, 0))],
        out_shape=[jax.ShapeDtypeStruct((SUBLANES, n_tok), I32),
                   jax.ShapeDtypeStruct((SUBLANES, n_tok), F32),
                   jax.ShapeDtypeStruct((N_EXPERTS, LANES), F32)],
        scratch_shapes=[pltpu.VMEM((N_EXPERTS, LANES), F32)],
        compiler_params=pltpu.CompilerParams(dimension_semantics=("arbitrary",),
                                             vmem_limit_bytes=VMEM_LIMIT),
        name="route",
    )(lgt)


def _sc_workers():
    info = plsc.get_sparse_core_info()
    return info.num_cores, info.num_cores * info.num_subcores


def _dispatch(h2p, dest0, dest1, n_slots):
    n_tok, width = h2p.shape
    n_cores, n_workers = _sc_workers()
    per_w = n_tok // n_workers
    ch = min(SC_SCATTER_CHUNK, per_w)
    n_ch = per_w // ch
    assert n_tok % n_workers == 0 and per_w % ch == 0 and ch % SUBLANES == 0 and n_ch % 2 == 0
    mesh = plsc.VectorSubcoreMesh(core_axis_name="c", subcore_axis_name="s")

    @functools.partial(
        pl.kernel, mesh=mesh,
        out_type=jax.ShapeDtypeStruct((n_slots, width), h2p.dtype),
        scratch_types=[pltpu.VMEM((2, ch), I32), pltpu.VMEM((2, ch), I32),
                       pltpu.VMEM((2, ch, width), h2p.dtype),
                       pltpu.SemaphoreType.DMA((2,)), pltpu.SemaphoreType.DMA((2,))],
        name="dispatch")
    def scatter(h_hbm, d0_hbm, d1_hbm, xs_hbm, i0_v, i1_v, rows_v, sem_in, sem_out):
        wid = lax.axis_index("s") * n_cores + lax.axis_index("c")
        base = wid * per_w

        def loads(t0, slot):
            return (pltpu.make_async_copy(d0_hbm.at[pl.ds(t0, ch)], i0_v.at[slot], sem_in.at[slot]),
                    pltpu.make_async_copy(d1_hbm.at[pl.ds(t0, ch)], i1_v.at[slot], sem_in.at[slot]),
                    pltpu.make_async_copy(h_hbm.at[pl.ds(t0, ch)], rows_v.at[slot], sem_in.at[slot]))

        def scatters(slot):
            return (pltpu.make_async_copy(rows_v.at[slot], xs_hbm.at[i0_v.at[slot]], sem_out.at[slot]),
                    pltpu.make_async_copy(rows_v.at[slot], xs_hbm.at[i1_v.at[slot]], sem_out.at[slot]))

        for cp in loads(base, 0):
            cp.start()

        @pl.loop(0, n_ch, step=2)
        def _(k):
            for slot in range(2):
                for cp in loads(base + (k + slot) * ch, slot):
                    cp.wait()
                out_cps = scatters(slot)
                for cp in out_cps:
                    cp.start()
                nxt = k + slot + 1

                @pl.when(nxt < n_ch)
                def _():
                    for cp in loads(base + nxt * ch, 1 - slot):
                        cp.start()

                for cp in out_cps:
                    cp.wait()

    return scatter(h2p, dest0, dest1)


def _collect(ys, dest0, dest1, tok0, n_tok):
    width = ys.shape[1]
    n_cores, n_workers = _sc_workers()
    per_w = n_tok // n_workers
    ch = min(SC_GATHER_CHUNK, per_w)
    n_ch = per_w // ch
    assert n_tok % n_workers == 0 and per_w % ch == 0 and ch % SUBLANES == 0 and n_ch % 2 == 0
    mesh = plsc.VectorSubcoreMesh(core_axis_name="c", subcore_axis_name="s")

    @functools.partial(
        pl.kernel, mesh=mesh,
        out_type=jax.ShapeDtypeStruct((TOP_K, n_tok, width), ys.dtype),
        scratch_types=[pltpu.VMEM((2, ch), I32), pltpu.VMEM((2, ch), I32),
                       pltpu.VMEM((2, ch, width), ys.dtype), pltpu.VMEM((2, ch, width), ys.dtype),
                       pltpu.SemaphoreType.DMA((2,)), pltpu.SemaphoreType.DMA((2,)),
                       pltpu.SemaphoreType.DMA((2,))],
        name="collect")
    def gather(ys_hbm, d0_hbm, d1_hbm, o_hbm, i0_v, i1_v, r0_v, r1_v, sem_idx, sem_in, sem_out):
        wid = lax.axis_index("s") * n_cores + lax.axis_index("c")
        base = wid * per_w

        def idx_loads(t0, slot):
            return (pltpu.make_async_copy(d0_hbm.at[pl.ds(tok0 + t0, ch)], i0_v.at[slot], sem_idx.at[slot]),
                    pltpu.make_async_copy(d1_hbm.at[pl.ds(tok0 + t0, ch)], i1_v.at[slot], sem_idx.at[slot]))

        def gathers(slot):
            return (pltpu.make_async_copy(ys_hbm.at[i0_v.at[slot]], r0_v.at[slot], sem_in.at[slot]),
                    pltpu.make_async_copy(ys_hbm.at[i1_v.at[slot]], r1_v.at[slot], sem_in.at[slot]))

        def stores(t0, slot):
            return (pltpu.make_async_copy(r0_v.at[slot], o_hbm.at[0, pl.ds(t0, ch)], sem_out.at[slot]),
                    pltpu.make_async_copy(r1_v.at[slot], o_hbm.at[1, pl.ds(t0, ch)], sem_out.at[slot]))

        def start_chunk(t0, slot):
            for cp in idx_loads(t0, slot):
                cp.start()
            for cp in idx_loads(t0, slot):
                cp.wait()
            for cp in gathers(slot):
                cp.start()

        start_chunk(base, 0)

        @pl.loop(0, n_ch, step=2)
        def _(k):
            for slot in range(2):
                nxt = k + slot + 1

                @pl.when(nxt < n_ch)
                def _():
                    start_chunk(base + nxt * ch, 1 - slot)

                for cp in gathers(slot):
                    cp.wait()
                out_cps = stores(base + (k + slot) * ch, slot)
                for cp in out_cps:
                    cp.start()
                for cp in out_cps:
                    cp.wait()

    return gather(ys, dest0, dest1)


def _experts_kernel(first_ref, nblk_ref, cnt_ref, xs_hbm, wg_ref, wu_ref, wd_ref, ys_hbm,
                    wgu_s, wd_s, xbuf, ybuf, sem_in, sem_out, *, de, blk):
    e = pl.program_id(0)
    n_exp = pl.num_programs(0)
    wgu_s[:, 0:de] = wg_ref[0].astype(BF16)
    wgu_s[:, de:2 * de] = wu_ref[0].astype(BF16)
    wd_s[...] = wd_ref[0].astype(BF16)
    first = first_ref[e]
    n_blk = nblk_ref[e]
    count = cnt_ref[e]
    total = first_ref[n_exp - 1] + nblk_ref[n_exp - 1]

    def in_copy(g, slot):
        return pltpu.make_async_copy(xs_hbm.at[pl.ds(pl.multiple_of(g * blk, blk), blk)],
                                     xbuf.at[slot], sem_in.at[slot])

    def out_copy(g, slot):
        return pltpu.make_async_copy(ybuf.at[slot],
                                     ys_hbm.at[pl.ds(pl.multiple_of(g * blk, blk), blk)],
                                     sem_out.at[slot])

    @pl.when(e == 0)
    def _():
        in_copy(0, 0).start(priority=1)
        ybuf[...] = jnp.zeros_like(ybuf)

    def block(j, carry):
        g = first + j
        slot = lax.rem(g, 2)
        in_copy(g, slot).wait()

        @pl.when(g + 1 < total)
        def _():
            in_copy(g + 1, 1 - slot).start(priority=1)

        @pl.when(g >= 2)
        def _():
            out_copy(g - 2, slot).wait()

        n_left = count - j * blk

        def ffn(n_rows):
            words = xbuf[slot, 0:n_rows, :]
            rows = lax.broadcasted_iota(I32, words.shape, 0)
            xb = _unpack_bf16_pairs(jnp.where(rows < n_left, words, 0)).astype(BF16)
            ab = jnp.dot(xb, wgu_s[...], preferred_element_type=F32)
            a = ab[:, 0:de]
            b = ab[:, de:2 * de]
            hmid = (a * _sigmoid(a)) * b
            y = jnp.dot(hmid.astype(BF16), wd_s[...], preferred_element_type=F32)
            ybuf[slot, 0:n_rows, :] = _pack_bf16_pairs(y)

        step = blk // EXPERT_BLOCK_PATHS
        for p in range(EXPERT_BLOCK_PATHS):
            lo_rows, hi_rows = p * step, (p + 1) * step
            last = p == EXPERT_BLOCK_PATHS - 1

            @pl.when((n_left > lo_rows) & ((n_left <= hi_rows) | last))
            def _(hi_rows=hi_rows):
                ffn(hi_rows)

        out_copy(g, slot).start(priority=1)
        return carry

    lax.fori_loop(0, n_blk, block, 0)

    @pl.when(e == n_exp - 1)
    def _():
        @pl.when(total >= 2)
        def _():
            out_copy(total - 2, lax.rem(total, 2)).wait()

        out_copy(total - 1, lax.rem(total + 1, 2)).wait()


def _experts(plan, xs, w_gate, w_up, w_down):
    n_slots, width = xs.shape
    n_exp, dm, de = w_gate.shape
    blk = EXPERT_BLOCK
    return pl.pallas_call(
        functools.partial(_experts_kernel, de=de, blk=blk),
        grid_spec=pltpu.PrefetchScalarGridSpec(
            num_scalar_prefetch=3,
            grid=(n_exp,),
            in_specs=[pl.BlockSpec(memory_space=pl.ANY),
                      pl.BlockSpec((1, dm, de), lambda e, f, n, c: (e, 0, 0)),
                      pl.BlockSpec((1, dm, de), lambda e, f, n, c: (e, 0, 0)),
                      pl.BlockSpec((1, de, dm), lambda e, f, n, c: (e, 0, 0))],
            out_specs=pl.BlockSpec(memory_space=pl.ANY),
            scratch_shapes=[pltpu.VMEM((dm, 2 * de), BF16), pltpu.VMEM((de, dm), BF16),
                            pltpu.VMEM((2, blk, width), I32), pltpu.VMEM((2, blk, width), I32),
                            pltpu.SemaphoreType.DMA((2,)), pltpu.SemaphoreType.DMA((2,))]),
        out_shape=jax.ShapeDtypeStruct((n_slots, width), I32),
        compiler_params=pltpu.CompilerParams(dimension_semantics=("arbitrary",),
                                             vmem_limit_bytes=VMEM_LIMIT),
        name="experts",
    )(*plan, xs, w_gate, w_up, w_down)


def _pick(onehot, table):
    return jnp.sum(jnp.where(onehot, table[None, :], 0), axis=1).astype(I32)


def _expert_plan(counts, n_assign):
    blk = EXPERT_BLOCK
    per_e = (counts + blk - 1) // blk
    first_blk = jnp.cumsum(per_e) - per_e
    n_slots = (n_assign // blk + N_EXPERTS) * blk
    return (first_blk * blk).astype(I32), (first_blk.astype(I32), per_e.astype(I32), counts), n_slots


def _slot_of(starts, expert, rank):
    onehot = jnp.arange(N_EXPERTS, dtype=I32)[None, :] == expert[:, None]
    return _pick(onehot, starts) + rank


def _combine_kernel(rows_ref, gate_ref, x1_ref, mod_ref, fg_ref, *rest):
    o_ref = rest[-1]
    tc = x1_ref.shape[0]
    gate_f = mod_ref[0][5:6]
    for q in range(tc // LANES):
        tok = slice(q * LANES, (q + 1) * LANES)
        g_rows = jnp.concatenate([gate_ref[:, tok], jnp.zeros((LANES - SUBLANES, LANES), F32)], axis=0)
        gc = g_rows.T
        y = (gc[:, 0:1] * _unpack_bf16_pairs(rows_ref[0, tok, :])
             + gc[:, 1:2] * _unpack_bf16_pairs(rows_ref[1, tok, :]))
        x2 = x1_ref[tok, :] + gate_f * y
        r = lax.rsqrt(jnp.mean(x2 * x2, axis=-1, keepdims=True) + EPS)
        o_ref[tok, :] = (x2 * r) * fg_ref[...]


def _combine(rows, gcol, x1, mod, final_g, seq, lb0, b0, nb, bsz, out_prev):
    dm = x1.shape[1]
    width = rows.shape[2]
    tc = min(COMBINE_TILE, seq)
    nst = seq // tc
    in_specs = [pl.BlockSpec((TOP_K, tc, width), lambda b, s: (0, b * nst + s, 0)),
                pl.BlockSpec((SUBLANES, tc), lambda b, s: (0, (b + lb0) * nst + s)),
                pl.BlockSpec((tc, dm), lambda b, s: ((b + lb0) * nst + s, 0)),
                pl.BlockSpec((1, 6, dm), lambda b, s: (b + b0, 0, 0)),
                pl.BlockSpec((1, dm), lambda b, s: (0, 0))]
    args = [rows, gcol, x1, mod, final_g.reshape(1, dm)]
    aliases = {}
    if out_prev is not None:
        in_specs.append(pl.BlockSpec(memory_space=pl.ANY))
        args.append(out_prev)
        aliases = {len(args) - 1: 0}
    return pl.pallas_call(
        _combine_kernel,
        grid=(nb, nst),
        in_specs=in_specs,
        out_specs=pl.BlockSpec((tc, dm), lambda b, s: ((b + b0) * nst + s, 0)),
        out_shape=jax.ShapeDtypeStruct((bsz * seq, dm), F32),
        input_output_aliases=aliases,
        compiler_params=pltpu.CompilerParams(dimension_semantics=("arbitrary", "arbitrary"),
                                             vmem_limit_bytes=VMEM_LIMIT),
        name="combine",
    )(*args)


def _layer(x, c, ada_w, ada_b, norm1_g, w_in, conv_w, conv_b, w_q, w_k, b_igate, b_fgate,
           mlstm_norm_g, mlstm_skip, w_pool, b_pool, pool_scale, w_out, norm2_g,
           w_rg, b_rg, w_re, b_re, w_eg, w_eu, w_ed, out_g):
    bsz, seq, dm = x.shape
    dml = conv_w.shape[1]
    n_tok = bsz * seq
    ts = min(SEQ_TILE, seq)

    mod = _ada(c, ada_w, ada_b).reshape(bsz, 6, dm)

    w_in_r, w_out_b = _prep_weights(w_in, w_out, dml)
    gbias = jnp.pad(jnp.concatenate([b_igate, b_fgate]), (0, LANES - 2 * N_HEADS)).reshape(1, LANES)
    wqk = jnp.concatenate([w_q, w_k], axis=-1).astype(BF16)
    wrt = jnp.zeros((LOGIT_ROWS, dm), F32)
    wrt = wrt.at[0:N_GROUPS].set(w_rg.T).at[SUBLANES:SUBLANES + N_EXPERTS].set(w_re.T).astype(BF16)
    rb = jnp.zeros((LOGIT_ROWS,), F32).at[0:N_GROUPS].set(b_rg).at[SUBLANES:SUBLANES + N_EXPERTS].set(b_re)
    rbias = jnp.broadcast_to(rb[:, None], (LOGIT_ROWS, ts))

    gdim = (dm - dml) // len(POOL_WINDOWS)
    win = jnp.repeat(jnp.array(POOL_WINDOWS, F32), gdim)[None, :]
    t1 = jnp.arange(1, ts + 1, dtype=F32)[:, None]
    pinv = jnp.stack([1.0 / jnp.minimum(t1, win), jnp.broadcast_to(1.0 / win, (ts, dm - dml))])

    mod = mod.at[:, 1].set(norm1_g * (1.0 + mod[:, 1])).at[:, 4].set(norm2_g * (1.0 + mod[:, 4]))

    mixer_params = (w_in_r, gbias, conv_w, conv_b.reshape(1, dml), wqk,
                    mlstm_norm_g.reshape(1, dml), mlstm_skip.reshape(1, dml), w_pool.astype(BF16),
                    b_pool.reshape(1, dm - dml), pool_scale.reshape(1, dm - dml), pinv,
                    w_out_b, wrt, rbias)

    x1, h2, lgt = _mixer(x, mod, *mixer_params, 0, bsz)
    idx, gcol, cnt = _route(lgt, n_tok)
    counts = cnt[:, 0].astype(I32)
    starts, plan, n_slots = _expert_plan(counts, n_tok * TOP_K)
    dest0 = _slot_of(starts, idx[0], idx[2])
    dest1 = _slot_of(starts, idx[1], idx[3])
    xs = _dispatch(h2, dest0, dest1, n_slots)
    ys = _experts(plan, xs, w_eg, w_eu, w_ed)

    nb = bsz // COMBINE_GROUPS
    assert bsz % COMBINE_GROUPS == 0
    out = None
    for b0 in range(0, bsz, nb):
        rows = _collect(ys, dest0, dest1, b0 * seq, nb * seq)
        out = _combine(rows, gcol, x1, mod, out_g, seq, b0, b0, nb, bsz, out)
    return out.reshape(bsz, seq, dm)


def kernel(x, c, ada_w, ada_b, norm1_g, w_in, conv_w, conv_b, w_q, w_k, b_igate, b_fgate, mlstm_norm_g, mlstm_skip, w_pool, b_pool, pool_scale, w_out, norm2_g, w_router_group, b_router_group, w_router_expert, b_router_expert, w_expert_gate, w_expert_up, w_expert_down, final_g):
    depth = ada_w.shape[0]
    assert depth == 1, "the final norm is fused into the last layer's combine kernel"
    l = 0
    return _layer(x, c, ada_w[l], ada_b[l], norm1_g[l], w_in[l], conv_w[l], conv_b[l], w_q[l],
                  w_k[l], b_igate[l], b_fgate[l], mlstm_norm_g[l], mlstm_skip[l], w_pool[l],
                  b_pool[l], pool_scale[l], w_out[l], norm2_g[l], w_router_group[l],
                  b_router_group[l], w_router_expert[l], b_router_expert[l],
                  w_expert_gate[l], w_expert_up[l], w_expert_down[l], final_g)
```

```python
import functools

import jax
import jax.numpy as jnp
from jax import lax
from jax.experimental import pallas as pl
from jax.experimental.pallas import tpu as pltpu
from jax.experimental.pallas import tpu_sc as plsc

F32 = jnp.float32
BF16 = jnp.bfloat16
I32 = jnp.int32
U32 = jnp.uint32

EPS = 1e-6
N_HEADS = 4
HEAD_DIM = 128
CONV_WIDTH = 4
POOL_WINDOWS = (2, 4, 8, 16)
N_GROUPS = 4
EXPERTS_PER_GROUP = 8
N_EXPERTS = N_GROUPS * EXPERTS_PER_GROUP
TOP_K = 2

LANES = 128
SUBLANES = 8
CHUNK = 128
SEQ_TILE = 1024
ADA_TILE = 1024
PREP_COLS = 256
COMBINE_GROUPS = 8
ROUTE_TILE = 512
ROUTE_SUBTILES = 4
SC_SCATTER_CHUNK = 64
SC_GATHER_CHUNK = 32
COMBINE_TILE = 1024
EXPERT_BLOCK = 1024
EXPERT_BLOCK_PATHS = 8
LOGIT_ROWS = 48
UHIST = 8
PHIST = 16
VMEM_LIMIT = 60 * 1024 * 1024


def _sigmoid(x):
    return 1.0 / (1.0 + jnp.exp(-x))


def _pack_bf16_pairs(x):
    w = x.shape[1] // 2
    half_ulp = jnp.uint32(0x8000)
    hi = lax.bitcast_convert_type(x[:, :w], U32) + half_ulp
    lo = lax.bitcast_convert_type(x[:, w:], U32) + half_ulp
    return lax.bitcast_convert_type((hi & jnp.uint32(0xFFFF0000)) | (lo >> 16), I32)


def _unpack_bf16_pairs(words):
    u = lax.bitcast_convert_type(words, U32)
    hi = lax.bitcast_convert_type(u & jnp.uint32(0xFFFF0000), F32)
    lo = lax.bitcast_convert_type(u << 16, F32)
    return jnp.concatenate([hi, lo], axis=1)


def _ada_kernel(c_ref, w_ref, b_ref, o_ref):
    c = c_ref[...]
    s = c * _sigmoid(c)
    o_ref[...] = jnp.dot(s.astype(BF16), w_ref[...].astype(BF16),
                         preferred_element_type=F32) + b_ref[...]


def _ada(c, ada_w, ada_b):
    bsz, dm = c.shape
    n = ada_w.shape[1]
    tn = ADA_TILE
    return pl.pallas_call(
        _ada_kernel,
        grid=(n // tn,),
        in_specs=[pl.BlockSpec((bsz, dm), lambda j: (0, 0)),
                  pl.BlockSpec((dm, tn), lambda j: (0, j)),
                  pl.BlockSpec((1, tn), lambda j: (0, j))],
        out_specs=pl.BlockSpec((bsz, tn), lambda j: (0, j)),
        out_shape=jax.ShapeDtypeStruct((bsz, n), F32),
        compiler_params=pltpu.CompilerParams(dimension_semantics=("arbitrary",),
                                             vmem_limit_bytes=VMEM_LIMIT),
        name="ada",
    )(c, ada_w, ada_b.reshape(1, n))


def _prep_kernel(wint_ref, wout_ref, winr_ref, woutb_ref, *, dml, dp):
    col_v, col_o, col_i = dml, 2 * dml, 3 * dml
    col_p = col_i + 2 * N_HEADS
    kb = wint_ref.shape[1]
    winr_ref[:, 0:dml] = wint_ref[0:col_v, :].T.astype(BF16)
    winr_ref[:, dml:dml + dp] = wint_ref[col_p:col_p + dp, :].T.astype(BF16)
    winr_ref[:, dml + dp:2 * dml + dp] = wint_ref[col_v:col_o, :].T.astype(BF16)
    winr_ref[:, 2 * dml + dp:3 * dml + dp] = wint_ref[col_o:col_i, :].T.astype(BF16)
    gates = jnp.concatenate([wint_ref[col_i:col_p, :], jnp.zeros((LANES - 2 * N_HEADS, kb), F32)], axis=0)
    winr_ref[:, 3 * dml + dp:3 * dml + dp + LANES] = gates.T.astype(BF16)
    woutb_ref[...] = wout_ref[...].astype(BF16)


def _prep_weights(w_in, w_out, dml):
    dm, ncol = w_in.shape
    dp = ncol - 3 * dml - 2 * N_HEADS
    nout = 3 * dml + dp + LANES
    kb = PREP_COLS
    assert dm % kb == 0 and w_out.shape[1] % kb == 0 and 2 * N_HEADS == SUBLANES
    return pl.pallas_call(
        functools.partial(_prep_kernel, dml=dml, dp=dp),
        grid=(dm // kb,),
        in_specs=[pl.BlockSpec((ncol, kb), lambda i: (0, i)),
                  pl.BlockSpec((w_out.shape[0], kb), lambda i: (0, i))],
        out_specs=[pl.BlockSpec((kb, nout), lambda i: (i, 0)),
                   pl.BlockSpec((w_out.shape[0], kb), lambda i: (0, i))],
        out_shape=[jax.ShapeDtypeStruct((dm, nout), BF16),
                   jax.ShapeDtypeStruct(w_out.shape, BF16)],
        compiler_params=pltpu.CompilerParams(dimension_semantics=("arbitrary",),
                                             vmem_limit_bytes=VMEM_LIMIT),
        name="prep",
    )(w_in.T, w_out)


def _split3(x):
    hi = x.astype(BF16)
    r1 = x - hi.astype(F32)
    mid = r1.astype(BF16)
    lo = (r1 - mid.astype(F32)).astype(BF16)
    return hi, mid, lo


def _mixer_kernel(x_ref, mod_ref, win_ref, gbias_ref, convw_ref, convb_ref, wqk_ref,
                  ng_ref, skip_ref, wpool_ref, bpool_ref, pscale_ref, pinv_ref, wout_ref,
                  wrt_ref, rbias_ref,
                  x1_ref, h2_ref, lgt_ref,
                  uext_s, pext_s, proj_s, uc_s, qk_s, mix_s, pool4_s, ctv_s, ctn_s, mprev_s,
                  *, ts, dm, dml):
    s_idx = pl.program_id(1)
    n_chunks = ts // CHUNK
    dp = dm - dml

    @pl.when(s_idx == 0)
    def _():
        uext_s[0:UHIST, :] = jnp.zeros((UHIST, dml), F32)
        pext_s[0:PHIST, :] = jnp.zeros((PHIST, dp), F32)
        ctv_s[...] = jnp.zeros_like(ctv_s)
        ctn_s[...] = jnp.zeros_like(ctn_s)
        mprev_s[...] = jnp.zeros_like(mprev_s)

    row_i = lax.broadcasted_iota(I32, (CHUNK, CHUNK), 0)
    col_i = lax.broadcasted_iota(I32, (CHUNK, CHUNK), 1)
    causal = row_i >= col_i
    triu = jnp.where(row_i <= col_i, 1.0, 0.0).astype(BF16)
    ones_blk = jnp.ones((CHUNK, HEAD_DIM), BF16)
    q_scale = HEAD_DIM ** -0.5

    col_o = dml
    col_g = 2 * dml

    x = x_ref[0]
    mod = mod_ref[0]
    r = lax.rsqrt(jnp.mean(x * x, axis=-1, keepdims=True) + EPS)
    h = (x * r) * mod[1:2] + mod[0:1]
    res = jnp.dot(h.astype(BF16), win_ref[...], preferred_element_type=F32)
    uext_s[UHIST:, :] = res[:, 0:dml]
    pext_s[PHIST:, :] = res[:, dml:dm]
    proj_s[...] = res[:, dm:]

    acc = None
    for j in reversed(range(CONV_WIDTH)):
        tap = uext_s[pl.ds(UHIST - (CONV_WIDTH - 1 - j), ts), :] * convw_ref[j:j + 1, :]
        acc = tap if acc is None else acc + tap
    conv = acc + convb_ref[...]
    uc = conv * _sigmoid(conv)
    uc_s[...] = uc

    for hd in range(N_HEADS):
        qk = jnp.dot(uc[:, HEAD_DIM * hd:HEAD_DIM * (hd + 1)].astype(BF16), wqk_ref[hd],
                     preferred_element_type=F32)
        c0 = 2 * HEAD_DIM * hd
        qk_s[:, c0:c0 + HEAD_DIM] = qk[:, 0:HEAD_DIM] * q_scale
        qk_s[:, c0 + HEAD_DIM:c0 + 2 * HEAD_DIM] = qk[:, HEAD_DIM:2 * HEAD_DIM]

    pairs = [(c, hd) for c in range(n_chunks) for hd in range(N_HEADS)]
    g_rows, b_rows, b_cols = [], [], []
    pad_rows = jnp.zeros((CHUNK - SUBLANES, CHUNK), F32)
    for c in range(n_chunks):
        g_tile = proj_s[pl.ds(c * CHUNK, CHUNK), col_g:col_g + LANES] + gbias_ref[...]
        g_row = g_tile.T[0:SUBLANES, :]
        logf = -(jnp.maximum(-g_row, 0.0) + jnp.log1p(jnp.exp(-jnp.abs(g_row))))
        hi, mid, lo = _split3(logf)
        cs = jnp.dot(jnp.concatenate([hi, mid, lo], axis=0), triu, preferred_element_type=F32)
        b_row = cs[0:SUBLANES] + cs[SUBLANES:2 * SUBLANES] + cs[2 * SUBLANES:3 * SUBLANES]
        g_rows.append(g_row)
        b_rows.append(b_row)
        b_cols.append(jnp.concatenate([b_row, pad_rows], axis=0).T)

    def rows(ref, c, lo_col, width=HEAD_DIM):
        return ref[pl.ds(c * CHUNK, CHUNK), lo_col:lo_col + width]

    b_bcs, dlogs, rmaxs, p_mats = {}, {}, {}, {}
    for c, hd in pairs:
        b_bc = jnp.broadcast_to(b_cols[c][:, N_HEADS + hd:N_HEADS + hd + 1], (CHUNK, CHUNK))
        i_row = g_rows[c][hd:hd + 1, :]
        b_row = b_rows[c][N_HEADS + hd:N_HEADS + hd + 1, :]
        dlog = jnp.where(causal, (b_bc - b_row) + i_row, -jnp.inf)
        b_bcs[c, hd], dlogs[c, hd] = b_bc, dlog
        rmaxs[c, hd] = jnp.max(dlog, axis=-1, keepdims=True)
        q_c = rows(qk_s, c, 2 * HEAD_DIM * hd)
        k_c = rows(qk_s, c, 2 * HEAD_DIM * hd + HEAD_DIM)
        p_mats[c, hd] = lax.dot_general(q_c.astype(BF16), k_c.astype(BF16),
                                        (((1,), (1,)), ((), ())), preferred_element_type=F32)

    inters, m_ts = {}, {}
    for hd in range(N_HEADS):
        m_prev = mprev_s[hd]
        for c in range(n_chunks):
            inter = b_bcs[c, hd] + m_prev
            m_t = jnp.maximum(inter, rmaxs[c, hd])
            inters[c, hd], m_ts[c, hd] = inter, m_t
            m_prev = jnp.broadcast_to(m_t[CHUNK - 1:CHUNK, :], (CHUNK, LANES))
        mprev_s[hd] = m_prev

    lhs, v_augs, upds, a_prevs, e_negms = {}, {}, {}, {}, {}
    for c, hd in pairs:
        m_t = m_ts[c, hd]
        wm = jnp.exp(dlogs[c, hd] - m_t)
        a_inter = jnp.exp(inters[c, hd] - m_t)
        e_negms[c, hd] = jnp.exp(-m_t)
        q_c = rows(qk_s, c, 2 * HEAD_DIM * hd)
        k_c = rows(qk_s, c, 2 * HEAD_DIM * hd + HEAD_DIM)
        v_c = rows(proj_s, c, HEAD_DIM * hd)
        s_mat = (p_mats[c, hd] * wm).astype(BF16)
        qa = (q_c * a_inter).astype(BF16)
        lhs[c, hd] = jnp.concatenate([s_mat, qa], axis=1)
        v_aug = jnp.concatenate([v_c.astype(BF16), ones_blk], axis=1)
        v_augs[c, hd] = v_aug
        ktw = (k_c.T * wm[CHUNK - 1:CHUNK, :]).astype(BF16)
        upds[c, hd] = jnp.dot(ktw, v_aug, preferred_element_type=F32)
        a_prevs[c, hd] = a_inter[CHUNK - 1:CHUNK, :]

    ct_in = {}
    for hd in range(N_HEADS):
        ctv, ctn = ctv_s[hd], ctn_s[hd]
        for c in range(n_chunks):
            ct_in[c, hd] = jnp.concatenate([ctv, ctn], axis=1).astype(BF16)
            ctv = a_prevs[c, hd] * ctv + upds[c, hd][:, 0:HEAD_DIM]
            ctn = a_prevs[c, hd] * ctn + upds[c, hd][:, HEAD_DIM:2 * HEAD_DIM]
        ctv_s[hd], ctn_s[hd] = ctv, ctn

    for c, hd in pairs:
        numden = jnp.dot(lhs[c, hd], jnp.concatenate([v_augs[c, hd], ct_in[c, hd]], axis=0),
                         preferred_element_type=F32)
        num = numden[:, 0:HEAD_DIM]
        den = numden[:, HEAD_DIM:2 * HEAD_DIM]
        hh = num / jnp.maximum(jnp.abs(den), e_negms[c, hd])
        ms = jnp.mean(hh * hh, axis=-1, keepdims=True)
        hn = hh * lax.rsqrt(ms + EPS) * ng_ref[:, HEAD_DIM * hd:HEAD_DIM * (hd + 1)]
        o_c = rows(proj_s, c, col_o + HEAD_DIM * hd)
        uc_c = rows(uc_s, c, HEAD_DIM * hd)
        out_c = _sigmoid(o_c) * (hn + skip_ref[:, HEAD_DIM * hd:HEAD_DIM * (hd + 1)] * uc_c)
        mix_s[pl.ds(c * CHUNK, CHUNK), HEAD_DIM * hd:HEAD_DIM * (hd + 1)] = out_c.astype(BF16)

    def pe(shift, rows, lanes):
        return pext_s[pl.ds(PHIST - shift, rows), lanes]

    gd = LANES
    sums = []
    for gi in range(2):
        lanes = slice(gd * gi, gd * (gi + 1))
        tot = pe(0, ts, lanes)
        for j in range(1, POOL_WINDOWS[gi]):
            tot = tot + pe(j, ts, lanes)
        sums.append(tot)
    wide = slice(2 * gd, 4 * gd)
    s4 = pe(12, ts + 12, wide)
    for j in range(1, 4):
        s4 = s4 + pe(12 + j, ts + 12, wide)
    pool4_s[0:ts + 12, :] = s4
    s8 = pool4_s[pl.ds(4, ts + 8), :] + pool4_s[pl.ds(0, ts + 8), :]
    sums.append(s8[8:, 0:gd])
    sums.append(s8[8:, gd:2 * gd] + s8[0:ts, gd:2 * gd])
    for gi in range(len(POOL_WINDOWS)):
        lanes = slice(gd * gi, gd * (gi + 1))
        pooled = sums[gi] * pinv_ref[0, :, lanes] - pe(0, ts, lanes)
        yp = jnp.dot(pooled.astype(BF16), wpool_ref[gi], preferred_element_type=F32)
        yp = (yp + bpool_ref[:, lanes]) * pscale_ref[:, lanes]
        mix_s[:, dml + gd * gi:dml + gd * (gi + 1)] = yp.astype(BF16)

    mix = jnp.dot(mix_s[...], wout_ref[...], preferred_element_type=F32)
    x1 = x + mod[2:3] * mix
    x1_ref[...] = x1
    r2 = lax.rsqrt(jnp.mean(x1 * x1, axis=-1, keepdims=True) + EPS)
    h2 = (x1 * r2) * mod[4:5] + mod[3:4]
    h2_ref[...] = _pack_bf16_pairs(h2)
    lgt_ref[...] = lax.dot_general(wrt_ref[...], h2.astype(BF16), (((1,), (1,)), ((), ())),
                                   preferred_element_type=F32) + rbias_ref[...]

    uext_s[0:UHIST, :] = uext_s[ts:ts + UHIST, :]
    pext_s[0:PHIST, :] = pext_s[ts:ts + PHIST, :]


def _mixer(x, mod, w_in_r, gbias, conv_w, conv_b, wqk, ng, skip, wpool, bpool, pscale, pinv,
           w_out, wrt, rbias, b0, nb):
    _, seq, dm = x.shape
    dml = conv_w.shape[1]
    ts = min(SEQ_TILE, seq)
    ncols = w_in_r.shape[1]
    nst = seq // ts
    n_tok = nb * seq
    assert seq % ts == 0 and ts % CHUNK == 0
    full = lambda a: pl.BlockSpec(a.shape, lambda b, s: (0,) * a.ndim, pipeline_mode=pl.Buffered(1))
    kern = functools.partial(_mixer_kernel, ts=ts, dm=dm, dml=dml)
    return pl.pallas_call(
        kern,
        grid=(nb, nst),
        in_specs=[pl.BlockSpec((1, ts, dm), lambda b, s: (b + b0, s, 0)),
                  pl.BlockSpec((1, 6, dm), lambda b, s: (b + b0, 0, 0)),
                  full(w_in_r), full(gbias), full(conv_w), full(conv_b), full(wqk),
                  full(ng), full(skip), full(wpool), full(bpool), full(pscale),
                  pl.BlockSpec((1,) + pinv.shape[1:], lambda b, s: (jnp.minimum(s, 1), 0, 0)),
                  full(w_out), full(wrt), full(rbias)],
        out_specs=[pl.BlockSpec((ts, dm), lambda b, s: (b * nst + s, 0)),
                   pl.BlockSpec((ts, dm // 2), lambda b, s: (b * nst + s, 0)),
                   pl.BlockSpec((LOGIT_ROWS, ts), lambda b, s: (0, b * nst + s))],
        out_shape=[jax.ShapeDtypeStruct((n_tok, dm), F32),
                   jax.ShapeDtypeStruct((n_tok, dm // 2), I32),
                   jax.ShapeDtypeStruct((LOGIT_ROWS, n_tok), F32)],
        scratch_shapes=[pltpu.VMEM((UHIST + ts, dml), F32),
                        pltpu.VMEM((PHIST + ts, dm - dml), F32),
                        pltpu.VMEM((ts, ncols - dm), F32),
                        pltpu.VMEM((ts, dml), F32),
                        pltpu.VMEM((ts, 2 * dml), F32),
                        pltpu.VMEM((ts, dm), BF16),
                        pltpu.VMEM((ts + PHIST, 2 * LANES), F32),
                        pltpu.VMEM((N_HEADS, HEAD_DIM, HEAD_DIM), F32),
                        pltpu.VMEM((N_HEADS, HEAD_DIM, HEAD_DIM), F32),
                        pltpu.VMEM((N_HEADS, CHUNK, LANES), F32)],
        compiler_params=pltpu.CompilerParams(dimension_semantics=("arbitrary", "arbitrary"),
                                             vmem_limit_bytes=VMEM_LIMIT),
        name="mixer",
    )(x, mod, w_in_r, gbias, conv_w, conv_b, wqk, ng, skip, wpool, bpool, pscale, pinv, w_out,
      wrt, rbias)


def _route_kernel(lgt_ref, idx_ref, gate_ref, cnt_ref, carry_s, *, tr, n_sub):
    @pl.when(pl.program_id(0) == 0)
    def _():
        carry_s[...] = jnp.zeros_like(carry_s)

    tr_r = lax.broadcasted_iota(I32, (tr, tr), 0)
    tr_c = lax.broadcasted_iota(I32, (tr, tr), 1)
    upper = jnp.where(tr_r < tr_c, 1.0, 0.0).astype(BF16)
    for q in range(n_sub):
        cols = slice(q * tr, (q + 1) * tr)
        idx, gates = _route_tile(lgt_ref[:, cols], upper, carry_s, tr)
        idx_ref[:, cols] = idx
        gate_ref[:, cols] = gates
    cnt_ref[...] = carry_s[...]


def _route_tile(lg, upper, carry_s, tr):
    best = lg[0:1]
    gidx = jnp.zeros((1, tr), I32)
    for j in range(1, N_GROUPS):
        cand = lg[j:j + 1]
        better = cand > best
        gidx = jnp.where(better, j, gidx)
        best = jnp.where(better, cand, best)
    sumexp = jnp.zeros((1, tr), F32)
    for j in range(N_GROUPS):
        sumexp = sumexp + jnp.exp(lg[j:j + 1] - best)
    g_gate = 1.0 / sumexp

    sel = lg[SUBLANES:2 * SUBLANES]
    for j in range(1, N_GROUPS):
        sel = jnp.where(gidx == j, lg[SUBLANES * (j + 1):SUBLANES * (j + 2)], sel)
    sub = lax.broadcasted_iota(I32, (EXPERTS_PER_GROUP, tr), 0)
    v1 = jnp.max(sel, axis=0, keepdims=True)
    i1 = jnp.min(jnp.where(sel == v1, sub, EXPERTS_PER_GROUP), axis=0, keepdims=True)
    sel2 = jnp.where(sub == i1, -jnp.inf, sel)
    v2 = jnp.max(sel2, axis=0, keepdims=True)
    i2 = jnp.min(jnp.where(sel2 == v2, sub, EXPERTS_PER_GROUP), axis=0, keepdims=True)
    e2 = jnp.exp(v2 - v1)
    den = 1.0 + e2
    gate0 = (1.0 / den) * g_gate
    gate1 = (e2 / den) * g_gate
    ex0 = gidx * EXPERTS_PER_GROUP + i1
    ex1 = gidx * EXPERTS_PER_GROUP + i2

    erow = lax.broadcasted_iota(I32, (N_EXPERTS, tr), 0)
    oh0 = erow == ex0
    oh1 = erow == ex1
    oh = jnp.where(oh0 | oh1, 1.0, 0.0).astype(BF16)
    carry = carry_s[...]
    before = jnp.dot(oh, upper, preferred_element_type=F32)
    before = before + jnp.concatenate([carry] * (tr // LANES), axis=1)
    rank0 = jnp.sum(jnp.where(oh0, before, 0.0), axis=0, keepdims=True)
    rank1 = jnp.sum(jnp.where(oh1, before, 0.0), axis=0, keepdims=True)
    carry_s[...] = carry + jnp.dot(oh, jnp.ones((tr, LANES), BF16), preferred_element_type=F32)

    idx = jnp.concatenate([ex0, ex1, rank0.astype(I32), rank1.astype(I32),
                           jnp.zeros((SUBLANES - 4, tr), I32)], axis=0)
    gates = jnp.concatenate([gate0, gate1, jnp.zeros((SUBLANES - 2, tr), F32)], axis=0)
    return idx, gates


def _route(lgt, n_tok):
    tr = ROUTE_TILE
    n_sub = min(ROUTE_SUBTILES, n_tok // tr)
    step = tr * n_sub
    assert n_tok % step == 0
    return pl.pallas_call(
        functools.partial(_route_kernel, tr=tr, n_sub=n_sub),
        grid=(n_tok // step,),
        in_specs=[pl.BlockSpec((LOGIT_ROWS, step), lambda i: (0, i))],
        out_specs=[pl.BlockSpec((SUBLANES, step), lambda i: (0, i)),
                   pl.BlockSpec((SUBLANES, step), lambda i: (0, i)),
                   pl.BlockSpec((N_EXPERTS, LANES), lambda i: (0, 0))],
        out_shape=[jax.ShapeDtypeStruct((SUBLANES, n_tok), I32),
                   jax.ShapeDtypeStruct((SUBLANES, n_tok), F32),
                   jax.ShapeDtypeStruct((N_EXPERTS, LANES), F32)],
        scratch_shapes=[pltpu.VMEM((N_EXPERTS, LANES), F32)],
        compiler_params=pltpu.CompilerParams(dimension_semantics=("arbitrary",),
                                             vmem_limit_bytes=VMEM_LIMIT),
        name="route",
    )(lgt)


def _sc_workers():
    info = plsc.get_sparse_core_info()
    return info.num_cores, info.num_cores * info.num_subcores


def _dispatch(h2p, dest0, dest1, n_slots):
    n_tok, width = h2p.shape
    n_cores, n_workers = _sc_workers()
    per_w = n_tok // n_workers
    ch = min(SC_SCATTER_CHUNK, per_w)
    n_ch = per_w // ch
    assert n_tok % n_workers == 0 and per_w % ch == 0 and ch % SUBLANES == 0 and n_ch % 2 == 0
    mesh = plsc.VectorSubcoreMesh(core_axis_name="c", subcore_axis_name="s")

    @functools.partial(
        pl.kernel, mesh=mesh,
        out_type=jax.ShapeDtypeStruct((n_slots, width), h2p.dtype),
        scratch_types=[pltpu.VMEM((2, ch), I32), pltpu.VMEM((2, ch), I32),
                       pltpu.VMEM((2, ch, width), h2p.dtype),
                       pltpu.SemaphoreType.DMA((2,)), pltpu.SemaphoreType.DMA((2,))],
        name="dispatch")
    def scatter(h_hbm, d0_hbm, d1_hbm, xs_hbm, i0_v, i1_v, rows_v, sem_in, sem_out):
        wid = lax.axis_index("s") * n_cores + lax.axis_index("c")
        base = wid * per_w

        def loads(t0, slot):
            return (pltpu.make_async_copy(d0_hbm.at[pl.ds(t0, ch)], i0_v.at[slot], sem_in.at[slot]),
                    pltpu.make_async_copy(d1_hbm.at[pl.ds(t0, ch)], i1_v.at[slot], sem_in.at[slot]),
                    pltpu.make_async_copy(h_hbm.at[pl.ds(t0, ch)], rows_v.at[slot], sem_in.at[slot]))

        def scatters(slot):
            return (pltpu.make_async_copy(rows_v.at[slot], xs_hbm.at[i0_v.at[slot]], sem_out.at[slot]),
                    pltpu.make_async_copy(rows_v.at[slot], xs_hbm.at[i1_v.at[slot]], sem_out.at[slot]))

        for cp in loads(base, 0):
            cp.start()

        @pl.loop(0, n_ch, step=2)
        def _(k):
            for slot in range(2):
                for cp in loads(base + (k + slot) * ch, slot):
                    cp.wait()
                out_cps = scatters(slot)
                for cp in out_cps:
                    cp.start()
                nxt = k + slot + 1

                @pl.when(nxt < n_ch)
                def _():
                    for cp in loads(base + nxt * ch, 1 - slot):
                        cp.start()

                for cp in out_cps:
                    cp.wait()

    return scatter(h2p, dest0, dest1)


def _collect(ys, dest0, dest1, tok0, n_tok):
    width = ys.shape[1]
    n_cores, n_workers = _sc_workers()
    per_w = n_tok // n_workers
    ch = min(SC_GATHER_CHUNK, per_w)
    n_ch = per_w // ch
    assert n_tok % n_workers == 0 and per_w % ch == 0 and ch % SUBLANES == 0 and n_ch % 2 == 0
    mesh = plsc.VectorSubcoreMesh(core_axis_name="c", subcore_axis_name="s")

    @functools.partial(
        pl.kernel, mesh=mesh,
        out_type=jax.ShapeDtypeStruct((TOP_K, n_tok, width), ys.dtype),
        scratch_types=[pltpu.VMEM((2, ch), I32), pltpu.VMEM((2, ch), I32),
                       pltpu.VMEM((2, ch, width), ys.dtype), pltpu.VMEM((2, ch, width), ys.dtype),
                       pltpu.SemaphoreType.DMA((2,)), pltpu.SemaphoreType.DMA((2,)),
                       pltpu.SemaphoreType.DMA((2,))],
        name="collect")
    def gather(ys_hbm, d0_hbm, d1_hbm, o_hbm, i0_v, i1_v, r0_v, r1_v, sem_idx, sem_in, sem_out):
        wid = lax.axis_index("s") * n_cores + lax.axis_index("c")
        base = wid * per_w

        def idx_loads(t0, slot):
            return (pltpu.make_async_copy(d0_hbm.at[pl.ds(tok0 + t0, ch)], i0_v.at[slot], sem_idx.at[slot]),
                    pltpu.make_async_copy(d1_hbm.at[pl.ds(tok0 + t0, ch)], i1_v.at[slot], sem_idx.at[slot]))

        def gathers(slot):
            return (pltpu.make_async_copy(ys_hbm.at[i0_v.at[slot]], r0_v.at[slot], sem_in.at[slot]),
                    pltpu.make_async_copy(ys_hbm.at[i1_v.at[slot]], r1_v.at[slot], sem_in.at[slot]))

        def stores(t0, slot):
            return (pltpu.make_async_copy(r0_v.at[slot], o_hbm.at[0, pl.ds(t0, ch)], sem_out.at[slot]),
                    pltpu.make_async_copy(r1_v.at[slot], o_hbm.at[1, pl.ds(t0, ch)], sem_out.at[slot]))

        def start_chunk(t0, slot):
            for cp in idx_loads(t0, slot):
                cp.start()
            for cp in idx_loads(t0, slot):
                cp.wait()
            for cp in gathers(slot):
                cp.start()

        start_chunk(base, 0)

        @pl.loop(0, n_ch, step=2)
        def _(k):
            for slot in range(2):
                nxt = k + slot + 1

                @pl.when(nxt < n_ch)
                def _():
                    start_chunk(base + nxt * ch, 1 - slot)

                for cp in gathers(slot):
                    cp.wait()
                out_cps = stores(base + (k + slot) * ch, slot)
                for cp in out_cps:
                    cp.start()
                for cp in out_cps:
                    cp.wait()

    return gather(ys, dest0, dest1)


def _experts_kernel(first_ref, nblk_ref, cnt_ref, xs_hbm, wg_ref, wu_ref, wd_ref, ys_hbm,
                    wgu_s, wd_s, xbuf, ybuf, sem_in, sem_out, *, de, blk):
    e = pl.program_id(0)
    n_exp = pl.num_programs(0)
    wgu_s[:, 0:de] = wg_ref[0].astype(BF16)
    wgu_s[:, de:2 * de] = wu_ref[0].astype(BF16)
    wd_s[...] = wd_ref[0].astype(BF16)
    first = first_ref[e]
    n_blk = nblk_ref[e]
    count = cnt_ref[e]
    total = first_ref[n_exp - 1] + nblk_ref[n_exp - 1]

    def in_copy(g, slot):
        return pltpu.make_async_copy(xs_hbm.at[pl.ds(pl.multiple_of(g * blk, blk), blk)],
                                     xbuf.at[slot], sem_in.at[slot])

    def out_copy(g, slot):
        return pltpu.make_async_copy(ybuf.at[slot],
                                     ys_hbm.at[pl.ds(pl.multiple_of(g * blk, blk), blk)],
                                     sem_out.at[slot])

    @pl.when(e == 0)
    def _():
        in_copy(0, 0).start(priority=1)
        ybuf[...] = jnp.zeros_like(ybuf)

    def block(j, carry):
        g = first + j
        slot = lax.rem(g, 2)
        in_copy(g, slot).wait()

        @pl.when(g + 1 < total)
        def _():
            in_copy(g + 1, 1 - slot).start(priority=1)

        @pl.when(g >= 2)
        def _():
            out_copy(g - 2, slot).wait()

        n_left = count - j * blk

        def ffn(n_rows):
            words = xbuf[slot, 0:n_rows, :]
            rows = lax.broadcasted_iota(I32, words.shape, 0)
            xb = _unpack_bf16_pairs(jnp.where(rows < n_left, words, 0)).astype(BF16)
            ab = jnp.dot(xb, wgu_s[...], preferred_element_type=F32)
            a = ab[:, 0:de]
            b = ab[:, de:2 * de]
            hmid = (a * _sigmoid(a)) * b
            y = jnp.dot(hmid.astype(BF16), wd_s[...], preferred_element_type=F32)
            ybuf[slot, 0:n_rows, :] = _pack_bf16_pairs(y)

        step = blk // EXPERT_BLOCK_PATHS
        for p in range(EXPERT_BLOCK_PATHS):
            lo_rows, hi_rows = p * step, (p + 1) * step
            last = p == EXPERT_BLOCK_PATHS - 1

            @pl.when((n_left > lo_rows) & ((n_left <= hi_rows) | last))
            def _(hi_rows=hi_rows):
                ffn(hi_rows)

        out_copy(g, slot).start(priority=1)
        return carry

    lax.fori_loop(0, n_blk, block, 0)

    @pl.when(e == n_exp - 1)
    def _():
        @pl.when(total >= 2)
        def _():
            out_copy(total - 2, lax.rem(total, 2)).wait()

        out_copy(total - 1, lax.rem(total + 1, 2)).wait()


def _experts(plan, xs, w_gate, w_up, w_down):
    n_slots, width = xs.shape
    n_exp, dm, de = w_gate.shape
    blk = EXPERT_BLOCK
    return pl.pallas_call(
        functools.partial(_experts_kernel, de=de, blk=blk),
        grid_spec=pltpu.PrefetchScalarGridSpec(
            num_scalar_prefetch=3,
            grid=(n_exp,),
            in_specs=[pl.BlockSpec(memory_space=pl.ANY),
                      pl.BlockSpec((1, dm, de), lambda e, f, n, c: (e, 0, 0)),
                      pl.BlockSpec((1, dm, de), lambda e, f, n, c: (e, 0, 0)),
                      pl.BlockSpec((1, de, dm), lambda e, f, n, c: (e, 0, 0))],
            out_specs=pl.BlockSpec(memory_space=pl.ANY),
            scratch_shapes=[pltpu.VMEM((dm, 2 * de), BF16), pltpu.VMEM((de, dm), BF16),
                            pltpu.VMEM((2, blk, width), I32), pltpu.VMEM((2, blk, width), I32),
                            pltpu.SemaphoreType.DMA((2,)), pltpu.SemaphoreType.DMA((2,))]),
        out_shape=jax.ShapeDtypeStruct((n_slots, width), I32),
        compiler_params=pltpu.CompilerParams(dimension_semantics=("arbitrary",),
                                             vmem_limit_bytes=VMEM_LIMIT),
        name="experts",
    )(*plan, xs, w_gate, w_up, w_down)


def _pick(onehot, table):
    return jnp.sum(jnp.where(onehot, table[None, :], 0), axis=1).astype(I32)


def _expert_plan(counts, n_assign):
    blk = EXPERT_BLOCK
    per_e = (counts + blk - 1) // blk
    first_blk = jnp.cumsum(per_e) - per_e
    n_slots = (n_assign // blk + N_EXPERTS) * blk
    return (first_blk * blk).astype(I32), (first_blk.astype(I32), per_e.astype(I32), counts), n_slots


def _slot_of(starts, expert, rank):
    onehot = jnp.arange(N_EXPERTS, dtype=I32)[None, :] == expert[:, None]
    return _pick(onehot, starts) + rank


def _combine_kernel(rows_ref, gate_ref, x1_ref, mod_ref, fg_ref, *rest):
    o_ref = rest[-1]
    tc = x1_ref.shape[0]
    gate_f = mod_ref[0][5:6]
    for q in range(tc // LANES):
        tok = slice(q * LANES, (q + 1) * LANES)
        g_rows = jnp.concatenate([gate_ref[:, tok], jnp.zeros((LANES - SUBLANES, LANES), F32)], axis=0)
        gc = g_rows.T
        y = (gc[:, 0:1] * _unpack_bf16_pairs(rows_ref[0, tok, :])
             + gc[:, 1:2] * _unpack_bf16_pairs(rows_ref[1, tok, :]))
        x2 = x1_ref[tok, :] + gate_f * y
        r = lax.rsqrt(jnp.mean(x2 * x2, axis=-1, keepdims=True) + EPS)
        o_ref[tok, :] = (x2 * r) * fg_ref[...]


def _combine(rows, gcol, x1, mod, final_g, seq, lb0, b0, nb, bsz, out_prev):
    dm = x1.shape[1]
    width = rows.shape[2]
    tc = min(COMBINE_TILE, seq)
    nst = seq // tc
    in_specs = [pl.BlockSpec((TOP_K, tc, width), lambda b, s: (0, b * nst + s, 0)),
                pl.BlockSpec((SUBLANES, tc), lambda b, s: (0, (b + lb0) * nst + s)),
                pl.BlockSpec((tc, dm), lambda b, s: ((b + lb0) * nst + s, 0)),
                pl.BlockSpec((1, 6, dm), lambda b, s: (b + b0, 0, 0)),
                pl.BlockSpec((1, dm), lambda b, s: (0, 0))]
    args = [rows, gcol, x1, mod, final_g.reshape(1, dm)]
    aliases = {}
    if out_prev is not None:
        in_specs.append(pl.BlockSpec(memory_space=pl.ANY))
        args.append(out_prev)
        aliases = {len(args) - 1: 0}
    return pl.pallas_call(
        _combine_kernel,
        grid=(nb, nst),
        in_specs=in_specs,
        out_specs=pl.BlockSpec((tc, dm), lambda b, s: ((b + b0) * nst + s, 0)),
        out_shape=jax.ShapeDtypeStruct((bsz * seq, dm), F32),
        input_output_aliases=aliases,
        compiler_params=pltpu.CompilerParams(dimension_semantics=("arbitrary", "arbitrary"),
                                             vmem_limit_bytes=VMEM_LIMIT),
        name="combine",
    )(*args)


def _layer(x, c, ada_w, ada_b, norm1_g, w_in, conv_w, conv_b, w_q, w_k, b_igate, b_fgate,
           mlstm_norm_g, mlstm_skip, w_pool, b_pool, pool_scale, w_out, norm2_g,
           w_rg, b_rg, w_re, b_re, w_eg, w_eu, w_ed, out_g):
    bsz, seq, dm = x.shape
    dml = conv_w.shape[1]
    n_tok = bsz * seq
    ts = min(SEQ_TILE, seq)

    mod = _ada(c, ada_w, ada_b).reshape(bsz, 6, dm)

    w_in_r, w_out_b = _prep_weights(w_in, w_out, dml)
    gbias = jnp.pad(jnp.concatenate([b_igate, b_fgate]), (0, LANES - 2 * N_HEADS)).reshape(1, LANES)
    wqk = jnp.concatenate([w_q, w_k], axis=-1).astype(BF16)
    wrt = jnp.zeros((LOGIT_ROWS, dm), F32)
    wrt = wrt.at[0:N_GROUPS].set(w_rg.T).at[SUBLANES:SUBLANES + N_EXPERTS].set(w_re.T).astype(BF16)
    rb = jnp.zeros((LOGIT_ROWS,), F32).at[0:N_GROUPS].set(b_rg).at[SUBLANES:SUBLANES + N_EXPERTS].set(b_re)
    rbias = jnp.broadcast_to(rb[:, None], (LOGIT_ROWS, ts))

    gdim = (dm - dml) // len(POOL_WINDOWS)
    win = jnp.repeat(jnp.array(POOL_WINDOWS, F32), gdim)[None, :]
    t1 = jnp.arange(1, ts + 1, dtype=F32)[:, None]
    pinv = jnp.stack([1.0 / jnp.minimum(t1, win), jnp.broadcast_to(1.0 / win, (ts, dm - dml))])

    mod = mod.at[:, 1].set(norm1_g * (1.0 + mod[:, 1])).at[:, 4].set(norm2_g * (1.0 + mod[:, 4]))

    mixer_params = (w_in_r, gbias, conv_w, conv_b.reshape(1, dml), wqk,
                    mlstm_norm_g.reshape(1, dml), mlstm_skip.reshape(1, dml), w_pool.astype(BF16),
                    b_pool.reshape(1, dm - dml), pool_scale.reshape(1, dm - dml), pinv,
                    w_out_b, wrt, rbias)

    x1, h2, lgt = _mixer(x, mod, *mixer_params, 0, bsz)
    idx, gcol, cnt = _route(lgt, n_tok)
    counts = cnt[:, 0].astype(I32)
    starts, plan, n_slots = _expert_plan(counts, n_tok * TOP_K)
    dest0 = _slot_of(starts, idx[0], idx[2])
    dest1 = _slot_of(starts, idx[1], idx[3])
    xs = _dispatch(h2, dest0, dest1, n_slots)
    ys = _experts(plan, xs, w_eg, w_eu, w_ed)

    nb = bsz // COMBINE_GROUPS
    assert bsz % COMBINE_GROUPS == 0
    out = None
    for b0 in range(0, bsz, nb):
        rows = _collect(ys, dest0, dest1, b0 * seq, nb * seq)
        out = _combine(rows, gcol, x1, mod, out_g, seq, b0, b0, nb, bsz, out)
    return out.reshape(bsz, seq, dm)


def kernel(x, c, ada_w, ada_b, norm1_g, w_in, conv_w, conv_b, w_q, w_k, b_igate, b_fgate, mlstm_norm_g, mlstm_skip, w_pool, b_pool, pool_scale, w_out, norm2_g, w_router_group, b_router_group, w_router_expert, b_router_expert, w_expert_gate, w_expert_up, w_expert_down, final_g):
    depth = ada_w.shape[0]
    assert depth == 1, "the final norm is fused into the last layer's combine kernel"
    l = 0
    return _layer(x, c, ada_w[l], ada_b[l], norm1_g[l], w_in[l], conv_w[l], conv_b[l], w_q[l],
                  w_k[l], b_igate[l], b_fgate[l], mlstm_norm_g[l], mlstm_skip[l], w_pool[l],
                  b_pool[l], pool_scale[l], w_out[l], norm2_g[l], w_router_group[l],
                  b_router_group[l], w_router_expert[l], b_router_expert[l],
                  w_expert_gate[l], w_expert_up[l], w_expert_down[l], final_g)
```

```python
import functools

import jax
import jax.numpy as jnp
from jax import lax
from jax.experimental import pallas as pl
from jax.experimental.pallas import tpu as pltpu
from jax.experimental.pallas import tpu_sc as plsc

F32 = jnp.float32
BF16 = jnp.bfloat16
I32 = jnp.int32
U32 = jnp.uint32

EPS = 1e-6
N_HEADS = 4
HEAD_DIM = 128
CONV_WIDTH = 4
POOL_WINDOWS = (2, 4, 8, 16)
N_GROUPS = 4
EXPERTS_PER_GROUP = 8
N_EXPERTS = N_GROUPS * EXPERTS_PER_GROUP
TOP_K = 2

LANES = 128
SUBLANES = 8
CHUNK = 128
SEQ_TILE = 1024
ADA_TILE = 1024
PREP_COLS = 512
COMBINE_GROUPS = 8
ROUTE_TILE = 512
ROUTE_SUBTILES = 16
SC_SCATTER_CHUNK = 64
SC_GATHER_CHUNK = 32
COMBINE_TILE = 2048
EXPERT_BLOCK = 1024
EXPERT_BLOCK_PATHS = 8
LOGIT_ROWS = 48
UHIST = 8
PHIST = 16
VMEM_LIMIT = 60 * 1024 * 1024


def _sigmoid(x):
    return 1.0 / (1.0 + jnp.exp(-x))


def _pack_bf16_pairs(x):
    w = x.shape[1] // 2
    half_ulp = jnp.uint32(0x8000)
    hi = lax.bitcast_convert_type(x[:, :w], U32) + half_ulp
    lo = lax.bitcast_convert_type(x[:, w:], U32) + half_ulp
    return lax.bitcast_convert_type((hi & jnp.uint32(0xFFFF0000)) | (lo >> 16), I32)


def _unpack_bf16_pairs(words):
    u = lax.bitcast_convert_type(words, U32)
    hi = lax.bitcast_convert_type(u & jnp.uint32(0xFFFF0000), F32)
    lo = lax.bitcast_convert_type(u << 16, F32)
    return jnp.concatenate([hi, lo], axis=1)


def _ada_kernel(c_ref, w_ref, b_ref, o_ref):
    c = c_ref[...]
    s = c * _sigmoid(c)
    o_ref[...] = jnp.dot(s.astype(BF16), w_ref[...].astype(BF16),
                         preferred_element_type=F32) + b_ref[...]


def _ada(c, ada_w, ada_b):
    bsz, dm = c.shape
    n = ada_w.shape[1]
    tn = ADA_TILE
    return pl.pallas_call(
        _ada_kernel,
        grid=(n // tn,),
        in_specs=[pl.BlockSpec((bsz, dm), lambda j: (0, 0)),
                  pl.BlockSpec((dm, tn), lambda j: (0, j)),
                  pl.BlockSpec((1, tn), lambda j: (0, j))],
        out_specs=pl.BlockSpec((bsz, tn), lambda j: (0, j)),
        out_shape=jax.ShapeDtypeStruct((bsz, n), F32),
        compiler_params=pltpu.CompilerParams(dimension_semantics=("arbitrary",),
                                             vmem_limit_bytes=VMEM_LIMIT),
        name="ada",
    )(c, ada_w, ada_b.reshape(1, n))


def _prep_kernel(wint_ref, wout_ref, winr_ref, woutb_ref, *, dml, dp):
    col_v, col_o, col_i = dml, 2 * dml, 3 * dml
    col_p = col_i + 2 * N_HEADS
    kb = wint_ref.shape[1]
    winr_ref[:, 0:dml] = wint_ref[0:col_v, :].T.astype(BF16)
    winr_ref[:, dml:dml + dp] = wint_ref[col_p:col_p + dp, :].T.astype(BF16)
    winr_ref[:, dml + dp:2 * dml + dp] = wint_ref[col_v:col_o, :].T.astype(BF16)
    winr_ref[:, 2 * dml + dp:3 * dml + dp] = wint_ref[col_o:col_i, :].T.astype(BF16)
    gates = jnp.concatenate([wint_ref[col_i:col_p, :], jnp.zeros((LANES - 2 * N_HEADS, kb), F32)], axis=0)
    winr_ref[:, 3 * dml + dp:3 * dml + dp + LANES] = gates.T.astype(BF16)
    woutb_ref[...] = wout_ref[...].astype(BF16)


def _prep_weights(w_in, w_out, dml):
    dm, ncol = w_in.shape
    dp = ncol - 3 * dml - 2 * N_HEADS
    nout = 3 * dml + dp + LANES
    kb = PREP_COLS
    assert dm % kb == 0 and w_out.shape[1] % kb == 0 and 2 * N_HEADS == SUBLANES
    return pl.pallas_call(
        functools.partial(_prep_kernel, dml=dml, dp=dp),
        grid=(dm // kb,),
        in_specs=[pl.BlockSpec((ncol, kb), lambda i: (0, i)),
                  pl.BlockSpec((w_out.shape[0], kb), lambda i: (0, i))],
        out_specs=[pl.BlockSpec((kb, nout), lambda i: (i, 0)),
                   pl.BlockSpec((w_out.shape[0], kb), lambda i: (0, i))],
        out_shape=[jax.ShapeDtypeStruct((dm, nout), BF16),
                   jax.ShapeDtypeStruct(w_out.shape, BF16)],
        compiler_params=pltpu.CompilerParams(dimension_semantics=("arbitrary",),
                                             vmem_limit_bytes=VMEM_LIMIT),
        name="prep",
    )(w_in.T, w_out)


def _split3(x):
    hi = x.astype(BF16)
    r1 = x - hi.astype(F32)
    mid = r1.astype(BF16)
    lo = (r1 - mid.astype(F32)).astype(BF16)
    return hi, mid, lo


def _mixer_kernel(x_ref, mod_ref, win_ref, gbias_ref, convw_ref, convb_ref, wqk_ref,
                  ng_ref, skip_ref, wpool_ref, bpool_ref, pscale_ref, pinv_ref, wout_ref,
                  wrt_ref, rbias_ref,
                  x1_ref, h2_ref, lgt_ref,
                  uext_s, pext_s, proj_s, uc_s, qk_s, mix_s, pool4_s, ctv_s, ctn_s, mprev_s,
                  *, ts, dm, dml):
    s_idx = pl.program_id(1)
    n_chunks = ts // CHUNK
    dp = dm - dml

    @pl.when(s_idx == 0)
    def _():
        uext_s[0:UHIST, :] = jnp.zeros((UHIST, dml), F32)
        pext_s[0:PHIST, :] = jnp.zeros((PHIST, dp), F32)
        ctv_s[...] = jnp.zeros_like(ctv_s)
        ctn_s[...] = jnp.zeros_like(ctn_s)
        mprev_s[...] = jnp.zeros_like(mprev_s)

    row_i = lax.broadcasted_iota(I32, (CHUNK, CHUNK), 0)
    col_i = lax.broadcasted_iota(I32, (CHUNK, CHUNK), 1)
    causal = row_i >= col_i
    triu = jnp.where(row_i <= col_i, 1.0, 0.0).astype(BF16)
    ones_blk = jnp.ones((CHUNK, HEAD_DIM), BF16)
    q_scale = HEAD_DIM ** -0.5

    col_o = dml
    col_g = 2 * dml

    x = x_ref[0]
    mod = mod_ref[0]
    r = lax.rsqrt(jnp.mean(x * x, axis=-1, keepdims=True) + EPS)
    h = (x * r) * mod[1:2] + mod[0:1]
    res = jnp.dot(h.astype(BF16), win_ref[...], preferred_element_type=F32)
    uext_s[UHIST:, :] = res[:, 0:dml]
    pext_s[PHIST:, :] = res[:, dml:dm]
    proj_s[...] = res[:, dm:]

    acc = None
    for j in reversed(range(CONV_WIDTH)):
        tap = uext_s[pl.ds(UHIST - (CONV_WIDTH - 1 - j), ts), :] * convw_ref[j:j + 1, :]
        acc = tap if acc is None else acc + tap
    conv = acc + convb_ref[...]
    uc = conv * _sigmoid(conv)
    uc_s[...] = uc

    for hd in range(N_HEADS):
        qk = jnp.dot(uc[:, HEAD_DIM * hd:HEAD_DIM * (hd + 1)].astype(BF16), wqk_ref[hd],
                     preferred_element_type=F32)
        c0 = 2 * HEAD_DIM * hd
        qk_s[:, c0:c0 + HEAD_DIM] = qk[:, 0:HEAD_DIM] * q_scale
        qk_s[:, c0 + HEAD_DIM:c0 + 2 * HEAD_DIM] = qk[:, HEAD_DIM:2 * HEAD_DIM]

    pairs = [(c, hd) for c in range(n_chunks) for hd in range(N_HEADS)]
    g_rows, b_rows, b_cols = [], [], []
    pad_rows = jnp.zeros((CHUNK - SUBLANES, CHUNK), F32)
    for c in range(n_chunks):
        g_tile = proj_s[pl.ds(c * CHUNK, CHUNK), col_g:col_g + LANES] + gbias_ref[...]
        g_row = g_tile.T[0:SUBLANES, :]
        logf = -(jnp.maximum(-g_row, 0.0) + jnp.log1p(jnp.exp(-jnp.abs(g_row))))
        hi, mid, lo = _split3(logf)
        cs = jnp.dot(jnp.concatenate([hi, mid, lo], axis=0), triu, preferred_element_type=F32)
        b_row = cs[0:SUBLANES] + cs[SUBLANES:2 * SUBLANES] + cs[2 * SUBLANES:3 * SUBLANES]
        g_rows.append(g_row)
        b_rows.append(b_row)
        b_cols.append(jnp.concatenate([b_row, pad_rows], axis=0).T)

    def rows(ref, c, lo_col, width=HEAD_DIM):
        return ref[pl.ds(c * CHUNK, CHUNK), lo_col:lo_col + width]

    b_bcs, dlogs, rmaxs, p_mats = {}, {}, {}, {}
    for c, hd in pairs:
        b_bc = jnp.broadcast_to(b_cols[c][:, N_HEADS + hd:N_HEADS + hd + 1], (CHUNK, CHUNK))
        i_row = g_rows[c][hd:hd + 1, :]
        b_row = b_rows[c][N_HEADS + hd:N_HEADS + hd + 1, :]
        dlog = jnp.where(causal, (b_bc - b_row) + i_row, -jnp.inf)
        b_bcs[c, hd], dlogs[c, hd] = b_bc, dlog
        rmaxs[c, hd] = jnp.max(dlog, axis=-1, keepdims=True)
        q_c = rows(qk_s, c, 2 * HEAD_DIM * hd)
        k_c = rows(qk_s, c, 2 * HEAD_DIM * hd + HEAD_DIM)
        p_mats[c, hd] = lax.dot_general(q_c.astype(BF16), k_c.astype(BF16),
                                        (((1,), (1,)), ((), ())), preferred_element_type=F32)

    inters, m_ts = {}, {}
    for hd in range(N_HEADS):
        m_prev = mprev_s[hd]
        for c in range(n_chunks):
            inter = b_bcs[c, hd] + m_prev
            m_t = jnp.maximum(inter, rmaxs[c, hd])
            inters[c, hd], m_ts[c, hd] = inter, m_t
            m_prev = jnp.broadcast_to(m_t[CHUNK - 1:CHUNK, :], (CHUNK, LANES))
        mprev_s[hd] = m_prev

    lhs, v_augs, upds, a_prevs, e_negms = {}, {}, {}, {}, {}
    for c, hd in pairs:
        m_t = m_ts[c, hd]
        wm = jnp.exp(dlogs[c, hd] - m_t)
        a_inter = jnp.exp(inters[c, hd] - m_t)
        e_negms[c, hd] = jnp.exp(-m_t)
        q_c = rows(qk_s, c, 2 * HEAD_DIM * hd)
        k_c = rows(qk_s, c, 2 * HEAD_DIM * hd + HEAD_DIM)
        v_c = rows(proj_s, c, HEAD_DIM * hd)
        s_mat = (p_mats[c, hd] * wm).astype(BF16)
        qa = (q_c * a_inter).astype(BF16)
        lhs[c, hd] = jnp.concatenate([s_mat, qa], axis=1)
        v_aug = jnp.concatenate([v_c.astype(BF16), ones_blk], axis=1)
        v_augs[c, hd] = v_aug
        ktw = (k_c.T * wm[CHUNK - 1:CHUNK, :]).astype(BF16)
        upds[c, hd] = jnp.dot(ktw, v_aug, preferred_element_type=F32)
        a_prevs[c, hd] = a_inter[CHUNK - 1:CHUNK, :]

    ct_in = {}
    for hd in range(N_HEADS):
        ctv, ctn = ctv_s[hd], ctn_s[hd]
        for c in range(n_chunks):
            ct_in[c, hd] = jnp.concatenate([ctv, ctn], axis=1).astype(BF16)
            ctv = a_prevs[c, hd] * ctv + upds[c, hd][:, 0:HEAD_DIM]
            ctn = a_prevs[c, hd] * ctn + upds[c, hd][:, HEAD_DIM:2 * HEAD_DIM]
        ctv_s[hd], ctn_s[hd] = ctv, ctn

    for c, hd in pairs:
        numden = jnp.dot(lhs[c, hd], jnp.concatenate([v_augs[c, hd], ct_in[c, hd]], axis=0),
                         preferred_element_type=F32)
        num = numden[:, 0:HEAD_DIM]
        den = numden[:, HEAD_DIM:2 * HEAD_DIM]
        hh = num / jnp.maximum(jnp.abs(den), e_negms[c, hd])
        ms = jnp.mean(hh * hh, axis=-1, keepdims=True)
        hn = hh * lax.rsqrt(ms + EPS) * ng_ref[:, HEAD_DIM * hd:HEAD_DIM * (hd + 1)]
        o_c = rows(proj_s, c, col_o + HEAD_DIM * hd)
        uc_c = rows(uc_s, c, HEAD_DIM * hd)
        out_c = _sigmoid(o_c) * (hn + skip_ref[:, HEAD_DIM * hd:HEAD_DIM * (hd + 1)] * uc_c)
        mix_s[pl.ds(c * CHUNK, CHUNK), HEAD_DIM * hd:HEAD_DIM * (hd + 1)] = out_c.astype(BF16)

    def pe(shift, rows, lanes):
        return pext_s[pl.ds(PHIST - shift, rows), lanes]

    gd = LANES
    sums = []
    for gi in range(2):
        lanes = slice(gd * gi, gd * (gi + 1))
        tot = pe(0, ts, lanes)
        for j in range(1, POOL_WINDOWS[gi]):
            tot = tot + pe(j, ts, lanes)
        sums.append(tot)
    wide = slice(2 * gd, 4 * gd)
    s4 = pe(12, ts + 12, wide)
    for j in range(1, 4):
        s4 = s4 + pe(12 + j, ts + 12, wide)
    pool4_s[0:ts + 12, :] = s4
    s8 = pool4_s[pl.ds(4, ts + 8), :] + pool4_s[pl.ds(0, ts + 8), :]
    sums.append(s8[8:, 0:gd])
    sums.append(s8[8:, gd:2 * gd] + s8[0:ts, gd:2 * gd])
    for gi in range(len(POOL_WINDOWS)):
        lanes = slice(gd * gi, gd * (gi + 1))
        pooled = sums[gi] * pinv_ref[0, :, lanes] - pe(0, ts, lanes)
        yp = jnp.dot(pooled.astype(BF16), wpool_ref[gi], preferred_element_type=F32)
        yp = (yp + bpool_ref[:, lanes]) * pscale_ref[:, lanes]
        mix_s[:, dml + gd * gi:dml + gd * (gi + 1)] = yp.astype(BF16)

    mix = jnp.dot(mix_s[...], wout_ref[...], preferred_element_type=F32)
    x1 = x + mod[2:3] * mix
    x1_ref[...] = x1
    r2 = lax.rsqrt(jnp.mean(x1 * x1, axis=-1, keepdims=True) + EPS)
    h2 = (x1 * r2) * mod[4:5] + mod[3:4]
    h2_ref[...] = _pack_bf16_pairs(h2)
    lgt_ref[...] = lax.dot_general(wrt_ref[...], h2.astype(BF16), (((1,), (1,)), ((), ())),
                                   preferred_element_type=F32) + rbias_ref[...]

    uext_s[0:UHIST, :] = uext_s[ts:ts + UHIST, :]
    pext_s[0:PHIST, :] = pext_s[ts:ts + PHIST, :]


def _mixer(x, mod, w_in_r, gbias, conv_w, conv_b, wqk, ng, skip, wpool, bpool, pscale, pinv,
           w_out, wrt, rbias, b0, nb):
    _, seq, dm = x.shape
    dml = conv_w.shape[1]
    ts = min(SEQ_TILE, seq)
    ncols = w_in_r.shape[1]
    nst = seq // ts
    n_tok = nb * seq
    assert seq % ts == 0 and ts % CHUNK == 0
    full = lambda a: pl.BlockSpec(a.shape, lambda b, s: (0,) * a.ndim, pipeline_mode=pl.Buffered(1))
    kern = functools.partial(_mixer_kernel, ts=ts, dm=dm, dml=dml)
    return pl.pallas_call(
        kern,
        grid=(nb, nst),
        in_specs=[pl.BlockSpec((1, ts, dm), lambda b, s: (b + b0, s, 0)),
                  pl.BlockSpec((1, 6, dm), lambda b, s: (b + b0, 0, 0)),
                  full(w_in_r), full(gbias), full(conv_w), full(conv_b), full(wqk),
                  full(ng), full(skip), full(wpool), full(bpool), full(pscale),
                  pl.BlockSpec((1,) + pinv.shape[1:], lambda b, s: (jnp.minimum(s, 1), 0, 0)),
                  full(w_out), full(wrt), full(rbias)],
        out_specs=[pl.BlockSpec((ts, dm), lambda b, s: (b * nst + s, 0)),
                   pl.BlockSpec((ts, dm // 2), lambda b, s: (b * nst + s, 0)),
                   pl.BlockSpec((LOGIT_ROWS, ts), lambda b, s: (0, b * nst + s))],
        out_shape=[jax.ShapeDtypeStruct((n_tok, dm), F32),
                   jax.ShapeDtypeStruct((n_tok, dm // 2), I32),
                   jax.ShapeDtypeStruct((LOGIT_ROWS, n_tok), F32)],
        scratch_shapes=[pltpu.VMEM((UHIST + ts, dml), F32),
                        pltpu.VMEM((PHIST + ts, dm - dml), F32),
                        pltpu.VMEM((ts, ncols - dm), F32),
                        pltpu.VMEM((ts, dml), F32),
                        pltpu.VMEM((ts, 2 * dml), F32),
                        pltpu.VMEM((ts, dm), BF16),
                        pltpu.VMEM((ts + PHIST, 2 * LANES), F32),
                        pltpu.VMEM((N_HEADS, HEAD_DIM, HEAD_DIM), F32),
                        pltpu.VMEM((N_HEADS, HEAD_DIM, HEAD_DIM), F32),
                        pltpu.VMEM((N_HEADS, CHUNK, LANES), F32)],
        compiler_params=pltpu.CompilerParams(dimension_semantics=("arbitrary", "arbitrary"),
                                             vmem_limit_bytes=VMEM_LIMIT),
        name="mixer",
    )(x, mod, w_in_r, gbias, conv_w, conv_b, wqk, ng, skip, wpool, bpool, pscale, pinv, w_out,
      wrt, rbias)


def _route_kernel(lgt_ref, idx_ref, gate_ref, cnt_ref, carry_s, *, tr, n_sub):
    @pl.when(pl.program_id(0) == 0)
    def _():
        carry_s[...] = jnp.zeros_like(carry_s)

    tr_r = lax.broadcasted_iota(I32, (tr, tr), 0)
    tr_c = lax.broadcasted_iota(I32, (tr, tr), 1)
    upper = jnp.where(tr_r < tr_c, 1.0, 0.0).astype(BF16)
    for q in range(n_sub):
        cols = slice(q * tr, (q + 1) * tr)
        idx, gates = _route_tile(lgt_ref[:, cols], upper, carry_s, tr)
        idx_ref[:, cols] = idx
        gate_ref[:, cols] = gates
    cnt_ref[...] = carry_s[...]


def _route_tile(lg, upper, carry_s, tr):
    best = lg[0:1]
    gidx = jnp.zeros((1, tr), I32)
    for j in range(1, N_GROUPS):
        cand = lg[j:j + 1]
        better = cand > best
        gidx = jnp.where(better, j, gidx)
        best = jnp.where(better, cand, best)
    sumexp = jnp.zeros((1, tr), F32)
    for j in range(N_GROUPS):
        sumexp = sumexp + jnp.exp(lg[j:j + 1] - best)
    g_gate = 1.0 / sumexp

    sel = lg[SUBLANES:2 * SUBLANES]
    for j in range(1, N_GROUPS):
        sel = jnp.where(gidx == j, lg[SUBLANES * (j + 1):SUBLANES * (j + 2)], sel)
    sub = lax.broadcasted_iota(I32, (EXPERTS_PER_GROUP, tr), 0)
    v1 = jnp.max(sel, axis=0, keepdims=True)
    i1 = jnp.min(jnp.where(sel == v1, sub, EXPERTS_PER_GROUP), axis=0, keepdims=True)
    sel2 = jnp.where(sub == i1, -jnp.inf, sel)
    v2 = jnp.max(sel2, axis=0, keepdims=True)
    i2 = jnp.min(jnp.where(sel2 == v2, sub, EXPERTS_PER_GROUP), axis=0, keepdims=True)
    e2 = jnp.exp(v2 - v1)
    den = 1.0 + e2
    gate0 = (1.0 / den) * g_gate
    gate1 = (e2 / den) * g_gate
    ex0 = gidx * EXPERTS_PER_GROUP + i1
    ex1 = gidx * EXPERTS_PER_GROUP + i2

    erow = lax.broadcasted_iota(I32, (N_EXPERTS, tr), 0)
    oh0 = erow == ex0
    oh1 = erow == ex1
    oh = jnp.where(oh0 | oh1, 1.0, 0.0).astype(BF16)
    carry = carry_s[...]
    before = jnp.dot(oh, upper, preferred_element_type=F32)
    before = before + jnp.concatenate([carry] * (tr // LANES), axis=1)
    rank0 = jnp.sum(jnp.where(oh0, before, 0.0), axis=0, keepdims=True)
    rank1 = jnp.sum(jnp.where(oh1, before, 0.0), axis=0, keepdims=True)
    carry_s[...] = carry + jnp.dot(oh, jnp.ones((tr, LANES), BF16), preferred_element_type=F32)

    idx = jnp.concatenate([ex0, ex1, rank0.astype(I32), rank1.astype(I32),
                           jnp.zeros((SUBLANES - 4, tr), I32)], axis=0)
    gates = jnp.concatenate([gate0, gate1, jnp.zeros((SUBLANES - 2, tr), F32)], axis=0)
    return idx, gates


def _route(lgt, n_tok):
    tr = ROUTE_TILE
    n_sub = min(ROUTE_SUBTILES, n_tok // tr)
    step = tr * n_sub
    assert n_tok % step == 0
    return pl.pallas_call(
        functools.partial(_route_kernel, tr=tr, n_sub=n_sub),
        grid=(n_tok // step,),
        in_specs=[pl.BlockSpec((LOGIT_ROWS, step), lambda i: (0, i))],
        out_specs=[pl.BlockSpec((SUBLANES, step), lambda i: (0, i)),
                   pl.BlockSpec((SUBLANES, step), lambda i: (0, i)),
                   pl.BlockSpec((N_EXPERTS, LANES), lambda i: (0, 0))],
        out_shape=[jax.ShapeDtypeStruct((SUBLANES, n_tok), I32),
                   jax.ShapeDtypeStruct((SUBLANES, n_tok), F32),
                   jax.ShapeDtypeStruct((N_EXPERTS, LANES), F32)],
        scratch_shapes=[pltpu.VMEM((N_EXPERTS, LANES), F32)],
        compiler_params=pltpu.CompilerParams(dimension_semantics=("arbitrary",),
                                             vmem_limit_bytes=VMEM_LIMIT),
        name="route",
    )(lgt)


def _sc_workers():
    info = plsc.get_sparse_core_info()
    return info.num_cores, info.num_cores * info.num_subcores


def _dispatch(h2p, dest0, dest1, n_slots):
    n_tok, width = h2p.shape
    n_cores, n_workers = _sc_workers()
    per_w = n_tok // n_workers
    ch = min(SC_SCATTER_CHUNK, per_w)
    n_ch = per_w // ch
    assert n_tok % n_workers == 0 and per_w % ch == 0 and ch % SUBLANES == 0 and n_ch % 2 == 0
    mesh = plsc.VectorSubcoreMesh(core_axis_name="c", subcore_axis_name="s")

    @functools.partial(
        pl.kernel, mesh=mesh,
        out_type=jax.ShapeDtypeStruct((n_slots, width), h2p.dtype),
        scratch_types=[pltpu.VMEM((2, ch), I32), pltpu.VMEM((2, ch), I32),
                       pltpu.VMEM((2, ch, width), h2p.dtype),
                       pltpu.SemaphoreType.DMA((2,)), pltpu.SemaphoreType.DMA((2,))],
        name="dispatch")
    def scatter(h_hbm, d0_hbm, d1_hbm, xs_hbm, i0_v, i1_v, rows_v, sem_in, sem_out):
        wid = lax.axis_index("s") * n_cores + lax.axis_index("c")
        base = wid * per_w

        def loads(t0, slot):
            return (pltpu.make_async_copy(d0_hbm.at[pl.ds(t0, ch)], i0_v.at[slot], sem_in.at[slot]),
                    pltpu.make_async_copy(d1_hbm.at[pl.ds(t0, ch)], i1_v.at[slot], sem_in.at[slot]),
                    pltpu.make_async_copy(h_hbm.at[pl.ds(t0, ch)], rows_v.at[slot], sem_in.at[slot]))

        def scatters(slot):
            return (pltpu.make_async_copy(rows_v.at[slot], xs_hbm.at[i0_v.at[slot]], sem_out.at[slot]),
                    pltpu.make_async_copy(rows_v.at[slot], xs_hbm.at[i1_v.at[slot]], sem_out.at[slot]))

        for cp in loads(base, 0):
            cp.start()

        @pl.loop(0, n_ch, step=2)
        def _(k):
            for slot in range(2):
                for cp in loads(base + (k + slot) * ch, slot):
                    cp.wait()
                out_cps = scatters(slot)
                for cp in out_cps:
                    cp.start()
                nxt = k + slot + 1

                @pl.when(nxt < n_ch)
                def _():
                    for cp in loads(base + nxt * ch, 1 - slot):
                        cp.start()

                for cp in out_cps:
                    cp.wait()

    return scatter(h2p, dest0, dest1)


def _collect(ys, dest0, dest1, tok0, n_tok):
    width = ys.shape[1]
    n_cores, n_workers = _sc_workers()
    per_w = n_tok // n_workers
    ch = min(SC_GATHER_CHUNK, per_w)
    n_ch = per_w // ch
    assert n_tok % n_workers == 0 and per_w % ch == 0 and ch % SUBLANES == 0 and n_ch % 2 == 0
    mesh = plsc.VectorSubcoreMesh(core_axis_name="c", subcore_axis_name="s")

    @functools.partial(
        pl.kernel, mesh=mesh,
        out_type=jax.ShapeDtypeStruct((TOP_K, n_tok, width), ys.dtype),
        scratch_types=[pltpu.VMEM((2, ch), I32), pltpu.VMEM((2, ch), I32),
                       pltpu.VMEM((2, ch, width), ys.dtype), pltpu.VMEM((2, ch, width), ys.dtype),
                       pltpu.SemaphoreType.DMA((2,)), pltpu.SemaphoreType.DMA((2,)),
                       pltpu.SemaphoreType.DMA((2,))],
        name="collect")
    def gather(ys_hbm, d0_hbm, d1_hbm, o_hbm, i0_v, i1_v, r0_v, r1_v, sem_idx, sem_in, sem_out):
        wid = lax.axis_index("s") * n_cores + lax.axis_index("c")
        base = wid * per_w

        def idx_loads(t0, slot):
            return (pltpu.make_async_copy(d0_hbm.at[pl.ds(tok0 + t0, ch)], i0_v.at[slot], sem_idx.at[slot]),
                    pltpu.make_async_copy(d1_hbm.at[pl.ds(tok0 + t0, ch)], i1_v.at[slot], sem_idx.at[slot]))

        def gathers(slot):
            return (pltpu.make_async_copy(ys_hbm.at[i0_v.at[slot]], r0_v.at[slot], sem_in.at[slot]),
                    pltpu.make_async_copy(ys_hbm.at[i1_v.at[slot]], r1_v.at[slot], sem_in.at[slot]))

        def stores(t0, slot):
            return (pltpu.make_async_copy(r0_v.at[slot], o_hbm.at[0, pl.ds(t0, ch)], sem_out.at[slot]),
                    pltpu.make_async_copy(r1_v.at[slot], o_hbm.at[1, pl.ds(t0, ch)], sem_out.at[slot]))

        def start_chunk(t0, slot):
            for cp in idx_loads(t0, slot):
                cp.start()
            for cp in idx_loads(t0, slot):
                cp.wait()
            for cp in gathers(slot):
                cp.start()

        start_chunk(base, 0)

        @pl.loop(0, n_ch, step=2)
        def _(k):
            for slot in range(2):
                nxt = k + slot + 1

                @pl.when(nxt < n_ch)
                def _():
                    start_chunk(base + nxt * ch, 1 - slot)

                for cp in gathers(slot):
                    cp.wait()
                out_cps = stores(base + (k + slot) * ch, slot)
                for cp in out_cps:
                    cp.start()
                for cp in out_cps:
                    cp.wait()

    return gather(ys, dest0, dest1)


def _experts_kernel(first_ref, nblk_ref, cnt_ref, xs_hbm, wg_ref, wu_ref, wd_ref, ys_hbm,
                    wgu_s, wd_s, xbuf, ybuf, sem_in, sem_out, *, de, blk):
    e = pl.program_id(0)
    n_exp = pl.num_programs(0)
    wgu_s[:, 0:de] = wg_ref[0].astype(BF16)
    wgu_s[:, de:2 * de] = wu_ref[0].astype(BF16)
    wd_s[...] = wd_ref[0].astype(BF16)
    first = first_ref[e]
    n_blk = nblk_ref[e]
    count = cnt_ref[e]
    total = first_ref[n_exp - 1] + nblk_ref[n_exp - 1]

    def in_copy(g, slot):
        return pltpu.make_async_copy(xs_hbm.at[pl.ds(pl.multiple_of(g * blk, blk), blk)],
                                     xbuf.at[slot], sem_in.at[slot])

    def out_copy(g, slot):
        return pltpu.make_async_copy(ybuf.at[slot],
                                     ys_hbm.at[pl.ds(pl.multiple_of(g * blk, blk), blk)],
                                     sem_out.at[slot])

    @pl.when(e == 0)
    def _():
        in_copy(0, 0).start(priority=1)
        ybuf[...] = jnp.zeros_like(ybuf)

    def block(j, carry):
        g = first + j
        slot = lax.rem(g, 2)
        in_copy(g, slot).wait()

        @pl.when(g + 1 < total)
        def _():
            in_copy(g + 1, 1 - slot).start(priority=1)

        @pl.when(g >= 2)
        def _():
            out_copy(g - 2, slot).wait()

        n_left = count - j * blk

        def ffn(n_rows):
            words = xbuf[slot, 0:n_rows, :]
            rows = lax.broadcasted_iota(I32, words.shape, 0)
            xb = _unpack_bf16_pairs(jnp.where(rows < n_left, words, 0)).astype(BF16)
            ab = jnp.dot(xb, wgu_s[...], preferred_element_type=F32)
            a = ab[:, 0:de]
            b = ab[:, de:2 * de]
            hmid = (a * _sigmoid(a)) * b
            y = jnp.dot(hmid.astype(BF16), wd_s[...], preferred_element_type=F32)
            ybuf[slot, 0:n_rows, :] = _pack_bf16_pairs(y)

        step = blk // EXPERT_BLOCK_PATHS
        for p in range(EXPERT_BLOCK_PATHS):
            lo_rows, hi_rows = p * step, (p + 1) * step
            last = p == EXPERT_BLOCK_PATHS - 1

            @pl.when((n_left > lo_rows) & ((n_left <= hi_rows) | last))
            def _(hi_rows=hi_rows):
                ffn(hi_rows)

        out_copy(g, slot).start(priority=1)
        return carry

    lax.fori_loop(0, n_blk, block, 0)

    @pl.when(e == n_exp - 1)
    def _():
        @pl.when(total >= 2)
        def _():
            out_copy(total - 2, lax.rem(total, 2)).wait()

        out_copy(total - 1, lax.rem(total + 1, 2)).wait()


def _experts(plan, xs, w_gate, w_up, w_down):
    n_slots, width = xs.shape
    n_exp, dm, de = w_gate.shape
    blk = EXPERT_BLOCK
    return pl.pallas_call(
        functools.partial(_experts_kernel, de=de, blk=blk),
        grid_spec=pltpu.PrefetchScalarGridSpec(
            num_scalar_prefetch=3,
            grid=(n_exp,),
            in_specs=[pl.BlockSpec(memory_space=pl.ANY),
                      pl.BlockSpec((1, dm, de), lambda e, f, n, c: (e, 0, 0)),
                      pl.BlockSpec((1, dm, de), lambda e, f, n, c: (e, 0, 0)),
                      pl.BlockSpec((1, de, dm), lambda e, f, n, c: (e, 0, 0))],
            out_specs=pl.BlockSpec(memory_space=pl.ANY),
            scratch_shapes=[pltpu.VMEM((dm, 2 * de), BF16), pltpu.VMEM((de, dm), BF16),
                            pltpu.VMEM((2, blk, width), I32), pltpu.VMEM((2, blk, width), I32),
                            pltpu.SemaphoreType.DMA((2,)), pltpu.SemaphoreType.DMA((2,))]),
        out_shape=jax.ShapeDtypeStruct((n_slots, width), I32),
        compiler_params=pltpu.CompilerParams(dimension_semantics=("arbitrary",),
                                             vmem_limit_bytes=VMEM_LIMIT),
        name="experts",
    )(*plan, xs, w_gate, w_up, w_down)


def _pick(onehot, table):
    return jnp.sum(jnp.where(onehot, table[None, :], 0), axis=1).astype(I32)


def _expert_plan(counts, n_assign):
    blk = EXPERT_BLOCK
    per_e = (counts + blk - 1) // blk
    first_blk = jnp.cumsum(per_e) - per_e
    n_slots = (n_assign // blk + N_EXPERTS) * blk
    return (first_blk * blk).astype(I32), (first_blk.astype(I32), per_e.astype(I32), counts), n_slots


def _slot_of(starts, expert, rank):
    onehot = jnp.arange(N_EXPERTS, dtype=I32)[None, :] == expert[:, None]
    return _pick(onehot, starts) + rank


def _combine_kernel(rows_ref, gate_ref, x1_ref, mod_ref, fg_ref, *rest):
    o_ref = rest[-1]
    tc = x1_ref.shape[0]
    gate_f = mod_ref[0][5:6]
    for q in range(tc // LANES):
        tok = slice(q * LANES, (q + 1) * LANES)
        g_rows = jnp.concatenate([gate_ref[:, tok], jnp.zeros((LANES - SUBLANES, LANES), F32)], axis=0)
        gc = g_rows.T
        y = (gc[:, 0:1] * _unpack_bf16_pairs(rows_ref[0, tok, :])
             + gc[:, 1:2] * _unpack_bf16_pairs(rows_ref[1, tok, :]))
        x2 = x1_ref[tok, :] + gate_f * y
        r = lax.rsqrt(jnp.mean(x2 * x2, axis=-1, keepdims=True) + EPS)
        o_ref[tok, :] = (x2 * r) * fg_ref[...]


def _combine(rows, gcol, x1, mod, final_g, seq, lb0, b0, nb, bsz, out_prev):
    dm = x1.shape[1]
    width = rows.shape[2]
    tc = min(COMBINE_TILE, seq)
    nst = seq // tc
    in_specs = [pl.BlockSpec((TOP_K, tc, width), lambda b, s: (0, b * nst + s, 0)),
                pl.BlockSpec((SUBLANES, tc), lambda b, s: (0, (b + lb0) * nst + s)),
                pl.BlockSpec((tc, dm), lambda b, s: ((b + lb0) * nst + s, 0)),
                pl.BlockSpec((1, 6, dm), lambda b, s: (b + b0, 0, 0)),
                pl.BlockSpec((1, dm), lambda b, s: (0, 0))]
    args = [rows, gcol, x1, mod, final_g.reshape(1, dm)]
    aliases = {}
    if out_prev is not None:
        in_specs.append(pl.BlockSpec(memory_space=pl.ANY))
        args.append(out_prev)
        aliases = {len(args) - 1: 0}
    return pl.pallas_call(
        _combine_kernel,
        grid=(nb, nst),
        in_specs=in_specs,
        out_specs=pl.BlockSpec((tc, dm), lambda b, s: ((b + b0) * nst + s, 0)),
        out_shape=jax.ShapeDtypeStruct((bsz * seq, dm), F32),
        input_output_aliases=aliases,
        compiler_params=pltpu.CompilerParams(dimension_semantics=("arbitrary", "arbitrary"),
                                             vmem_limit_bytes=VMEM_LIMIT),
        name="combine",
    )(*args)


def _layer(x, c, ada_w, ada_b, norm1_g, w_in, conv_w, conv_b, w_q, w_k, b_igate, b_fgate,
           mlstm_norm_g, mlstm_skip, w_pool, b_pool, pool_scale, w_out, norm2_g,
           w_rg, b_rg, w_re, b_re, w_eg, w_eu, w_ed, out_g):
    bsz, seq, dm = x.shape
    dml = conv_w.shape[1]
    n_tok = bsz * seq
    ts = min(SEQ_TILE, seq)

    mod = _ada(c, ada_w, ada_b).reshape(bsz, 6, dm)

    w_in_r, w_out_b = _prep_weights(w_in, w_out, dml)
    gbias = jnp.pad(jnp.concatenate([b_igate, b_fgate]), (0, LANES - 2 * N_HEADS)).reshape(1, LANES)
    wqk = jnp.concatenate([w_q, w_k], axis=-1).astype(BF16)
    wrt = jnp.zeros((LOGIT_ROWS, dm), F32)
    wrt = wrt.at[0:N_GROUPS].set(w_rg.T).at[SUBLANES:SUBLANES + N_EXPERTS].set(w_re.T).astype(BF16)
    rb = jnp.zeros((LOGIT_ROWS,), F32).at[0:N_GROUPS].set(b_rg).at[SUBLANES:SUBLANES + N_EXPERTS].set(b_re)
    rbias = jnp.broadcast_to(rb[:, None], (LOGIT_ROWS, ts))

    gdim = (dm - dml) // len(POOL_WINDOWS)
    win = jnp.repeat(jnp.array(POOL_WINDOWS, F32), gdim)[None, :]
    t1 = jnp.arange(1, ts + 1, dtype=F32)[:, None]
    pinv = jnp.stack([1.0 / jnp.minimum(t1, win), jnp.broadcast_to(1.0 / win, (ts, dm - dml))])

    mod = mod.at[:, 1].set(norm1_g * (1.0 + mod[:, 1])).at[:, 4].set(norm2_g * (1.0 + mod[:, 4]))

    mixer_params = (w_in_r, gbias, conv_w, conv_b.reshape(1, dml), wqk,
                    mlstm_norm_g.reshape(1, dml), mlstm_skip.reshape(1, dml), w_pool.astype(BF16),
                    b_pool.reshape(1, dm - dml), pool_scale.reshape(1, dm - dml), pinv,
                    w_out_b, wrt, rbias)

    x1, h2, lgt = _mixer(x, mod, *mixer_params, 0, bsz)
    idx, gcol, cnt = _route(lgt, n_tok)
    counts = cnt[:, 0].astype(I32)
    starts, plan, n_slots = _expert_plan(counts, n_tok * TOP_K)
    dest0 = _slot_of(starts, idx[0], idx[2])
    dest1 = _slot_of(starts, idx[1], idx[3])
    xs = _dispatch(h2, dest0, dest1, n_slots)
    ys = _experts(plan, xs, w_eg, w_eu, w_ed)

    nb = bsz // COMBINE_GROUPS
    assert bsz % COMBINE_GROUPS == 0
    out = None
    for b0 in range(0, bsz, nb):
        rows = _collect(ys, dest0, dest1, b0 * seq, nb * seq)
        out = _combine(rows, gcol, x1, mod, out_g, seq, b0, b0, nb, bsz, out)
    return out.reshape(bsz, seq, dm)


def kernel(x, c, ada_w, ada_b, norm1_g, w_in, conv_w, conv_b, w_q, w_k, b_igate, b_fgate, mlstm_norm_g, mlstm_skip, w_pool, b_pool, pool_scale, w_out, norm2_g, w_router_group, b_router_group, w_router_expert, b_router_expert, w_expert_gate, w_expert_up, w_expert_down, final_g):
    depth = ada_w.shape[0]
    assert depth == 1, "the final norm is fused into the last layer's combine kernel"
    l = 0
    return _layer(x, c, ada_w[l], ada_b[l], norm1_g[l], w_in[l], conv_w[l], conv_b[l], w_q[l],
                  w_k[l], b_igate[l], b_fgate[l], mlstm_norm_g[l], mlstm_skip[l], w_pool[l],
                  b_pool[l], pool_scale[l], w_out[l], norm2_g[l], w_router_group[l],
                  b_router_group[l], w_router_expert[l], b_router_expert[l],
                  w_expert_gate[l], w_expert_up[l], w_expert_down[l], final_g)
```

```python
import functools

import jax
import jax.numpy as jnp
from jax import lax
from jax.experimental import pallas as pl
from jax.experimental.pallas import tpu as pltpu
from jax.experimental.pallas import tpu_sc as plsc

F32 = jnp.float32
BF16 = jnp.bfloat16
I32 = jnp.int32
U32 = jnp.uint32

EPS = 1e-6
N_HEADS = 4
HEAD_DIM = 128
CONV_WIDTH = 4
POOL_WINDOWS = (2, 4, 8, 16)
N_GROUPS = 4
EXPERTS_PER_GROUP = 8
N_EXPERTS = N_GROUPS * EXPERTS_PER_GROUP
TOP_K = 2

LANES = 128
SUBLANES = 8
CHUNK = 128
SEQ_TILE = 1024
ADA_TILE = 1024
PREP_COLS = 512
COMBINE_GROUPS = 8
ROUTE_TILE = 512
ROUTE_SUBTILES = 16
SC_SCATTER_CHUNK = 64
SC_GATHER_CHUNK = 32
COMBINE_TILE = 1024
COMBINE_IN_SLOTS = 3
COMBINE_OUT_SLOTS = 2
EXPERT_BLOCK = 1024
EXPERT_BLOCK_PATHS = 8
LOGIT_ROWS = 48
UHIST = 8
PHIST = 16
VMEM_LIMIT = 60 * 1024 * 1024


def _sigmoid(x):
    return 1.0 / (1.0 + jnp.exp(-x))


def _pack_bf16_pairs(x):
    w = x.shape[1] // 2
    half_ulp = jnp.uint32(0x8000)
    hi = lax.bitcast_convert_type(x[:, :w], U32) + half_ulp
    lo = lax.bitcast_convert_type(x[:, w:], U32) + half_ulp
    return lax.bitcast_convert_type((hi & jnp.uint32(0xFFFF0000)) | (lo >> 16), I32)


def _unpack_bf16_pairs(words):
    u = lax.bitcast_convert_type(words, U32)
    hi = lax.bitcast_convert_type(u & jnp.uint32(0xFFFF0000), F32)
    lo = lax.bitcast_convert_type(u << 16, F32)
    return jnp.concatenate([hi, lo], axis=1)


def _ada_kernel(c_ref, w_ref, b_ref, o_ref):
    c = c_ref[...]
    s = c * _sigmoid(c)
    o_ref[...] = jnp.dot(s.astype(BF16), w_ref[...].astype(BF16),
                         preferred_element_type=F32) + b_ref[...]


def _ada(c, ada_w, ada_b):
    bsz, dm = c.shape
    n = ada_w.shape[1]
    tn = ADA_TILE
    return pl.pallas_call(
        _ada_kernel,
        grid=(n // tn,),
        in_specs=[pl.BlockSpec((bsz, dm), lambda j: (0, 0)),
                  pl.BlockSpec((dm, tn), lambda j: (0, j)),
                  pl.BlockSpec((1, tn), lambda j: (0, j))],
        out_specs=pl.BlockSpec((bsz, tn), lambda j: (0, j)),
        out_shape=jax.ShapeDtypeStruct((bsz, n), F32),
        compiler_params=pltpu.CompilerParams(dimension_semantics=("arbitrary",),
                                             vmem_limit_bytes=VMEM_LIMIT),
        name="ada",
    )(c, ada_w, ada_b.reshape(1, n))


def _prep_kernel(wint_ref, wout_ref, winr_ref, woutb_ref, *, dml, dp):
    col_v, col_o, col_i = dml, 2 * dml, 3 * dml
    col_p = col_i + 2 * N_HEADS
    kb = wint_ref.shape[1]
    winr_ref[:, 0:dml] = wint_ref[0:col_v, :].T.astype(BF16)
    winr_ref[:, dml:dml + dp] = wint_ref[col_p:col_p + dp, :].T.astype(BF16)
    winr_ref[:, dml + dp:2 * dml + dp] = wint_ref[col_v:col_o, :].T.astype(BF16)
    winr_ref[:, 2 * dml + dp:3 * dml + dp] = wint_ref[col_o:col_i, :].T.astype(BF16)
    gates = jnp.concatenate([wint_ref[col_i:col_p, :], jnp.zeros((LANES - 2 * N_HEADS, kb), F32)], axis=0)
    winr_ref[:, 3 * dml + dp:3 * dml + dp + LANES] = gates.T.astype(BF16)
    woutb_ref[...] = wout_ref[...].astype(BF16)


def _prep_weights(w_in, w_out, dml):
    dm, ncol = w_in.shape
    dp = ncol - 3 * dml - 2 * N_HEADS
    nout = 3 * dml + dp + LANES
    kb = PREP_COLS
    assert dm % kb == 0 and w_out.shape[1] % kb == 0 and 2 * N_HEADS == SUBLANES
    return pl.pallas_call(
        functools.partial(_prep_kernel, dml=dml, dp=dp),
        grid=(dm // kb,),
        in_specs=[pl.BlockSpec((ncol, kb), lambda i: (0, i)),
                  pl.BlockSpec((w_out.shape[0], kb), lambda i: (0, i))],
        out_specs=[pl.BlockSpec((kb, nout), lambda i: (i, 0)),
                   pl.BlockSpec((w_out.shape[0], kb), lambda i: (0, i))],
        out_shape=[jax.ShapeDtypeStruct((dm, nout), BF16),
                   jax.ShapeDtypeStruct(w_out.shape, BF16)],
        compiler_params=pltpu.CompilerParams(dimension_semantics=("arbitrary",),
                                             vmem_limit_bytes=VMEM_LIMIT),
        name="prep",
    )(w_in.T, w_out)


def _split3(x):
    hi = x.astype(BF16)
    r1 = x - hi.astype(F32)
    mid = r1.astype(BF16)
    lo = (r1 - mid.astype(F32)).astype(BF16)
    return hi, mid, lo


def _mixer_kernel(x_ref, mod_ref, win_ref, gbias_ref, convw_ref, convb_ref, wqk_ref,
                  ng_ref, skip_ref, wpool_ref, bpool_ref, pscale_ref, pinv_ref, wout_ref,
                  wrt_ref, rbias_ref,
                  x1_ref, h2_ref, lgt_ref,
                  uext_s, pext_s, proj_s, uc_s, qk_s, mix_s, pool4_s, ctv_s, ctn_s, mprev_s,
                  *, ts, dm, dml):
    s_idx = pl.program_id(1)
    n_chunks = ts // CHUNK
    dp = dm - dml

    @pl.when(s_idx == 0)
    def _():
        uext_s[0:UHIST, :] = jnp.zeros((UHIST, dml), F32)
        pext_s[0:PHIST, :] = jnp.zeros((PHIST, dp), F32)
        ctv_s[...] = jnp.zeros_like(ctv_s)
        ctn_s[...] = jnp.zeros_like(ctn_s)
        mprev_s[...] = jnp.zeros_like(mprev_s)

    row_i = lax.broadcasted_iota(I32, (CHUNK, CHUNK), 0)
    col_i = lax.broadcasted_iota(I32, (CHUNK, CHUNK), 1)
    causal = row_i >= col_i
    triu = jnp.where(row_i <= col_i, 1.0, 0.0).astype(BF16)
    ones_blk = jnp.ones((CHUNK, HEAD_DIM), BF16)
    q_scale = HEAD_DIM ** -0.5

    col_o = dml
    col_g = 2 * dml

    x = x_ref[0]
    mod = mod_ref[0]
    r = lax.rsqrt(jnp.mean(x * x, axis=-1, keepdims=True) + EPS)
    h = (x * r) * mod[1:2] + mod[0:1]
    res = jnp.dot(h.astype(BF16), win_ref[...], preferred_element_type=F32)
    uext_s[UHIST:, :] = res[:, 0:dml]
    pext_s[PHIST:, :] = res[:, dml:dm]
    proj_s[...] = res[:, dm:]

    acc = None
    for j in reversed(range(CONV_WIDTH)):
        tap = uext_s[pl.ds(UHIST - (CONV_WIDTH - 1 - j), ts), :] * convw_ref[j:j + 1, :]
        acc = tap if acc is None else acc + tap
    conv = acc + convb_ref[...]
    uc = conv * _sigmoid(conv)
    uc_s[...] = uc

    for hd in range(N_HEADS):
        qk = jnp.dot(uc[:, HEAD_DIM * hd:HEAD_DIM * (hd + 1)].astype(BF16), wqk_ref[hd],
                     preferred_element_type=F32)
        c0 = 2 * HEAD_DIM * hd
        qk_s[:, c0:c0 + HEAD_DIM] = qk[:, 0:HEAD_DIM] * q_scale
        qk_s[:, c0 + HEAD_DIM:c0 + 2 * HEAD_DIM] = qk[:, HEAD_DIM:2 * HEAD_DIM]

    pairs = [(c, hd) for c in range(n_chunks) for hd in range(N_HEADS)]
    g_rows, b_rows, b_cols = [], [], []
    pad_rows = jnp.zeros((CHUNK - SUBLANES, CHUNK), F32)
    for c in range(n_chunks):
        g_tile = proj_s[pl.ds(c * CHUNK, CHUNK), col_g:col_g + LANES] + gbias_ref[...]
        g_row = g_tile.T[0:SUBLANES, :]
        logf = -(jnp.maximum(-g_row, 0.0) + jnp.log1p(jnp.exp(-jnp.abs(g_row))))
        hi, mid, lo = _split3(logf)
        cs = jnp.dot(jnp.concatenate([hi, mid, lo], axis=0), triu, preferred_element_type=F32)
        b_row = cs[0:SUBLANES] + cs[SUBLANES:2 * SUBLANES] + cs[2 * SUBLANES:3 * SUBLANES]
        g_rows.append(g_row)
        b_rows.append(b_row)
        b_cols.append(jnp.concatenate([b_row, pad_rows], axis=0).T)

    def rows(ref, c, lo_col, width=HEAD_DIM):
        return ref[pl.ds(c * CHUNK, CHUNK), lo_col:lo_col + width]

    b_bcs, dlogs, rmaxs, p_mats = {}, {}, {}, {}
    for c, hd in pairs:
        b_bc = jnp.broadcast_to(b_cols[c][:, N_HEADS + hd:N_HEADS + hd + 1], (CHUNK, CHUNK))
        i_row = g_rows[c][hd:hd + 1, :]
        b_row = b_rows[c][N_HEADS + hd:N_HEADS + hd + 1, :]
        dlog = jnp.where(causal, (b_bc - b_row) + i_row, -jnp.inf)
        b_bcs[c, hd], dlogs[c, hd] = b_bc, dlog
        rmaxs[c, hd] = jnp.max(dlog, axis=-1, keepdims=True)
        q_c = rows(qk_s, c, 2 * HEAD_DIM * hd)
        k_c = rows(qk_s, c, 2 * HEAD_DIM * hd + HEAD_DIM)
        p_mats[c, hd] = lax.dot_general(q_c.astype(BF16), k_c.astype(BF16),
                                        (((1,), (1,)), ((), ())), preferred_element_type=F32)

    inters, m_ts = {}, {}
    for hd in range(N_HEADS):
        m_prev = mprev_s[hd]
        for c in range(n_chunks):
            inter = b_bcs[c, hd] + m_prev
            m_t = jnp.maximum(inter, rmaxs[c, hd])
            inters[c, hd], m_ts[c, hd] = inter, m_t
            m_prev = jnp.broadcast_to(m_t[CHUNK - 1:CHUNK, :], (CHUNK, LANES))
        mprev_s[hd] = m_prev

    lhs, v_augs, upds, a_prevs, e_negms = {}, {}, {}, {}, {}
    for c, hd in pairs:
        m_t = m_ts[c, hd]
        wm = jnp.exp(dlogs[c, hd] - m_t)
        a_inter = jnp.exp(inters[c, hd] - m_t)
        e_negms[c, hd] = jnp.exp(-m_t)
        q_c = rows(qk_s, c, 2 * HEAD_DIM * hd)
        k_c = rows(qk_s, c, 2 * HEAD_DIM * hd + HEAD_DIM)
        v_c = rows(proj_s, c, HEAD_DIM * hd)
        s_mat = (p_mats[c, hd] * wm).astype(BF16)
        qa = (q_c * a_inter).astype(BF16)
        lhs[c, hd] = jnp.concatenate([s_mat, qa], axis=1)
        v_aug = jnp.concatenate([v_c.astype(BF16), ones_blk], axis=1)
        v_augs[c, hd] = v_aug
        ktw = (k_c.T * wm[CHUNK - 1:CHUNK, :]).astype(BF16)
        upds[c, hd] = jnp.dot(ktw, v_aug, preferred_element_type=F32)
        a_prevs[c, hd] = a_inter[CHUNK - 1:CHUNK, :]

    ct_in = {}
    for hd in range(N_HEADS):
        ctv, ctn = ctv_s[hd], ctn_s[hd]
        for c in range(n_chunks):
            ct_in[c, hd] = jnp.concatenate([ctv, ctn], axis=1).astype(BF16)
            ctv = a_prevs[c, hd] * ctv + upds[c, hd][:, 0:HEAD_DIM]
            ctn = a_prevs[c, hd] * ctn + upds[c, hd][:, HEAD_DIM:2 * HEAD_DIM]
        ctv_s[hd], ctn_s[hd] = ctv, ctn

    for c, hd in pairs:
        numden = jnp.dot(lhs[c, hd], jnp.concatenate([v_augs[c, hd], ct_in[c, hd]], axis=0),
                         preferred_element_type=F32)
        num = numden[:, 0:HEAD_DIM]
        den = numden[:, HEAD_DIM:2 * HEAD_DIM]
        hh = num / jnp.maximum(jnp.abs(den), e_negms[c, hd])
        ms = jnp.mean(hh * hh, axis=-1, keepdims=True)
        hn = hh * lax.rsqrt(ms + EPS) * ng_ref[:, HEAD_DIM * hd:HEAD_DIM * (hd + 1)]
        o_c = rows(proj_s, c, col_o + HEAD_DIM * hd)
        uc_c = rows(uc_s, c, HEAD_DIM * hd)
        out_c = _sigmoid(o_c) * (hn + skip_ref[:, HEAD_DIM * hd:HEAD_DIM * (hd + 1)] * uc_c)
        mix_s[pl.ds(c * CHUNK, CHUNK), HEAD_DIM * hd:HEAD_DIM * (hd + 1)] = out_c.astype(BF16)

    def pe(shift, rows, lanes):
        return pext_s[pl.ds(PHIST - shift, rows), lanes]

    gd = LANES
    sums = []
    for gi in range(2):
        lanes = slice(gd * gi, gd * (gi + 1))
        tot = pe(0, ts, lanes)
        for j in range(1, POOL_WINDOWS[gi]):
            tot = tot + pe(j, ts, lanes)
        sums.append(tot)
    wide = slice(2 * gd, 4 * gd)
    s4 = pe(12, ts + 12, wide)
    for j in range(1, 4):
        s4 = s4 + pe(12 + j, ts + 12, wide)
    pool4_s[0:ts + 12, :] = s4
    s8 = pool4_s[pl.ds(4, ts + 8), :] + pool4_s[pl.ds(0, ts + 8), :]
    sums.append(s8[8:, 0:gd])
    sums.append(s8[8:, gd:2 * gd] + s8[0:ts, gd:2 * gd])
    for gi in range(len(POOL_WINDOWS)):
        lanes = slice(gd * gi, gd * (gi + 1))
        pooled = sums[gi] * pinv_ref[0, :, lanes] - pe(0, ts, lanes)
        yp = jnp.dot(pooled.astype(BF16), wpool_ref[gi], preferred_element_type=F32)
        yp = (yp + bpool_ref[:, lanes]) * pscale_ref[:, lanes]
        mix_s[:, dml + gd * gi:dml + gd * (gi + 1)] = yp.astype(BF16)

    mix = jnp.dot(mix_s[...], wout_ref[...], preferred_element_type=F32)
    x1 = x + mod[2:3] * mix
    x1_ref[...] = x1
    r2 = lax.rsqrt(jnp.mean(x1 * x1, axis=-1, keepdims=True) + EPS)
    h2 = (x1 * r2) * mod[4:5] + mod[3:4]
    h2_ref[...] = _pack_bf16_pairs(h2)
    lgt_ref[...] = lax.dot_general(wrt_ref[...], h2.astype(BF16), (((1,), (1,)), ((), ())),
                                   preferred_element_type=F32) + rbias_ref[...]

    uext_s[0:UHIST, :] = uext_s[ts:ts + UHIST, :]
    pext_s[0:PHIST, :] = pext_s[ts:ts + PHIST, :]


def _mixer(x, mod, w_in_r, gbias, conv_w, conv_b, wqk, ng, skip, wpool, bpool, pscale, pinv,
           w_out, wrt, rbias, b0, nb):
    _, seq, dm = x.shape
    dml = conv_w.shape[1]
    ts = min(SEQ_TILE, seq)
    ncols = w_in_r.shape[1]
    nst = seq // ts
    n_tok = nb * seq
    assert seq % ts == 0 and ts % CHUNK == 0
    full = lambda a: pl.BlockSpec(a.shape, lambda b, s: (0,) * a.ndim, pipeline_mode=pl.Buffered(1))
    kern = functools.partial(_mixer_kernel, ts=ts, dm=dm, dml=dml)
    return pl.pallas_call(
        kern,
        grid=(nb, nst),
        in_specs=[pl.BlockSpec((1, ts, dm), lambda b, s: (b + b0, s, 0)),
                  pl.BlockSpec((1, 6, dm), lambda b, s: (b + b0, 0, 0)),
                  full(w_in_r), full(gbias), full(conv_w), full(conv_b), full(wqk),
                  full(ng), full(skip), full(wpool), full(bpool), full(pscale),
                  pl.BlockSpec((1,) + pinv.shape[1:], lambda b, s: (jnp.minimum(s, 1), 0, 0)),
                  full(w_out), full(wrt), full(rbias)],
        out_specs=[pl.BlockSpec((ts, dm), lambda b, s: (b * nst + s, 0)),
                   pl.BlockSpec((ts, dm // 2), lambda b, s: (b * nst + s, 0)),
                   pl.BlockSpec((LOGIT_ROWS, ts), lambda b, s: (0, b * nst + s))],
        out_shape=[jax.ShapeDtypeStruct((n_tok, dm), F32),
                   jax.ShapeDtypeStruct((n_tok, dm // 2), I32),
                   jax.ShapeDtypeStruct((LOGIT_ROWS, n_tok), F32)],
        scratch_shapes=[pltpu.VMEM((UHIST + ts, dml), F32),
                        pltpu.VMEM((PHIST + ts, dm - dml), F32),
                        pltpu.VMEM((ts, ncols - dm), F32),
                        pltpu.VMEM((ts, dml), F32),
                        pltpu.VMEM((ts, 2 * dml), F32),
                        pltpu.VMEM((ts, dm), BF16),
                        pltpu.VMEM((ts + PHIST, 2 * LANES), F32),
                        pltpu.VMEM((N_HEADS, HEAD_DIM, HEAD_DIM), F32),
                        pltpu.VMEM((N_HEADS, HEAD_DIM, HEAD_DIM), F32),
                        pltpu.VMEM((N_HEADS, CHUNK, LANES), F32)],
        compiler_params=pltpu.CompilerParams(dimension_semantics=("arbitrary", "arbitrary"),
                                             vmem_limit_bytes=VMEM_LIMIT),
        name="mixer",
    )(x, mod, w_in_r, gbias, conv_w, conv_b, wqk, ng, skip, wpool, bpool, pscale, pinv, w_out,
      wrt, rbias)


def _route_kernel(lgt_ref, idx_ref, gate_ref, cnt_ref, carry_s, *, tr, n_sub):
    @pl.when(pl.program_id(0) == 0)
    def _():
        carry_s[...] = jnp.zeros_like(carry_s)

    tr_r = lax.broadcasted_iota(I32, (tr, tr), 0)
    tr_c = lax.broadcasted_iota(I32, (tr, tr), 1)
    upper = jnp.where(tr_r < tr_c, 1.0, 0.0).astype(BF16)
    for q in range(n_sub):
        cols = slice(q * tr, (q + 1) * tr)
        idx, gates = _route_tile(lgt_ref[:, cols], upper, carry_s, tr)
        idx_ref[:, cols] = idx
        gate_ref[:, cols] = gates
    cnt_ref[...] = carry_s[...]


def _route_tile(lg, upper, carry_s, tr):
    best = lg[0:1]
    gidx = jnp.zeros((1, tr), I32)
    for j in range(1, N_GROUPS):
        cand = lg[j:j + 1]
        better = cand > best
        gidx = jnp.where(better, j, gidx)
        best = jnp.where(better, cand, best)
    sumexp = jnp.zeros((1, tr), F32)
    for j in range(N_GROUPS):
        sumexp = sumexp + jnp.exp(lg[j:j + 1] - best)
    g_gate = 1.0 / sumexp

    sel = lg[SUBLANES:2 * SUBLANES]
    for j in range(1, N_GROUPS):
        sel = jnp.where(gidx == j, lg[SUBLANES * (j + 1):SUBLANES * (j + 2)], sel)
    sub = lax.broadcasted_iota(I32, (EXPERTS_PER_GROUP, tr), 0)
    v1 = jnp.max(sel, axis=0, keepdims=True)
    i1 = jnp.min(jnp.where(sel == v1, sub, EXPERTS_PER_GROUP), axis=0, keepdims=True)
    sel2 = jnp.where(sub == i1, -jnp.inf, sel)
    v2 = jnp.max(sel2, axis=0, keepdims=True)
    i2 = jnp.min(jnp.where(sel2 == v2, sub, EXPERTS_PER_GROUP), axis=0, keepdims=True)
    e2 = jnp.exp(v2 - v1)
    den = 1.0 + e2
    gate0 = (1.0 / den) * g_gate
    gate1 = (e2 / den) * g_gate
    ex0 = gidx * EXPERTS_PER_GROUP + i1
    ex1 = gidx * EXPERTS_PER_GROUP + i2

    erow = lax.broadcasted_iota(I32, (N_EXPERTS, tr), 0)
    oh0 = erow == ex0
    oh1 = erow == ex1
    oh = jnp.where(oh0 | oh1, 1.0, 0.0).astype(BF16)
    carry = carry_s[...]
    before = jnp.dot(oh, upper, preferred_element_type=F32)
    before = before + jnp.concatenate([carry] * (tr // LANES), axis=1)
    rank0 = jnp.sum(jnp.where(oh0, before, 0.0), axis=0, keepdims=True)
    rank1 = jnp.sum(jnp.where(oh1, before, 0.0), axis=0, keepdims=True)
    carry_s[...] = carry + jnp.dot(oh, jnp.ones((tr, LANES), BF16), preferred_element_type=F32)

    idx = jnp.concatenate([ex0, ex1, rank0.astype(I32), rank1.astype(I32),
                           jnp.zeros((SUBLANES - 4, tr), I32)], axis=0)
    gates = jnp.concatenate([gate0, gate1, jnp.zeros((SUBLANES - 2, tr), F32)], axis=0)
    return idx, gates


def _route(lgt, n_tok):
    tr = ROUTE_TILE
    n_sub = min(ROUTE_SUBTILES, n_tok // tr)
    step = tr * n_sub
    assert n_tok % step == 0
    return pl.pallas_call(
        functools.partial(_route_kernel, tr=tr, n_sub=n_sub),
        grid=(n_tok // step,),
        in_specs=[pl.BlockSpec((LOGIT_ROWS, step), lambda i: (0, i))],
        out_specs=[pl.BlockSpec((SUBLANES, step), lambda i: (0, i)),
                   pl.BlockSpec((SUBLANES, step), lambda i: (0, i)),
                   pl.BlockSpec((N_EXPERTS, LANES), lambda i: (0, 0))],
        out_shape=[jax.ShapeDtypeStruct((SUBLANES, n_tok), I32),
                   jax.ShapeDtypeStruct((SUBLANES, n_tok), F32),
                   jax.ShapeDtypeStruct((N_EXPERTS, LANES), F32)],
        scratch_shapes=[pltpu.VMEM((N_EXPERTS, LANES), F32)],
        compiler_params=pltpu.CompilerParams(dimension_semantics=("arbitrary",),
                                             vmem_limit_bytes=VMEM_LIMIT),
        name="route",
    )(lgt)


def _sc_workers():
    info = plsc.get_sparse_core_info()
    return info.num_cores, info.num_cores * info.num_subcores


def _dispatch(h2p, dest0, dest1, n_slots):
    n_tok, width = h2p.shape
    n_cores, n_workers = _sc_workers()
    per_w = n_tok // n_workers
    ch = min(SC_SCATTER_CHUNK, per_w)
    n_ch = per_w // ch
    assert n_tok % n_workers == 0 and per_w % ch == 0 and ch % SUBLANES == 0 and n_ch % 2 == 0
    mesh = plsc.VectorSubcoreMesh(core_axis_name="c", subcore_axis_name="s")

    @functools.partial(
        pl.kernel, mesh=mesh,
        out_type=jax.ShapeDtypeStruct((n_slots, width), h2p.dtype),
        scratch_types=[pltpu.VMEM((2, ch), I32), pltpu.VMEM((2, ch), I32),
                       pltpu.VMEM((2, ch, width), h2p.dtype),
                       pltpu.SemaphoreType.DMA((2,)), pltpu.SemaphoreType.DMA((2,))],
        name="dispatch")
    def scatter(h_hbm, d0_hbm, d1_hbm, xs_hbm, i0_v, i1_v, rows_v, sem_in, sem_out):
        wid = lax.axis_index("s") * n_cores + lax.axis_index("c")
        base = wid * per_w

        def loads(t0, slot):
            return (pltpu.make_async_copy(d0_hbm.at[pl.ds(t0, ch)], i0_v.at[slot], sem_in.at[slot]),
                    pltpu.make_async_copy(d1_hbm.at[pl.ds(t0, ch)], i1_v.at[slot], sem_in.at[slot]),
                    pltpu.make_async_copy(h_hbm.at[pl.ds(t0, ch)], rows_v.at[slot], sem_in.at[slot]))

        def scatters(slot):
            return (pltpu.make_async_copy(rows_v.at[slot], xs_hbm.at[i0_v.at[slot]], sem_out.at[slot]),
                    pltpu.make_async_copy(rows_v.at[slot], xs_hbm.at[i1_v.at[slot]], sem_out.at[slot]))

        for cp in loads(base, 0):
            cp.start()

        @pl.loop(0, n_ch, step=2)
        def _(k):
            for slot in range(2):
                for cp in loads(base + (k + slot) * ch, slot):
                    cp.wait()
                out_cps = scatters(slot)
                for cp in out_cps:
                    cp.start()
                nxt = k + slot + 1

                @pl.when(nxt < n_ch)
                def _():
                    for cp in loads(base + nxt * ch, 1 - slot):
                        cp.start()

                for cp in out_cps:
                    cp.wait()

    return scatter(h2p, dest0, dest1)


def _collect(ys, dest0, dest1, tok0, n_tok):
    width = ys.shape[1]
    n_cores, n_workers = _sc_workers()
    per_w = n_tok // n_workers
    ch = min(SC_GATHER_CHUNK, per_w)
    n_ch = per_w // ch
    assert n_tok % n_workers == 0 and per_w % ch == 0 and ch % SUBLANES == 0 and n_ch % 2 == 0
    mesh = plsc.VectorSubcoreMesh(core_axis_name="c", subcore_axis_name="s")

    @functools.partial(
        pl.kernel, mesh=mesh,
        out_type=jax.ShapeDtypeStruct((TOP_K, n_tok, width), ys.dtype),
        scratch_types=[pltpu.VMEM((2, ch), I32), pltpu.VMEM((2, ch), I32),
                       pltpu.VMEM((2, ch, width), ys.dtype), pltpu.VMEM((2, ch, width), ys.dtype),
                       pltpu.SemaphoreType.DMA((2,)), pltpu.SemaphoreType.DMA((2,)),
                       pltpu.SemaphoreType.DMA((2,))],
        name="collect")
    def gather(ys_hbm, d0_hbm, d1_hbm, o_hbm, i0_v, i1_v, r0_v, r1_v, sem_idx, sem_in, sem_out):
        wid = lax.axis_index("s") * n_cores + lax.axis_index("c")
        base = wid * per_w

        def idx_loads(t0, slot):
            return (pltpu.make_async_copy(d0_hbm.at[pl.ds(tok0 + t0, ch)], i0_v.at[slot], sem_idx.at[slot]),
                    pltpu.make_async_copy(d1_hbm.at[pl.ds(tok0 + t0, ch)], i1_v.at[slot], sem_idx.at[slot]))

        def gathers(slot):
            return (pltpu.make_async_copy(ys_hbm.at[i0_v.at[slot]], r0_v.at[slot], sem_in.at[slot]),
                    pltpu.make_async_copy(ys_hbm.at[i1_v.at[slot]], r1_v.at[slot], sem_in.at[slot]))

        def stores(t0, slot):
            return (pltpu.make_async_copy(r0_v.at[slot], o_hbm.at[0, pl.ds(t0, ch)], sem_out.at[slot]),
                    pltpu.make_async_copy(r1_v.at[slot], o_hbm.at[1, pl.ds(t0, ch)], sem_out.at[slot]))

        def start_chunk(t0, slot):
            for cp in idx_loads(t0, slot):
                cp.start()
            for cp in idx_loads(t0, slot):
                cp.wait()
            for cp in gathers(slot):
                cp.start()

        start_chunk(base, 0)

        @pl.loop(0, n_ch, step=2)
        def _(k):
            for slot in range(2):
                nxt = k + slot + 1

                @pl.when(nxt < n_ch)
                def _():
                    start_chunk(base + nxt * ch, 1 - slot)

                for cp in gathers(slot):
                    cp.wait()
                out_cps = stores(base + (k + slot) * ch, slot)
                for cp in out_cps:
                    cp.start()
                for cp in out_cps:
                    cp.wait()

    return gather(ys, dest0, dest1)


def _experts_kernel(first_ref, nblk_ref, cnt_ref, xs_hbm, wg_ref, wu_ref, wd_ref, ys_hbm,
                    wgu_s, wd_s, xbuf, ybuf, sem_in, sem_out, *, de, blk):
    e = pl.program_id(0)
    n_exp = pl.num_programs(0)
    wgu_s[:, 0:de] = wg_ref[0].astype(BF16)
    wgu_s[:, de:2 * de] = wu_ref[0].astype(BF16)
    wd_s[...] = wd_ref[0].astype(BF16)
    first = first_ref[e]
    n_blk = nblk_ref[e]
    count = cnt_ref[e]
    total = first_ref[n_exp - 1] + nblk_ref[n_exp - 1]

    def in_copy(g, slot):
        return pltpu.make_async_copy(xs_hbm.at[pl.ds(pl.multiple_of(g * blk, blk), blk)],
                                     xbuf.at[slot], sem_in.at[slot])

    def out_copy(g, slot):
        return pltpu.make_async_copy(ybuf.at[slot],
                                     ys_hbm.at[pl.ds(pl.multiple_of(g * blk, blk), blk)],
                                     sem_out.at[slot])

    @pl.when(e == 0)
    def _():
        in_copy(0, 0).start(priority=1)
        ybuf[...] = jnp.zeros_like(ybuf)

    def block(j, carry):
        g = first + j
        slot = lax.rem(g, 2)
        in_copy(g, slot).wait()

        @pl.when(g + 1 < total)
        def _():
            in_copy(g + 1, 1 - slot).start(priority=1)

        @pl.when(g >= 2)
        def _():
            out_copy(g - 2, slot).wait()

        n_left = count - j * blk

        def ffn(n_rows):
            words = xbuf[slot, 0:n_rows, :]
            rows = lax.broadcasted_iota(I32, words.shape, 0)
            xb = _unpack_bf16_pairs(jnp.where(rows < n_left, words, 0)).astype(BF16)
            ab = jnp.dot(xb, wgu_s[...], preferred_element_type=F32)
            a = ab[:, 0:de]
            b = ab[:, de:2 * de]
            hmid = (a * _sigmoid(a)) * b
            y = jnp.dot(hmid.astype(BF16), wd_s[...], preferred_element_type=F32)
            ybuf[slot, 0:n_rows, :] = _pack_bf16_pairs(y)

        step = blk // EXPERT_BLOCK_PATHS
        for p in range(EXPERT_BLOCK_PATHS):
            lo_rows, hi_rows = p * step, (p + 1) * step
            last = p == EXPERT_BLOCK_PATHS - 1

            @pl.when((n_left > lo_rows) & ((n_left <= hi_rows) | last))
            def _(hi_rows=hi_rows):
                ffn(hi_rows)

        out_copy(g, slot).start(priority=1)
        return carry

    lax.fori_loop(0, n_blk, block, 0)

    @pl.when(e == n_exp - 1)
    def _():
        @pl.when(total >= 2)
        def _():
            out_copy(total - 2, lax.rem(total, 2)).wait()

        out_copy(total - 1, lax.rem(total + 1, 2)).wait()


def _experts(plan, xs, w_gate, w_up, w_down):
    n_slots, width = xs.shape
    n_exp, dm, de = w_gate.shape
    blk = EXPERT_BLOCK
    return pl.pallas_call(
        functools.partial(_experts_kernel, de=de, blk=blk),
        grid_spec=pltpu.PrefetchScalarGridSpec(
            num_scalar_prefetch=3,
            grid=(n_exp,),
            in_specs=[pl.BlockSpec(memory_space=pl.ANY),
                      pl.BlockSpec((1, dm, de), lambda e, f, n, c: (e, 0, 0)),
                      pl.BlockSpec((1, dm, de), lambda e, f, n, c: (e, 0, 0)),
                      pl.BlockSpec((1, de, dm), lambda e, f, n, c: (e, 0, 0))],
            out_specs=pl.BlockSpec(memory_space=pl.ANY),
            scratch_shapes=[pltpu.VMEM((dm, 2 * de), BF16), pltpu.VMEM((de, dm), BF16),
                            pltpu.VMEM((2, blk, width), I32), pltpu.VMEM((2, blk, width), I32),
                            pltpu.SemaphoreType.DMA((2,)), pltpu.SemaphoreType.DMA((2,))]),
        out_shape=jax.ShapeDtypeStruct((n_slots, width), I32),
        compiler_params=pltpu.CompilerParams(dimension_semantics=("arbitrary",),
                                             vmem_limit_bytes=VMEM_LIMIT),
        name="experts",
    )(*plan, xs, w_gate, w_up, w_down)


def _pick(onehot, table):
    return jnp.sum(jnp.where(onehot, table[None, :], 0), axis=1).astype(I32)


def _expert_plan(counts, n_assign):
    blk = EXPERT_BLOCK
    per_e = (counts + blk - 1) // blk
    first_blk = jnp.cumsum(per_e) - per_e
    n_slots = (n_assign // blk + N_EXPERTS) * blk
    return (first_blk * blk).astype(I32), (first_blk.astype(I32), per_e.astype(I32), counts), n_slots


def _slot_of(starts, expert, rank):
    onehot = jnp.arange(N_EXPERTS, dtype=I32)[None, :] == expert[:, None]
    return _pick(onehot, starts) + rank


def _combine_kernel(gate_ref, mod_ref, fg_ref, rows_hbm, x1_hbm, *rest, tc, n_tiles, tiles_per_batch,
                    x_row0, o_row0):
    o_hbm, rbuf, xbuf, obuf, sem_r, sem_x, sem_o = rest[-7:]
    n_in, n_out = rbuf.shape[0], obuf.shape[0]

    def in_copies(t, slot):
        return (pltpu.make_async_copy(rows_hbm.at[:, pl.ds(t * tc, tc)], rbuf.at[slot], sem_r.at[slot]),
                pltpu.make_async_copy(x1_hbm.at[pl.ds(x_row0 + t * tc, tc)], xbuf.at[slot], sem_x.at[slot]))

    def out_copy(t, slot):
        return pltpu.make_async_copy(obuf.at[slot], o_hbm.at[pl.ds(o_row0 + t * tc, tc)], sem_o.at[slot])

    for t in range(min(n_in - 1, n_tiles)):
        for cp in in_copies(t, t % n_in):
            cp.start()
    for t in range(n_tiles):
        slot, oslot = t % n_in, t % n_out
        ahead = t + n_in - 1
        if ahead < n_tiles:
            for cp in in_copies(ahead, ahead % n_in):
                cp.start()
        for cp in in_copies(t, slot):
            cp.wait()
        if t >= n_out:
            out_copy(t - n_out, oslot).wait()
        gate_f = mod_ref[t // tiles_per_batch][5:6]
        for q in range(tc // LANES):
            tok = slice(q * LANES, (q + 1) * LANES)
            g_rows = jnp.concatenate([gate_ref[:, t * tc + q * LANES:t * tc + (q + 1) * LANES],
                                      jnp.zeros((LANES - SUBLANES, LANES), F32)], axis=0)
            gc = g_rows.T
            y = (gc[:, 0:1] * _unpack_bf16_pairs(rbuf[slot, 0, tok, :])
                 + gc[:, 1:2] * _unpack_bf16_pairs(rbuf[slot, 1, tok, :]))
            x2 = xbuf[slot, tok, :] + gate_f * y
            r = lax.rsqrt(jnp.mean(x2 * x2, axis=-1, keepdims=True) + EPS)
            obuf[oslot, tok, :] = (x2 * r) * fg_ref[...]
        out_copy(t, oslot).start()
    for t in range(max(n_tiles - n_out, 0), n_tiles):
        out_copy(t, t % n_out).wait()


def _combine(rows, gcol, x1, mod, final_g, seq, lb0, b0, nb, bsz, out_prev):
    dm = x1.shape[1]
    width = rows.shape[2]
    tc = min(COMBINE_TILE, seq)
    n_tok = nb * seq
    assert seq % tc == 0 and lb0 % nb == 0 and b0 % nb == 0
    any_spec = pl.BlockSpec(memory_space=pl.ANY)
    in_specs = [pl.BlockSpec((SUBLANES, n_tok), lambda i: (0, lb0 // nb)),
                pl.BlockSpec((nb, 6, dm), lambda i: (b0 // nb, 0, 0)),
                pl.BlockSpec((1, dm), lambda i: (0, 0)),
                any_spec, any_spec]
    args = [gcol, mod, final_g.reshape(1, dm), rows, x1]
    aliases = {}
    if out_prev is not None:
        in_specs.append(any_spec)
        args.append(out_prev)
        aliases = {len(args) - 1: 0}
    kern = functools.partial(_combine_kernel, tc=tc, n_tiles=n_tok // tc, tiles_per_batch=seq // tc,
                             x_row0=lb0 * seq, o_row0=b0 * seq)
    return pl.pallas_call(
        kern,
        grid=(1,),
        in_specs=in_specs,
        out_specs=any_spec,
        out_shape=jax.ShapeDtypeStruct((bsz * seq, dm), F32),
        scratch_shapes=[pltpu.VMEM((COMBINE_IN_SLOTS, TOP_K, tc, width), I32),
                        pltpu.VMEM((COMBINE_IN_SLOTS, tc, dm), F32),
                        pltpu.VMEM((COMBINE_OUT_SLOTS, tc, dm), F32),
                        pltpu.SemaphoreType.DMA((COMBINE_IN_SLOTS,)),
                        pltpu.SemaphoreType.DMA((COMBINE_IN_SLOTS,)),
                        pltpu.SemaphoreType.DMA((COMBINE_OUT_SLOTS,))],
        input_output_aliases=aliases,
        compiler_params=pltpu.CompilerParams(dimension_semantics=("arbitrary",),
                                             vmem_limit_bytes=VMEM_LIMIT),
        name="combine",
    )(*args)


def _layer(x, c, ada_w, ada_b, norm1_g, w_in, conv_w, conv_b, w_q, w_k, b_igate, b_fgate,
           mlstm_norm_g, mlstm_skip, w_pool, b_pool, pool_scale, w_out, norm2_g,
           w_rg, b_rg, w_re, b_re, w_eg, w_eu, w_ed, out_g):
    bsz, seq, dm = x.shape
    dml = conv_w.shape[1]
    n_tok = bsz * seq
    ts = min(SEQ_TILE, seq)

    mod = _ada(c, ada_w, ada_b).reshape(bsz, 6, dm)

    w_in_r, w_out_b = _prep_weights(w_in, w_out, dml)
    gbias = jnp.pad(jnp.concatenate([b_igate, b_fgate]), (0, LANES - 2 * N_HEADS)).reshape(1, LANES)
    wqk = jnp.concatenate([w_q, w_k], axis=-1).astype(BF16)
    wrt = jnp.zeros((LOGIT_ROWS, dm), F32)
    wrt = wrt.at[0:N_GROUPS].set(w_rg.T).at[SUBLANES:SUBLANES + N_EXPERTS].set(w_re.T).astype(BF16)
    rb = jnp.zeros((LOGIT_ROWS,), F32).at[0:N_GROUPS].set(b_rg).at[SUBLANES:SUBLANES + N_EXPERTS].set(b_re)
    rbias = jnp.broadcast_to(rb[:, None], (LOGIT_ROWS, ts))

    gdim = (dm - dml) // len(POOL_WINDOWS)
    win = jnp.repeat(jnp.array(POOL_WINDOWS, F32), gdim)[None, :]
    t1 = jnp.arange(1, ts + 1, dtype=F32)[:, None]
    pinv = jnp.stack([1.0 / jnp.minimum(t1, win), jnp.broadcast_to(1.0 / win, (ts, dm - dml))])

    mod = mod.at[:, 1].set(norm1_g * (1.0 + mod[:, 1])).at[:, 4].set(norm2_g * (1.0 + mod[:, 4]))

    mixer_params = (w_in_r, gbias, conv_w, conv_b.reshape(1, dml), wqk,
                    mlstm_norm_g.reshape(1, dml), mlstm_skip.reshape(1, dml), w_pool.astype(BF16),
                    b_pool.reshape(1, dm - dml), pool_scale.reshape(1, dm - dml), pinv,
                    w_out_b, wrt, rbias)

    x1, h2, lgt = _mixer(x, mod, *mixer_params, 0, bsz)
    idx, gcol, cnt = _route(lgt, n_tok)
    counts = cnt[:, 0].astype(I32)
    starts, plan, n_slots = _expert_plan(counts, n_tok * TOP_K)
    dest0 = _slot_of(starts, idx[0], idx[2])
    dest1 = _slot_of(starts, idx[1], idx[3])
    xs = _dispatch(h2, dest0, dest1, n_slots)
    ys = _experts(plan, xs, w_eg, w_eu, w_ed)

    nb = bsz // COMBINE_GROUPS
    assert bsz % COMBINE_GROUPS == 0
    out = None
    for b0 in range(0, bsz, nb):
        rows = _collect(ys, dest0, dest1, b0 * seq, nb * seq)
        out = _combine(rows, gcol, x1, mod, out_g, seq, b0, b0, nb, bsz, out)
    return out.reshape(bsz, seq, dm)


def kernel(x, c, ada_w, ada_b, norm1_g, w_in, conv_w, conv_b, w_q, w_k, b_igate, b_fgate, mlstm_norm_g, mlstm_skip, w_pool, b_pool, pool_scale, w_out, norm2_g, w_router_group, b_router_group, w_router_expert, b_router_expert, w_expert_gate, w_expert_up, w_expert_down, final_g):
    depth = ada_w.shape[0]
    assert depth == 1, "the final norm is fused into the last layer's combine kernel"
    l = 0
    return _layer(x, c, ada_w[l], ada_b[l], norm1_g[l], w_in[l], conv_w[l], conv_b[l], w_q[l],
                  w_k[l], b_igate[l], b_fgate[l], mlstm_norm_g[l], mlstm_skip[l], w_pool[l],
                  b_pool[l], pool_scale[l], w_out[l], norm2_g[l], w_router_group[l],
                  b_router_group[l], w_router_expert[l], b_router_expert[l],
                  w_expert_gate[l], w_expert_up[l], w_expert_down[l], final_g)
```

```python
import functools

import jax
import jax.numpy as jnp
from jax import lax
from jax.experimental import pallas as pl
from jax.experimental.pallas import tpu as pltpu
from jax.experimental.pallas import tpu_sc as plsc

F32 = jnp.float32
BF16 = jnp.bfloat16
I32 = jnp.int32
U32 = jnp.uint32

EPS = 1e-6
N_HEADS = 4
HEAD_DIM = 128
CONV_WIDTH = 4
POOL_WINDOWS = (2, 4, 8, 16)
N_GROUPS = 4
EXPERTS_PER_GROUP = 8
N_EXPERTS = N_GROUPS * EXPERTS_PER_GROUP
TOP_K = 2

LANES = 128
SUBLANES = 8
CHUNK = 128
SEQ_TILE = 1024
ADA_TILE = 1024
PREP_COLS = 512
COMBINE_GROUPS = 8
ROUTE_TILE = 512
ROUTE_SUBTILES = 16
SC_SCATTER_CHUNK = 64
SC_GATHER_CHUNK = 32
COMBINE_TILE = 2048
EXPERT_BLOCK = 1024
EXPERT_BLOCK_PATHS = 8
LOGIT_ROWS = 48
UHIST = 8
PHIST = 16
VMEM_LIMIT = 60 * 1024 * 1024


def _sigmoid(x):
    return 1.0 / (1.0 + jnp.exp(-x))


def _pack_bf16_pairs(x):
    w = x.shape[1] // 2
    half_ulp = jnp.uint32(0x8000)
    hi = lax.bitcast_convert_type(x[:, :w], U32) + half_ulp
    lo = lax.bitcast_convert_type(x[:, w:], U32) + half_ulp
    return lax.bitcast_convert_type((hi & jnp.uint32(0xFFFF0000)) | (lo >> 16), I32)


def _unpack_bf16_pairs(words):
    u = lax.bitcast_convert_type(words, U32)
    hi = lax.bitcast_convert_type(u & jnp.uint32(0xFFFF0000), F32)
    lo = lax.bitcast_convert_type(u << 16, F32)
    return jnp.concatenate([hi, lo], axis=1)


def _ada_kernel(c_ref, w_ref, b_ref, o_ref):
    c = c_ref[...]
    s = c * _sigmoid(c)
    o_ref[...] = jnp.dot(s.astype(BF16), w_ref[...].astype(BF16),
                         preferred_element_type=F32) + b_ref[...]


def _ada(c, ada_w, ada_b):
    bsz, dm = c.shape
    n = ada_w.shape[1]
    tn = ADA_TILE
    return pl.pallas_call(
        _ada_kernel,
        grid=(n // tn,),
        in_specs=[pl.BlockSpec((bsz, dm), lambda j: (0, 0)),
                  pl.BlockSpec((dm, tn), lambda j: (0, j)),
                  pl.BlockSpec((1, tn), lambda j: (0, j))],
        out_specs=pl.BlockSpec((bsz, tn), lambda j: (0, j)),
        out_shape=jax.ShapeDtypeStruct((bsz, n), F32),
        compiler_params=pltpu.CompilerParams(dimension_semantics=("arbitrary",),
                                             vmem_limit_bytes=VMEM_LIMIT),
        name="ada",
    )(c, ada_w, ada_b.reshape(1, n))


def _prep_kernel(wint_ref, wout_ref, winr_ref, woutb_ref, *, dml, dp):
    col_v, col_o, col_i = dml, 2 * dml, 3 * dml
    col_p = col_i + 2 * N_HEADS
    kb = wint_ref.shape[1]
    winr_ref[:, 0:dml] = wint_ref[0:col_v, :].T.astype(BF16)
    winr_ref[:, dml:dml + dp] = wint_ref[col_p:col_p + dp, :].T.astype(BF16)
    winr_ref[:, dml + dp:2 * dml + dp] = wint_ref[col_v:col_o, :].T.astype(BF16)
    winr_ref[:, 2 * dml + dp:3 * dml + dp] = wint_ref[col_o:col_i, :].T.astype(BF16)
    gates = jnp.concatenate([wint_ref[col_i:col_p, :], jnp.zeros((LANES - 2 * N_HEADS, kb), F32)], axis=0)
    winr_ref[:, 3 * dml + dp:3 * dml + dp + LANES] = gates.T.astype(BF16)
    woutb_ref[...] = wout_ref[...].astype(BF16)


def _prep_weights(w_in, w_out, dml):
    dm, ncol = w_in.shape
    dp = ncol - 3 * dml - 2 * N_HEADS
    nout = 3 * dml + dp + LANES
    kb = PREP_COLS
    assert dm % kb == 0 and w_out.shape[1] % kb == 0 and 2 * N_HEADS == SUBLANES
    return pl.pallas_call(
        functools.partial(_prep_kernel, dml=dml, dp=dp),
        grid=(dm // kb,),
        in_specs=[pl.BlockSpec((ncol, kb), lambda i: (0, i)),
                  pl.BlockSpec((w_out.shape[0], kb), lambda i: (0, i))],
        out_specs=[pl.BlockSpec((kb, nout), lambda i: (i, 0)),
                   pl.BlockSpec((w_out.shape[0], kb), lambda i: (0, i))],
        out_shape=[jax.ShapeDtypeStruct((dm, nout), BF16),
                   jax.ShapeDtypeStruct(w_out.shape, BF16)],
        compiler_params=pltpu.CompilerParams(dimension_semantics=("arbitrary",),
                                             vmem_limit_bytes=VMEM_LIMIT),
        name="prep",
    )(w_in.T, w_out)


def _split3(x):
    hi = x.astype(BF16)
    r1 = x - hi.astype(F32)
    mid = r1.astype(BF16)
    lo = (r1 - mid.astype(F32)).astype(BF16)
    return hi, mid, lo


def _mixer_kernel(x_ref, mod_ref, win_ref, gbias_ref, convw_ref, convb_ref, wqk_ref,
                  ng_ref, skip_ref, wpool_ref, bpool_ref, pscale_ref, pinv_ref, wout_ref,
                  wrt_ref, rbias_ref,
                  x1_ref, h2_ref, lgt_ref,
                  uext_s, pext_s, proj_s, uc_s, qk_s, mix_s, pool4_s, ctv_s, ctn_s, mprev_s,
                  *, ts, dm, dml):
    s_idx = pl.program_id(1)
    n_chunks = ts // CHUNK
    dp = dm - dml

    @pl.when(s_idx == 0)
    def _():
        uext_s[0:UHIST, :] = jnp.zeros((UHIST, dml), F32)
        pext_s[0:PHIST, :] = jnp.zeros((PHIST, dp), F32)
        ctv_s[...] = jnp.zeros_like(ctv_s)
        ctn_s[...] = jnp.zeros_like(ctn_s)
        mprev_s[...] = jnp.zeros_like(mprev_s)

    row_i = lax.broadcasted_iota(I32, (CHUNK, CHUNK), 0)
    col_i = lax.broadcasted_iota(I32, (CHUNK, CHUNK), 1)
    causal = row_i >= col_i
    triu = jnp.where(row_i <= col_i, 1.0, 0.0).astype(BF16)
    ones_blk = jnp.ones((CHUNK, HEAD_DIM), BF16)
    q_scale = HEAD_DIM ** -0.5

    col_o = dml
    col_g = 2 * dml

    x = x_ref[0]
    mod = mod_ref[0]
    r = lax.rsqrt(jnp.mean(x * x, axis=-1, keepdims=True) + EPS)
    h = (x * r) * mod[1:2] + mod[0:1]
    res = jnp.dot(h.astype(BF16), win_ref[...], preferred_element_type=F32)
    uext_s[UHIST:, :] = res[:, 0:dml]
    pext_s[PHIST:, :] = res[:, dml:dm]
    proj_s[...] = res[:, dm:]

    acc = None
    for j in reversed(range(CONV_WIDTH)):
        tap = uext_s[pl.ds(UHIST - (CONV_WIDTH - 1 - j), ts), :] * convw_ref[j:j + 1, :]
        acc = tap if acc is None else acc + tap
    conv = acc + convb_ref[...]
    uc = conv * _sigmoid(conv)
    uc_s[...] = uc

    def pe(shift, rows, lanes):
        return pext_s[pl.ds(PHIST - shift, rows), lanes]

    gd = LANES
    sums = []
    for gi in range(2):
        lanes = slice(gd * gi, gd * (gi + 1))
        tot = pe(0, ts, lanes)
        for j in range(1, POOL_WINDOWS[gi]):
            tot = tot + pe(j, ts, lanes)
        sums.append(tot)
    wide = slice(2 * gd, 4 * gd)
    s4 = pe(12, ts + 12, wide)
    for j in range(1, 4):
        s4 = s4 + pe(12 + j, ts + 12, wide)
    pool4_s[0:ts + 12, :] = s4
    s8 = pool4_s[pl.ds(4, ts + 8), :] + pool4_s[pl.ds(0, ts + 8), :]
    sums.append(s8[8:, 0:gd])
    sums.append(s8[8:, gd:2 * gd] + s8[0:ts, gd:2 * gd])
    for gi in range(len(POOL_WINDOWS)):
        lanes = slice(gd * gi, gd * (gi + 1))
        pooled = sums[gi] * pinv_ref[0, :, lanes] - pe(0, ts, lanes)
        yp = jnp.dot(pooled.astype(BF16), wpool_ref[gi], preferred_element_type=F32)
        yp = (yp + bpool_ref[:, lanes]) * pscale_ref[:, lanes]
        mix_s[:, dml + gd * gi:dml + gd * (gi + 1)] = yp.astype(BF16)

    for hd in range(N_HEADS):
        qk = jnp.dot(uc[:, HEAD_DIM * hd:HEAD_DIM * (hd + 1)].astype(BF16), wqk_ref[hd],
                     preferred_element_type=F32)
        c0 = 2 * HEAD_DIM * hd
        qk_s[:, c0:c0 + HEAD_DIM] = qk[:, 0:HEAD_DIM] * q_scale
        qk_s[:, c0 + HEAD_DIM:c0 + 2 * HEAD_DIM] = qk[:, HEAD_DIM:2 * HEAD_DIM]

    pairs = [(c, hd) for c in range(n_chunks) for hd in range(N_HEADS)]
    g_rows, b_rows, b_cols = [], [], []
    pad_rows = jnp.zeros((CHUNK - SUBLANES, CHUNK), F32)
    for c in range(n_chunks):
        g_tile = proj_s[pl.ds(c * CHUNK, CHUNK), col_g:col_g + LANES] + gbias_ref[...]
        g_row = g_tile.T[0:SUBLANES, :]
        logf = -(jnp.maximum(-g_row, 0.0) + jnp.log1p(jnp.exp(-jnp.abs(g_row))))
        hi, mid, lo = _split3(logf)
        cs = jnp.dot(jnp.concatenate([hi, mid, lo], axis=0), triu, preferred_element_type=F32)
        b_row = cs[0:SUBLANES] + cs[SUBLANES:2 * SUBLANES] + cs[2 * SUBLANES:3 * SUBLANES]
        g_rows.append(g_row)
        b_rows.append(b_row)
        b_cols.append(jnp.concatenate([b_row, pad_rows], axis=0).T)

    def rows(ref, c, lo_col, width=HEAD_DIM):
        return ref[pl.ds(c * CHUNK, CHUNK), lo_col:lo_col + width]

    b_bcs, dlogs, rmaxs, p_mats = {}, {}, {}, {}
    for c, hd in pairs:
        b_bc = jnp.broadcast_to(b_cols[c][:, N_HEADS + hd:N_HEADS + hd + 1], (CHUNK, CHUNK))
        i_row = g_rows[c][hd:hd + 1, :]
        b_row = b_rows[c][N_HEADS + hd:N_HEADS + hd + 1, :]
        dlog = jnp.where(causal, (b_bc - b_row) + i_row, -jnp.inf)
        b_bcs[c, hd], dlogs[c, hd] = b_bc, dlog
        rmaxs[c, hd] = jnp.max(dlog, axis=-1, keepdims=True)
        q_c = rows(qk_s, c, 2 * HEAD_DIM * hd)
        k_c = rows(qk_s, c, 2 * HEAD_DIM * hd + HEAD_DIM)
        p_mats[c, hd] = lax.dot_general(q_c.astype(BF16), k_c.astype(BF16),
                                        (((1,), (1,)), ((), ())), preferred_element_type=F32)

    inters, m_ts = {}, {}
    for hd in range(N_HEADS):
        m_prev = mprev_s[hd]
        for c in range(n_chunks):
            inter = b_bcs[c, hd] + m_prev
            m_t = jnp.maximum(inter, rmaxs[c, hd])
            inters[c, hd], m_ts[c, hd] = inter, m_t
            m_prev = jnp.broadcast_to(m_t[CHUNK - 1:CHUNK, :], (CHUNK, LANES))
        mprev_s[hd] = m_prev

    lhs, v_augs, upds, a_prevs, e_negms = {}, {}, {}, {}, {}
    for c, hd in pairs:
        m_t = m_ts[c, hd]
        wm = jnp.exp(dlogs[c, hd] - m_t)
        a_inter = jnp.exp(inters[c, hd] - m_t)
        e_negms[c, hd] = jnp.exp(-m_t)
        q_c = rows(qk_s, c, 2 * HEAD_DIM * hd)
        k_c = rows(qk_s, c, 2 * HEAD_DIM * hd + HEAD_DIM)
        v_c = rows(proj_s, c, HEAD_DIM * hd)
        s_mat = (p_mats[c, hd] * wm).astype(BF16)
        qa = (q_c * a_inter).astype(BF16)
        lhs[c, hd] = jnp.concatenate([s_mat, qa], axis=1)
        v_aug = jnp.concatenate([v_c.astype(BF16), ones_blk], axis=1)
        v_augs[c, hd] = v_aug
        ktw = (k_c.T * wm[CHUNK - 1:CHUNK, :]).astype(BF16)
        upds[c, hd] = jnp.dot(ktw, v_aug, preferred_element_type=F32)
        a_prevs[c, hd] = a_inter[CHUNK - 1:CHUNK, :]

    ct_in = {}
    for hd in range(N_HEADS):
        ctv, ctn = ctv_s[hd], ctn_s[hd]
        for c in range(n_chunks):
            ct_in[c, hd] = jnp.concatenate([ctv, ctn], axis=1).astype(BF16)
            ctv = a_prevs[c, hd] * ctv + upds[c, hd][:, 0:HEAD_DIM]
            ctn = a_prevs[c, hd] * ctn + upds[c, hd][:, HEAD_DIM:2 * HEAD_DIM]
        ctv_s[hd], ctn_s[hd] = ctv, ctn

    for c, hd in pairs:
        numden = jnp.dot(lhs[c, hd], jnp.concatenate([v_augs[c, hd], ct_in[c, hd]], axis=0),
                         preferred_element_type=F32)
        num = numden[:, 0:HEAD_DIM]
        den = numden[:, HEAD_DIM:2 * HEAD_DIM]
        hh = num / jnp.maximum(jnp.abs(den), e_negms[c, hd])
        ms = jnp.mean(hh * hh, axis=-1, keepdims=True)
        hn = hh * lax.rsqrt(ms + EPS) * ng_ref[:, HEAD_DIM * hd:HEAD_DIM * (hd + 1)]
        o_c = rows(proj_s, c, col_o + HEAD_DIM * hd)
        uc_c = rows(uc_s, c, HEAD_DIM * hd)
        out_c = _sigmoid(o_c) * (hn + skip_ref[:, HEAD_DIM * hd:HEAD_DIM * (hd + 1)] * uc_c)
        mix_s[pl.ds(c * CHUNK, CHUNK), HEAD_DIM * hd:HEAD_DIM * (hd + 1)] = out_c.astype(BF16)

    mix = jnp.dot(mix_s[...], wout_ref[...], preferred_element_type=F32)
    x1 = x + mod[2:3] * mix
    x1_ref[...] = x1
    r2 = lax.rsqrt(jnp.mean(x1 * x1, axis=-1, keepdims=True) + EPS)
    h2 = (x1 * r2) * mod[4:5] + mod[3:4]
    h2_ref[...] = _pack_bf16_pairs(h2)
    lgt_ref[...] = lax.dot_general(wrt_ref[...], h2.astype(BF16), (((1,), (1,)), ((), ())),
                                   preferred_element_type=F32) + rbias_ref[...]

    uext_s[0:UHIST, :] = uext_s[ts:ts + UHIST, :]
    pext_s[0:PHIST, :] = pext_s[ts:ts + PHIST, :]


def _mixer(x, mod, w_in_r, gbias, conv_w, conv_b, wqk, ng, skip, wpool, bpool, pscale, pinv,
           w_out, wrt, rbias, b0, nb):
    _, seq, dm = x.shape
    dml = conv_w.shape[1]
    ts = min(SEQ_TILE, seq)
    ncols = w_in_r.shape[1]
    nst = seq // ts
    n_tok = nb * seq
    assert seq % ts == 0 and ts % CHUNK == 0
    full = lambda a: pl.BlockSpec(a.shape, lambda b, s: (0,) * a.ndim, pipeline_mode=pl.Buffered(1))
    kern = functools.partial(_mixer_kernel, ts=ts, dm=dm, dml=dml)
    return pl.pallas_call(
        kern,
        grid=(nb, nst),
        in_specs=[pl.BlockSpec((1, ts, dm), lambda b, s: (b + b0, s, 0)),
                  pl.BlockSpec((1, 6, dm), lambda b, s: (b + b0, 0, 0)),
                  full(w_in_r), full(gbias), full(conv_w), full(conv_b), full(wqk),
                  full(ng), full(skip), full(wpool), full(bpool), full(pscale),
                  pl.BlockSpec((1,) + pinv.shape[1:], lambda b, s: (jnp.minimum(s, 1), 0, 0)),
                  full(w_out), full(wrt), full(rbias)],
        out_specs=[pl.BlockSpec((ts, dm), lambda b, s: (b * nst + s, 0)),
                   pl.BlockSpec((ts, dm // 2), lambda b, s: (b * nst + s, 0)),
                   pl.BlockSpec((LOGIT_ROWS, ts), lambda b, s: (0, b * nst + s))],
        out_shape=[jax.ShapeDtypeStruct((n_tok, dm), F32),
                   jax.ShapeDtypeStruct((n_tok, dm // 2), I32),
                   jax.ShapeDtypeStruct((LOGIT_ROWS, n_tok), F32)],
        scratch_shapes=[pltpu.VMEM((UHIST + ts, dml), F32),
                        pltpu.VMEM((PHIST + ts, dm - dml), F32),
                        pltpu.VMEM((ts, ncols - dm), F32),
                        pltpu.VMEM((ts, dml), F32),
                        pltpu.VMEM((ts, 2 * dml), F32),
                        pltpu.VMEM((ts, dm), BF16),
                        pltpu.VMEM((ts + PHIST, 2 * LANES), F32),
                        pltpu.VMEM((N_HEADS, HEAD_DIM, HEAD_DIM), F32),
                        pltpu.VMEM((N_HEADS, HEAD_DIM, HEAD_DIM), F32),
                        pltpu.VMEM((N_HEADS, CHUNK, LANES), F32)],
        compiler_params=pltpu.CompilerParams(dimension_semantics=("arbitrary", "arbitrary"),
                                             vmem_limit_bytes=VMEM_LIMIT),
        name="mixer",
    )(x, mod, w_in_r, gbias, conv_w, conv_b, wqk, ng, skip, wpool, bpool, pscale, pinv, w_out,
      wrt, rbias)


def _route_kernel(lgt_ref, idx_ref, gate_ref, cnt_ref, carry_s, *, tr, n_sub):
    @pl.when(pl.program_id(0) == 0)
    def _():
        carry_s[...] = jnp.zeros_like(carry_s)

    tr_r = lax.broadcasted_iota(I32, (tr, tr), 0)
    tr_c = lax.broadcasted_iota(I32, (tr, tr), 1)
    upper = jnp.where(tr_r < tr_c, 1.0, 0.0).astype(BF16)
    for q in range(n_sub):
        cols = slice(q * tr, (q + 1) * tr)
        idx, gates = _route_tile(lgt_ref[:, cols], upper, carry_s, tr)
        idx_ref[:, cols] = idx
        gate_ref[:, cols] = gates
    cnt_ref[...] = carry_s[...]


def _route_tile(lg, upper, carry_s, tr):
    best = lg[0:1]
    gidx = jnp.zeros((1, tr), I32)
    for j in range(1, N_GROUPS):
        cand = lg[j:j + 1]
        better = cand > best
        gidx = jnp.where(better, j, gidx)
        best = jnp.where(better, cand, best)
    sumexp = jnp.zeros((1, tr), F32)
    for j in range(N_GROUPS):
        sumexp = sumexp + jnp.exp(lg[j:j + 1] - best)
    g_gate = 1.0 / sumexp

    sel = lg[SUBLANES:2 * SUBLANES]
    for j in range(1, N_GROUPS):
        sel = jnp.where(gidx == j, lg[SUBLANES * (j + 1):SUBLANES * (j + 2)], sel)
    sub = lax.broadcasted_iota(I32, (EXPERTS_PER_GROUP, tr), 0)
    v1 = jnp.max(sel, axis=0, keepdims=True)
    i1 = jnp.min(jnp.where(sel == v1, sub, EXPERTS_PER_GROUP), axis=0, keepdims=True)
    sel2 = jnp.where(sub == i1, -jnp.inf, sel)
    v2 = jnp.max(sel2, axis=0, keepdims=True)
    i2 = jnp.min(jnp.where(sel2 == v2, sub, EXPERTS_PER_GROUP), axis=0, keepdims=True)
    e2 = jnp.exp(v2 - v1)
    den = 1.0 + e2
    gate0 = (1.0 / den) * g_gate
    gate1 = (e2 / den) * g_gate
    ex0 = gidx * EXPERTS_PER_GROUP + i1
    ex1 = gidx * EXPERTS_PER_GROUP + i2

    erow = lax.broadcasted_iota(I32, (N_EXPERTS, tr), 0)
    oh0 = erow == ex0
    oh1 = erow == ex1
    oh = jnp.where(oh0 | oh1, 1.0, 0.0).astype(BF16)
    carry = carry_s[...]
    before = jnp.dot(oh, upper, preferred_element_type=F32)
    before = before + jnp.concatenate([carry] * (tr // LANES), axis=1)
    rank0 = jnp.sum(jnp.where(oh0, before, 0.0), axis=0, keepdims=True)
    rank1 = jnp.sum(jnp.where(oh1, before, 0.0), axis=0, keepdims=True)
    carry_s[...] = carry + jnp.dot(oh, jnp.ones((tr, LANES), BF16), preferred_element_type=F32)

    idx = jnp.concatenate([ex0, ex1, rank0.astype(I32), rank1.astype(I32),
                           jnp.zeros((SUBLANES - 4, tr), I32)], axis=0)
    gates = jnp.concatenate([gate0, gate1, jnp.zeros((SUBLANES - 2, tr), F32)], axis=0)
    return idx, gates


def _route(lgt, n_tok):
    tr = ROUTE_TILE
    n_sub = min(ROUTE_SUBTILES, n_tok // tr)
    step = tr * n_sub
    assert n_tok % step == 0
    return pl.pallas_call(
        functools.partial(_route_kernel, tr=tr, n_sub=n_sub),
        grid=(n_tok // step,),
        in_specs=[pl.BlockSpec((LOGIT_ROWS, step), lambda i: (0, i))],
        out_specs=[pl.BlockSpec((SUBLANES, step), lambda i: (0, i)),
                   pl.BlockSpec((SUBLANES, step), lambda i: (0, i)),
                   pl.BlockSpec((N_EXPERTS, LANES), lambda i: (0, 0))],
        out_shape=[jax.ShapeDtypeStruct((SUBLANES, n_tok), I32),
                   jax.ShapeDtypeStruct((SUBLANES, n_tok), F32),
                   jax.ShapeDtypeStruct((N_EXPERTS, LANES), F32)],
        scratch_shapes=[pltpu.VMEM((N_EXPERTS, LANES), F32)],
        compiler_params=pltpu.CompilerParams(dimension_semantics=("arbitrary",),
                                             vmem_limit_bytes=VMEM_LIMIT),
        name="route",
    )(lgt)


def _sc_workers():
    info = plsc.get_sparse_core_info()
    return info.num_cores, info.num_cores * info.num_subcores


def _dispatch(h2p, dest0, dest1, n_slots):
    n_tok, width = h2p.shape
    n_cores, n_workers = _sc_workers()
    per_w = n_tok // n_workers
    ch = min(SC_SCATTER_CHUNK, per_w)
    n_ch = per_w // ch
    assert n_tok % n_workers == 0 and per_w % ch == 0 and ch % SUBLANES == 0 and n_ch % 2 == 0
    mesh = plsc.VectorSubcoreMesh(core_axis_name="c", subcore_axis_name="s")

    @functools.partial(
        pl.kernel, mesh=mesh,
        out_type=jax.ShapeDtypeStruct((n_slots, width), h2p.dtype),
        scratch_types=[pltpu.VMEM((2, ch), I32), pltpu.VMEM((2, ch), I32),
                       pltpu.VMEM((2, ch, width), h2p.dtype),
                       pltpu.SemaphoreType.DMA((2,)), pltpu.SemaphoreType.DMA((2,))],
        name="dispatch")
    def scatter(h_hbm, d0_hbm, d1_hbm, xs_hbm, i0_v, i1_v, rows_v, sem_in, sem_out):
        wid = lax.axis_index("s") * n_cores + lax.axis_index("c")
        base = wid * per_w

        def loads(t0, slot):
            return (pltpu.make_async_copy(d0_hbm.at[pl.ds(t0, ch)], i0_v.at[slot], sem_in.at[slot]),
                    pltpu.make_async_copy(d1_hbm.at[pl.ds(t0, ch)], i1_v.at[slot], sem_in.at[slot]),
                    pltpu.make_async_copy(h_hbm.at[pl.ds(t0, ch)], rows_v.at[slot], sem_in.at[slot]))

        def scatters(slot):
            return (pltpu.make_async_copy(rows_v.at[slot], xs_hbm.at[i0_v.at[slot]], sem_out.at[slot]),
                    pltpu.make_async_copy(rows_v.at[slot], xs_hbm.at[i1_v.at[slot]], sem_out.at[slot]))

        for cp in loads(base, 0):
            cp.start()

        @pl.loop(0, n_ch, step=2)
        def _(k):
            for slot in range(2):
                for cp in loads(base + (k + slot) * ch, slot):
                    cp.wait()
                out_cps = scatters(slot)
                for cp in out_cps:
                    cp.start()
                nxt = k + slot + 1

                @pl.when(nxt < n_ch)
                def _():
                    for cp in loads(base + nxt * ch, 1 - slot):
                        cp.start()

                for cp in out_cps:
                    cp.wait()

    return scatter(h2p, dest0, dest1)


def _collect(ys, dest0, dest1, tok0, n_tok):
    width = ys.shape[1]
    n_cores, n_workers = _sc_workers()
    per_w = n_tok // n_workers
    ch = min(SC_GATHER_CHUNK, per_w)
    n_ch = per_w // ch
    assert n_tok % n_workers == 0 and per_w % ch == 0 and ch % SUBLANES == 0 and n_ch % 2 == 0
    mesh = plsc.VectorSubcoreMesh(core_axis_name="c", subcore_axis_name="s")

    @functools.partial(
        pl.kernel, mesh=mesh,
        out_type=jax.ShapeDtypeStruct((TOP_K, n_tok, width), ys.dtype),
        scratch_types=[pltpu.VMEM((2, ch), I32), pltpu.VMEM((2, ch), I32),
                       pltpu.VMEM((2, ch, width), ys.dtype), pltpu.VMEM((2, ch, width), ys.dtype),
                       pltpu.SemaphoreType.DMA((2,)), pltpu.SemaphoreType.DMA((2,)),
                       pltpu.SemaphoreType.DMA((2,))],
        name="collect")
    def gather(ys_hbm, d0_hbm, d1_hbm, o_hbm, i0_v, i1_v, r0_v, r1_v, sem_idx, sem_in, sem_out):
        wid = lax.axis_index("s") * n_cores + lax.axis_index("c")
        base = wid * per_w

        def idx_loads(t0, slot):
            return (pltpu.make_async_copy(d0_hbm.at[pl.ds(tok0 + t0, ch)], i0_v.at[slot], sem_idx.at[slot]),
                    pltpu.make_async_copy(d1_hbm.at[pl.ds(tok0 + t0, ch)], i1_v.at[slot], sem_idx.at[slot]))

        def gathers(slot):
            return (pltpu.make_async_copy(ys_hbm.at[i0_v.at[slot]], r0_v.at[slot], sem_in.at[slot]),
                    pltpu.make_async_copy(ys_hbm.at[i1_v.at[slot]], r1_v.at[slot], sem_in.at[slot]))

        def stores(t0, slot):
            return (pltpu.make_async_copy(r0_v.at[slot], o_hbm.at[0, pl.ds(t0, ch)], sem_out.at[slot]),
                    pltpu.make_async_copy(r1_v.at[slot], o_hbm.at[1, pl.ds(t0, ch)], sem_out.at[slot]))

        def start_chunk(t0, slot):
            for cp in idx_loads(t0, slot):
                cp.start()
            for cp in idx_loads(t0, slot):
                cp.wait()
            for cp in gathers(slot):
                cp.start()

        start_chunk(base, 0)

        @pl.loop(0, n_ch, step=2)
        def _(k):
            for slot in range(2):
                nxt = k + slot + 1

                @pl.when(nxt < n_ch)
                def _():
                    start_chunk(base + nxt * ch, 1 - slot)

                for cp in gathers(slot):
                    cp.wait()
                out_cps = stores(base + (k + slot) * ch, slot)
                for cp in out_cps:
                    cp.start()
                for cp in out_cps:
                    cp.wait()

    return gather(ys, dest0, dest1)


def _experts_kernel(first_ref, nblk_ref, cnt_ref, xs_hbm, wg_ref, wu_ref, wd_ref, ys_hbm,
                    wgu_s, wd_s, xbuf, ybuf, sem_in, sem_out, *, de, blk):
    e = pl.program_id(0)
    n_exp = pl.num_programs(0)
    wgu_s[:, 0:de] = wg_ref[0].astype(BF16)
    wgu_s[:, de:2 * de] = wu_ref[0].astype(BF16)
    wd_s[...] = wd_ref[0].astype(BF16)
    first = first_ref[e]
    n_blk = nblk_ref[e]
    count = cnt_ref[e]
    total = first_ref[n_exp - 1] + nblk_ref[n_exp - 1]

    def in_copy(g, slot):
        return pltpu.make_async_copy(xs_hbm.at[pl.ds(pl.multiple_of(g * blk, blk), blk)],
                                     xbuf.at[slot], sem_in.at[slot])

    def out_copy(g, slot):
        return pltpu.make_async_copy(ybuf.at[slot],
                                     ys_hbm.at[pl.ds(pl.multiple_of(g * blk, blk), blk)],
                                     sem_out.at[slot])

    @pl.when(e == 0)
    def _():
        in_copy(0, 0).start(priority=1)
        ybuf[...] = jnp.zeros_like(ybuf)

    def block(j, carry):
        g = first + j
        slot = lax.rem(g, 2)
        in_copy(g, slot).wait()

        @pl.when(g + 1 < total)
        def _():
            in_copy(g + 1, 1 - slot).start(priority=1)

        @pl.when(g >= 2)
        def _():
            out_copy(g - 2, slot).wait()

        n_left = count - j * blk

        def ffn(n_rows):
            words = xbuf[slot, 0:n_rows, :]
            rows = lax.broadcasted_iota(I32, words.shape, 0)
            xb = _unpack_bf16_pairs(jnp.where(rows < n_left, words, 0)).astype(BF16)
            ab = jnp.dot(xb, wgu_s[...], preferred_element_type=F32)
            a = ab[:, 0:de]
            b = ab[:, de:2 * de]
            hmid = (a * _sigmoid(a)) * b
            y = jnp.dot(hmid.astype(BF16), wd_s[...], preferred_element_type=F32)
            ybuf[slot, 0:n_rows, :] = _pack_bf16_pairs(y)

        step = blk // EXPERT_BLOCK_PATHS
        for p in range(EXPERT_BLOCK_PATHS):
            lo_rows, hi_rows = p * step, (p + 1) * step
            last = p == EXPERT_BLOCK_PATHS - 1

            @pl.when((n_left > lo_rows) & ((n_left <= hi_rows) | last))
            def _(hi_rows=hi_rows):
                ffn(hi_rows)

        out_copy(g, slot).start(priority=1)
        return carry

    lax.fori_loop(0, n_blk, block, 0)

    @pl.when(e == n_exp - 1)
    def _():
        @pl.when(total >= 2)
        def _():
            out_copy(total - 2, lax.rem(total, 2)).wait()

        out_copy(total - 1, lax.rem(total + 1, 2)).wait()


def _experts(plan, xs, w_gate, w_up, w_down):
    n_slots, width = xs.shape
    n_exp, dm, de = w_gate.shape
    blk = EXPERT_BLOCK
    return pl.pallas_call(
        functools.partial(_experts_kernel, de=de, blk=blk),
        grid_spec=pltpu.PrefetchScalarGridSpec(
            num_scalar_prefetch=3,
            grid=(n_exp,),
            in_specs=[pl.BlockSpec(memory_space=pl.ANY),
                      pl.BlockSpec((1, dm, de), lambda e, f, n, c: (e, 0, 0)),
                      pl.BlockSpec((1, dm, de), lambda e, f, n, c: (e, 0, 0)),
                      pl.BlockSpec((1, de, dm), lambda e, f, n, c: (e, 0, 0))],
            out_specs=pl.BlockSpec(memory_space=pl.ANY),
            scratch_shapes=[pltpu.VMEM((dm, 2 * de), BF16), pltpu.VMEM((de, dm), BF16),
                            pltpu.VMEM((2, blk, width), I32), pltpu.VMEM((2, blk, width), I32),
                            pltpu.SemaphoreType.DMA((2,)), pltpu.SemaphoreType.DMA((2,))]),
        out_shape=jax.ShapeDtypeStruct((n_slots, width), I32),
        compiler_params=pltpu.CompilerParams(dimension_semantics=("arbitrary",),
                                             vmem_limit_bytes=VMEM_LIMIT),
        name="experts",
    )(*plan, xs, w_gate, w_up, w_down)


def _pick(onehot, table):
    return jnp.sum(jnp.where(onehot, table[None, :], 0), axis=1).astype(I32)


def _expert_plan(counts, n_assign):
    blk = EXPERT_BLOCK
    per_e = (counts + blk - 1) // blk
    first_blk = jnp.cumsum(per_e) - per_e
    n_slots = (n_assign // blk + N_EXPERTS) * blk
    return (first_blk * blk).astype(I32), (first_blk.astype(I32), per_e.astype(I32), counts), n_slots


def _slot_of(starts, expert, rank):
    onehot = jnp.arange(N_EXPERTS, dtype=I32)[None, :] == expert[:, None]
    return _pick(onehot, starts) + rank


def _combine_kernel(rows_ref, gate_ref, x1_ref, mod_ref, fg_ref, *rest):
    o_ref = rest[-1]
    tc = x1_ref.shape[0]
    gate_f = mod_ref[0][5:6]
    for q in range(tc // LANES):
        tok = slice(q * LANES, (q + 1) * LANES)
        g_rows = jnp.concatenate([gate_ref[:, tok], jnp.zeros((LANES - SUBLANES, LANES), F32)], axis=0)
        gc = g_rows.T
        y = (gc[:, 0:1] * _unpack_bf16_pairs(rows_ref[0, tok, :])
             + gc[:, 1:2] * _unpack_bf16_pairs(rows_ref[1, tok, :]))
        x2 = x1_ref[tok, :] + gate_f * y
        r = lax.rsqrt(jnp.mean(x2 * x2, axis=-1, keepdims=True) + EPS)
        o_ref[tok, :] = (x2 * r) * fg_ref[...]


def _combine(rows, gcol, x1, mod, final_g, seq, lb0, b0, nb, bsz, out_prev):
    dm = x1.shape[1]
    width = rows.shape[2]
    tc = min(COMBINE_TILE, seq)
    nst = seq // tc
    in_specs = [pl.BlockSpec((TOP_K, tc, width), lambda b, s: (0, b * nst + s, 0)),
                pl.BlockSpec((SUBLANES, tc), lambda b, s: (0, (b + lb0) * nst + s)),
                pl.BlockSpec((tc, dm), lambda b, s: ((b + lb0) * nst + s, 0)),
                pl.BlockSpec((1, 6, dm), lambda b, s: (b + b0, 0, 0)),
                pl.BlockSpec((1, dm), lambda b, s: (0, 0))]
    args = [rows, gcol, x1, mod, final_g.reshape(1, dm)]
    aliases = {}
    if out_prev is not None:
        in_specs.append(pl.BlockSpec(memory_space=pl.ANY))
        args.append(out_prev)
        aliases = {len(args) - 1: 0}
    return pl.pallas_call(
        _combine_kernel,
        grid=(nb, nst),
        in_specs=in_specs,
        out_specs=pl.BlockSpec((tc, dm), lambda b, s: ((b + b0) * nst + s, 0)),
        out_shape=jax.ShapeDtypeStruct((bsz * seq, dm), F32),
        input_output_aliases=aliases,
        compiler_params=pltpu.CompilerParams(dimension_semantics=("arbitrary", "arbitrary"),
                                             vmem_limit_bytes=VMEM_LIMIT),
        name="combine",
    )(*args)


def _layer(x, c, ada_w, ada_b, norm1_g, w_in, conv_w, conv_b, w_q, w_k, b_igate, b_fgate,
           mlstm_norm_g, mlstm_skip, w_pool, b_pool, pool_scale, w_out, norm2_g,
           w_rg, b_rg, w_re, b_re, w_eg, w_eu, w_ed, out_g):
    bsz, seq, dm = x.shape
    dml = conv_w.shape[1]
    n_tok = bsz * seq
    ts = min(SEQ_TILE, seq)

    mod = _ada(c, ada_w, ada_b).reshape(bsz, 6, dm)

    w_in_r, w_out_b = _prep_weights(w_in, w_out, dml)
    gbias = jnp.pad(jnp.concatenate([b_igate, b_fgate]), (0, LANES - 2 * N_HEADS)).reshape(1, LANES)
    wqk = jnp.concatenate([w_q, w_k], axis=-1).astype(BF16)
    wrt = jnp.zeros((LOGIT_ROWS, dm), F32)
    wrt = wrt.at[0:N_GROUPS].set(w_rg.T).at[SUBLANES:SUBLANES + N_EXPERTS].set(w_re.T).astype(BF16)
    rb = jnp.zeros((LOGIT_ROWS,), F32).at[0:N_GROUPS].set(b_rg).at[SUBLANES:SUBLANES + N_EXPERTS].set(b_re)
    rbias = jnp.broadcast_to(rb[:, None], (LOGIT_ROWS, ts))

    gdim = (dm - dml) // len(POOL_WINDOWS)
    win = jnp.repeat(jnp.array(POOL_WINDOWS, F32), gdim)[None, :]
    t1 = jnp.arange(1, ts + 1, dtype=F32)[:, None]
    pinv = jnp.stack([1.0 / jnp.minimum(t1, win), jnp.broadcast_to(1.0 / win, (ts, dm - dml))])

    mod = mod.at[:, 1].set(norm1_g * (1.0 + mod[:, 1])).at[:, 4].set(norm2_g * (1.0 + mod[:, 4]))

    mixer_params = (w_in_r, gbias, conv_w, conv_b.reshape(1, dml), wqk,
                    mlstm_norm_g.reshape(1, dml), mlstm_skip.reshape(1, dml), w_pool.astype(BF16),
                    b_pool.reshape(1, dm - dml), pool_scale.reshape(1, dm - dml), pinv,
                    w_out_b, wrt, rbias)

    x1, h2, lgt = _mixer(x, mod, *mixer_params, 0, bsz)
    idx, gcol, cnt = _route(lgt, n_tok)
    counts = cnt[:, 0].astype(I32)
    starts, plan, n_slots = _expert_plan(counts, n_tok * TOP_K)
    dest0 = _slot_of(starts, idx[0], idx[2])
    dest1 = _slot_of(starts, idx[1], idx[3])
    xs = _dispatch(h2, dest0, dest1, n_slots)
    ys = _experts(plan, xs, w_eg, w_eu, w_ed)

    nb = bsz // COMBINE_GROUPS
    assert bsz % COMBINE_GROUPS == 0
    out = None
    for b0 in range(0, bsz, nb):
        rows = _collect(ys, dest0, dest1, b0 * seq, nb * seq)
        out = _combine(rows, gcol, x1, mod, out_g, seq, b0, b0, nb, bsz, out)
    return out.reshape(bsz, seq, dm)


def kernel(x, c, ada_w, ada_b, norm1_g, w_in, conv_w, conv_b, w_q, w_k, b_igate, b_fgate, mlstm_norm_g, mlstm_skip, w_pool, b_pool, pool_scale, w_out, norm2_g, w_router_group, b_router_group, w_router_expert, b_router_expert, w_expert_gate, w_expert_up, w_expert_down, final_g):
    depth = ada_w.shape[0]
    assert depth == 1, "the final norm is fused into the last layer's combine kernel"
    l = 0
    return _layer(x, c, ada_w[l], ada_b[l], norm1_g[l], w_in[l], conv_w[l], conv_b[l], w_q[l],
                  w_k[l], b_igate[l], b_fgate[l], mlstm_norm_g[l], mlstm_skip[l], w_pool[l],
                  b_pool[l], pool_scale[l], w_out[l], norm2_g[l], w_router_group[l],
                  b_router_group[l], w_router_expert[l], b_router_expert[l],
                  w_expert_gate[l], w_expert_up[l], w_expert_down[l], final_g)
```

```python
import functools

import jax
import jax.numpy as jnp
from jax import lax
from jax.experimental import pallas as pl
from jax.experimental.pallas import tpu as pltpu
from jax.experimental.pallas import tpu_sc as plsc

F32 = jnp.float32
BF16 = jnp.bfloat16
I32 = jnp.int32
U32 = jnp.uint32

EPS = 1e-6
N_HEADS = 4
HEAD_DIM = 128
CONV_WIDTH = 4
POOL_WINDOWS = (2, 4, 8, 16)
N_GROUPS = 4
EXPERTS_PER_GROUP = 8
N_EXPERTS = N_GROUPS * EXPERTS_PER_GROUP
TOP_K = 2

LANES = 128
SUBLANES = 8
CHUNK = 128
SEQ_TILE = 1024
ADA_TILE = 1024
PREP_COLS = 512
COMBINE_GROUPS = 8
ROUTE_TILE = 512
ROUTE_SUBTILES = 16
SC_SCATTER_CHUNK = 64
SC_GATHER_CHUNK = 32
COMBINE_TILE = 2048
EXPERT_BLOCK = 1024
EXPERT_BLOCK_PATHS = 8
LOGIT_ROWS = 48
UHIST = 8
PHIST = 16
VMEM_LIMIT = 60 * 1024 * 1024


def _sigmoid(x):
    return 1.0 / (1.0 + jnp.exp(-x))


def _pack_bf16_pairs(x):
    w = x.shape[1] // 2
    half_ulp = jnp.uint32(0x8000)
    hi = lax.bitcast_convert_type(x[:, :w], U32) + half_ulp
    lo = lax.bitcast_convert_type(x[:, w:], U32) + half_ulp
    return lax.bitcast_convert_type((hi & jnp.uint32(0xFFFF0000)) | (lo >> 16), I32)


def _unpack_bf16_pairs(words):
    u = lax.bitcast_convert_type(words, U32)
    hi = lax.bitcast_convert_type(u & jnp.uint32(0xFFFF0000), F32)
    lo = lax.bitcast_convert_type(u << 16, F32)
    return jnp.concatenate([hi, lo], axis=1)


def _ada_kernel(c_ref, w_ref, b_ref, o_ref):
    c = c_ref[...]
    s = c * _sigmoid(c)
    o_ref[...] = jnp.dot(s.astype(BF16), w_ref[...].astype(BF16),
                         preferred_element_type=F32) + b_ref[...]


def _ada(c, ada_w, ada_b):
    bsz, dm = c.shape
    n = ada_w.shape[1]
    tn = ADA_TILE
    return pl.pallas_call(
        _ada_kernel,
        grid=(n // tn,),
        in_specs=[pl.BlockSpec((bsz, dm), lambda j: (0, 0)),
                  pl.BlockSpec((dm, tn), lambda j: (0, j)),
                  pl.BlockSpec((1, tn), lambda j: (0, j))],
        out_specs=pl.BlockSpec((bsz, tn), lambda j: (0, j)),
        out_shape=jax.ShapeDtypeStruct((bsz, n), F32),
        compiler_params=pltpu.CompilerParams(dimension_semantics=("arbitrary",),
                                             vmem_limit_bytes=VMEM_LIMIT),
        name="ada",
    )(c, ada_w, ada_b.reshape(1, n))


def _prep_kernel(wint_ref, wout_ref, winr_ref, woutb_ref, *, dml, dp):
    col_v, col_o, col_i = dml, 2 * dml, 3 * dml
    col_p = col_i + 2 * N_HEADS
    kb = wint_ref.shape[1]
    winr_ref[:, 0:dml] = wint_ref[0:col_v, :].T.astype(BF16)
    winr_ref[:, dml:dml + dp] = wint_ref[col_p:col_p + dp, :].T.astype(BF16)
    winr_ref[:, dml + dp:2 * dml + dp] = wint_ref[col_v:col_o, :].T.astype(BF16)
    winr_ref[:, 2 * dml + dp:3 * dml + dp] = wint_ref[col_o:col_i, :].T.astype(BF16)
    gates = jnp.concatenate([wint_ref[col_i:col_p, :], jnp.zeros((LANES - 2 * N_HEADS, kb), F32)], axis=0)
    winr_ref[:, 3 * dml + dp:3 * dml + dp + LANES] = gates.T.astype(BF16)
    woutb_ref[...] = wout_ref[...].astype(BF16)


def _prep_weights(w_in, w_out, dml):
    dm, ncol = w_in.shape
    dp = ncol - 3 * dml - 2 * N_HEADS
    nout = 3 * dml + dp + LANES
    kb = PREP_COLS
    assert dm % kb == 0 and w_out.shape[1] % kb == 0 and 2 * N_HEADS == SUBLANES
    return pl.pallas_call(
        functools.partial(_prep_kernel, dml=dml, dp=dp),
        grid=(dm // kb,),
        in_specs=[pl.BlockSpec((ncol, kb), lambda i: (0, i)),
                  pl.BlockSpec((w_out.shape[0], kb), lambda i: (0, i))],
        out_specs=[pl.BlockSpec((kb, nout), lambda i: (i, 0)),
                   pl.BlockSpec((w_out.shape[0], kb), lambda i: (0, i))],
        out_shape=[jax.ShapeDtypeStruct((dm, nout), BF16),
                   jax.ShapeDtypeStruct(w_out.shape, BF16)],
        compiler_params=pltpu.CompilerParams(dimension_semantics=("arbitrary",),
                                             vmem_limit_bytes=VMEM_LIMIT),
        name="prep",
    )(w_in.T, w_out)


def _split3(x):
    hi = x.astype(BF16)
    r1 = x - hi.astype(F32)
    mid = r1.astype(BF16)
    lo = (r1 - mid.astype(F32)).astype(BF16)
    return hi, mid, lo


def _mixer_kernel(x_ref, mod_ref, win_ref, gbias_ref, convw_ref, convb_ref, wqk_ref,
                  ng_ref, skip_ref, wpool_ref, bpool_ref, pscale_ref, pinv_ref, wout_ref,
                  wrt_ref, rbias_ref,
                  x1_ref, h2_ref, lgt_ref,
                  uext_s, pext_s, proj_s, uc_s, qk_s, mix_s, pool4_s, ctv_s, ctn_s, mprev_s,
                  *, ts, dm, dml):
    s_idx = pl.program_id(1)
    n_chunks = ts // CHUNK
    dp = dm - dml

    @pl.when(s_idx == 0)
    def _():
        uext_s[0:UHIST, :] = jnp.zeros((UHIST, dml), F32)
        pext_s[0:PHIST, :] = jnp.zeros((PHIST, dp), F32)
        ctv_s[...] = jnp.zeros_like(ctv_s)
        ctn_s[...] = jnp.zeros_like(ctn_s)
        mprev_s[...] = jnp.zeros_like(mprev_s)

    row_i = lax.broadcasted_iota(I32, (CHUNK, CHUNK), 0)
    col_i = lax.broadcasted_iota(I32, (CHUNK, CHUNK), 1)
    causal = row_i >= col_i
    triu = jnp.where(row_i <= col_i, 1.0, 0.0).astype(BF16)
    ones_blk = jnp.ones((CHUNK, HEAD_DIM), BF16)
    q_scale = HEAD_DIM ** -0.5

    col_o = dml
    col_g = 2 * dml

    x = x_ref[0]
    mod = mod_ref[0]
    r = lax.rsqrt(jnp.mean(x * x, axis=-1, keepdims=True) + EPS)
    h = (x * r) * mod[1:2] + mod[0:1]
    res = jnp.dot(h.astype(BF16), win_ref[...], preferred_element_type=F32)
    uext_s[UHIST:, :] = res[:, 0:dml]
    pext_s[PHIST:, :] = res[:, dml:dm]
    proj_s[...] = res[:, dm:]

    acc = None
    for j in reversed(range(CONV_WIDTH)):
        tap = uext_s[pl.ds(UHIST - (CONV_WIDTH - 1 - j), ts), :] * convw_ref[j:j + 1, :]
        acc = tap if acc is None else acc + tap
    conv = acc + convb_ref[...]
    uc = conv * _sigmoid(conv)
    uc_s[...] = uc

    for hd in range(N_HEADS):
        qk = jnp.dot(uc[:, HEAD_DIM * hd:HEAD_DIM * (hd + 1)].astype(BF16), wqk_ref[hd],
                     preferred_element_type=F32)
        c0 = 2 * HEAD_DIM * hd
        qk_s[:, c0:c0 + HEAD_DIM] = qk[:, 0:HEAD_DIM] * q_scale
        qk_s[:, c0 + HEAD_DIM:c0 + 2 * HEAD_DIM] = qk[:, HEAD_DIM:2 * HEAD_DIM]

    def pe(shift, rows, lanes):
        return pext_s[pl.ds(PHIST - shift, rows), lanes]

    gd = LANES
    sums = []
    for gi in range(2):
        lanes = slice(gd * gi, gd * (gi + 1))
        tot = pe(0, ts, lanes)
        for j in range(1, POOL_WINDOWS[gi]):
            tot = tot + pe(j, ts, lanes)
        sums.append(tot)
    wide = slice(2 * gd, 4 * gd)
    s4 = pe(12, ts + 12, wide)
    for j in range(1, 4):
        s4 = s4 + pe(12 + j, ts + 12, wide)
    pool4_s[0:ts + 12, :] = s4
    s8 = pool4_s[pl.ds(4, ts + 8), :] + pool4_s[pl.ds(0, ts + 8), :]
    sums.append(s8[8:, 0:gd])
    sums.append(s8[8:, gd:2 * gd] + s8[0:ts, gd:2 * gd])
    for gi in range(len(POOL_WINDOWS)):
        lanes = slice(gd * gi, gd * (gi + 1))
        pooled = sums[gi] * pinv_ref[0, :, lanes] - pe(0, ts, lanes)
        yp = jnp.dot(pooled.astype(BF16), wpool_ref[gi], preferred_element_type=F32)
        yp = (yp + bpool_ref[:, lanes]) * pscale_ref[:, lanes]
        mix_s[:, dml + gd * gi:dml + gd * (gi + 1)] = yp.astype(BF16)

    pairs = [(c, hd) for c in range(n_chunks) for hd in range(N_HEADS)]
    g_rows, b_rows, b_cols = [], [], []
    pad_rows = jnp.zeros((CHUNK - SUBLANES, CHUNK), F32)
    for c in range(n_chunks):
        g_tile = proj_s[pl.ds(c * CHUNK, CHUNK), col_g:col_g + LANES] + gbias_ref[...]
        g_row = g_tile.T[0:SUBLANES, :]
        logf = -(jnp.maximum(-g_row, 0.0) + jnp.log1p(jnp.exp(-jnp.abs(g_row))))
        hi, mid, lo = _split3(logf)
        cs = jnp.dot(jnp.concatenate([hi, mid, lo], axis=0), triu, preferred_element_type=F32)
        b_row = cs[0:SUBLANES] + cs[SUBLANES:2 * SUBLANES] + cs[2 * SUBLANES:3 * SUBLANES]
        g_rows.append(g_row)
        b_rows.append(b_row)
        b_cols.append(jnp.concatenate([b_row, pad_rows], axis=0).T)

    def rows(ref, c, lo_col, width=HEAD_DIM):
        return ref[pl.ds(c * CHUNK, CHUNK), lo_col:lo_col + width]

    b_bcs, dlogs, rmaxs, p_mats = {}, {}, {}, {}
    for c, hd in pairs:
        b_bc = jnp.broadcast_to(b_cols[c][:, N_HEADS + hd:N_HEADS + hd + 1], (CHUNK, CHUNK))
        i_row = g_rows[c][hd:hd + 1, :]
        b_row = b_rows[c][N_HEADS + hd:N_HEADS + hd + 1, :]
        dlog = jnp.where(causal, (b_bc - b_row) + i_row, -jnp.inf)
        b_bcs[c, hd], dlogs[c, hd] = b_bc, dlog
        rmaxs[c, hd] = jnp.max(dlog, axis=-1, keepdims=True)
        q_c = rows(qk_s, c, 2 * HEAD_DIM * hd)
        k_c = rows(qk_s, c, 2 * HEAD_DIM * hd + HEAD_DIM)
        p_mats[c, hd] = lax.dot_general(q_c.astype(BF16), k_c.astype(BF16),
                                        (((1,), (1,)), ((), ())), preferred_element_type=F32)

    inters, m_ts = {}, {}
    for hd in range(N_HEADS):
        m_prev = mprev_s[hd]
        for c in range(n_chunks):
            inter = b_bcs[c, hd] + m_prev
            m_t = jnp.maximum(inter, rmaxs[c, hd])
            inters[c, hd], m_ts[c, hd] = inter, m_t
            m_prev = jnp.broadcast_to(m_t[CHUNK - 1:CHUNK, :], (CHUNK, LANES))
        mprev_s[hd] = m_prev

    lhs, v_augs, upds, a_prevs, e_negms = {}, {}, {}, {}, {}
    for c, hd in pairs:
        m_t = m_ts[c, hd]
        wm = jnp.exp(dlogs[c, hd] - m_t)
        a_inter = jnp.exp(inters[c, hd] - m_t)
        e_negms[c, hd] = jnp.exp(-m_t)
        q_c = rows(qk_s, c, 2 * HEAD_DIM * hd)
        k_c = rows(qk_s, c, 2 * HEAD_DIM * hd + HEAD_DIM)
        v_c = rows(proj_s, c, HEAD_DIM * hd)
        s_mat = (p_mats[c, hd] * wm).astype(BF16)
        qa = (q_c * a_inter).astype(BF16)
        lhs[c, hd] = jnp.concatenate([s_mat, qa], axis=1)
        v_aug = jnp.concatenate([v_c.astype(BF16), ones_blk], axis=1)
        v_augs[c, hd] = v_aug
        ktw = (k_c.T * wm[CHUNK - 1:CHUNK, :]).astype(BF16)
        upds[c, hd] = jnp.dot(ktw, v_aug, preferred_element_type=F32)
        a_prevs[c, hd] = a_inter[CHUNK - 1:CHUNK, :]

    ct_in = {}
    for hd in range(N_HEADS):
        ctv, ctn = ctv_s[hd], ctn_s[hd]
        for c in range(n_chunks):
            ct_in[c, hd] = jnp.concatenate([ctv, ctn], axis=1).astype(BF16)
            ctv = a_prevs[c, hd] * ctv + upds[c, hd][:, 0:HEAD_DIM]
            ctn = a_prevs[c, hd] * ctn + upds[c, hd][:, HEAD_DIM:2 * HEAD_DIM]
        ctv_s[hd], ctn_s[hd] = ctv, ctn

    for c, hd in pairs:
        numden = jnp.dot(lhs[c, hd], jnp.concatenate([v_augs[c, hd], ct_in[c, hd]], axis=0),
                         preferred_element_type=F32)
        num = numden[:, 0:HEAD_DIM]
        den = numden[:, HEAD_DIM:2 * HEAD_DIM]
        hh = num / jnp.maximum(jnp.abs(den), e_negms[c, hd])
        ms = jnp.mean(hh * hh, axis=-1, keepdims=True)
        hn = hh * lax.rsqrt(ms + EPS) * ng_ref[:, HEAD_DIM * hd:HEAD_DIM * (hd + 1)]
        o_c = rows(proj_s, c, col_o + HEAD_DIM * hd)
        uc_c = rows(uc_s, c, HEAD_DIM * hd)
        out_c = _sigmoid(o_c) * (hn + skip_ref[:, HEAD_DIM * hd:HEAD_DIM * (hd + 1)] * uc_c)
        mix_s[pl.ds(c * CHUNK, CHUNK), HEAD_DIM * hd:HEAD_DIM * (hd + 1)] = out_c.astype(BF16)

    mix = jnp.dot(mix_s[...], wout_ref[...], preferred_element_type=F32)
    x1 = x + mod[2:3] * mix
    x1_ref[...] = x1
    r2 = lax.rsqrt(jnp.mean(x1 * x1, axis=-1, keepdims=True) + EPS)
    h2 = (x1 * r2) * mod[4:5] + mod[3:4]
    h2_ref[...] = _pack_bf16_pairs(h2)
    lgt_ref[...] = lax.dot_general(wrt_ref[...], h2.astype(BF16), (((1,), (1,)), ((), ())),
                                   preferred_element_type=F32) + rbias_ref[...]

    uext_s[0:UHIST, :] = uext_s[ts:ts + UHIST, :]
    pext_s[0:PHIST, :] = pext_s[ts:ts + PHIST, :]


def _mixer(x, mod, w_in_r, gbias, conv_w, conv_b, wqk, ng, skip, wpool, bpool, pscale, pinv,
           w_out, wrt, rbias, b0, nb):
    _, seq, dm = x.shape
    dml = conv_w.shape[1]
    ts = min(SEQ_TILE, seq)
    ncols = w_in_r.shape[1]
    nst = seq // ts
    n_tok = nb * seq
    assert seq % ts == 0 and ts % CHUNK == 0
    full = lambda a: pl.BlockSpec(a.shape, lambda b, s: (0,) * a.ndim, pipeline_mode=pl.Buffered(1))
    kern = functools.partial(_mixer_kernel, ts=ts, dm=dm, dml=dml)
    return pl.pallas_call(
        kern,
        grid=(nb, nst),
        in_specs=[pl.BlockSpec((1, ts, dm), lambda b, s: (b + b0, s, 0)),
                  pl.BlockSpec((1, 6, dm), lambda b, s: (b + b0, 0, 0)),
                  full(w_in_r), full(gbias), full(conv_w), full(conv_b), full(wqk),
                  full(ng), full(skip), full(wpool), full(bpool), full(pscale),
                  pl.BlockSpec((1,) + pinv.shape[1:], lambda b, s: (jnp.minimum(s, 1), 0, 0)),
                  full(w_out), full(wrt), full(rbias)],
        out_specs=[pl.BlockSpec((ts, dm), lambda b, s: (b * nst + s, 0)),
                   pl.BlockSpec((ts, dm // 2), lambda b, s: (b * nst + s, 0)),
                   pl.BlockSpec((LOGIT_ROWS, ts), lambda b, s: (0, b * nst + s))],
        out_shape=[jax.ShapeDtypeStruct((n_tok, dm), F32),
                   jax.ShapeDtypeStruct((n_tok, dm // 2), I32),
                   jax.ShapeDtypeStruct((LOGIT_ROWS, n_tok), F32)],
        scratch_shapes=[pltpu.VMEM((UHIST + ts, dml), F32),
                        pltpu.VMEM((PHIST + ts, dm - dml), F32),
                        pltpu.VMEM((ts, ncols - dm), F32),
                        pltpu.VMEM((ts, dml), F32),
                        pltpu.VMEM((ts, 2 * dml), F32),
                        pltpu.VMEM((ts, dm), BF16),
                        pltpu.VMEM((ts + PHIST, 2 * LANES), F32),
                        pltpu.VMEM((N_HEADS, HEAD_DIM, HEAD_DIM), F32),
                        pltpu.VMEM((N_HEADS, HEAD_DIM, HEAD_DIM), F32),
                        pltpu.VMEM((N_HEADS, CHUNK, LANES), F32)],
        compiler_params=pltpu.CompilerParams(dimension_semantics=("arbitrary", "arbitrary"),
                                             vmem_limit_bytes=VMEM_LIMIT),
        name="mixer",
    )(x, mod, w_in_r, gbias, conv_w, conv_b, wqk, ng, skip, wpool, bpool, pscale, pinv, w_out,
      wrt, rbias)


def _route_kernel(lgt_ref, idx_ref, gate_ref, cnt_ref, carry_s, *, tr, n_sub):
    @pl.when(pl.program_id(0) == 0)
    def _():
        carry_s[...] = jnp.zeros_like(carry_s)

    tr_r = lax.broadcasted_iota(I32, (tr, tr), 0)
    tr_c = lax.broadcasted_iota(I32, (tr, tr), 1)
    upper = jnp.where(tr_r < tr_c, 1.0, 0.0).astype(BF16)
    for q in range(n_sub):
        cols = slice(q * tr, (q + 1) * tr)
        idx, gates = _route_tile(lgt_ref[:, cols], upper, carry_s, tr)
        idx_ref[:, cols] = idx
        gate_ref[:, cols] = gates
    cnt_ref[...] = carry_s[...]


def _route_tile(lg, upper, carry_s, tr):
    best = lg[0:1]
    gidx = jnp.zeros((1, tr), I32)
    for j in range(1, N_GROUPS):
        cand = lg[j:j + 1]
        better = cand > best
        gidx = jnp.where(better, j, gidx)
        best = jnp.where(better, cand, best)
    sumexp = jnp.zeros((1, tr), F32)
    for j in range(N_GROUPS):
        sumexp = sumexp + jnp.exp(lg[j:j + 1] - best)
    g_gate = 1.0 / sumexp

    sel = lg[SUBLANES:2 * SUBLANES]
    for j in range(1, N_GROUPS):
        sel = jnp.where(gidx == j, lg[SUBLANES * (j + 1):SUBLANES * (j + 2)], sel)
    sub = lax.broadcasted_iota(I32, (EXPERTS_PER_GROUP, tr), 0)
    v1 = jnp.max(sel, axis=0, keepdims=True)
    i1 = jnp.min(jnp.where(sel == v1, sub, EXPERTS_PER_GROUP), axis=0, keepdims=True)
    sel2 = jnp.where(sub == i1, -jnp.inf, sel)
    v2 = jnp.max(sel2, axis=0, keepdims=True)
    i2 = jnp.min(jnp.where(sel2 == v2, sub, EXPERTS_PER_GROUP), axis=0, keepdims=True)
    e2 = jnp.exp(v2 - v1)
    den = 1.0 + e2
    gate0 = (1.0 / den) * g_gate
    gate1 = (e2 / den) * g_gate
    ex0 = gidx * EXPERTS_PER_GROUP + i1
    ex1 = gidx * EXPERTS_PER_GROUP + i2

    erow = lax.broadcasted_iota(I32, (N_EXPERTS, tr), 0)
    oh0 = erow == ex0
    oh1 = erow == ex1
    oh = jnp.where(oh0 | oh1, 1.0, 0.0).astype(BF16)
    carry = carry_s[...]
    before = jnp.dot(oh, upper, preferred_element_type=F32)
    before = before + jnp.concatenate([carry] * (tr // LANES), axis=1)
    rank0 = jnp.sum(jnp.where(oh0, before, 0.0), axis=0, keepdims=True)
    rank1 = jnp.sum(jnp.where(oh1, before, 0.0), axis=0, keepdims=True)
    carry_s[...] = carry + jnp.dot(oh, jnp.ones((tr, LANES), BF16), preferred_element_type=F32)

    idx = jnp.concatenate([ex0, ex1, rank0.astype(I32), rank1.astype(I32),
                           jnp.zeros((SUBLANES - 4, tr), I32)], axis=0)
    gates = jnp.concatenate([gate0, gate1, jnp.zeros((SUBLANES - 2, tr), F32)], axis=0)
    return idx, gates


def _route(lgt, n_tok):
    tr = ROUTE_TILE
    n_sub = min(ROUTE_SUBTILES, n_tok // tr)
    step = tr * n_sub
    assert n_tok % step == 0
    return pl.pallas_call(
        functools.partial(_route_kernel, tr=tr, n_sub=n_sub),
        grid=(n_tok // step,),
        in_specs=[pl.BlockSpec((LOGIT_ROWS, step), lambda i: (0, i))],
        out_specs=[pl.BlockSpec((SUBLANES, step), lambda i: (0, i)),
                   pl.BlockSpec((SUBLANES, step), lambda i: (0, i)),
                   pl.BlockSpec((N_EXPERTS, LANES), lambda i: (0, 0))],
        out_shape=[jax.ShapeDtypeStruct((SUBLANES, n_tok), I32),
                   jax.ShapeDtypeStruct((SUBLANES, n_tok), F32),
                   jax.ShapeDtypeStruct((N_EXPERTS, LANES), F32)],
        scratch_shapes=[pltpu.VMEM((N_EXPERTS, LANES), F32)],
        compiler_params=pltpu.CompilerParams(dimension_semantics=("arbitrary",),
                                             vmem_limit_bytes=VMEM_LIMIT),
        name="route",
    )(lgt)


def _sc_workers():
    info = plsc.get_sparse_core_info()
    return info.num_cores, info.num_cores * info.num_subcores


def _dispatch(h2p, dest0, dest1, n_slots):
    n_tok, width = h2p.shape
    n_cores, n_workers = _sc_workers()
    per_w = n_tok // n_workers
    ch = min(SC_SCATTER_CHUNK, per_w)
    n_ch = per_w // ch
    assert n_tok % n_workers == 0 and per_w % ch == 0 and ch % SUBLANES == 0 and n_ch % 2 == 0
    mesh = plsc.VectorSubcoreMesh(core_axis_name="c", subcore_axis_name="s")

    @functools.partial(
        pl.kernel, mesh=mesh,
        out_type=jax.ShapeDtypeStruct((n_slots, width), h2p.dtype),
        scratch_types=[pltpu.VMEM((2, ch), I32), pltpu.VMEM((2, ch), I32),
                       pltpu.VMEM((2, ch, width), h2p.dtype),
                       pltpu.SemaphoreType.DMA((2,)), pltpu.SemaphoreType.DMA((2,))],
        name="dispatch")
    def scatter(h_hbm, d0_hbm, d1_hbm, xs_hbm, i0_v, i1_v, rows_v, sem_in, sem_out):
        wid = lax.axis_index("s") * n_cores + lax.axis_index("c")
        base = wid * per_w

        def loads(t0, slot):
            return (pltpu.make_async_copy(d0_hbm.at[pl.ds(t0, ch)], i0_v.at[slot], sem_in.at[slot]),
                    pltpu.make_async_copy(d1_hbm.at[pl.ds(t0, ch)], i1_v.at[slot], sem_in.at[slot]),
                    pltpu.make_async_copy(h_hbm.at[pl.ds(t0, ch)], rows_v.at[slot], sem_in.at[slot]))

        def scatters(slot):
            return (pltpu.make_async_copy(rows_v.at[slot], xs_hbm.at[i0_v.at[slot]], sem_out.at[slot]),
                    pltpu.make_async_copy(rows_v.at[slot], xs_hbm.at[i1_v.at[slot]], sem_out.at[slot]))

        for cp in loads(base, 0):
            cp.start()

        @pl.loop(0, n_ch, step=2)
        def _(k):
            for slot in range(2):
                for cp in loads(base + (k + slot) * ch, slot):
                    cp.wait()
                out_cps = scatters(slot)
                for cp in out_cps:
                    cp.start()
                nxt = k + slot + 1

                @pl.when(nxt < n_ch)
                def _():
                    for cp in loads(base + nxt * ch, 1 - slot):
                        cp.start()

                for cp in out_cps:
                    cp.wait()

    return scatter(h2p, dest0, dest1)


def _collect(ys, dest0, dest1, tok0, n_tok):
    width = ys.shape[1]
    n_cores, n_workers = _sc_workers()
    per_w = n_tok // n_workers
    ch = min(SC_GATHER_CHUNK, per_w)
    n_ch = per_w // ch
    assert n_tok % n_workers == 0 and per_w % ch == 0 and ch % SUBLANES == 0 and n_ch % 2 == 0
    mesh = plsc.VectorSubcoreMesh(core_axis_name="c", subcore_axis_name="s")

    @functools.partial(
        pl.kernel, mesh=mesh,
        out_type=jax.ShapeDtypeStruct((TOP_K, n_tok, width), ys.dtype),
        scratch_types=[pltpu.VMEM((2, ch), I32), pltpu.VMEM((2, ch), I32),
                       pltpu.VMEM((2, ch, width), ys.dtype), pltpu.VMEM((2, ch, width), ys.dtype),
                       pltpu.SemaphoreType.DMA((2,)), pltpu.SemaphoreType.DMA((2,)),
                       pltpu.SemaphoreType.DMA((2,))],
        name="collect")
    def gather(ys_hbm, d0_hbm, d1_hbm, o_hbm, i0_v, i1_v, r0_v, r1_v, sem_idx, sem_in, sem_out):
        wid = lax.axis_index("s") * n_cores + lax.axis_index("c")
        base = wid * per_w

        def idx_loads(t0, slot):
            return (pltpu.make_async_copy(d0_hbm.at[pl.ds(tok0 + t0, ch)], i0_v.at[slot], sem_idx.at[slot]),
                    pltpu.make_async_copy(d1_hbm.at[pl.ds(tok0 + t0, ch)], i1_v.at[slot], sem_idx.at[slot]))

        def gathers(slot):
            return (pltpu.make_async_copy(ys_hbm.at[i0_v.at[slot]], r0_v.at[slot], sem_in.at[slot]),
                    pltpu.make_async_copy(ys_hbm.at[i1_v.at[slot]], r1_v.at[slot], sem_in.at[slot]))

        def stores(t0, slot):
            return (pltpu.make_async_copy(r0_v.at[slot], o_hbm.at[0, pl.ds(t0, ch)], sem_out.at[slot]),
                    pltpu.make_async_copy(r1_v.at[slot], o_hbm.at[1, pl.ds(t0, ch)], sem_out.at[slot]))

        def start_chunk(t0, slot):
            for cp in idx_loads(t0, slot):
                cp.start()
            for cp in idx_loads(t0, slot):
                cp.wait()
            for cp in gathers(slot):
                cp.start()

        start_chunk(base, 0)

        @pl.loop(0, n_ch, step=2)
        def _(k):
            for slot in range(2):
                nxt = k + slot + 1

                @pl.when(nxt < n_ch)
                def _():
                    start_chunk(base + nxt * ch, 1 - slot)

                for cp in gathers(slot):
                    cp.wait()
                out_cps = stores(base + (k + slot) * ch, slot)
                for cp in out_cps:
                    cp.start()
                for cp in out_cps:
                    cp.wait()

    return gather(ys, dest0, dest1)


def _experts_kernel(first_ref, nblk_ref, cnt_ref, xs_hbm, wg_ref, wu_ref, wd_ref, ys_hbm,
                    wgu_s, wd_s, xbuf, ybuf, sem_in, sem_out, *, de, blk):
    e = pl.program_id(0)
    n_exp = pl.num_programs(0)
    wgu_s[:, 0:de] = wg_ref[0].astype(BF16)
    wgu_s[:, de:2 * de] = wu_ref[0].astype(BF16)
    wd_s[...] = wd_ref[0].astype(BF16)
    first = first_ref[e]
    n_blk = nblk_ref[e]
    count = cnt_ref[e]
    total = first_ref[n_exp - 1] + nblk_ref[n_exp - 1]

    def in_copy(g, slot):
        return pltpu.make_async_copy(xs_hbm.at[pl.ds(pl.multiple_of(g * blk, blk), blk)],
                                     xbuf.at[slot], sem_in.at[slot])

    def out_copy(g, slot):
        return pltpu.make_async_copy(ybuf.at[slot],
                                     ys_hbm.at[pl.ds(pl.multiple_of(g * blk, blk), blk)],
                                     sem_out.at[slot])

    @pl.when(e == 0)
    def _():
        in_copy(0, 0).start(priority=1)
        ybuf[...] = jnp.zeros_like(ybuf)

    def block(j, carry):
        g = first + j
        slot = lax.rem(g, 2)
        in_copy(g, slot).wait()

        @pl.when(g + 1 < total)
        def _():
            in_copy(g + 1, 1 - slot).start(priority=1)

        @pl.when(g >= 2)
        def _():
            out_copy(g - 2, slot).wait()

        n_left = count - j * blk

        def ffn(n_rows):
            words = xbuf[slot, 0:n_rows, :]
            rows = lax.broadcasted_iota(I32, words.shape, 0)
            xb = _unpack_bf16_pairs(jnp.where(rows < n_left, words, 0)).astype(BF16)
            ab = jnp.dot(xb, wgu_s[...], preferred_element_type=F32)
            a = ab[:, 0:de]
            b = ab[:, de:2 * de]
            hmid = (a * _sigmoid(a)) * b
            y = jnp.dot(hmid.astype(BF16), wd_s[...], preferred_element_type=F32)
            ybuf[slot, 0:n_rows, :] = _pack_bf16_pairs(y)

        step = blk // EXPERT_BLOCK_PATHS
        for p in range(EXPERT_BLOCK_PATHS):
            lo_rows, hi_rows = p * step, (p + 1) * step
            last = p == EXPERT_BLOCK_PATHS - 1

            @pl.when((n_left > lo_rows) & ((n_left <= hi_rows) | last))
            def _(hi_rows=hi_rows):
                ffn(hi_rows)

        out_copy(g, slot).start(priority=1)
        return carry

    lax.fori_loop(0, n_blk, block, 0)

    @pl.when(e == n_exp - 1)
    def _():
        @pl.when(total >= 2)
        def _():
            out_copy(total - 2, lax.rem(total, 2)).wait()

        out_copy(total - 1, lax.rem(total + 1, 2)).wait()


def _experts(plan, xs, w_gate, w_up, w_down):
    n_slots, width = xs.shape
    n_exp, dm, de = w_gate.shape
    blk = EXPERT_BLOCK
    return pl.pallas_call(
        functools.partial(_experts_kernel, de=de, blk=blk),
        grid_spec=pltpu.PrefetchScalarGridSpec(
            num_scalar_prefetch=3,
            grid=(n_exp,),
            in_specs=[pl.BlockSpec(memory_space=pl.ANY),
                      pl.BlockSpec((1, dm, de), lambda e, f, n, c: (e, 0, 0)),
                      pl.BlockSpec((1, dm, de), lambda e, f, n, c: (e, 0, 0)),
                      pl.BlockSpec((1, de, dm), lambda e, f, n, c: (e, 0, 0))],
            out_specs=pl.BlockSpec(memory_space=pl.ANY),
            scratch_shapes=[pltpu.VMEM((dm, 2 * de), BF16), pltpu.VMEM((de, dm), BF16),
                            pltpu.VMEM((2, blk, width), I32), pltpu.VMEM((2, blk, width), I32),
                            pltpu.SemaphoreType.DMA((2,)), pltpu.SemaphoreType.DMA((2,))]),
        out_shape=jax.ShapeDtypeStruct((n_slots, width), I32),
        compiler_params=pltpu.CompilerParams(dimension_semantics=("arbitrary",),
                                             vmem_limit_bytes=VMEM_LIMIT),
        name="experts",
    )(*plan, xs, w_gate, w_up, w_down)


def _pick(onehot, table):
    return jnp.sum(jnp.where(onehot, table[None, :], 0), axis=1).astype(I32)


def _expert_plan(counts, n_assign):
    blk = EXPERT_BLOCK
    per_e = (counts + blk - 1) // blk
    first_blk = jnp.cumsum(per_e) - per_e
    n_slots = (n_assign // blk + N_EXPERTS) * blk
    return (first_blk * blk).astype(I32), (first_blk.astype(I32), per_e.astype(I32), counts), n_slots


def _slot_of(starts, expert, rank):
    onehot = jnp.arange(N_EXPERTS, dtype=I32)[None, :] == expert[:, None]
    return _pick(onehot, starts) + rank


def _combine_kernel(rows_ref, gate_ref, x1_ref, mod_ref, fg_ref, *rest):
    o_ref = rest[-1]
    tc = x1_ref.shape[0]
    gate_f = mod_ref[0][5:6]
    for q in range(tc // LANES):
        tok = slice(q * LANES, (q + 1) * LANES)
        g_rows = jnp.concatenate([gate_ref[:, tok], jnp.zeros((LANES - SUBLANES, LANES), F32)], axis=0)
        gc = g_rows.T
        y = (gc[:, 0:1] * _unpack_bf16_pairs(rows_ref[0, tok, :])
             + gc[:, 1:2] * _unpack_bf16_pairs(rows_ref[1, tok, :]))
        x2 = x1_ref[tok, :] + gate_f * y
        r = lax.rsqrt(jnp.mean(x2 * x2, axis=-1, keepdims=True) + EPS)
        o_ref[tok, :] = (x2 * r) * fg_ref[...]


def _combine(rows, gcol, x1, mod, final_g, seq, lb0, b0, nb, bsz, out_prev):
    dm = x1.shape[1]
    width = rows.shape[2]
    tc = min(COMBINE_TILE, seq)
    nst = seq // tc
    in_specs = [pl.BlockSpec((TOP_K, tc, width), lambda b, s: (0, b * nst + s, 0)),
                pl.BlockSpec((SUBLANES, tc), lambda b, s: (0, (b + lb0) * nst + s)),
                pl.BlockSpec((tc, dm), lambda b, s: ((b + lb0) * nst + s, 0)),
                pl.BlockSpec((1, 6, dm), lambda b, s: (b + b0, 0, 0)),
                pl.BlockSpec((1, dm), lambda b, s: (0, 0))]
    args = [rows, gcol, x1, mod, final_g.reshape(1, dm)]
    aliases = {}
    if out_prev is not None:
        in_specs.append(pl.BlockSpec(memory_space=pl.ANY))
        args.append(out_prev)
        aliases = {len(args) - 1: 0}
    return pl.pallas_call(
        _combine_kernel,
        grid=(nb, nst),
        in_specs=in_specs,
        out_specs=pl.BlockSpec((tc, dm), lambda b, s: ((b + b0) * nst + s, 0)),
        out_shape=jax.ShapeDtypeStruct((bsz * seq, dm), F32),
        input_output_aliases=aliases,
        compiler_params=pltpu.CompilerParams(dimension_semantics=("arbitrary", "arbitrary"),
                                             vmem_limit_bytes=VMEM_LIMIT),
        name="combine",
    )(*args)


def _layer(x, c, ada_w, ada_b, norm1_g, w_in, conv_w, conv_b, w_q, w_k, b_igate, b_fgate,
           mlstm_norm_g, mlstm_skip, w_pool, b_pool, pool_scale, w_out, norm2_g,
           w_rg, b_rg, w_re, b_re, w_eg, w_eu, w_ed, out_g):
    bsz, seq, dm = x.shape
    dml = conv_w.shape[1]
    n_tok = bsz * seq
    ts = min(SEQ_TILE, seq)

    mod = _ada(c, ada_w, ada_b).reshape(bsz, 6, dm)

    w_in_r, w_out_b = _prep_weights(w_in, w_out, dml)
    gbias = jnp.pad(jnp.concatenate([b_igate, b_fgate]), (0, LANES - 2 * N_HEADS)).reshape(1, LANES)
    wqk = jnp.concatenate([w_q, w_k], axis=-1).astype(BF16)
    wrt = jnp.zeros((LOGIT_ROWS, dm), F32)
    wrt = wrt.at[0:N_GROUPS].set(w_rg.T).at[SUBLANES:SUBLANES + N_EXPERTS].set(w_re.T).astype(BF16)
    rb = jnp.zeros((LOGIT_ROWS,), F32).at[0:N_GROUPS].set(b_rg).at[SUBLANES:SUBLANES + N_EXPERTS].set(b_re)
    rbias = jnp.broadcast_to(rb[:, None], (LOGIT_ROWS, ts))

    gdim = (dm - dml) // len(POOL_WINDOWS)
    win = jnp.repeat(jnp.array(POOL_WINDOWS, F32), gdim)[None, :]
    t1 = jnp.arange(1, ts + 1, dtype=F32)[:, None]
    pinv = jnp.stack([1.0 / jnp.minimum(t1, win), jnp.broadcast_to(1.0 / win, (ts, dm - dml))])

    mod = mod.at[:, 1].set(norm1_g * (1.0 + mod[:, 1])).at[:, 4].set(norm2_g * (1.0 + mod[:, 4]))

    mixer_params = (w_in_r, gbias, conv_w, conv_b.reshape(1, dml), wqk,
                    mlstm_norm_g.reshape(1, dml), mlstm_skip.reshape(1, dml), w_pool.astype(BF16),
                    b_pool.reshape(1, dm - dml), pool_scale.reshape(1, dm - dml), pinv,
                    w_out_b, wrt, rbias)

    x1, h2, lgt = _mixer(x, mod, *mixer_params, 0, bsz)
    idx, gcol, cnt = _route(lgt, n_tok)
    counts = cnt[:, 0].astype(I32)
    starts, plan, n_slots = _expert_plan(counts, n_tok * TOP_K)
    dest0 = _slot_of(starts, idx[0], idx[2])
    dest1 = _slot_of(starts, idx[1], idx[3])
    xs = _dispatch(h2, dest0, dest1, n_slots)
    ys = _experts(plan, xs, w_eg, w_eu, w_ed)

    nb = bsz // COMBINE_GROUPS
    assert bsz % COMBINE_GROUPS == 0
    out = None
    for b0 in range(0, bsz, nb):
        rows = _collect(ys, dest0, dest1, b0 * seq, nb * seq)
        out = _combine(rows, gcol, x1, mod, out_g, seq, b0, b0, nb, bsz, out)
    return out.reshape(bsz, seq, dm)


def kernel(x, c, ada_w, ada_b, norm1_g, w_in, conv_w, conv_b, w_q, w_k, b_igate, b_fgate, mlstm_norm_g, mlstm_skip, w_pool, b_pool, pool_scale, w_out, norm2_g, w_router_group, b_router_group, w_router_expert, b_router_expert, w_expert_gate, w_expert_up, w_expert_down, final_g):
    depth = ada_w.shape[0]
    assert depth == 1, "the final norm is fused into the last layer's combine kernel"
    l = 0
    return _layer(x, c, ada_w[l], ada_b[l], norm1_g[l], w_in[l], conv_w[l], conv_b[l], w_q[l],
                  w_k[l], b_igate[l], b_fgate[l], mlstm_norm_g[l], mlstm_skip[l], w_pool[l],
                  b_pool[l], pool_scale[l], w_out[l], norm2_g[l], w_router_group[l],
                  b_router_group[l], w_router_expert[l], b_router_expert[l],
                  w_expert_gate[l], w_expert_up[l], w_expert_down[l], final_g)
```
